```python
import jax, jax.numpy as jnp
from jax import lax
import numpy as np

D_MODEL = 2048
BATCH = 8
SEQ = 2048
DEPTH = 2

POOL_WINDOWS = (2, 4, 8, 16)
N_POOL_GROUPS = len(POOL_WINDOWS)
POOL_GROUP_WIDTH = D_MODEL // 8
POOL_WIDTH = N_POOL_GROUPS * POOL_GROUP_WIDTH
HEAD_DIM = 128
N_HEADS = D_MODEL // (2 * HEAD_DIM)
ATTN_WIDTH = N_HEADS * HEAD_DIM
Q_BLOCK = 128
N_BRANCHES = 2
BRANCH_WIDTH = POOL_WIDTH
IN_WIDTH = POOL_WIDTH + 3 * ATTN_WIDTH + N_HEADS
N_EXPERTS = 16
N_EXPERT_GROUPS = 4
EXPERTS_PER_GROUP = N_EXPERTS // N_EXPERT_GROUPS
TOP_K = 2
D_EXPERT = D_MODEL // 2
DISPATCH_BLOCK = 128
N_MOD = 6
EPS = 1e-6

kernel_name = "hybrid_pool_fox_grouped_moe_adaln"


def rmsnorm(x, g):
    x32 = x.astype(jnp.float32)
    y = x32 * lax.rsqrt(jnp.mean(x32 * x32, axis=-1, keepdims=True) + EPS)
    return (y * g.astype(jnp.float32)).astype(x.dtype)


def causal_multiscale_pool(u, w_pool, pool_scale):
    B, S, _ = u.shape
    ug_all = u.reshape(B, S, N_POOL_GROUPS, POOL_GROUP_WIDTH)
    t_idx = jnp.arange(S)
    outs = []
    for g, w in enumerate(POOL_WINDOWS):
        ug = ug_all[:, :, g].astype(jnp.float32)
        cs = jnp.cumsum(ug, axis=1)
        lagged = jnp.pad(cs, ((0, 0), (w, 0), (0, 0)))[:, :S]
        cnt = jnp.minimum(t_idx + 1, w).astype(jnp.float32)[None, :, None]
        outs.append((cs - lagged) / cnt - ug)
    pooled = jnp.stack(outs, axis=2).astype(u.dtype)
    mixed = jnp.einsum('bsgc,gcd->bsgd', pooled, w_pool)
    return mixed.reshape(B, S, POOL_WIDTH) * pool_scale


def forgetting_attention(q, k, v, f_logit):
    B, S, _ = q.shape
    def heads(a):
        return a.reshape(B, S, N_HEADS, HEAD_DIM).transpose(0, 2, 1, 3)
    q, k, v = heads(q), heads(k), heads(v)
    log_f = jax.nn.log_sigmoid(f_logit.astype(jnp.float32)).transpose(0, 2, 1)
    F = jnp.cumsum(log_f, axis=-1)
    nb = S // Q_BLOCK
    q_blocks = q.reshape(B, N_HEADS, nb, Q_BLOCK, HEAD_DIM).transpose(2, 0, 1, 3, 4)
    F_blocks = F.reshape(B, N_HEADS, nb, Q_BLOCK).transpose(2, 0, 1, 3)
    k_pos = jnp.arange(S)
    scale = HEAD_DIM ** -0.5

    def block(args):
        qb, Fq, i = args
        s = jnp.einsum('bhqd,bhkd->bhqk', qb, k, preferred_element_type=jnp.float32) * scale
        s = s + (Fq[..., None] - F[:, :, None, :])
        q_pos = i * Q_BLOCK + jnp.arange(Q_BLOCK)
        s = jnp.where(k_pos[None, :] <= q_pos[:, None], s, -jnp.inf)
        p = jax.nn.softmax(s, axis=-1)
        return jnp.einsum('bhqk,bhkd->bhqd', p.astype(v.dtype), v)

    o = lax.map(block, (q_blocks, F_blocks, jnp.arange(nb)))
    return o.transpose(1, 0, 3, 2, 4).reshape(B, S, ATTN_WIDTH)


def hybrid_mixer(h, w_in, b_forget, w_pool, pool_scale, w_branch, w_gate, b_gate, w_out):
    B, S, D = h.shape
    proj = jnp.einsum('bsd,dn->bsn', h, w_in)
    p0 = POOL_WIDTH
    u_pool = proj[..., :p0]
    q = proj[..., p0:p0 + ATTN_WIDTH]
    k = proj[..., p0 + ATTN_WIDTH:p0 + 2 * ATTN_WIDTH]
    v = proj[..., p0 + 2 * ATTN_WIDTH:p0 + 3 * ATTN_WIDTH]
    f_logit = proj[..., p0 + 3 * ATTN_WIDTH:] + b_forget
    pool_out = causal_multiscale_pool(u_pool, w_pool, pool_scale)
    attn_out = forgetting_attention(q, k, v, f_logit)
    branches = jnp.stack([pool_out, attn_out], axis=2)
    y = jnp.einsum('bsnw,nwd->bsnd', branches, w_branch)
    g = jax.nn.sigmoid(jnp.einsum('bsd,dm->bsm', h, w_gate) + b_gate).reshape(B, S, N_BRANCHES, D)
    merged = jnp.sum(g * y, axis=2)
    return jnp.einsum('bsd,de->bse', merged, w_out)


def grouped_moe(h, w_router, b_router, w_g, w_u, w_d):
    B, S, D = h.shape
    T = B * S
    hf = h.reshape(T, D)
    logits = jnp.einsum('td,de->te', hf, w_router, preferred_element_type=jnp.float32)
    probs = jax.nn.softmax(logits, axis=-1)
    sel = (probs + b_router.astype(jnp.float32)).reshape(T, N_EXPERT_GROUPS, EXPERTS_PER_GROUP)
    group_score = lax.top_k(sel, TOP_K)[0].sum(-1)
    g_star = jnp.argmax(group_score, axis=-1)
    sel_in = sel[jnp.arange(T), g_star]
    _, local = lax.top_k(sel_in, TOP_K)
    expert_idx = g_star[:, None] * EXPERTS_PER_GROUP + local
    gate = jnp.take_along_axis(probs, expert_idx, axis=1)
    gate = gate / jnp.sum(gate, axis=-1, keepdims=True)

    A = T * TOP_K
    e_flat = expert_idx.reshape(A)
    tok_flat = jnp.repeat(jnp.arange(T), TOP_K)
    w_flat = gate.reshape(A)
    order = jnp.argsort(e_flat)
    e_s, tok_s, w_s = e_flat[order], tok_flat[order], w_flat[order]
    counts = jnp.bincount(e_flat, length=N_EXPERTS)
    starts = jnp.cumsum(counts) - counts
    pcounts = (counts + DISPATCH_BLOCK - 1) // DISPATCH_BLOCK * DISPATCH_BLOCK
    pends = jnp.cumsum(pcounts)
    pstarts = pends - pcounts
    dest = pstarts[e_s] + (jnp.arange(A) - starts[e_s])
    n_blocks = (A + DISPATCH_BLOCK - 1) // DISPATCH_BLOCK + N_EXPERTS
    buf = jnp.zeros((n_blocks * DISPATCH_BLOCK, D), h.dtype).at[dest].set(hf[tok_s])
    block_expert = jnp.clip(
        jnp.searchsorted(pends, jnp.arange(n_blocks) * DISPATCH_BLOCK, side='right'), 0, N_EXPERTS - 1)

    def expert_block(args):
        xb, e = args
        a = xb @ w_g[e]
        u = xb @ w_u[e]
        return (jax.nn.silu(a) * u) @ w_d[e]

    y_buf = lax.map(expert_block, (buf.reshape(n_blocks, DISPATCH_BLOCK, D), block_expert))
    y = y_buf.reshape(-1, D)[dest] * w_s[:, None].astype(h.dtype)
    out = jax.ops.segment_sum(y, tok_s, num_segments=T)
    return out.reshape(B, S, D)


def setup_inputs(seed: int = 0) -> dict:
    key = jax.random.key(seed)
    ks = jax.random.split(key, 20)
    nrm = jax.random.normal
    D = D_MODEL
    return {
        "x": nrm(ks[0], (BATCH, SEQ, D), jnp.float32),
        "c": nrm(ks[1], (BATCH, D), jnp.float32),
        "w_ada": nrm(ks[2], (DEPTH, D, N_MOD * D), jnp.float32) * (0.5 * D ** -0.5),
        "b_ada": 0.01 * nrm(ks[3], (DEPTH, N_MOD * D), jnp.float32),
        "norm_mix": 1.0 + 0.1 * nrm(ks[4], (DEPTH, D), jnp.float32),
        "norm_moe": 1.0 + 0.1 * nrm(ks[5], (DEPTH, D), jnp.float32),
        "w_in": nrm(ks[6], (DEPTH, D, IN_WIDTH), jnp.float32) * D ** -0.5,
        "b_forget": 3.0 + 0.5 * nrm(ks[7], (DEPTH, N_HEADS), jnp.float32),
        "w_pool": nrm(ks[8], (DEPTH, N_POOL_GROUPS, POOL_GROUP_WIDTH, POOL_GROUP_WIDTH), jnp.float32) * POOL_GROUP_WIDTH ** -0.5,
        "pool_scale": 1.0 + 0.1 * nrm(ks[9], (DEPTH, POOL_WIDTH), jnp.float32),
        "w_branch": nrm(ks[10], (DEPTH, N_BRANCHES, BRANCH_WIDTH, D), jnp.float32) * BRANCH_WIDTH ** -0.5,
        "w_gate": nrm(ks[11], (DEPTH, D, N_BRANCHES * D), jnp.float32) * D ** -0.5,
        "b_gate": 0.01 * nrm(ks[12], (DEPTH, N_BRANCHES * D), jnp.float32),
        "w_out": nrm(ks[13], (DEPTH, D, D), jnp.float32) * D ** -0.5,
        "w_router": nrm(ks[14], (D, N_EXPERTS), jnp.float32) * D ** -0.5,
        "b_router": 0.01 * nrm(ks[15], (N_EXPERTS,), jnp.float32),
        "w_exp_gate": nrm(ks[16], (DEPTH, N_EXPERTS, D, D_EXPERT), jnp.float32) * D ** -0.5,
        "w_exp_up": nrm(ks[17], (DEPTH, N_EXPERTS, D, D_EXPERT), jnp.float32) * D ** -0.5,
        "w_exp_down": nrm(ks[18], (DEPTH, N_EXPERTS, D_EXPERT, D), jnp.float32) * D_EXPERT ** -0.5,
        "norm_final": 1.0 + 0.1 * nrm(ks[19], (D,), jnp.float32),
    }


def reference(x, c, w_ada, b_ada, norm_mix, norm_moe, w_in, b_forget, w_pool, pool_scale,
              w_branch, w_gate, b_gate, w_out, w_router, b_router, w_exp_gate, w_exp_up,
              w_exp_down, norm_final):
    c_act = jax.nn.silu(c)
    for l in range(DEPTH):
        mod = (jnp.einsum('bd,dm->bm', c_act, w_ada[l]) + b_ada[l])[:, None, :]
        shift_a, scale_a, gate_a, shift_m, scale_m, gate_m = jnp.split(mod, N_MOD, axis=-1)
        h = rmsnorm(x, norm_mix[l]) * (1.0 + scale_a) + shift_a
        x = x + gate_a * hybrid_mixer(h, w_in[l], b_forget[l], w_pool[l], pool_scale[l],
                                      w_branch[l], w_gate[l], b_gate[l], w_out[l])
        h = rmsnorm(x, norm_moe[l]) * (1.0 + scale_m) + shift_m
        x = x + gate_m * grouped_moe(h, w_router, b_router, w_exp_gate[l], w_exp_up[l], w_exp_down[l])
    return rmsnorm(x, norm_final)
```

```python
import functools

import jax
import jax.numpy as jnp
from jax import lax
from jax.experimental import pallas as pl
from jax.experimental.pallas import tpu as pltpu

N_MOD = 6
EPS = 1e-6
POOL_WINDOWS = (2, 4, 8, 16)
HEAD_DIM = 128
N_EXPERTS = 16
N_EXPERT_GROUPS = 4
EXPERTS_PER_GROUP = N_EXPERTS // N_EXPERT_GROUPS
TOP_K = 2
LANES = 128
VMEM_LIMIT = 56 * 1024 * 1024

_F32 = jnp.float32
_BF16 = jnp.bfloat16


def _params(sem, vmem=VMEM_LIMIT):
    return pltpu.CompilerParams(dimension_semantics=sem, vmem_limit_bytes=vmem)


def _sigmoid(v):
    return 1.0 / (1.0 + jnp.exp(-v))


def _rms_mod(x, g, scale, shift):
    ms = jnp.mean(x * x, axis=-1, keepdims=True)
    return x * lax.rsqrt(ms + EPS) * g * (1.0 + scale) + shift


def _ada_kernel(c_ref, w_ref, b_ref, o_ref):
    c = c_ref[...]
    ca = (c * _sigmoid(c)).astype(_BF16)
    o_ref[...] = jnp.dot(ca, w_ref[...].astype(_BF16), preferred_element_type=_F32) + b_ref[...]


def _ada(c, w_ada, b_ada):
    L, D, M = w_ada.shape
    B = c.shape[0]
    tn = min(1024, M)
    return pl.pallas_call(
        _ada_kernel,
        grid=(L, M // tn),
        in_specs=[
            pl.BlockSpec((B, D), lambda l, n: (0, 0)),
            pl.BlockSpec((None, D, tn), lambda l, n: (l, 0, n)),
            pl.BlockSpec((None, 1, tn), lambda l, n: (l, 0, n)),
        ],
        out_specs=pl.BlockSpec((None, B, tn), lambda l, n: (l, 0, n)),
        out_shape=jax.ShapeDtypeStruct((L, B, M), _F32),
        compiler_params=_params(("arbitrary", "arbitrary")),
        name="ada_mod",
    )(c, w_ada, b_ada.reshape(L, 1, M))


def _normmod_kernel(x_ref, g_ref, sc_ref, sh_ref, h_ref):
    h_ref[...] = _rms_mod(x_ref[...], g_ref[...], sc_ref[...], sh_ref[...]).astype(h_ref.dtype)


def _norm_mod(x, g, mod, i_shift, i_scale):
    B, S, D = x.shape
    ts = min(512, S)
    return pl.pallas_call(
        _normmod_kernel,
        grid=(B, S // ts),
        in_specs=[
            pl.BlockSpec((None, ts, D), lambda b, s: (b, s, 0)),
            pl.BlockSpec((1, D), lambda b, s: (0, 0)),
            pl.BlockSpec((None, None, 1, D), lambda b, s: (i_scale, b, 0, 0)),
            pl.BlockSpec((None, None, 1, D), lambda b, s: (i_shift, b, 0, 0)),
        ],
        out_specs=pl.BlockSpec((None, ts, D), lambda b, s: (b, s, 0)),
        out_shape=jax.ShapeDtypeStruct((B, S, D), _BF16),
        compiler_params=_params(("arbitrary", "arbitrary")),
        name="norm_mod",
    )(x, g.reshape(1, D), mod, mod)


def _proj_kernel(h_ref, w_ref, o_ref):
    acc = jnp.dot(h_ref[...], w_ref[...], preferred_element_type=_F32)
    for j in range(o_ref.shape[0]):
        o_ref[j] = acc[:, j * LANES:(j + 1) * LANES].astype(o_ref.dtype)


def _proj(h, w):
    B, S, D = h.shape
    N = w.shape[1]
    tn = min(1024, N)
    tm = min(1024, S)
    return pl.pallas_call(
        _proj_kernel,
        grid=(N // tn, B, S // tm),
        in_specs=[
            pl.BlockSpec((None, tm, D), lambda n, b, m: (b, m, 0)),
            pl.BlockSpec((D, tn), lambda n, b, m: (0, n)),
        ],
        out_specs=pl.BlockSpec((None, tn // LANES, tm, LANES), lambda n, b, m: (b, n, m, 0)),
        out_shape=jax.ShapeDtypeStruct((B, N // LANES, S, LANES), _BF16),
        compiler_params=_params(("arbitrary", "arbitrary", "arbitrary")),
        name="in_proj",
    )(h, w)


def _forget_kernel(h_ref, w_ref, b_ref, o_ref):
    S = h_ref.shape[0]
    fl = jnp.dot(h_ref[...], w_ref[...], preferred_element_type=_F32) + b_ref[...]
    acc = jnp.minimum(fl, 0.0) - jnp.log(1.0 + jnp.exp(-jnp.abs(fl)))
    row = lax.broadcasted_iota(jnp.int32, acc.shape, 0)
    sh = 1
    while sh < S:
        acc = acc + jnp.where(row >= sh, pltpu.roll(acc, sh, axis=0), 0.0)
        sh *= 2
    o_ref[...] = acc


def _forget(h, wf, bf):
    B, S, D = h.shape
    return pl.pallas_call(
        _forget_kernel,
        grid=(B,),
        in_specs=[
            pl.BlockSpec((None, S, D), lambda b: (b, 0, 0)),
            pl.BlockSpec((D, LANES), lambda b: (0, 0)),
            pl.BlockSpec((1, LANES), lambda b: (0, 0)),
        ],
        out_specs=pl.BlockSpec((None, S, LANES), lambda b: (b, 0, 0)),
        out_shape=jax.ShapeDtypeStruct((B, S, LANES), _F32),
        compiler_params=_params(("arbitrary",)),
        name="forget_cumsum",
    )(h, wf, bf)


def _pool_kernel(u_ref, wp_ref, ps_ref, o_ref, *, cpg):
    S = u_ref.shape[1]
    gw = cpg * LANES
    row = lax.broadcasted_iota(jnp.int32, (S, gw), 0)
    for g, w in enumerate(POOL_WINDOWS):
        parts = [u_ref[g * cpg + j] for j in range(cpg)]
        u = (parts[0] if cpg == 1 else jnp.concatenate(parts, axis=-1)).astype(_F32)
        s = u
        sh = 1
        while sh < w:
            s = s + jnp.where(row >= sh, pltpu.roll(s, sh, axis=0), 0.0)
            sh *= 2
        cnt = jnp.minimum(row + 1, w).astype(_F32)
        pooled = s / cnt - u
        mixed = jnp.dot(pooled.astype(_BF16), wp_ref[g], preferred_element_type=_F32)
        o_ref[:, g * gw:(g + 1) * gw] = (mixed * ps_ref[:, g * gw:(g + 1) * gw]).astype(o_ref.dtype)


def _pool(proj5, wp, ps):
    B, _, S, _ = proj5.shape
    G, gw, _ = wp.shape
    cpg = gw // LANES
    W = G * gw
    return pl.pallas_call(
        functools.partial(_pool_kernel, cpg=cpg),
        grid=(B,),
        in_specs=[
            pl.BlockSpec((None, G * cpg, S, LANES), lambda b: (b, 0, 0, 0)),
            pl.BlockSpec((G, gw, gw), lambda b: (0, 0, 0)),
            pl.BlockSpec((1, W), lambda b: (0, 0)),
        ],
        out_specs=pl.BlockSpec((None, S, W), lambda b: (b, 0, 0)),
        out_shape=jax.ShapeDtypeStruct((B, S, W), _BF16),
        compiler_params=_params(("arbitrary",)),
        name="pool_mixer",
    )(proj5, wp, ps)


def _attn_kernel(q_ref, k_ref, v_ref, fq_ref, fk_ref, o_ref, *, blk):
    S = q_ref.shape[0]
    nt = (((1,), (1,)), ((), ()))
    r_io = lax.broadcasted_iota(jnp.int32, (blk, blk), 0)
    c_io = lax.broadcasted_iota(jnp.int32, (blk, blk), 1)
    causal = c_io <= r_io

    for i in range(S // blk):
        q = q_ref[i * blk:(i + 1) * blk, :]
        fq = fq_ref[i * blk:(i + 1) * blk, :]

        def step(k, v, fk, carry, mask):
            m, l, acc = carry
            s = lax.dot_general(q, k, nt, preferred_element_type=_F32) + (fq - fk)
            if mask:
                s = jnp.where(causal, s, -jnp.inf)
            m_new = jnp.maximum(m, jnp.max(s, axis=-1, keepdims=True))
            p = jnp.exp(s - m_new)
            alpha = jnp.exp(m - m_new)
            l = alpha * l + jnp.sum(p, axis=-1, keepdims=True)
            acc = alpha * acc + jnp.dot(p.astype(_BF16), v, preferred_element_type=_F32)
            return m_new, l, acc

        def body(j, carry):
            off = pl.multiple_of(j * blk, blk)
            return step(k_ref[pl.ds(off, blk), :], v_ref[pl.ds(off, blk), :],
                        fk_ref[:, pl.ds(off, blk)], carry, False)

        carry = (jnp.full((blk, 1), -jnp.inf, _F32), jnp.zeros((blk, 1), _F32),
                 jnp.zeros((blk, HEAD_DIM), _F32))
        if i > 0:
            carry = lax.fori_loop(0, i, body, carry)
        sl = slice(i * blk, (i + 1) * blk)
        m, l, acc = step(k_ref[sl, :], v_ref[sl, :], fk_ref[:, sl], carry, True)
        o_ref[sl, :] = (acc / l).astype(o_ref.dtype)


def _attn(proj5, fq, fk, nh):
    B, _, S, _ = proj5.shape
    blk = min(256, S)
    chunk = lambda o: pl.BlockSpec((None, None, S, HEAD_DIM), lambda b, h: (b, o + h, 0, 0))
    return pl.pallas_call(
        functools.partial(_attn_kernel, blk=blk),
        grid=(B, nh),
        in_specs=[
            chunk(nh), chunk(2 * nh), chunk(3 * nh),
            pl.BlockSpec((None, None, S, 1), lambda b, h: (b, h, 0, 0)),
            pl.BlockSpec((None, None, 1, S), lambda b, h: (b, h, 0, 0)),
        ],
        out_specs=pl.BlockSpec((None, None, S, HEAD_DIM), lambda b, h: (b, h, 0, 0)),
        out_shape=jax.ShapeDtypeStruct((B, nh, S, HEAD_DIM), _BF16),
        compiler_params=_params(("arbitrary", "arbitrary")),
        name="forget_attn",
    )(proj5, proj5, proj5, fq, fk)


def _merge_kernel(h_ref, p_ref, a_ref, wg0_ref, wg1_ref, bg0_ref, bg1_ref, wb0_ref, wb1_ref, o_ref):
    h = h_ref[...]
    nh = a_ref.shape[0]
    a = jnp.concatenate([a_ref[j] for j in range(nh)], axis=-1)
    g0 = _sigmoid(jnp.dot(h, wg0_ref[...], preferred_element_type=_F32) + bg0_ref[...])
    y0 = jnp.dot(p_ref[...], wb0_ref[...], preferred_element_type=_F32)
    acc = g0 * y0
    g1 = _sigmoid(jnp.dot(h, wg1_ref[...], preferred_element_type=_F32) + bg1_ref[...])
    y1 = jnp.dot(a, wb1_ref[...], preferred_element_type=_F32)
    o_ref[...] = (acc + g1 * y1).astype(o_ref.dtype)


def _merge(h, pool_out, attn, wg, bg, wb):
    B, S, D = h.shape
    W = pool_out.shape[-1]
    nh = attn.shape[1]
    tn = min(512, D)
    tm = min(1024, S)
    nn = D // tn
    return pl.pallas_call(
        _merge_kernel,
        grid=(nn, B, S // tm),
        in_specs=[
            pl.BlockSpec((None, tm, D), lambda n, b, m: (b, m, 0)),
            pl.BlockSpec((None, tm, W), lambda n, b, m: (b, m, 0)),
            pl.BlockSpec((None, nh, tm, HEAD_DIM), lambda n, b, m: (b, 0, m, 0)),
            pl.BlockSpec((D, tn), lambda n, b, m: (0, n)),
            pl.BlockSpec((D, tn), lambda n, b, m: (0, nn + n)),
            pl.BlockSpec((1, tn), lambda n, b, m: (0, n)),
            pl.BlockSpec((1, tn), lambda n, b, m: (0, nn + n)),
            pl.BlockSpec((None, W, tn), lambda n, b, m: (0, 0, n)),
            pl.BlockSpec((None, W, tn), lambda n, b, m: (1, 0, n)),
        ],
        out_specs=pl.BlockSpec((None, tm, tn), lambda n, b, m: (b, m, n)),
        out_shape=jax.ShapeDtypeStruct((B, S, D), _BF16),
        compiler_params=_params(("arbitrary", "arbitrary", "arbitrary")),
        name="branch_merge",
    )(h, pool_out, attn, wg, wg, bg, bg, wb, wb)


def _out_kernel(x_ref, m_ref, w_ref, g_ref, o_ref):
    y = jnp.dot(m_ref[...], w_ref[...], preferred_element_type=_F32)
    o_ref[...] = x_ref[...] + g_ref[...] * y


def _out_proj(x, merged, w, mod, i_gate):
    B, S, D = x.shape
    tm = min(512, S)
    return pl.pallas_call(
        _out_kernel,
        grid=(B, S // tm),
        in_specs=[
            pl.BlockSpec((None, tm, D), lambda b, m: (b, m, 0)),
            pl.BlockSpec((None, tm, D), lambda b, m: (b, m, 0)),
            pl.BlockSpec((D, D), lambda b, m: (0, 0)),
            pl.BlockSpec((None, None, 1, D), lambda b, m: (i_gate, b, 0, 0)),
        ],
        out_specs=pl.BlockSpec((None, tm, D), lambda b, m: (b, m, 0)),
        out_shape=jax.ShapeDtypeStruct((B, S, D), _F32),
        compiler_params=_params(("arbitrary", "arbitrary")),
        name="out_proj",
    )(x, merged, w, mod)


def _route_kernel(x_ref, g_ref, sc_ref, sh_ref, wr_ref, br_ref,
                  hp_ref, e_ref, pos_ref, gate_ref, cnt_ref):
    first_step = (pl.program_id(0) == 0) & (pl.program_id(1) == 0)

    @pl.when(first_step)
    def _():
        cnt_ref[...] = jnp.zeros_like(cnt_ref)

    h = _rms_mod(x_ref[...], g_ref[...], sc_ref[...], sh_ref[...])
    tr, D = h.shape

    bits = lax.bitcast_convert_type(h.astype(_BF16).astype(_F32), jnp.uint32)
    hp_ref[...] = (bits[:, :D // 2] >> 16) | (bits[:, D // 2:] & jnp.uint32(0xFFFF0000))

    logits = lax.dot_general(wr_ref[...], h, (((1,), (1,)), ((), ())),
                             precision=lax.Precision.HIGHEST, preferred_element_type=_F32)
    rows = [logits[e:e + 1, :] for e in range(N_EXPERTS)]
    mx = functools.reduce(jnp.maximum, rows)
    ex = [jnp.exp(r - mx) for r in rows]
    den = functools.reduce(lambda a, b: a + b, ex)
    probs = [v / den for v in ex]
    sel = [probs[e] + br_ref[e:e + 1, :] for e in range(N_EXPERTS)]

    in_top = [None] * N_EXPERTS
    gscore = []
    for g in range(N_EXPERT_GROUPS):
        ids = range(g * EXPERTS_PER_GROUP, (g + 1) * EXPERTS_PER_GROUP)
        score = None
        for i in ids:
            rank = None
            for j in ids:
                if j == i:
                    continue
                beats = (sel[j] > sel[i]) | ((sel[j] == sel[i]) if j < i else False)
                beats = beats.astype(_F32)
                rank = beats if rank is None else rank + beats
            in_top[i] = rank < float(TOP_K)
            term = jnp.where(in_top[i], sel[i], 0.0)
            score = term if score is None else score + term
        gscore.append(score)
    chosen = []
    for g in range(N_EXPERT_GROUPS):
        lose = None
        for g2 in range(N_EXPERT_GROUPS):
            if g2 == g:
                continue
            b = (gscore[g2] > gscore[g]) | ((gscore[g2] == gscore[g]) if g2 < g else False)
            lose = b if lose is None else (lose | b)
        chosen.append(jnp.logical_not(lose))
    picked = [in_top[e] & chosen[e // EXPERTS_PER_GROUP] for e in range(N_EXPERTS)]
    pf = [p.astype(_F32) for p in picked]

    onehot = jnp.concatenate(pf, axis=0)
    r_io = lax.broadcasted_iota(jnp.int32, (tr, tr), 0)
    c_io = lax.broadcasted_iota(jnp.int32, (tr, tr), 1)
    upper = (r_io <= c_io).astype(_BF16)
    incl = jnp.dot(onehot.astype(_BF16), upper, preferred_element_type=_F32)
    base = cnt_ref[:, 0:1]
    posm = incl - 1.0 + base
    cnt_ref[...] = jnp.broadcast_to(base + incl[:, tr - 1:tr], cnt_ref.shape)

    gnum = [pf[e] * probs[e] for e in range(N_EXPERTS)]
    gden = functools.reduce(lambda a, b: a + b, gnum)
    zero = jnp.zeros_like(pf[0])
    seen = zero
    e_out = [zero, zero]
    p_out = [zero, zero]
    g_out = [zero, zero]
    for e in range(N_EXPERTS):
        for k in range(TOP_K):
            hit = pf[e] * (seen == float(k)).astype(_F32)
            e_out[k] = e_out[k] + hit * float(e)
            p_out[k] = p_out[k] + hit * posm[e:e + 1, :]
            g_out[k] = g_out[k] + hit * gnum[e]
        seen = seen + pf[e]
    e_ref[...] = jnp.concatenate(e_out, axis=0).astype(jnp.int32)
    pos_ref[...] = jnp.concatenate(p_out, axis=0).astype(jnp.int32)
    gate_ref[...] = jnp.concatenate(g_out, axis=0) / gden


def _route(x, g, mod, i_shift, i_scale, wr_t, br):
    B, S, D = x.shape
    tr = min(512, S)
    nb = S // tr
    E = N_EXPERTS
    tok = lambda dt: jax.ShapeDtypeStruct((B * nb, TOP_K, tr), dt)
    tok_spec = pl.BlockSpec((None, TOP_K, tr), lambda b, s: (b * nb + s, 0, 0))
    return pl.pallas_call(
        _route_kernel,
        grid=(B, nb),
        in_specs=[
            pl.BlockSpec((None, tr, D), lambda b, s: (b, s, 0)),
            pl.BlockSpec((1, D), lambda b, s: (0, 0)),
            pl.BlockSpec((None, None, 1, D), lambda b, s: (i_scale, b, 0, 0)),
            pl.BlockSpec((None, None, 1, D), lambda b, s: (i_shift, b, 0, 0)),
            pl.BlockSpec((E, D), lambda b, s: (0, 0)),
            pl.BlockSpec((E, 1), lambda b, s: (0, 0)),
        ],
        out_specs=[
            pl.BlockSpec((None, tr, D // 2), lambda b, s: (b, s, 0)),
            tok_spec, tok_spec, tok_spec,
            pl.BlockSpec((E, LANES), lambda b, s: (0, 0)),
        ],
        out_shape=[
            jax.ShapeDtypeStruct((B, S, D // 2), jnp.uint32),
            tok(jnp.int32), tok(jnp.int32), tok(_F32),
            jax.ShapeDtypeStruct((E, LANES), _F32),
        ],
        compiler_params=_params(("arbitrary", "arbitrary")),
        name="moe_route",
    )(x, g.reshape(1, D), mod, mod, wr_t, br)


def _dispatch_kernel(dest_ref, hp_ref, buf_in_ref, buf_ref, sem):
    del buf_in_ref
    tr = hp_ref.shape[0]

    def issue(r, carry):
        for k in range(TOP_K):
            pltpu.make_async_copy(hp_ref.at[pl.ds(r, 1), :],
                                  buf_ref.at[pl.ds(dest_ref[k, r], 1), :], sem).start()
        return carry

    lax.fori_loop(0, tr, issue, 0)

    def drain(r, carry):
        for k in range(TOP_K):
            pltpu.make_async_copy(hp_ref.at[pl.ds(0, 1), :], buf_ref.at[pl.ds(0, 1), :], sem).wait()
        return carry

    lax.fori_loop(0, tr, drain, 0)


def _dispatch(hp, dest, rows):
    T, Dh = hp.shape
    nbt, _, tr = dest.shape
    buf0 = jnp.zeros((rows, Dh), jnp.uint32)
    return pl.pallas_call(
        _dispatch_kernel,
        grid=(nbt,),
        in_specs=[
            pl.BlockSpec((None, TOP_K, tr), lambda i: (i, 0, 0), memory_space=pltpu.SMEM),
            pl.BlockSpec((tr, Dh), lambda i: (i, 0)),
            pl.BlockSpec(memory_space=pl.ANY),
        ],
        out_specs=pl.BlockSpec(memory_space=pl.ANY),
        out_shape=jax.ShapeDtypeStruct((rows, Dh), jnp.uint32),
        scratch_shapes=[pltpu.SemaphoreType.DMA(())],
        input_output_aliases={2: 0},
        compiler_params=_params(("arbitrary",)),
        name="moe_dispatch",
    )(dest, hp, buf0)


def _expert_kernel(be_ref, nu_ref, x_ref, wg_ref, wu_ref, wd_ref, o_ref):
    del be_ref
    i = pl.program_id(0)

    @pl.when(i < nu_ref[0])
    def _():
        xp = x_ref[...]
        lo = lax.bitcast_convert_type(xp << 16, _F32).astype(_BF16)
        hi = lax.bitcast_convert_type(xp & jnp.uint32(0xFFFF0000), _F32).astype(_BF16)
        xb = jnp.concatenate([lo, hi], axis=-1)
        a = jnp.dot(xb, wg_ref[...], preferred_element_type=_F32)
        u = jnp.dot(xb, wu_ref[...], preferred_element_type=_F32)
        act = (a * _sigmoid(a) * u).astype(_BF16)
        o_ref[...] = jnp.dot(act, wd_ref[...], preferred_element_type=_F32)

    @pl.when(i >= nu_ref[0])
    def _():
        o_ref[...] = jnp.zeros_like(o_ref)


def _experts(buf, block_expert, n_used, wg, wu, wd, bm):
    rows, Dh = buf.shape
    E, D, De = wg.shape
    nblk = rows // bm
    last = lambda i, be, nu: jnp.minimum(i, nu[0] - 1)
    grid_spec = pltpu.PrefetchScalarGridSpec(
        num_scalar_prefetch=2,
        grid=(nblk,),
        in_specs=[
            pl.BlockSpec((bm, Dh), lambda i, be, nu: (last(i, be, nu), 0)),
            pl.BlockSpec((None, D, De), lambda i, be, nu: (be[i], 0, 0)),
            pl.BlockSpec((None, D, De), lambda i, be, nu: (be[i], 0, 0)),
            pl.BlockSpec((None, De, D), lambda i, be, nu: (be[i], 0, 0)),
        ],
        out_specs=pl.BlockSpec((bm, D), lambda i, be, nu: (i, 0)),
    )
    return pl.pallas_call(
        _expert_kernel,
        grid_spec=grid_spec,
        out_shape=jax.ShapeDtypeStruct((rows, D), _F32),
        compiler_params=_params(("arbitrary",)),
        name="moe_experts",
    )(block_expert, n_used, buf, wg, wu, wd)


def _combine_kernel(dest_ref, x_ref, y_ref, gt_ref, gm_ref, nf_ref, o_ref, ybuf, sem, *, final):
    tr = x_ref.shape[0]

    def issue(r, carry):
        for k in range(TOP_K):
            pltpu.make_async_copy(y_ref.at[pl.ds(dest_ref[k, r], 1), :],
                                  ybuf.at[k, pl.ds(r, 1), :], sem).start()
        return carry

    lax.fori_loop(0, tr, issue, 0)

    def drain(r, carry):
        for k in range(TOP_K):
            pltpu.make_async_copy(y_ref.at[pl.ds(0, 1), :], ybuf.at[k, pl.ds(0, 1), :], sem).wait()
        return carry

    lax.fori_loop(0, tr, drain, 0)

    gt = gt_ref[...]
    moe = gt[:, 0:1] * ybuf[0] + gt[:, 1:2] * ybuf[1]
    xn = x_ref[...] + gm_ref[...] * moe
    if final:
        ms = jnp.mean(xn * xn, axis=-1, keepdims=True)
        xn = xn * lax.rsqrt(ms + EPS) * nf_ref[...]
    o_ref[...] = xn


def _combine(x, y, dest, gate_t, mod, i_gate, nf, final):
    B, S, D = x.shape
    nbt, _, tr = dest.shape
    nb = S // tr
    return pl.pallas_call(
        functools.partial(_combine_kernel, final=final),
        grid=(B, nb),
        in_specs=[
            pl.BlockSpec((None, TOP_K, tr), lambda b, s: (b * nb + s, 0, 0), memory_space=pltpu.SMEM),
            pl.BlockSpec((None, tr, D), lambda b, s: (b, s, 0)),
            pl.BlockSpec(memory_space=pl.ANY),
            pl.BlockSpec((None, tr, TOP_K), lambda b, s: (b * nb + s, 0, 0)),
            pl.BlockSpec((None, None, 1, D), lambda b, s: (i_gate, b, 0, 0)),
            pl.BlockSpec((1, D), lambda b, s: (0, 0)),
        ],
        out_specs=pl.BlockSpec((None, tr, D), lambda b, s: (b, s, 0)),
        out_shape=jax.ShapeDtypeStruct((B, S, D), _F32),
        scratch_shapes=[pltpu.VMEM((TOP_K, tr, D), _F32), pltpu.SemaphoreType.DMA(())],
        compiler_params=_params(("arbitrary", "arbitrary")),
        name="moe_combine",
    )(dest, x, y, gate_t, mod, nf.reshape(1, D))


def kernel(x, c, w_ada, b_ada, norm_mix, norm_moe, w_in, b_forget, w_pool, pool_scale, w_branch,
           w_gate, b_gate, w_out, w_router, b_router, w_exp_gate, w_exp_up, w_exp_down, norm_final):
    B, S, D = x.shape
    L = w_ada.shape[0]
    W = D // 2
    nh = W // HEAD_DIM
    T = B * S
    A = T * TOP_K
    bm = min(256, A // N_EXPERTS)
    rows = (A // bm + N_EXPERTS) * bm

    mod_all = _ada(c, w_ada, b_ada)
    mod_all = mod_all.reshape(L, B, N_MOD, D).transpose(0, 2, 1, 3)[:, :, :, None, :]
    wr_t = w_router.T
    br = b_router.reshape(N_EXPERTS, 1)
    colscale = jnp.concatenate([jnp.ones((W,), _F32), jnp.full((W,), HEAD_DIM ** -0.5, _F32),
                                jnp.ones((2 * W,), _F32)])

    for l in range(L):
        mod = mod_all[l]
        h = _norm_mod(x, norm_mix[l], mod, 0, 1)
        w_main = (w_in[l][:, :4 * W] * colscale).astype(_BF16)
        wf = jnp.zeros((D, LANES), _F32).at[:, :nh].set(w_in[l][:, 4 * W:]).astype(_BF16)
        bf = jnp.zeros((1, LANES), _F32).at[0, :nh].set(b_forget[l])
        proj5 = _proj(h, w_main)
        fcum = _forget(h, wf, bf)[:, :, :nh].transpose(0, 2, 1)
        pool_out = _pool(proj5, w_pool[l].astype(_BF16), pool_scale[l].reshape(1, W))
        attn = _attn(proj5, fcum[:, :, :, None], fcum[:, :, None, :], nh)
        merged = _merge(h, pool_out, attn, w_gate[l].astype(_BF16), b_gate[l].reshape(1, 2 * D),
                        w_branch[l].astype(_BF16))
        x = _out_proj(x, merged, w_out[l].astype(_BF16), mod, 2)

        hp, eidx, pos, gate, counts = _route(x, norm_moe[l], mod, 3, 4, wr_t, br)
        counts = counts[:, 0].astype(jnp.int32)
        pcounts = (counts + bm - 1) // bm * bm
        pends = jnp.cumsum(pcounts)
        pstarts = pends - pcounts
        dest = pstarts[eidx] + pos
        n_used = (pends[-1] // bm).astype(jnp.int32)
        blk_ids = jnp.arange(rows // bm, dtype=jnp.int32)
        blk_ids = jnp.minimum(blk_ids, n_used - 1)
        block_expert = jnp.clip(jnp.searchsorted(pends, blk_ids * bm, side='right'),
                                0, N_EXPERTS - 1).astype(jnp.int32)
        buf = _dispatch(hp.reshape(T, D // 2), dest, rows)
        y = _experts(buf, block_expert, n_used.reshape(1), w_exp_gate[l].astype(_BF16),
                     w_exp_up[l].astype(_BF16), w_exp_down[l].astype(_BF16), bm)
        gate_t = gate.transpose(0, 2, 1)
        x = _combine(x, y, dest, gate_t, mod, 5, norm_final, final=(l == L - 1))
    return x
```

```python
import functools

import jax
import jax.numpy as jnp
from jax import lax
from jax.experimental import pallas as pl
from jax.experimental.pallas import tpu as pltpu

N_MOD = 6
EPS = 1e-6
POOL_WINDOWS = (2, 4, 8, 16)
HEAD_DIM = 128
N_EXPERTS = 16
N_EXPERT_GROUPS = 4
EXPERTS_PER_GROUP = N_EXPERTS // N_EXPERT_GROUPS
TOP_K = 2
LOG2E = 1.4426950408889634
DMA_UNROLL = 8
LANES = 128
VMEM_LIMIT = 56 * 1024 * 1024

_F32 = jnp.float32
_BF16 = jnp.bfloat16


def _params(sem, vmem=VMEM_LIMIT):
    return pltpu.CompilerParams(dimension_semantics=sem, vmem_limit_bytes=vmem)


def _sigmoid(v):
    return 1.0 / (1.0 + jnp.exp(-v))


def _rms_mod(x, g, scale, shift):
    ms = jnp.mean(x * x, axis=-1, keepdims=True)
    return x * lax.rsqrt(ms + EPS) * g * (1.0 + scale) + shift


def _ada_kernel(c_ref, w_ref, b_ref, o_ref):
    c = c_ref[...]
    ca = (c * _sigmoid(c)).astype(_BF16)
    o_ref[...] = jnp.dot(ca, w_ref[...].astype(_BF16), preferred_element_type=_F32) + b_ref[...]


def _ada(c, w_ada, b_ada):
    L, D, M = w_ada.shape
    B = c.shape[0]
    tn = min(1024, M)
    return pl.pallas_call(
        _ada_kernel,
        grid=(L, M // tn),
        in_specs=[
            pl.BlockSpec((B, D), lambda l, n: (0, 0)),
            pl.BlockSpec((None, D, tn), lambda l, n: (l, 0, n)),
            pl.BlockSpec((None, 1, tn), lambda l, n: (l, 0, n)),
        ],
        out_specs=pl.BlockSpec((None, B, tn), lambda l, n: (l, 0, n)),
        out_shape=jax.ShapeDtypeStruct((L, B, M), _F32),
        compiler_params=_params(("arbitrary", "arbitrary")),
        name="ada_mod",
    )(c, w_ada, b_ada.reshape(L, 1, M))


def _normmod_kernel(x_ref, g_ref, sc_ref, sh_ref, h_ref):
    h_ref[...] = _rms_mod(x_ref[...], g_ref[...], sc_ref[...], sh_ref[...]).astype(h_ref.dtype)


def _norm_mod(x, g, mod, i_shift, i_scale):
    B, S, D = x.shape
    ts = min(512, S)
    return pl.pallas_call(
        _normmod_kernel,
        grid=(B, S // ts),
        in_specs=[
            pl.BlockSpec((None, ts, D), lambda b, s: (b, s, 0)),
            pl.BlockSpec((1, D), lambda b, s: (0, 0)),
            pl.BlockSpec((None, None, 1, D), lambda b, s: (i_scale, b, 0, 0)),
            pl.BlockSpec((None, None, 1, D), lambda b, s: (i_shift, b, 0, 0)),
        ],
        out_specs=pl.BlockSpec((None, ts, D), lambda b, s: (b, s, 0)),
        out_shape=jax.ShapeDtypeStruct((B, S, D), _BF16),
        compiler_params=_params(("arbitrary", "arbitrary")),
        name="norm_mod",
    )(x, g.reshape(1, D), mod, mod)


def _first_inner_step():
    return (pl.program_id(1) == 0) & (pl.program_id(2) == 0)


def _proj_kernel(h_ref, w_ref, cs_ref, o_ref, wb_ref):
    @pl.when(_first_inner_step())
    def _():
        wb_ref[...] = (w_ref[...] * cs_ref[...]).astype(_BF16)

    acc = jnp.dot(h_ref[...], wb_ref[...], preferred_element_type=_F32)
    for j in range(o_ref.shape[0]):
        o_ref[j] = acc[:, j * LANES:(j + 1) * LANES].astype(o_ref.dtype)


def _proj(h, w_in, layer, colscale):
    B, S, D = h.shape
    N = colscale.shape[1]
    tn = min(1024, N)
    tm = min(1024, S)
    return pl.pallas_call(
        _proj_kernel,
        grid=(N // tn, B, S // tm),
        in_specs=[
            pl.BlockSpec((None, tm, D), lambda n, b, m: (b, m, 0)),
            pl.BlockSpec((None, D, tn), lambda n, b, m: (layer, 0, n)),
            pl.BlockSpec((1, tn), lambda n, b, m: (0, n)),
        ],
        out_specs=pl.BlockSpec((None, tn // LANES, tm, LANES), lambda n, b, m: (b, n, m, 0)),
        out_shape=jax.ShapeDtypeStruct((B, N // LANES, S, LANES), _BF16),
        scratch_shapes=[pltpu.VMEM((D, tn), _BF16)],
        compiler_params=_params(("arbitrary", "arbitrary", "arbitrary")),
        name="in_proj",
    )(h, w_in, colscale)


def _forget_kernel(h_ref, w_ref, b_ref, o_ref):
    S = h_ref.shape[0]
    fl = jnp.dot(h_ref[...], w_ref[...], preferred_element_type=_F32) + b_ref[...]
    acc = jnp.minimum(fl, 0.0) - jnp.log(1.0 + jnp.exp(-jnp.abs(fl)))
    row = lax.broadcasted_iota(jnp.int32, acc.shape, 0)
    sh = 1
    while sh < S:
        acc = acc + jnp.where(row >= sh, pltpu.roll(acc, sh, axis=0), 0.0)
        sh *= 2
    o_ref[...] = acc


def _forget(h, wf, bf):
    B, S, D = h.shape
    return pl.pallas_call(
        _forget_kernel,
        grid=(B,),
        in_specs=[
            pl.BlockSpec((None, S, D), lambda b: (b, 0, 0)),
            pl.BlockSpec((D, LANES), lambda b: (0, 0)),
            pl.BlockSpec((1, LANES), lambda b: (0, 0)),
        ],
        out_specs=pl.BlockSpec((None, S, LANES), lambda b: (b, 0, 0)),
        out_shape=jax.ShapeDtypeStruct((B, S, LANES), _F32),
        compiler_params=_params(("arbitrary",)),
        name="forget_cumsum",
    )(h, wf, bf)


def _pool_kernel(u_ref, wp_ref, ps_ref, o_ref, *, cpg):
    S = u_ref.shape[1]
    gw = cpg * LANES
    row = lax.broadcasted_iota(jnp.int32, (S, gw), 0)
    for g, w in enumerate(POOL_WINDOWS):
        parts = [u_ref[g * cpg + j] for j in range(cpg)]
        u = (parts[0] if cpg == 1 else jnp.concatenate(parts, axis=-1)).astype(_F32)
        s = u
        sh = 1
        while sh < w:
            s = s + jnp.where(row >= sh, pltpu.roll(s, sh, axis=0), 0.0)
            sh *= 2
        cnt = jnp.minimum(row + 1, w).astype(_F32)
        pooled = s / cnt - u
        mixed = jnp.dot(pooled.astype(_BF16), wp_ref[g].astype(_BF16), preferred_element_type=_F32)
        o_ref[:, g * gw:(g + 1) * gw] = (mixed * ps_ref[:, g * gw:(g + 1) * gw]).astype(o_ref.dtype)


def _pool(proj5, w_pool, layer, ps):
    B, _, S, _ = proj5.shape
    _, G, gw, _ = w_pool.shape
    cpg = gw // LANES
    W = G * gw
    return pl.pallas_call(
        functools.partial(_pool_kernel, cpg=cpg),
        grid=(B,),
        in_specs=[
            pl.BlockSpec((None, G * cpg, S, LANES), lambda b: (b, 0, 0, 0)),
            pl.BlockSpec((None, G, gw, gw), lambda b: (layer, 0, 0, 0)),
            pl.BlockSpec((1, W), lambda b: (0, 0)),
        ],
        out_specs=pl.BlockSpec((None, S, W), lambda b: (b, 0, 0)),
        out_shape=jax.ShapeDtypeStruct((B, S, W), _BF16),
        compiler_params=_params(("arbitrary",)),
        name="pool_mixer",
    )(proj5, w_pool, ps)


def _attn_kernel(q_ref, k_ref, v_ref, f_ref, o_ref, qa_ref, ka_ref, va_ref, *, blk):
    S = q_ref.shape[0]
    h = pl.program_id(1)
    lane = lax.broadcasted_iota(jnp.int32, (S, LANES), 1)
    f = jnp.sum(jnp.where(lane == h, f_ref[...], 0.0), axis=-1, keepdims=True) * LOG2E
    f1 = f.astype(_BF16).astype(_F32)
    r1 = f - f1
    f2 = r1.astype(_BF16).astype(_F32)
    f3 = (r1 - f2).astype(_BF16).astype(_F32)
    qa_ref[:, :HEAD_DIM] = q_ref[...]
    qa_ref[:, HEAD_DIM:] = jnp.where(lane == 0, f1, jnp.where(lane == 1, f2, jnp.where(
        lane == 2, f3, jnp.where(lane < 6, 1.0, 0.0)))).astype(_BF16)
    ka_ref[:, :HEAD_DIM] = k_ref[...]
    ka_ref[:, HEAD_DIM:] = jnp.where(lane < 3, 1.0, jnp.where(lane == 3, -f1, jnp.where(
        lane == 4, -f2, jnp.where(lane == 5, -f3, 0.0)))).astype(_BF16)
    va_ref[:, :HEAD_DIM] = v_ref[...]
    va_ref[:, HEAD_DIM:] = jnp.where(lane == 0, 1.0, 0.0).astype(_BF16)

    nt = (((1,), (1,)), ((), ()))
    r_io = lax.broadcasted_iota(jnp.int32, (blk, blk), 0)
    c_io = lax.broadcasted_iota(jnp.int32, (blk, blk), 1)
    causal = c_io <= r_io
    nb = S // blk
    m = [jnp.full((blk, 1), -jnp.inf, _F32)] * nb
    acc = [jnp.zeros((blk, 2 * HEAD_DIM), _F32)] * nb
    for j in range(nb):
        ka = ka_ref[j * blk:(j + 1) * blk, :]
        va = va_ref[j * blk:(j + 1) * blk, :]
        for i in range(j, nb):
            s = lax.dot_general(qa_ref[i * blk:(i + 1) * blk, :], ka, nt, preferred_element_type=_F32)
            if i == j:
                s = jnp.where(causal, s, -jnp.inf)
            m_new = jnp.maximum(m[i], jnp.max(s, axis=-1, keepdims=True))
            p = jnp.exp2(s - m_new).astype(_BF16)
            acc[i] = jnp.exp2(m[i] - m_new) * acc[i] + jnp.dot(p, va, preferred_element_type=_F32)
            m[i] = m_new
    for i in range(nb):
        o_ref[i * blk:(i + 1) * blk, :] = (
            acc[i][:, :HEAD_DIM] / acc[i][:, HEAD_DIM:HEAD_DIM + 1]).astype(o_ref.dtype)


def _attn(proj5, fcum, nh):
    B, _, S, _ = proj5.shape
    blk = min(512, S)
    chunk = lambda o: pl.BlockSpec((None, None, S, HEAD_DIM), lambda b, h: (b, o + h, 0, 0))
    return pl.pallas_call(
        functools.partial(_attn_kernel, blk=blk),
        grid=(B, nh),
        in_specs=[
            chunk(nh), chunk(2 * nh), chunk(3 * nh),
            pl.BlockSpec((None, S, LANES), lambda b, h: (b, 0, 0)),
        ],
        out_specs=pl.BlockSpec((None, None, S, HEAD_DIM), lambda b, h: (b, h, 0, 0)),
        out_shape=jax.ShapeDtypeStruct((B, nh, S, HEAD_DIM), _BF16),
        scratch_shapes=[pltpu.VMEM((S, 2 * HEAD_DIM), _BF16)] * 3,
        compiler_params=_params(("arbitrary", "arbitrary")),
        name="forget_attn",
    )(proj5, proj5, proj5, fcum)


def _merge_kernel(h_ref, p_ref, a_ref, wg0_ref, wg1_ref, bg0_ref, bg1_ref, wb0_ref, wb1_ref, o_ref,
                  cg0_ref, cg1_ref, cb0_ref, cb1_ref):
    @pl.when(_first_inner_step())
    def _():
        cg0_ref[...] = wg0_ref[...].astype(_BF16)
        cg1_ref[...] = wg1_ref[...].astype(_BF16)
        cb0_ref[...] = wb0_ref[...].astype(_BF16)
        cb1_ref[...] = wb1_ref[...].astype(_BF16)

    h = h_ref[...]
    nh = a_ref.shape[0]
    a = jnp.concatenate([a_ref[j] for j in range(nh)], axis=-1)
    g0 = _sigmoid(jnp.dot(h, cg0_ref[...], preferred_element_type=_F32) + bg0_ref[...])
    y0 = jnp.dot(p_ref[...], cb0_ref[...], preferred_element_type=_F32)
    acc = g0 * y0
    g1 = _sigmoid(jnp.dot(h, cg1_ref[...], preferred_element_type=_F32) + bg1_ref[...])
    y1 = jnp.dot(a, cb1_ref[...], preferred_element_type=_F32)
    o_ref[...] = (acc + g1 * y1).astype(o_ref.dtype)


def _merge(h, pool_out, attn, w_gate, b_gate, w_branch, layer):
    B, S, D = h.shape
    W = pool_out.shape[-1]
    nh = attn.shape[1]
    L = w_gate.shape[0]
    tn = min(512, D)
    tm = min(512, S)
    nn = D // tn
    bg = b_gate.reshape(L, 1, 2 * D)
    return pl.pallas_call(
        _merge_kernel,
        grid=(nn, B, S // tm),
        in_specs=[
            pl.BlockSpec((None, tm, D), lambda n, b, m: (b, m, 0)),
            pl.BlockSpec((None, tm, W), lambda n, b, m: (b, m, 0)),
            pl.BlockSpec((None, nh, tm, HEAD_DIM), lambda n, b, m: (b, 0, m, 0)),
            pl.BlockSpec((None, D, tn), lambda n, b, m: (layer, 0, n)),
            pl.BlockSpec((None, D, tn), lambda n, b, m: (layer, 0, nn + n)),
            pl.BlockSpec((None, 1, tn), lambda n, b, m: (layer, 0, n)),
            pl.BlockSpec((None, 1, tn), lambda n, b, m: (layer, 0, nn + n)),
            pl.BlockSpec((None, None, W, tn), lambda n, b, m: (layer, 0, 0, n)),
            pl.BlockSpec((None, None, W, tn), lambda n, b, m: (layer, 1, 0, n)),
        ],
        out_specs=pl.BlockSpec((None, tm, tn), lambda n, b, m: (b, m, n)),
        out_shape=jax.ShapeDtypeStruct((B, S, D), _BF16),
        scratch_shapes=[pltpu.VMEM((D, tn), _BF16), pltpu.VMEM((D, tn), _BF16),
                        pltpu.VMEM((W, tn), _BF16), pltpu.VMEM((W, tn), _BF16)],
        compiler_params=_params(("arbitrary", "arbitrary", "arbitrary")),
        name="branch_merge",
    )(h, pool_out, attn, w_gate, w_gate, bg, bg, w_branch, w_branch)


def _out_kernel(x_ref, m_ref, w_ref, g_ref, o_ref, wb_ref):
    @pl.when(_first_inner_step())
    def _():
        wb_ref[...] = w_ref[...].astype(_BF16)

    y = jnp.dot(m_ref[...], wb_ref[...], preferred_element_type=_F32)
    o_ref[...] = x_ref[...] + g_ref[...] * y


def _out_proj(x, merged, w_out, layer, mod, i_gate):
    B, S, D = x.shape
    tm = min(512, S)
    tn = min(1024, D)
    return pl.pallas_call(
        _out_kernel,
        grid=(D // tn, B, S // tm),
        in_specs=[
            pl.BlockSpec((None, tm, tn), lambda n, b, m: (b, m, n)),
            pl.BlockSpec((None, tm, D), lambda n, b, m: (b, m, 0)),
            pl.BlockSpec((None, D, tn), lambda n, b, m: (layer, 0, n)),
            pl.BlockSpec((None, None, 1, tn), lambda n, b, m: (i_gate, b, 0, n)),
        ],
        out_specs=pl.BlockSpec((None, tm, tn), lambda n, b, m: (b, m, n)),
        out_shape=jax.ShapeDtypeStruct((B, S, D), _F32),
        scratch_shapes=[pltpu.VMEM((D, tn), _BF16)],
        compiler_params=_params(("arbitrary", "arbitrary", "arbitrary")),
        name="out_proj",
    )(x, merged, w_out, mod)


def _route_kernel(x_ref, g_ref, sc_ref, sh_ref, wr_ref, br_ref,
                  h_ref, e_ref, pos_ref, gate_ref, cnt_ref):
    first_step = (pl.program_id(0) == 0) & (pl.program_id(1) == 0)

    @pl.when(first_step)
    def _():
        cnt_ref[...] = jnp.zeros_like(cnt_ref)

    h = _rms_mod(x_ref[...], g_ref[...], sc_ref[...], sh_ref[...])
    tr, D = h.shape

    h_ref[...] = h

    logits = lax.dot_general(wr_ref[...], h, (((1,), (1,)), ((), ())),
                             precision=lax.Precision.HIGHEST, preferred_element_type=_F32)
    rows = [logits[e:e + 1, :] for e in range(N_EXPERTS)]
    mx = functools.reduce(jnp.maximum, rows)
    ex = [jnp.exp(r - mx) for r in rows]
    den = functools.reduce(lambda a, b: a + b, ex)
    probs = [v / den for v in ex]
    sel = [probs[e] + br_ref[e:e + 1, :] for e in range(N_EXPERTS)]

    in_top = [None] * N_EXPERTS
    gscore = []
    for g in range(N_EXPERT_GROUPS):
        ids = range(g * EXPERTS_PER_GROUP, (g + 1) * EXPERTS_PER_GROUP)
        score = None
        for i in ids:
            rank = None
            for j in ids:
                if j == i:
                    continue
                beats = (sel[j] > sel[i]) | ((sel[j] == sel[i]) if j < i else False)
                beats = beats.astype(_F32)
                rank = beats if rank is None else rank + beats
            in_top[i] = rank < float(TOP_K)
            term = jnp.where(in_top[i], sel[i], 0.0)
            score = term if score is None else score + term
        gscore.append(score)
    chosen = []
    for g in range(N_EXPERT_GROUPS):
        lose = None
        for g2 in range(N_EXPERT_GROUPS):
            if g2 == g:
                continue
            b = (gscore[g2] > gscore[g]) | ((gscore[g2] == gscore[g]) if g2 < g else False)
            lose = b if lose is None else (lose | b)
        chosen.append(jnp.logical_not(lose))
    picked = [in_top[e] & chosen[e // EXPERTS_PER_GROUP] for e in range(N_EXPERTS)]
    pf = [p.astype(_F32) for p in picked]

    onehot = jnp.concatenate(pf, axis=0)
    r_io = lax.broadcasted_iota(jnp.int32, (tr, tr), 0)
    c_io = lax.broadcasted_iota(jnp.int32, (tr, tr), 1)
    upper = (r_io <= c_io).astype(_BF16)
    incl = jnp.dot(onehot.astype(_BF16), upper, preferred_element_type=_F32)
    base = cnt_ref[:, 0:1]
    posm = incl - 1.0 + base
    cnt_ref[...] = jnp.broadcast_to(base + incl[:, tr - 1:tr], cnt_ref.shape)

    gnum = [pf[e] * probs[e] for e in range(N_EXPERTS)]
    gden = functools.reduce(lambda a, b: a + b, gnum)
    zero = jnp.zeros_like(pf[0])
    seen = zero
    e_out = [zero, zero]
    p_out = [zero, zero]
    g_out = [zero, zero]
    for e in range(N_EXPERTS):
        for k in range(TOP_K):
            hit = pf[e] * (seen == float(k)).astype(_F32)
            e_out[k] = e_out[k] + hit * float(e)
            p_out[k] = p_out[k] + hit * posm[e:e + 1, :]
            g_out[k] = g_out[k] + hit * gnum[e]
        seen = seen + pf[e]
    e_ref[...] = jnp.concatenate(e_out, axis=0).astype(jnp.int32)
    pos_ref[...] = jnp.concatenate(p_out, axis=0).astype(jnp.int32)
    gate_ref[...] = jnp.concatenate(g_out, axis=0) / gden


def _route(x, g, mod, i_shift, i_scale, wr_t, br):
    B, S, D = x.shape
    tr = min(512, S)
    nb = S // tr
    E = N_EXPERTS
    tok = lambda dt: jax.ShapeDtypeStruct((B * nb, TOP_K, tr), dt)
    tok_spec = pl.BlockSpec((None, TOP_K, tr), lambda b, s: (b * nb + s, 0, 0))
    return pl.pallas_call(
        _route_kernel,
        grid=(B, nb),
        in_specs=[
            pl.BlockSpec((None, tr, D), lambda b, s: (b, s, 0)),
            pl.BlockSpec((1, D), lambda b, s: (0, 0)),
            pl.BlockSpec((None, None, 1, D), lambda b, s: (i_scale, b, 0, 0)),
            pl.BlockSpec((None, None, 1, D), lambda b, s: (i_shift, b, 0, 0)),
            pl.BlockSpec((E, D), lambda b, s: (0, 0)),
            pl.BlockSpec((E, 1), lambda b, s: (0, 0)),
        ],
        out_specs=[
            pl.BlockSpec((None, tr, D), lambda b, s: (b, s, 0)),
            tok_spec, tok_spec, tok_spec,
            pl.BlockSpec((E, LANES), lambda b, s: (0, 0)),
        ],
        out_shape=[
            jax.ShapeDtypeStruct((B, S, D), _F32),
            tok(jnp.int32), tok(jnp.int32), tok(_F32),
            jax.ShapeDtypeStruct((E, LANES), _F32),
        ],
        compiler_params=_params(("arbitrary", "arbitrary")),
        name="moe_route",
    )(x, g.reshape(1, D), mod, mod, wr_t, br)


def _dispatch_kernel(pad_start_ref, pad_len_ref, nu_ref, dest_ref, h_ref, buf_ref, zblk, sem):
    tr = h_ref.shape[0]

    def issue(r, carry):
        for k in range(TOP_K):
            pltpu.make_async_copy(h_ref.at[pl.ds(r, 1), :],
                                  buf_ref.at[pl.ds(dest_ref[k, r], 1), :], sem).start(priority=k)
        return carry

    lax.fori_loop(0, tr, issue, 0, unroll=DMA_UNROLL)

    def row_wait():
        pltpu.make_async_copy(h_ref.at[pl.ds(0, 1), :], buf_ref.at[pl.ds(0, 1), :], sem).wait()

    def drain(r, carry):
        for k in range(TOP_K):
            row_wait()
        return carry

    lax.fori_loop(0, tr, drain, 0, unroll=DMA_UNROLL)

    @pl.when(pl.program_id(0) == 0)
    def _():
        bm = zblk.shape[0]
        nblk = buf_ref.shape[0] // bm
        zblk[...] = jnp.zeros_like(zblk)
        for e in range(N_EXPERTS):
            def zissue(r, carry, e=e):
                pltpu.make_async_copy(zblk.at[pl.ds(0, 1), :],
                                      buf_ref.at[pl.ds(pad_start_ref[e] + r, 1), :], sem).start()
                return carry
            lax.fori_loop(0, pad_len_ref[e], zissue, 0)
        for e in range(N_EXPERTS):
            def zdrain(r, carry):
                row_wait()
                return carry
            lax.fori_loop(0, pad_len_ref[e], zdrain, 0)

        def blk_copy(i):
            return pltpu.make_async_copy(zblk, buf_ref.at[pl.ds(pl.multiple_of(i * bm, bm), bm), :], sem)

        def bissue(i, carry):
            blk_copy(i).start()
            return carry

        def bdrain(i, carry):
            blk_copy(i).wait()
            return carry

        lax.fori_loop(nu_ref[0], nblk, bissue, 0)
        lax.fori_loop(nu_ref[0], nblk, bdrain, 0)


def _dispatch(h, dest, pad_start, pad_len, n_used, rows, bm):
    T, D = h.shape
    nbt, _, tr = dest.shape
    grid_spec = pltpu.PrefetchScalarGridSpec(
        num_scalar_prefetch=3,
        grid=(nbt,),
        in_specs=[
            pl.BlockSpec((None, TOP_K, tr), lambda i, ps, pn, nu: (i, 0, 0), memory_space=pltpu.SMEM),
            pl.BlockSpec((tr, D), lambda i, ps, pn, nu: (i, 0)),
        ],
        out_specs=pl.BlockSpec(memory_space=pl.ANY),
        scratch_shapes=[pltpu.VMEM((bm, D), _F32), pltpu.SemaphoreType.DMA(())],
    )
    return pl.pallas_call(
        _dispatch_kernel,
        grid_spec=grid_spec,
        out_shape=jax.ShapeDtypeStruct((rows, D), _F32),
        compiler_params=_params(("arbitrary",)),
        name="moe_dispatch",
    )(pad_start, pad_len, n_used, dest, h)


def _expert_kernel(be_ref, nu_ref, x_ref, wg_hbm, wu_hbm, wd_hbm, o_ref,
                   wgb, wub, wdb, stg_a, stg_b, sem, *, layer):
    i = pl.program_id(0)
    e = be_ref[i]
    live = i < nu_ref[0]
    new_expert = (i == 0) | (e != be_ref[jnp.maximum(i - 1, 0)])

    @pl.when(live & new_expert)
    def _():
        D, De = wgb.shape
        ca, cb = stg_a.shape[1], stg_b.shape[1]
        jobs = []
        for w_hbm, dst in ((wg_hbm, wgb), (wu_hbm, wub)):
            for c in range(D // ca):
                jobs.append((w_hbm.at[layer, e, pl.ds(c * ca, ca), :], stg_a, dst, c * ca, ca))
        for c in range(De // cb):
            jobs.append((wd_hbm.at[layer, e, pl.ds(c * cb, cb), :], stg_b, wdb, c * cb, cb))
        copies = [pltpu.make_async_copy(src, stg.at[n % 2], sem.at[n % 2])
                  for n, (src, stg, _, _, _) in enumerate(jobs)]
        copies[0].start()
        for n, (_, stg, dst, off, size) in enumerate(jobs):
            if n + 1 < len(jobs):
                copies[n + 1].start()
            copies[n].wait()
            dst[off:off + size, :] = stg[n % 2].astype(_BF16)

    @pl.when(live)
    def _():
        xb = x_ref[...].astype(_BF16)
        a = jnp.dot(xb, wgb[...], preferred_element_type=_F32)
        u = jnp.dot(xb, wub[...], preferred_element_type=_F32)
        act = (a * _sigmoid(a) * u).astype(_BF16)
        o_ref[...] = jnp.dot(act, wdb[...], preferred_element_type=_F32)

    @pl.when(jnp.logical_not(live))
    def _():
        o_ref[...] = jnp.zeros_like(o_ref)


def _experts(buf, block_expert, n_used, wg, wu, wd, layer, bm):
    rows, D = buf.shape
    De = wg.shape[-1]
    nblk = rows // bm
    ca = min(256, D)
    cb = min(128, De)
    grid_spec = pltpu.PrefetchScalarGridSpec(
        num_scalar_prefetch=2,
        grid=(nblk,),
        in_specs=[
            pl.BlockSpec((bm, D), lambda i, be, nu: (jnp.minimum(i, jnp.maximum(nu[0] - 1, 0)), 0)),
            pl.BlockSpec(memory_space=pl.ANY),
            pl.BlockSpec(memory_space=pl.ANY),
            pl.BlockSpec(memory_space=pl.ANY),
        ],
        out_specs=pl.BlockSpec((bm, D), lambda i, be, nu: (i, 0)),
        scratch_shapes=[
            pltpu.VMEM((D, De), _BF16), pltpu.VMEM((D, De), _BF16), pltpu.VMEM((De, D), _BF16),
            pltpu.VMEM((2, ca, De), _F32), pltpu.VMEM((2, cb, D), _F32),
            pltpu.SemaphoreType.DMA((2,)),
        ],
    )
    return pl.pallas_call(
        functools.partial(_expert_kernel, layer=layer),
        grid_spec=grid_spec,
        out_shape=jax.ShapeDtypeStruct((rows, D), _F32),
        compiler_params=_params(("arbitrary",)),
        name="moe_experts",
    )(block_expert, n_used, buf, wg, wu, wd)


def _combine_kernel(dest_ref, x_ref, y_ref, gt_ref, gm_ref, nf_ref, o_ref, ybuf, sem, *, final):
    tr = x_ref.shape[0]

    def issue(r, carry):
        for k in range(TOP_K):
            pltpu.make_async_copy(y_ref.at[pl.ds(dest_ref[k, r], 1), :],
                                  ybuf.at[k, pl.ds(r, 1), :], sem).start(priority=k)
        return carry

    lax.fori_loop(0, tr, issue, 0, unroll=DMA_UNROLL)

    def drain(r, carry):
        for k in range(TOP_K):
            pltpu.make_async_copy(y_ref.at[pl.ds(0, 1), :], ybuf.at[k, pl.ds(0, 1), :], sem).wait()
        return carry

    lax.fori_loop(0, tr, drain, 0, unroll=DMA_UNROLL)

    gt = gt_ref[...]
    moe = gt[:, 0:1] * ybuf[0] + gt[:, 1:2] * ybuf[1]
    xn = x_ref[...] + gm_ref[...] * moe
    if final:
        ms = jnp.mean(xn * xn, axis=-1, keepdims=True)
        xn = xn * lax.rsqrt(ms + EPS) * nf_ref[...]
    o_ref[...] = xn


def _combine(x, y, dest, gate_t, mod, i_gate, nf, final):
    B, S, D = x.shape
    nbt, _, tr = dest.shape
    nb = S // tr
    return pl.pallas_call(
        functools.partial(_combine_kernel, final=final),
        grid=(B, nb),
        in_specs=[
            pl.BlockSpec((None, TOP_K, tr), lambda b, s: (b * nb + s, 0, 0), memory_space=pltpu.SMEM),
            pl.BlockSpec((None, tr, D), lambda b, s: (b, s, 0)),
            pl.BlockSpec(memory_space=pl.ANY),
            pl.BlockSpec((None, tr, TOP_K), lambda b, s: (b * nb + s, 0, 0)),
            pl.BlockSpec((None, None, 1, D), lambda b, s: (i_gate, b, 0, 0)),
            pl.BlockSpec((1, D), lambda b, s: (0, 0)),
        ],
        out_specs=pl.BlockSpec((None, tr, D), lambda b, s: (b, s, 0)),
        out_shape=jax.ShapeDtypeStruct((B, S, D), _F32),
        scratch_shapes=[pltpu.VMEM((TOP_K, tr, D), _F32), pltpu.SemaphoreType.DMA(())],
        compiler_params=_params(("arbitrary", "arbitrary")),
        name="moe_combine",
    )(dest, x, y, gate_t, mod, nf.reshape(1, D))


def kernel(x, c, w_ada, b_ada, norm_mix, norm_moe, w_in, b_forget, w_pool, pool_scale, w_branch,
           w_gate, b_gate, w_out, w_router, b_router, w_exp_gate, w_exp_up, w_exp_down, norm_final):
    B, S, D = x.shape
    L = w_ada.shape[0]
    W = D // 2
    nh = W // HEAD_DIM
    T = B * S
    A = T * TOP_K
    bm = min(256, A // N_EXPERTS)
    rows = (A // bm + N_EXPERTS) * bm

    mod_all = _ada(c, w_ada, b_ada)
    mod_all = mod_all.reshape(L, B, N_MOD, D).transpose(0, 2, 1, 3)[:, :, :, None, :]
    wr_t = w_router.T
    br = b_router.reshape(N_EXPERTS, 1)
    colscale = jnp.concatenate([jnp.ones((W,), _F32), jnp.full((W,), HEAD_DIM ** -0.5 * LOG2E, _F32),
                                jnp.ones((2 * W,), _F32)]).reshape(1, 4 * W)

    for l in range(L):
        mod = mod_all[l]
        h = _norm_mod(x, norm_mix[l], mod, 0, 1)
        wf = jnp.zeros((D, LANES), _F32).at[:, :nh].set(w_in[l][:, 4 * W:]).astype(_BF16)
        bf = jnp.zeros((1, LANES), _F32).at[0, :nh].set(b_forget[l])
        proj5 = _proj(h, w_in, l, colscale)
        fcum = _forget(h, wf, bf)
        pool_out = _pool(proj5, w_pool, l, pool_scale[l].reshape(1, W))
        attn = _attn(proj5, fcum, nh)
        merged = _merge(h, pool_out, attn, w_gate, b_gate, w_branch, l)
        x = _out_proj(x, merged, w_out, l, mod, 2)

        h_moe, eidx, pos, gate, counts = _route(x, norm_moe[l], mod, 3, 4, wr_t, br)
        counts = counts[:, 0].astype(jnp.int32)
        pcounts = (counts + bm - 1) // bm * bm
        pends = jnp.cumsum(pcounts)
        pstarts = pends - pcounts
        dest = pos
        for e in range(N_EXPERTS):
            dest = dest + jnp.where(eidx == e, pstarts[e], 0)
        n_used = (pends[-1] // bm).astype(jnp.int32)
        blk_ids = jnp.arange(rows // bm, dtype=jnp.int32)
        blk_ids = jnp.minimum(blk_ids, n_used - 1)
        block_expert = jnp.sum((pends[None, :] <= (blk_ids * bm)[:, None]).astype(jnp.int32), axis=1)
        block_expert = jnp.minimum(block_expert, N_EXPERTS - 1)
        n_used = n_used.reshape(1)
        buf = _dispatch(h_moe.reshape(T, D), dest, pstarts + counts, pcounts - counts, n_used, rows, bm)
        y = _experts(buf, block_expert, n_used, w_exp_gate, w_exp_up, w_exp_down, l, bm)
        gate_t = gate.transpose(0, 2, 1)
        x = _combine(x, y, dest, gate_t, mod, 5, norm_final, final=(l == L - 1))
    return x
```

```python
import functools

import jax
import jax.numpy as jnp
from jax import lax
from jax.experimental import pallas as pl
from jax.experimental.pallas import tpu as pltpu

N_MOD = 6
EPS = 1e-6
POOL_WINDOWS = (2, 4, 8, 16)
HEAD_DIM = 128
N_EXPERTS = 16
N_EXPERT_GROUPS = 4
EXPERTS_PER_GROUP = N_EXPERTS // N_EXPERT_GROUPS
TOP_K = 2
LOG2E = 1.4426950408889634
DMA_UNROLL = 8
LANES = 128
VMEM_LIMIT = 56 * 1024 * 1024

_F32 = jnp.float32
_BF16 = jnp.bfloat16


def _params(sem, vmem=VMEM_LIMIT):
    return pltpu.CompilerParams(dimension_semantics=sem, vmem_limit_bytes=vmem)


def _sigmoid(v):
    return 1.0 / (1.0 + jnp.exp(-v))


def _rms_mod(x, g, scale, shift):
    ms = jnp.mean(x * x, axis=-1, keepdims=True)
    return x * lax.rsqrt(ms + EPS) * g * (1.0 + scale) + shift


def _ada_kernel(c_ref, w_ref, b_ref, o_ref):
    c = c_ref[...]
    ca = (c * _sigmoid(c)).astype(_BF16)
    o_ref[...] = jnp.dot(ca, w_ref[...].astype(_BF16), preferred_element_type=_F32) + b_ref[...]


def _ada(c, w_ada, b_ada):
    L, D, M = w_ada.shape
    B = c.shape[0]
    tn = min(1024, M)
    return pl.pallas_call(
        _ada_kernel,
        grid=(L, M // tn),
        in_specs=[
            pl.BlockSpec((B, D), lambda l, n: (0, 0)),
            pl.BlockSpec((None, D, tn), lambda l, n: (l, 0, n)),
            pl.BlockSpec((None, 1, tn), lambda l, n: (l, 0, n)),
        ],
        out_specs=pl.BlockSpec((None, B, tn), lambda l, n: (l, 0, n)),
        out_shape=jax.ShapeDtypeStruct((L, B, M), _F32),
        compiler_params=_params(("arbitrary", "arbitrary")),
        name="ada_mod",
    )(c, w_ada, b_ada.reshape(L, 1, M))


def _normmod_kernel(x_ref, g_ref, sc_ref, sh_ref, h_ref):
    h_ref[...] = _rms_mod(x_ref[...], g_ref[...], sc_ref[...], sh_ref[...]).astype(h_ref.dtype)


def _norm_mod(x, g, mod, i_shift, i_scale):
    B, S, D = x.shape
    ts = min(512, S)
    return pl.pallas_call(
        _normmod_kernel,
        grid=(B, S // ts),
        in_specs=[
            pl.BlockSpec((None, ts, D), lambda b, s: (b, s, 0)),
            pl.BlockSpec((1, D), lambda b, s: (0, 0)),
            pl.BlockSpec((None, None, 1, D), lambda b, s: (i_scale, b, 0, 0)),
            pl.BlockSpec((None, None, 1, D), lambda b, s: (i_shift, b, 0, 0)),
        ],
        out_specs=pl.BlockSpec((None, ts, D), lambda b, s: (b, s, 0)),
        out_shape=jax.ShapeDtypeStruct((B, S, D), _BF16),
        compiler_params=_params(("arbitrary", "arbitrary")),
        name="norm_mod",
    )(x, g.reshape(1, D), mod, mod)


def _first_inner_step():
    return (pl.program_id(1) == 0) & (pl.program_id(2) == 0)


def _proj_kernel(h_ref, w_ref, cs_ref, o_ref, wb_ref):
    @pl.when(_first_inner_step())
    def _():
        wb_ref[...] = (w_ref[...] * cs_ref[...]).astype(_BF16)

    acc = jnp.dot(h_ref[...], wb_ref[...], preferred_element_type=_F32)
    for j in range(o_ref.shape[0]):
        o_ref[j] = acc[:, j * LANES:(j + 1) * LANES].astype(o_ref.dtype)


def _proj(h, w_in, layer, colscale):
    B, S, D = h.shape
    N = colscale.shape[1]
    tn = min(1024, N)
    tm = min(1024, S)
    return pl.pallas_call(
        _proj_kernel,
        grid=(N // tn, B, S // tm),
        in_specs=[
            pl.BlockSpec((None, tm, D), lambda n, b, m: (b, m, 0)),
            pl.BlockSpec((None, D, tn), lambda n, b, m: (layer, 0, n)),
            pl.BlockSpec((1, tn), lambda n, b, m: (0, n)),
        ],
        out_specs=pl.BlockSpec((None, tn // LANES, tm, LANES), lambda n, b, m: (b, n, m, 0)),
        out_shape=jax.ShapeDtypeStruct((B, N // LANES, S, LANES), _BF16),
        scratch_shapes=[pltpu.VMEM((D, tn), _BF16)],
        compiler_params=_params(("arbitrary", "arbitrary", "arbitrary")),
        name="in_proj",
    )(h, w_in, colscale)


def _forget_kernel(h_ref, w_ref, b_ref, o_ref):
    S = h_ref.shape[0]
    fl = jnp.dot(h_ref[...], w_ref[...], preferred_element_type=_F32) + b_ref[...]
    acc = jnp.minimum(fl, 0.0) - jnp.log(1.0 + jnp.exp(-jnp.abs(fl)))
    row = lax.broadcasted_iota(jnp.int32, acc.shape, 0)
    sh = 1
    while sh < S:
        acc = acc + jnp.where(row >= sh, pltpu.roll(acc, sh, axis=0), 0.0)
        sh *= 2
    o_ref[...] = acc


def _forget(h, wf, bf):
    B, S, D = h.shape
    return pl.pallas_call(
        _forget_kernel,
        grid=(B,),
        in_specs=[
            pl.BlockSpec((None, S, D), lambda b: (b, 0, 0)),
            pl.BlockSpec((D, LANES), lambda b: (0, 0)),
            pl.BlockSpec((1, LANES), lambda b: (0, 0)),
        ],
        out_specs=pl.BlockSpec((None, S, LANES), lambda b: (b, 0, 0)),
        out_shape=jax.ShapeDtypeStruct((B, S, LANES), _F32),
        compiler_params=_params(("arbitrary",)),
        name="forget_cumsum",
    )(h, wf, bf)


def _pool_kernel(u_ref, wp_ref, ps_ref, o_ref, *, cpg):
    S = u_ref.shape[1]
    gw = cpg * LANES
    row = lax.broadcasted_iota(jnp.int32, (S, gw), 0)
    for g, w in enumerate(POOL_WINDOWS):
        parts = [u_ref[g * cpg + j] for j in range(cpg)]
        u = (parts[0] if cpg == 1 else jnp.concatenate(parts, axis=-1)).astype(_F32)
        s = u
        sh = 1
        while sh < w:
            s = s + jnp.where(row >= sh, pltpu.roll(s, sh, axis=0), 0.0)
            sh *= 2
        cnt = jnp.minimum(row + 1, w).astype(_F32)
        pooled = s / cnt - u
        mixed = jnp.dot(pooled.astype(_BF16), wp_ref[g].astype(_BF16), preferred_element_type=_F32)
        o_ref[:, g * gw:(g + 1) * gw] = (mixed * ps_ref[:, g * gw:(g + 1) * gw]).astype(o_ref.dtype)


def _pool(proj5, w_pool, layer, ps):
    B, _, S, _ = proj5.shape
    _, G, gw, _ = w_pool.shape
    cpg = gw // LANES
    W = G * gw
    return pl.pallas_call(
        functools.partial(_pool_kernel, cpg=cpg),
        grid=(B,),
        in_specs=[
            pl.BlockSpec((None, G * cpg, S, LANES), lambda b: (b, 0, 0, 0)),
            pl.BlockSpec((None, G, gw, gw), lambda b: (layer, 0, 0, 0)),
            pl.BlockSpec((1, W), lambda b: (0, 0)),
        ],
        out_specs=pl.BlockSpec((None, S, W), lambda b: (b, 0, 0)),
        out_shape=jax.ShapeDtypeStruct((B, S, W), _BF16),
        compiler_params=_params(("arbitrary",)),
        name="pool_mixer",
    )(proj5, w_pool, ps)


def _attn_kernel(q_ref, k_ref, v_ref, f_ref, o_ref, qa_ref, ka_ref, va_ref, *, blk):
    S = q_ref.shape[0]
    h = pl.program_id(1)
    lane = lax.broadcasted_iota(jnp.int32, (S, LANES), 1)
    f = jnp.sum(jnp.where(lane == h, f_ref[...], 0.0), axis=-1, keepdims=True) * LOG2E
    f1 = f.astype(_BF16).astype(_F32)
    r1 = f - f1
    f2 = r1.astype(_BF16).astype(_F32)
    f3 = (r1 - f2).astype(_BF16).astype(_F32)
    qa_ref[:, :HEAD_DIM] = q_ref[...]
    qa_ref[:, HEAD_DIM:] = jnp.where(lane == 0, f1, jnp.where(lane == 1, f2, jnp.where(
        lane == 2, f3, jnp.where(lane < 6, 1.0, 0.0)))).astype(_BF16)
    ka_ref[:, :HEAD_DIM] = k_ref[...]
    ka_ref[:, HEAD_DIM:] = jnp.where(lane < 3, 1.0, jnp.where(lane == 3, -f1, jnp.where(
        lane == 4, -f2, jnp.where(lane == 5, -f3, 0.0)))).astype(_BF16)
    va_ref[:, :HEAD_DIM] = v_ref[...]
    va_ref[:, HEAD_DIM:] = jnp.where(lane == 0, 1.0, 0.0).astype(_BF16)

    nt = (((1,), (1,)), ((), ()))
    r_io = lax.broadcasted_iota(jnp.int32, (blk, blk), 0)
    c_io = lax.broadcasted_iota(jnp.int32, (blk, blk), 1)
    causal = c_io <= r_io
    nb = S // blk
    m = [jnp.full((blk, 1), -jnp.inf, _F32)] * nb
    acc = [jnp.zeros((blk, 2 * HEAD_DIM), _F32)] * nb
    for j in range(nb):
        ka = ka_ref[j * blk:(j + 1) * blk, :]
        va = va_ref[j * blk:(j + 1) * blk, :]
        for i in range(j, nb):
            s = lax.dot_general(qa_ref[i * blk:(i + 1) * blk, :], ka, nt, preferred_element_type=_F32)
            if i == j:
                s = jnp.where(causal, s, -jnp.inf)
            m_new = jnp.maximum(m[i], jnp.max(s, axis=-1, keepdims=True))
            p = jnp.exp2(s - m_new).astype(_BF16)
            acc[i] = jnp.exp2(m[i] - m_new) * acc[i] + jnp.dot(p, va, preferred_element_type=_F32)
            m[i] = m_new
    for i in range(nb):
        o_ref[i * blk:(i + 1) * blk, :] = (
            acc[i][:, :HEAD_DIM] / acc[i][:, HEAD_DIM:HEAD_DIM + 1]).astype(o_ref.dtype)


def _attn(proj5, fcum, nh):
    B, _, S, _ = proj5.shape
    blk = min(512, S)
    chunk = lambda o: pl.BlockSpec((None, None, S, HEAD_DIM), lambda b, h: (b, o + h, 0, 0))
    return pl.pallas_call(
        functools.partial(_attn_kernel, blk=blk),
        grid=(B, nh),
        in_specs=[
            chunk(nh), chunk(2 * nh), chunk(3 * nh),
            pl.BlockSpec((None, S, LANES), lambda b, h: (b, 0, 0)),
        ],
        out_specs=pl.BlockSpec((None, None, S, HEAD_DIM), lambda b, h: (b, h, 0, 0)),
        out_shape=jax.ShapeDtypeStruct((B, nh, S, HEAD_DIM), _BF16),
        scratch_shapes=[pltpu.VMEM((S, 2 * HEAD_DIM), _BF16)] * 3,
        compiler_params=_params(("arbitrary", "arbitrary")),
        name="forget_attn",
    )(proj5, proj5, proj5, fcum)


def _merge_kernel(h_ref, p_ref, a_ref, wg0_ref, wg1_ref, bg0_ref, bg1_ref, wb0_ref, wb1_ref, o_ref,
                  cg0_ref, cg1_ref, cb0_ref, cb1_ref):
    @pl.when(_first_inner_step())
    def _():
        cg0_ref[...] = wg0_ref[...].astype(_BF16)
        cg1_ref[...] = wg1_ref[...].astype(_BF16)
        cb0_ref[...] = wb0_ref[...].astype(_BF16)
        cb1_ref[...] = wb1_ref[...].astype(_BF16)

    h = h_ref[...]
    nh = a_ref.shape[0]
    a = jnp.concatenate([a_ref[j] for j in range(nh)], axis=-1)
    g0 = _sigmoid(jnp.dot(h, cg0_ref[...], preferred_element_type=_F32) + bg0_ref[...])
    y0 = jnp.dot(p_ref[...], cb0_ref[...], preferred_element_type=_F32)
    acc = g0 * y0
    g1 = _sigmoid(jnp.dot(h, cg1_ref[...], preferred_element_type=_F32) + bg1_ref[...])
    y1 = jnp.dot(a, cb1_ref[...], preferred_element_type=_F32)
    o_ref[...] = (acc + g1 * y1).astype(o_ref.dtype)


def _merge(h, pool_out, attn, w_gate, b_gate, w_branch, layer):
    B, S, D = h.shape
    W = pool_out.shape[-1]
    nh = attn.shape[1]
    L = w_gate.shape[0]
    tn = min(512, D)
    tm = min(512, S)
    nn = D // tn
    bg = b_gate.reshape(L, 1, 2 * D)
    return pl.pallas_call(
        _merge_kernel,
        grid=(nn, B, S // tm),
        in_specs=[
            pl.BlockSpec((None, tm, D), lambda n, b, m: (b, m, 0)),
            pl.BlockSpec((None, tm, W), lambda n, b, m: (b, m, 0)),
            pl.BlockSpec((None, nh, tm, HEAD_DIM), lambda n, b, m: (b, 0, m, 0)),
            pl.BlockSpec((None, D, tn), lambda n, b, m: (layer, 0, n)),
            pl.BlockSpec((None, D, tn), lambda n, b, m: (layer, 0, nn + n)),
            pl.BlockSpec((None, 1, tn), lambda n, b, m: (layer, 0, n)),
            pl.BlockSpec((None, 1, tn), lambda n, b, m: (layer, 0, nn + n)),
            pl.BlockSpec((None, None, W, tn), lambda n, b, m: (layer, 0, 0, n)),
            pl.BlockSpec((None, None, W, tn), lambda n, b, m: (layer, 1, 0, n)),
        ],
        out_specs=pl.BlockSpec((None, tm, tn), lambda n, b, m: (b, m, n)),
        out_shape=jax.ShapeDtypeStruct((B, S, D), _BF16),
        scratch_shapes=[pltpu.VMEM((D, tn), _BF16), pltpu.VMEM((D, tn), _BF16),
                        pltpu.VMEM((W, tn), _BF16), pltpu.VMEM((W, tn), _BF16)],
        compiler_params=_params(("arbitrary", "arbitrary", "arbitrary")),
        name="branch_merge",
    )(h, pool_out, attn, w_gate, w_gate, bg, bg, w_branch, w_branch)


def _out_kernel(x_ref, m_ref, w_ref, g_ref, o_ref, wb_ref):
    @pl.when(_first_inner_step())
    def _():
        wb_ref[...] = w_ref[...].astype(_BF16)

    y = jnp.dot(m_ref[...], wb_ref[...], preferred_element_type=_F32)
    o_ref[...] = x_ref[...] + g_ref[...] * y


def _out_proj(x, merged, w_out, layer, mod, i_gate):
    B, S, D = x.shape
    tm = min(512, S)
    tn = min(1024, D)
    return pl.pallas_call(
        _out_kernel,
        grid=(D // tn, B, S // tm),
        in_specs=[
            pl.BlockSpec((None, tm, tn), lambda n, b, m: (b, m, n)),
            pl.BlockSpec((None, tm, D), lambda n, b, m: (b, m, 0)),
            pl.BlockSpec((None, D, tn), lambda n, b, m: (layer, 0, n)),
            pl.BlockSpec((None, None, 1, tn), lambda n, b, m: (i_gate, b, 0, n)),
        ],
        out_specs=pl.BlockSpec((None, tm, tn), lambda n, b, m: (b, m, n)),
        out_shape=jax.ShapeDtypeStruct((B, S, D), _F32),
        scratch_shapes=[pltpu.VMEM((D, tn), _BF16)],
        compiler_params=_params(("arbitrary", "arbitrary", "arbitrary")),
        name="out_proj",
    )(x, merged, w_out, mod)


def _route_kernel(x_ref, g_ref, sc_ref, sh_ref, wr_ref, br_ref,
                  h_ref, e_ref, pos_ref, gate_ref, cnt_ref):
    first_step = (pl.program_id(0) == 0) & (pl.program_id(1) == 0)

    @pl.when(first_step)
    def _():
        cnt_ref[...] = jnp.zeros_like(cnt_ref)

    h = _rms_mod(x_ref[...], g_ref[...], sc_ref[...], sh_ref[...])
    tr, D = h.shape

    h_ref[...] = h

    logits = lax.dot_general(wr_ref[...], h, (((1,), (1,)), ((), ())),
                             precision=lax.Precision.HIGHEST, preferred_element_type=_F32)
    rows = [logits[e:e + 1, :] for e in range(N_EXPERTS)]
    mx = functools.reduce(jnp.maximum, rows)
    ex = [jnp.exp(r - mx) for r in rows]
    den = functools.reduce(lambda a, b: a + b, ex)
    probs = [v / den for v in ex]
    sel = [probs[e] + br_ref[e:e + 1, :] for e in range(N_EXPERTS)]

    in_top = [None] * N_EXPERTS
    gscore = []
    for g in range(N_EXPERT_GROUPS):
        ids = range(g * EXPERTS_PER_GROUP, (g + 1) * EXPERTS_PER_GROUP)
        score = None
        for i in ids:
            rank = None
            for j in ids:
                if j == i:
                    continue
                beats = (sel[j] > sel[i]) | ((sel[j] == sel[i]) if j < i else False)
                beats = beats.astype(_F32)
                rank = beats if rank is None else rank + beats
            in_top[i] = rank < float(TOP_K)
            term = jnp.where(in_top[i], sel[i], 0.0)
            score = term if score is None else score + term
        gscore.append(score)
    chosen = []
    for g in range(N_EXPERT_GROUPS):
        lose = None
        for g2 in range(N_EXPERT_GROUPS):
            if g2 == g:
                continue
            b = (gscore[g2] > gscore[g]) | ((gscore[g2] == gscore[g]) if g2 < g else False)
            lose = b if lose is None else (lose | b)
        chosen.append(jnp.logical_not(lose))
    picked = [in_top[e] & chosen[e // EXPERTS_PER_GROUP] for e in range(N_EXPERTS)]
    pf = [p.astype(_F32) for p in picked]

    onehot = jnp.concatenate(pf, axis=0)
    r_io = lax.broadcasted_iota(jnp.int32, (tr, tr), 0)
    c_io = lax.broadcasted_iota(jnp.int32, (tr, tr), 1)
    upper = (r_io <= c_io).astype(_BF16)
    incl = jnp.dot(onehot.astype(_BF16), upper, preferred_element_type=_F32)
    base = cnt_ref[:, 0:1]
    posm = incl - 1.0 + base
    cnt_ref[...] = jnp.broadcast_to(base + incl[:, tr - 1:tr], cnt_ref.shape)

    gnum = [pf[e] * probs[e] for e in range(N_EXPERTS)]
    gden = functools.reduce(lambda a, b: a + b, gnum)
    zero = jnp.zeros_like(pf[0])
    seen = zero
    e_out = [zero, zero]
    p_out = [zero, zero]
    g_out = [zero, zero]
    for e in range(N_EXPERTS):
        for k in range(TOP_K):
            hit = pf[e] * (seen == float(k)).astype(_F32)
            e_out[k] = e_out[k] + hit * float(e)
            p_out[k] = p_out[k] + hit * posm[e:e + 1, :]
            g_out[k] = g_out[k] + hit * gnum[e]
        seen = seen + pf[e]
    e_ref[...] = jnp.concatenate(e_out, axis=0).astype(jnp.int32)
    pos_ref[...] = jnp.concatenate(p_out, axis=0).astype(jnp.int32)
    gate_ref[...] = jnp.concatenate(g_out, axis=0) / gden


def _route(x, g, mod, i_shift, i_scale, wr_t, br):
    B, S, D = x.shape
    tr = min(512, S)
    nb = S // tr
    E = N_EXPERTS
    tok = lambda dt: jax.ShapeDtypeStruct((B * nb, TOP_K, tr), dt)
    tok_spec = pl.BlockSpec((None, TOP_K, tr), lambda b, s: (b * nb + s, 0, 0))
    return pl.pallas_call(
        _route_kernel,
        grid=(B, nb),
        in_specs=[
            pl.BlockSpec((None, tr, D), lambda b, s: (b, s, 0)),
            pl.BlockSpec((1, D), lambda b, s: (0, 0)),
            pl.BlockSpec((None, None, 1, D), lambda b, s: (i_scale, b, 0, 0)),
            pl.BlockSpec((None, None, 1, D), lambda b, s: (i_shift, b, 0, 0)),
            pl.BlockSpec((E, D), lambda b, s: (0, 0)),
            pl.BlockSpec((E, 1), lambda b, s: (0, 0)),
        ],
        out_specs=[
            pl.BlockSpec((None, tr, D), lambda b, s: (b, s, 0)),
            tok_spec, tok_spec, tok_spec,
            pl.BlockSpec((E, LANES), lambda b, s: (0, 0)),
        ],
        out_shape=[
            jax.ShapeDtypeStruct((B, S, D), _F32),
            tok(jnp.int32), tok(jnp.int32), tok(_F32),
            jax.ShapeDtypeStruct((E, LANES), _F32),
        ],
        compiler_params=_params(("arbitrary", "arbitrary")),
        name="moe_route",
    )(x, g.reshape(1, D), mod, mod, wr_t, br)


def _dispatch_kernel(pad_start_ref, pad_len_ref, nu_ref, dest_ref, h_ref, buf_ref, zblk, sem):
    tr = h_ref.shape[0]

    for r in range(tr):
        for k in range(TOP_K):
            pltpu.make_async_copy(h_ref.at[pl.ds(r, 1), :],
                                  buf_ref.at[pl.ds(dest_ref[k, r], 1), :], sem).start(priority=k)

    def row_wait():
        pltpu.make_async_copy(h_ref.at[pl.ds(0, 1), :], buf_ref.at[pl.ds(0, 1), :], sem).wait()

    for _ in range(tr * TOP_K):
        row_wait()

    @pl.when(pl.program_id(0) == 0)
    def _():
        bm = zblk.shape[0]
        nblk = buf_ref.shape[0] // bm
        zblk[...] = jnp.zeros_like(zblk)
        for e in range(N_EXPERTS):
            def zissue(r, carry, e=e):
                pltpu.make_async_copy(zblk.at[pl.ds(0, 1), :],
                                      buf_ref.at[pl.ds(pad_start_ref[e] + r, 1), :], sem).start()
                return carry
            lax.fori_loop(0, pad_len_ref[e], zissue, 0)
        for e in range(N_EXPERTS):
            def zdrain(r, carry):
                row_wait()
                return carry
            lax.fori_loop(0, pad_len_ref[e], zdrain, 0)

        def blk_copy(i):
            return pltpu.make_async_copy(zblk, buf_ref.at[pl.ds(pl.multiple_of(i * bm, bm), bm), :], sem)

        def bissue(i, carry):
            blk_copy(i).start()
            return carry

        def bdrain(i, carry):
            blk_copy(i).wait()
            return carry

        lax.fori_loop(nu_ref[0], nblk, bissue, 0)
        lax.fori_loop(nu_ref[0], nblk, bdrain, 0)


def _dispatch(h, dest, pad_start, pad_len, n_used, rows, bm):
    T, D = h.shape
    nbt, _, tr = dest.shape
    grid_spec = pltpu.PrefetchScalarGridSpec(
        num_scalar_prefetch=3,
        grid=(nbt,),
        in_specs=[
            pl.BlockSpec((None, TOP_K, tr), lambda i, ps, pn, nu: (i, 0, 0), memory_space=pltpu.SMEM),
            pl.BlockSpec((tr, D), lambda i, ps, pn, nu: (i, 0)),
        ],
        out_specs=pl.BlockSpec(memory_space=pl.ANY),
        scratch_shapes=[pltpu.VMEM((bm, D), _F32), pltpu.SemaphoreType.DMA(())],
    )
    return pl.pallas_call(
        _dispatch_kernel,
        grid_spec=grid_spec,
        out_shape=jax.ShapeDtypeStruct((rows, D), _F32),
        compiler_params=_params(("arbitrary",)),
        name="moe_dispatch",
    )(pad_start, pad_len, n_used, dest, h)


def _expert_kernel(be_ref, nx_ref, nu_ref, x_ref, wg_hbm, wu_hbm, wd_hbm, o_ref,
                   wgb, wub, wdb, sg, su, sd, sem, *, layer):
    i = pl.program_id(0)
    e = be_ref[i]
    live = i < nu_ref[0]
    new_expert = (i == 0) | (e != be_ref[jnp.maximum(i - 1, 0)])

    def weight_copies(ex):
        return [pltpu.make_async_copy(w.at[layer, ex], s, sem.at[n])
                for n, (w, s) in enumerate(((wg_hbm, sg), (wu_hbm, su), (wd_hbm, sd)))]

    @pl.when(live & new_expert)
    def _():
        @pl.when(i == 0)
        def _():
            for cp in weight_copies(e):
                cp.start()

        for cp in weight_copies(e):
            cp.wait()
        wgb[...] = sg[...].astype(_BF16)
        wub[...] = su[...].astype(_BF16)
        wdb[...] = sd[...].astype(_BF16)

        @pl.when(nx_ref[i] >= 0)
        def _():
            for cp in weight_copies(nx_ref[i]):
                cp.start()

    @pl.when(live)
    def _():
        xb = x_ref[...].astype(_BF16)
        a = jnp.dot(xb, wgb[...], preferred_element_type=_F32)
        u = jnp.dot(xb, wub[...], preferred_element_type=_F32)
        act = (a * _sigmoid(a) * u).astype(_BF16)
        o_ref[...] = jnp.dot(act, wdb[...], preferred_element_type=_F32)

    @pl.when(jnp.logical_not(live))
    def _():
        o_ref[...] = jnp.zeros_like(o_ref)


def _experts(buf, block_expert, next_expert, n_used, wg, wu, wd, layer, bm):
    rows, D = buf.shape
    De = wg.shape[-1]
    nblk = rows // bm
    grid_spec = pltpu.PrefetchScalarGridSpec(
        num_scalar_prefetch=3,
        grid=(nblk,),
        in_specs=[
            pl.BlockSpec((bm, D), lambda i, be, nx, nu: (jnp.minimum(i, jnp.maximum(nu[0] - 1, 0)), 0)),
            pl.BlockSpec(memory_space=pl.ANY),
            pl.BlockSpec(memory_space=pl.ANY),
            pl.BlockSpec(memory_space=pl.ANY),
        ],
        out_specs=pl.BlockSpec((bm, D), lambda i, be, nx, nu: (i, 0)),
        scratch_shapes=[
            pltpu.VMEM((D, De), _BF16), pltpu.VMEM((D, De), _BF16), pltpu.VMEM((De, D), _BF16),
            pltpu.VMEM((D, De), _F32), pltpu.VMEM((D, De), _F32), pltpu.VMEM((De, D), _F32),
            pltpu.SemaphoreType.DMA((3,)),
        ],
    )
    return pl.pallas_call(
        functools.partial(_expert_kernel, layer=layer),
        grid_spec=grid_spec,
        out_shape=jax.ShapeDtypeStruct((rows, D), _F32),
        compiler_params=_params(("arbitrary",), vmem=60 * 1024 * 1024),
        name="moe_experts",
    )(block_expert, next_expert, n_used, buf, wg, wu, wd)


def _combine_kernel(dest_ref, x_ref, y_ref, gt_ref, gm_ref, nf_ref, o_ref, ybuf, sem, *, final):
    tr = x_ref.shape[0]

    for r in range(tr):
        for k in range(TOP_K):
            pltpu.make_async_copy(y_ref.at[pl.ds(dest_ref[k, r], 1), :],
                                  ybuf.at[k, pl.ds(r, 1), :], sem).start(priority=k)
    for _ in range(tr * TOP_K):
        pltpu.make_async_copy(y_ref.at[pl.ds(0, 1), :], ybuf.at[0, pl.ds(0, 1), :], sem).wait()

    gt = gt_ref[...]
    moe = gt[:, 0:1] * ybuf[0] + gt[:, 1:2] * ybuf[1]
    xn = x_ref[...] + gm_ref[...] * moe
    if final:
        ms = jnp.mean(xn * xn, axis=-1, keepdims=True)
        xn = xn * lax.rsqrt(ms + EPS) * nf_ref[...]
    o_ref[...] = xn


def _combine(x, y, dest, gate_t, mod, i_gate, nf, final):
    B, S, D = x.shape
    nbt, _, tr = dest.shape
    nb = S // tr
    return pl.pallas_call(
        functools.partial(_combine_kernel, final=final),
        grid=(B, nb),
        in_specs=[
            pl.BlockSpec((None, TOP_K, tr), lambda b, s: (b * nb + s, 0, 0), memory_space=pltpu.SMEM),
            pl.BlockSpec((None, tr, D), lambda b, s: (b, s, 0)),
            pl.BlockSpec(memory_space=pl.ANY),
            pl.BlockSpec((None, tr, TOP_K), lambda b, s: (b * nb + s, 0, 0)),
            pl.BlockSpec((None, None, 1, D), lambda b, s: (i_gate, b, 0, 0)),
            pl.BlockSpec((1, D), lambda b, s: (0, 0)),
        ],
        out_specs=pl.BlockSpec((None, tr, D), lambda b, s: (b, s, 0)),
        out_shape=jax.ShapeDtypeStruct((B, S, D), _F32),
        scratch_shapes=[pltpu.VMEM((TOP_K, tr, D), _F32), pltpu.SemaphoreType.DMA(())],
        compiler_params=_params(("arbitrary", "arbitrary")),
        name="moe_combine",
    )(dest, x, y, gate_t, mod, nf.reshape(1, D))


def kernel(x, c, w_ada, b_ada, norm_mix, norm_moe, w_in, b_forget, w_pool, pool_scale, w_branch,
           w_gate, b_gate, w_out, w_router, b_router, w_exp_gate, w_exp_up, w_exp_down, norm_final):
    B, S, D = x.shape
    L = w_ada.shape[0]
    W = D // 2
    nh = W // HEAD_DIM
    T = B * S
    A = T * TOP_K
    bm = min(256, A // N_EXPERTS)
    rows = (A // bm + N_EXPERTS) * bm

    mod_all = _ada(c, w_ada, b_ada)
    mod_all = mod_all.reshape(L, B, N_MOD, D).transpose(0, 2, 1, 3)[:, :, :, None, :]
    wr_t = w_router.T
    br = b_router.reshape(N_EXPERTS, 1)
    colscale = jnp.concatenate([jnp.ones((W,), _F32), jnp.full((W,), HEAD_DIM ** -0.5 * LOG2E, _F32),
                                jnp.ones((2 * W,), _F32)]).reshape(1, 4 * W)

    for l in range(L):
        mod = mod_all[l]
        h = _norm_mod(x, norm_mix[l], mod, 0, 1)
        wf = jnp.zeros((D, LANES), _F32).at[:, :nh].set(w_in[l][:, 4 * W:]).astype(_BF16)
        bf = jnp.zeros((1, LANES), _F32).at[0, :nh].set(b_forget[l])
        proj5 = _proj(h, w_in, l, colscale)
        fcum = _forget(h, wf, bf)
        pool_out = _pool(proj5, w_pool, l, pool_scale[l].reshape(1, W))
        attn = _attn(proj5, fcum, nh)
        merged = _merge(h, pool_out, attn, w_gate, b_gate, w_branch, l)
        x = _out_proj(x, merged, w_out, l, mod, 2)

        h_moe, eidx, pos, gate, counts = _route(x, norm_moe[l], mod, 3, 4, wr_t, br)
        counts = counts[:, 0].astype(jnp.int32)
        pcounts = (counts + bm - 1) // bm * bm
        pends = jnp.cumsum(pcounts)
        pstarts = pends - pcounts
        dest = pos
        for e in range(N_EXPERTS):
            dest = dest + jnp.where(eidx == e, pstarts[e], 0)
        n_used = (pends[-1] // bm).astype(jnp.int32)
        blk_ids = jnp.arange(rows // bm, dtype=jnp.int32)
        blk_ids = jnp.minimum(blk_ids, n_used - 1)
        block_expert = jnp.sum((pends[None, :] <= (blk_ids * bm)[:, None]).astype(jnp.int32), axis=1)
        block_expert = jnp.minimum(block_expert, N_EXPERTS - 1)
        ids = jnp.arange(N_EXPERTS, dtype=jnp.int32)
        later = (ids[None, :] > ids[:, None]) & (pcounts[None, :] > 0)
        next_of = jnp.min(jnp.where(later, ids[None, :], N_EXPERTS), axis=1)
        next_of = jnp.where(next_of == N_EXPERTS, -1, next_of)
        next_expert = jnp.sum(jnp.where(block_expert[:, None] == ids[None, :], next_of[None, :], 0), axis=1)
        n_used = n_used.reshape(1)
        buf = _dispatch(h_moe.reshape(T, D), dest, pstarts + counts, pcounts - counts, n_used, rows, bm)
        y = _experts(buf, block_expert, next_expert.astype(jnp.int32), n_used,
                     w_exp_gate, w_exp_up, w_exp_down, l, bm)
        gate_t = gate.transpose(0, 2, 1)
        x = _combine(x, y, dest, gate_t, mod, 5, norm_final, final=(l == L - 1))
    return x
```

```python
import functools

import jax
import jax.numpy as jnp
from jax import lax
from jax.experimental import pallas as pl
from jax.experimental.pallas import tpu as pltpu

N_MOD = 6
EPS = 1e-6
POOL_WINDOWS = (2, 4, 8, 16)
HEAD_DIM = 128
N_EXPERTS = 16
N_EXPERT_GROUPS = 4
EXPERTS_PER_GROUP = N_EXPERTS // N_EXPERT_GROUPS
TOP_K = 2
LOG2E = 1.4426950408889634
COMBINE_TILE = 256
LANES = 128
VMEM_LIMIT = 56 * 1024 * 1024

_F32 = jnp.float32
_BF16 = jnp.bfloat16


def _params(sem, vmem=VMEM_LIMIT):
    return pltpu.CompilerParams(dimension_semantics=sem, vmem_limit_bytes=vmem)


def _sigmoid(v):
    return 1.0 / (1.0 + jnp.exp(-v))


def _rms_mod(x, g, scale, shift):
    ms = jnp.mean(x * x, axis=-1, keepdims=True)
    return x * lax.rsqrt(ms + EPS) * g * (1.0 + scale) + shift


def _ada_kernel(c_ref, w_ref, b_ref, o_ref):
    c = c_ref[...]
    ca = (c * _sigmoid(c)).astype(_BF16)
    o_ref[...] = jnp.dot(ca, w_ref[...].astype(_BF16), preferred_element_type=_F32) + b_ref[...]


def _ada(c, w_ada, b_ada):
    L, D, M = w_ada.shape
    B = c.shape[0]
    tn = min(1024, M)
    return pl.pallas_call(
        _ada_kernel,
        grid=(L, M // tn),
        in_specs=[
            pl.BlockSpec((B, D), lambda l, n: (0, 0)),
            pl.BlockSpec((None, D, tn), lambda l, n: (l, 0, n)),
            pl.BlockSpec((None, 1, tn), lambda l, n: (l, 0, n)),
        ],
        out_specs=pl.BlockSpec((None, B, tn), lambda l, n: (l, 0, n)),
        out_shape=jax.ShapeDtypeStruct((L, B, M), _F32),
        compiler_params=_params(("arbitrary", "arbitrary")),
        name="ada_mod",
    )(c, w_ada, b_ada.reshape(L, 1, M))


def _normmod_kernel(x_ref, g_ref, sc_ref, sh_ref, h_ref):
    h_ref[...] = _rms_mod(x_ref[...], g_ref[...], sc_ref[...], sh_ref[...]).astype(h_ref.dtype)


def _norm_mod(x, g, mod, i_shift, i_scale):
    B, S, D = x.shape
    ts = min(512, S)
    return pl.pallas_call(
        _normmod_kernel,
        grid=(B, S // ts),
        in_specs=[
            pl.BlockSpec((None, ts, D), lambda b, s: (b, s, 0)),
            pl.BlockSpec((1, D), lambda b, s: (0, 0)),
            pl.BlockSpec((None, None, 1, D), lambda b, s: (i_scale, b, 0, 0)),
            pl.BlockSpec((None, None, 1, D), lambda b, s: (i_shift, b, 0, 0)),
        ],
        out_specs=pl.BlockSpec((None, ts, D), lambda b, s: (b, s, 0)),
        out_shape=jax.ShapeDtypeStruct((B, S, D), _BF16),
        compiler_params=_params(("arbitrary", "arbitrary")),
        name="norm_mod",
    )(x, g.reshape(1, D), mod, mod)


def _first_inner_step():
    return (pl.program_id(1) == 0) & (pl.program_id(2) == 0)


def _proj_kernel(h_ref, w_ref, cs_ref, o_ref, wb_ref):
    @pl.when(_first_inner_step())
    def _():
        wb_ref[...] = (w_ref[...] * cs_ref[...]).astype(_BF16)

    acc = jnp.dot(h_ref[...], wb_ref[...], preferred_element_type=_F32)
    for j in range(o_ref.shape[0]):
        o_ref[j] = acc[:, j * LANES:(j + 1) * LANES].astype(o_ref.dtype)


def _proj(h, w_in, layer, colscale):
    B, S, D = h.shape
    N = colscale.shape[1]
    tn = min(1024, N)
    tm = min(1024, S)
    return pl.pallas_call(
        _proj_kernel,
        grid=(N // tn, B, S // tm),
        in_specs=[
            pl.BlockSpec((None, tm, D), lambda n, b, m: (b, m, 0)),
            pl.BlockSpec((None, D, tn), lambda n, b, m: (layer, 0, n)),
            pl.BlockSpec((1, tn), lambda n, b, m: (0, n)),
        ],
        out_specs=pl.BlockSpec((None, tn // LANES, tm, LANES), lambda n, b, m: (b, n, m, 0)),
        out_shape=jax.ShapeDtypeStruct((B, N // LANES, S, LANES), _BF16),
        scratch_shapes=[pltpu.VMEM((D, tn), _BF16)],
        compiler_params=_params(("arbitrary", "arbitrary", "arbitrary")),
        name="in_proj",
    )(h, w_in, colscale)


def _forget_kernel(h_ref, w_ref, b_ref, o_ref):
    S = h_ref.shape[0]
    fl = jnp.dot(h_ref[...], w_ref[...], preferred_element_type=_F32) + b_ref[...]
    acc = jnp.minimum(fl, 0.0) - jnp.log(1.0 + jnp.exp(-jnp.abs(fl)))
    row = lax.broadcasted_iota(jnp.int32, acc.shape, 0)
    sh = 1
    while sh < S:
        acc = acc + jnp.where(row >= sh, pltpu.roll(acc, sh, axis=0), 0.0)
        sh *= 2
    o_ref[...] = acc


def _forget(h, wf, bf):
    B, S, D = h.shape
    return pl.pallas_call(
        _forget_kernel,
        grid=(B,),
        in_specs=[
            pl.BlockSpec((None, S, D), lambda b: (b, 0, 0)),
            pl.BlockSpec((D, LANES), lambda b: (0, 0)),
            pl.BlockSpec((1, LANES), lambda b: (0, 0)),
        ],
        out_specs=pl.BlockSpec((None, S, LANES), lambda b: (b, 0, 0)),
        out_shape=jax.ShapeDtypeStruct((B, S, LANES), _F32),
        compiler_params=_params(("arbitrary",)),
        name="forget_cumsum",
    )(h, wf, bf)


def _pool_kernel(u_ref, wp_ref, ps_ref, o_ref, *, cpg):
    S = u_ref.shape[1]
    gw = cpg * LANES
    row = lax.broadcasted_iota(jnp.int32, (S, gw), 0)
    for g, w in enumerate(POOL_WINDOWS):
        parts = [u_ref[g * cpg + j] for j in range(cpg)]
        u = (parts[0] if cpg == 1 else jnp.concatenate(parts, axis=-1)).astype(_F32)
        s = u
        sh = 1
        while sh < w:
            s = s + jnp.where(row >= sh, pltpu.roll(s, sh, axis=0), 0.0)
            sh *= 2
        cnt = jnp.minimum(row + 1, w).astype(_F32)
        pooled = s / cnt - u
        mixed = jnp.dot(pooled.astype(_BF16), wp_ref[g].astype(_BF16), preferred_element_type=_F32)
        o_ref[:, g * gw:(g + 1) * gw] = (mixed * ps_ref[:, g * gw:(g + 1) * gw]).astype(o_ref.dtype)


def _pool(proj5, w_pool, layer, ps):
    B, _, S, _ = proj5.shape
    _, G, gw, _ = w_pool.shape
    cpg = gw // LANES
    W = G * gw
    return pl.pallas_call(
        functools.partial(_pool_kernel, cpg=cpg),
        grid=(B,),
        in_specs=[
            pl.BlockSpec((None, G * cpg, S, LANES), lambda b: (b, 0, 0, 0)),
            pl.BlockSpec((None, G, gw, gw), lambda b: (layer, 0, 0, 0)),
            pl.BlockSpec((1, W), lambda b: (0, 0)),
        ],
        out_specs=pl.BlockSpec((None, S, W), lambda b: (b, 0, 0)),
        out_shape=jax.ShapeDtypeStruct((B, S, W), _BF16),
        compiler_params=_params(("arbitrary",)),
        name="pool_mixer",
    )(proj5, w_pool, ps)


def _attn_kernel(q_ref, k_ref, v_ref, f_ref, o_ref, qa_ref, ka_ref, va_ref, *, blk):
    S = q_ref.shape[0]
    h = pl.program_id(1)
    lane = lax.broadcasted_iota(jnp.int32, (S, LANES), 1)
    f = jnp.sum(jnp.where(lane == h, f_ref[...], 0.0), axis=-1, keepdims=True) * LOG2E
    f1 = f.astype(_BF16).astype(_F32)
    r1 = f - f1
    f2 = r1.astype(_BF16).astype(_F32)
    f3 = (r1 - f2).astype(_BF16).astype(_F32)
    qa_ref[:, :HEAD_DIM] = q_ref[...]
    qa_ref[:, HEAD_DIM:] = jnp.where(lane == 0, f1, jnp.where(lane == 1, f2, jnp.where(
        lane == 2, f3, jnp.where(lane < 6, 1.0, 0.0)))).astype(_BF16)
    ka_ref[:, :HEAD_DIM] = k_ref[...]
    ka_ref[:, HEAD_DIM:] = jnp.where(lane < 3, 1.0, jnp.where(lane == 3, -f1, jnp.where(
        lane == 4, -f2, jnp.where(lane == 5, -f3, 0.0)))).astype(_BF16)
    va_ref[:, :HEAD_DIM] = v_ref[...]
    va_ref[:, HEAD_DIM:] = jnp.where(lane == 0, 1.0, 0.0).astype(_BF16)

    nt = (((1,), (1,)), ((), ()))
    r_io = lax.broadcasted_iota(jnp.int32, (blk, blk), 0)
    c_io = lax.broadcasted_iota(jnp.int32, (blk, blk), 1)
    causal = c_io <= r_io
    nb = S // blk
    m = [jnp.full((blk, 1), -jnp.inf, _F32)] * nb
    acc = [jnp.zeros((blk, 2 * HEAD_DIM), _F32)] * nb
    for j in range(nb):
        ka = ka_ref[j * blk:(j + 1) * blk, :]
        va = va_ref[j * blk:(j + 1) * blk, :]
        for i in range(j, nb):
            s = lax.dot_general(qa_ref[i * blk:(i + 1) * blk, :], ka, nt, preferred_element_type=_F32)
            if i == j:
                s = jnp.where(causal, s, -jnp.inf)
            m_new = jnp.maximum(m[i], jnp.max(s, axis=-1, keepdims=True))
            p = jnp.exp2(s - m_new).astype(_BF16)
            acc[i] = jnp.exp2(m[i] - m_new) * acc[i] + jnp.dot(p, va, preferred_element_type=_F32)
            m[i] = m_new
    for i in range(nb):
        o_ref[i * blk:(i + 1) * blk, :] = (
            acc[i][:, :HEAD_DIM] / acc[i][:, HEAD_DIM:HEAD_DIM + 1]).astype(o_ref.dtype)


def _attn(proj5, fcum, nh):
    B, _, S, _ = proj5.shape
    blk = min(512, S)
    chunk = lambda o: pl.BlockSpec((None, None, S, HEAD_DIM), lambda b, h: (b, o + h, 0, 0))
    return pl.pallas_call(
        functools.partial(_attn_kernel, blk=blk),
        grid=(B, nh),
        in_specs=[
            chunk(nh), chunk(2 * nh), chunk(3 * nh),
            pl.BlockSpec((None, S, LANES), lambda b, h: (b, 0, 0)),
        ],
        out_specs=pl.BlockSpec((None, None, S, HEAD_DIM), lambda b, h: (b, h, 0, 0)),
        out_shape=jax.ShapeDtypeStruct((B, nh, S, HEAD_DIM), _BF16),
        scratch_shapes=[pltpu.VMEM((S, 2 * HEAD_DIM), _BF16)] * 3,
        compiler_params=_params(("arbitrary", "arbitrary")),
        name="forget_attn",
    )(proj5, proj5, proj5, fcum)


def _merge_kernel(h_ref, p_ref, a_ref, wg0_ref, wg1_ref, bg0_ref, bg1_ref, wb0_ref, wb1_ref, o_ref,
                  cg0_ref, cg1_ref, cb0_ref, cb1_ref):
    @pl.when(_first_inner_step())
    def _():
        cg0_ref[...] = wg0_ref[...].astype(_BF16)
        cg1_ref[...] = wg1_ref[...].astype(_BF16)
        cb0_ref[...] = wb0_ref[...].astype(_BF16)
        cb1_ref[...] = wb1_ref[...].astype(_BF16)

    h = h_ref[...]
    nh = a_ref.shape[0]
    a = jnp.concatenate([a_ref[j] for j in range(nh)], axis=-1)
    g0 = _sigmoid(jnp.dot(h, cg0_ref[...], preferred_element_type=_F32) + bg0_ref[...])
    y0 = jnp.dot(p_ref[...], cb0_ref[...], preferred_element_type=_F32)
    acc = g0 * y0
    g1 = _sigmoid(jnp.dot(h, cg1_ref[...], preferred_element_type=_F32) + bg1_ref[...])
    y1 = jnp.dot(a, cb1_ref[...], preferred_element_type=_F32)
    o_ref[...] = (acc + g1 * y1).astype(o_ref.dtype)


def _merge(h, pool_out, attn, w_gate, b_gate, w_branch, layer):
    B, S, D = h.shape
    W = pool_out.shape[-1]
    nh = attn.shape[1]
    L = w_gate.shape[0]
    tn = min(512, D)
    tm = min(512, S)
    nn = D // tn
    bg = b_gate.reshape(L, 1, 2 * D)
    return pl.pallas_call(
        _merge_kernel,
        grid=(nn, B, S // tm),
        in_specs=[
            pl.BlockSpec((None, tm, D), lambda n, b, m: (b, m, 0)),
            pl.BlockSpec((None, tm, W), lambda n, b, m: (b, m, 0)),
            pl.BlockSpec((None, nh, tm, HEAD_DIM), lambda n, b, m: (b, 0, m, 0)),
            pl.BlockSpec((None, D, tn), lambda n, b, m: (layer, 0, n)),
            pl.BlockSpec((None, D, tn), lambda n, b, m: (layer, 0, nn + n)),
            pl.BlockSpec((None, 1, tn), lambda n, b, m: (layer, 0, n)),
            pl.BlockSpec((None, 1, tn), lambda n, b, m: (layer, 0, nn + n)),
            pl.BlockSpec((None, None, W, tn), lambda n, b, m: (layer, 0, 0, n)),
            pl.BlockSpec((None, None, W, tn), lambda n, b, m: (layer, 1, 0, n)),
        ],
        out_specs=pl.BlockSpec((None, tm, tn), lambda n, b, m: (b, m, n)),
        out_shape=jax.ShapeDtypeStruct((B, S, D), _BF16),
        scratch_shapes=[pltpu.VMEM((D, tn), _BF16), pltpu.VMEM((D, tn), _BF16),
                        pltpu.VMEM((W, tn), _BF16), pltpu.VMEM((W, tn), _BF16)],
        compiler_params=_params(("arbitrary", "arbitrary", "arbitrary")),
        name="branch_merge",
    )(h, pool_out, attn, w_gate, w_gate, bg, bg, w_branch, w_branch)


def _out_kernel(x_ref, m_ref, w_ref, g_ref, o_ref, wb_ref):
    @pl.when(_first_inner_step())
    def _():
        wb_ref[...] = w_ref[...].astype(_BF16)

    y = jnp.dot(m_ref[...], wb_ref[...], preferred_element_type=_F32)
    o_ref[...] = x_ref[...] + g_ref[...] * y


def _out_proj(x, merged, w_out, layer, mod, i_gate):
    B, S, D = x.shape
    tm = min(1024, S)
    tn = min(1024, D)
    return pl.pallas_call(
        _out_kernel,
        grid=(D // tn, B, S // tm),
        in_specs=[
            pl.BlockSpec((None, tm, tn), lambda n, b, m: (b, m, n)),
            pl.BlockSpec((None, tm, D), lambda n, b, m: (b, m, 0)),
            pl.BlockSpec((None, D, tn), lambda n, b, m: (layer, 0, n)),
            pl.BlockSpec((None, None, 1, tn), lambda n, b, m: (i_gate, b, 0, n)),
        ],
        out_specs=pl.BlockSpec((None, tm, tn), lambda n, b, m: (b, m, n)),
        out_shape=jax.ShapeDtypeStruct((B, S, D), _F32),
        scratch_shapes=[pltpu.VMEM((D, tn), _BF16)],
        compiler_params=_params(("arbitrary", "arbitrary", "arbitrary")),
        name="out_proj",
    )(x, merged, w_out, mod)


def _route_kernel(x_ref, g_ref, sc_ref, sh_ref, wr_ref, br_ref,
                  h_ref, e_ref, pos_ref, gate_ref, cnt_ref):
    first_step = (pl.program_id(0) == 0) & (pl.program_id(1) == 0)

    @pl.when(first_step)
    def _():
        cnt_ref[...] = jnp.zeros_like(cnt_ref)

    h = _rms_mod(x_ref[...], g_ref[...], sc_ref[...], sh_ref[...])
    tr, D = h.shape

    h_ref[...] = h

    logits = lax.dot_general(wr_ref[...], h, (((1,), (1,)), ((), ())),
                             precision=lax.Precision.HIGHEST, preferred_element_type=_F32)
    rows = [logits[e:e + 1, :] for e in range(N_EXPERTS)]
    mx = functools.reduce(jnp.maximum, rows)
    ex = [jnp.exp(r - mx) for r in rows]
    den = functools.reduce(lambda a, b: a + b, ex)
    probs = [v / den for v in ex]
    sel = [probs[e] + br_ref[e:e + 1, :] for e in range(N_EXPERTS)]

    in_top = [None] * N_EXPERTS
    gscore = []
    for g in range(N_EXPERT_GROUPS):
        ids = range(g * EXPERTS_PER_GROUP, (g + 1) * EXPERTS_PER_GROUP)
        score = None
        for i in ids:
            rank = None
            for j in ids:
                if j == i:
                    continue
                beats = (sel[j] > sel[i]) | ((sel[j] == sel[i]) if j < i else False)
                beats = beats.astype(_F32)
                rank = beats if rank is None else rank + beats
            in_top[i] = rank < float(TOP_K)
            term = jnp.where(in_top[i], sel[i], 0.0)
            score = term if score is None else score + term
        gscore.append(score)
    chosen = []
    for g in range(N_EXPERT_GROUPS):
        lose = None
        for g2 in range(N_EXPERT_GROUPS):
            if g2 == g:
                continue
            b = (gscore[g2] > gscore[g]) | ((gscore[g2] == gscore[g]) if g2 < g else False)
            lose = b if lose is None else (lose | b)
        chosen.append(jnp.logical_not(lose))
    picked = [in_top[e] & chosen[e // EXPERTS_PER_GROUP] for e in range(N_EXPERTS)]
    pf = [p.astype(_F32) for p in picked]

    onehot = jnp.concatenate(pf, axis=0)
    r_io = lax.broadcasted_iota(jnp.int32, (tr, tr), 0)
    c_io = lax.broadcasted_iota(jnp.int32, (tr, tr), 1)
    upper = (r_io <= c_io).astype(_BF16)
    incl = jnp.dot(onehot.astype(_BF16), upper, preferred_element_type=_F32)
    base = cnt_ref[:, 0:1]
    posm = incl - 1.0 + base
    cnt_ref[...] = jnp.broadcast_to(base + incl[:, tr - 1:tr], cnt_ref.shape)

    gnum = [pf[e] * probs[e] for e in range(N_EXPERTS)]
    gden = functools.reduce(lambda a, b: a + b, gnum)
    zero = jnp.zeros_like(pf[0])
    seen = zero
    e_out = [zero, zero]
    p_out = [zero, zero]
    g_out = [zero, zero]
    for e in range(N_EXPERTS):
        for k in range(TOP_K):
            hit = pf[e] * (seen == float(k)).astype(_F32)
            e_out[k] = e_out[k] + hit * float(e)
            p_out[k] = p_out[k] + hit * posm[e:e + 1, :]
            g_out[k] = g_out[k] + hit * gnum[e]
        seen = seen + pf[e]
    e_ref[...] = jnp.concatenate(e_out, axis=0).astype(jnp.int32)
    pos_ref[...] = jnp.concatenate(p_out, axis=0).astype(jnp.int32)
    gate_ref[...] = jnp.concatenate(g_out, axis=0) / gden


def _route(x, g, mod, i_shift, i_scale, wr_t, br):
    B, S, D = x.shape
    tr = min(512, S)
    nb = S // tr
    E = N_EXPERTS
    tok = lambda dt: jax.ShapeDtypeStruct((B * nb, TOP_K, tr), dt)
    tok_spec = pl.BlockSpec((None, TOP_K, tr), lambda b, s: (b * nb + s, 0, 0))
    return pl.pallas_call(
        _route_kernel,
        grid=(B, nb),
        in_specs=[
            pl.BlockSpec((None, tr, D), lambda b, s: (b, s, 0)),
            pl.BlockSpec((1, D), lambda b, s: (0, 0)),
            pl.BlockSpec((None, None, 1, D), lambda b, s: (i_scale, b, 0, 0)),
            pl.BlockSpec((None, None, 1, D), lambda b, s: (i_shift, b, 0, 0)),
            pl.BlockSpec((E, D), lambda b, s: (0, 0)),
            pl.BlockSpec((E, 1), lambda b, s: (0, 0)),
        ],
        out_specs=[
            pl.BlockSpec((None, tr, D), lambda b, s: (b, s, 0)),
            tok_spec, tok_spec, tok_spec,
            pl.BlockSpec((E, LANES), lambda b, s: (0, 0)),
        ],
        out_shape=[
            jax.ShapeDtypeStruct((B, S, D), _F32),
            tok(jnp.int32), tok(jnp.int32), tok(_F32),
            jax.ShapeDtypeStruct((E, LANES), _F32),
        ],
        compiler_params=_params(("arbitrary", "arbitrary")),
        name="moe_route",
    )(x, g.reshape(1, D), mod, mod, wr_t, br)


def _dispatch_kernel(pad_start_ref, pad_len_ref, nu_ref, dest_ref, h_ref, buf_ref, zblk, sem):
    tr = h_ref.shape[0]

    for r in range(tr):
        for k in range(TOP_K):
            pltpu.make_async_copy(h_ref.at[pl.ds(r, 1), :],
                                  buf_ref.at[pl.ds(dest_ref[k, r], 1), :], sem).start(priority=k)

    def row_wait():
        pltpu.make_async_copy(h_ref.at[pl.ds(0, 1), :], buf_ref.at[pl.ds(0, 1), :], sem).wait()

    for _ in range(tr * TOP_K):
        row_wait()

    @pl.when(pl.program_id(0) == 0)
    def _():
        bm = zblk.shape[0]
        nblk = buf_ref.shape[0] // bm
        zblk[...] = jnp.zeros_like(zblk)
        for e in range(N_EXPERTS):
            def zissue(r, carry, e=e):
                pltpu.make_async_copy(zblk.at[pl.ds(0, 1), :],
                                      buf_ref.at[pl.ds(pad_start_ref[e] + r, 1), :], sem).start()
                return carry
            lax.fori_loop(0, pad_len_ref[e], zissue, 0)
        for e in range(N_EXPERTS):
            def zdrain(r, carry):
                row_wait()
                return carry
            lax.fori_loop(0, pad_len_ref[e], zdrain, 0)

        def blk_copy(i):
            return pltpu.make_async_copy(zblk, buf_ref.at[pl.ds(pl.multiple_of(i * bm, bm), bm), :], sem)

        def bissue(i, carry):
            blk_copy(i).start()
            return carry

        def bdrain(i, carry):
            blk_copy(i).wait()
            return carry

        lax.fori_loop(nu_ref[0], nblk, bissue, 0)
        lax.fori_loop(nu_ref[0], nblk, bdrain, 0)


def _dispatch(h, dest, pad_start, pad_len, n_used, rows, bm):
    T, D = h.shape
    nbt, _, tr = dest.shape
    grid_spec = pltpu.PrefetchScalarGridSpec(
        num_scalar_prefetch=3,
        grid=(nbt,),
        in_specs=[
            pl.BlockSpec((None, TOP_K, tr), lambda i, ps, pn, nu: (i, 0, 0), memory_space=pltpu.SMEM),
            pl.BlockSpec((tr, D), lambda i, ps, pn, nu: (i, 0)),
        ],
        out_specs=pl.BlockSpec(memory_space=pl.ANY),
        scratch_shapes=[pltpu.VMEM((bm, D), _F32), pltpu.SemaphoreType.DMA(())],
    )
    return pl.pallas_call(
        _dispatch_kernel,
        grid_spec=grid_spec,
        out_shape=jax.ShapeDtypeStruct((rows, D), _F32),
        compiler_params=_params(("arbitrary",)),
        name="moe_dispatch",
    )(pad_start, pad_len, n_used, dest, h)


def _expert_kernel(be_ref, nx_ref, nu_ref, x_ref, wg_hbm, wu_hbm, wd_hbm, o_ref,
                   wgb, wub, wdb, sg, su, sd, sem, *, layer):
    i = pl.program_id(0)
    e = be_ref[i]
    live = i < nu_ref[0]
    new_expert = (i == 0) | (e != be_ref[jnp.maximum(i - 1, 0)])

    def weight_copies(ex):
        return [pltpu.make_async_copy(w.at[layer, ex], s, sem.at[n])
                for n, (w, s) in enumerate(((wg_hbm, sg), (wu_hbm, su), (wd_hbm, sd)))]

    @pl.when(live & new_expert)
    def _():
        @pl.when(i == 0)
        def _():
            for cp in weight_copies(e):
                cp.start()

        for cp in weight_copies(e):
            cp.wait()
        wgb[...] = sg[...].astype(_BF16)
        wub[...] = su[...].astype(_BF16)
        wdb[...] = sd[...].astype(_BF16)

        @pl.when(nx_ref[i] >= 0)
        def _():
            for cp in weight_copies(nx_ref[i]):
                cp.start()

    @pl.when(live)
    def _():
        xb = x_ref[...].astype(_BF16)
        a = jnp.dot(xb, wgb[...], preferred_element_type=_F32)
        u = jnp.dot(xb, wub[...], preferred_element_type=_F32)
        act = (a * _sigmoid(a) * u).astype(_BF16)
        o_ref[...] = jnp.dot(act, wdb[...], preferred_element_type=_F32)

    @pl.when(jnp.logical_not(live))
    def _():
        o_ref[...] = jnp.zeros_like(o_ref)


def _experts(buf, block_expert, next_expert, n_used, wg, wu, wd, layer, bm):
    rows, D = buf.shape
    De = wg.shape[-1]
    nblk = rows // bm
    grid_spec = pltpu.PrefetchScalarGridSpec(
        num_scalar_prefetch=3,
        grid=(nblk,),
        in_specs=[
            pl.BlockSpec((bm, D), lambda i, be, nx, nu: (jnp.minimum(i, jnp.maximum(nu[0] - 1, 0)), 0)),
            pl.BlockSpec(memory_space=pl.ANY),
            pl.BlockSpec(memory_space=pl.ANY),
            pl.BlockSpec(memory_space=pl.ANY),
        ],
        out_specs=pl.BlockSpec((bm, D), lambda i, be, nx, nu: (i, 0)),
        scratch_shapes=[
            pltpu.VMEM((D, De), _BF16), pltpu.VMEM((D, De), _BF16), pltpu.VMEM((De, D), _BF16),
            pltpu.VMEM((D, De), _F32), pltpu.VMEM((D, De), _F32), pltpu.VMEM((De, D), _F32),
            pltpu.SemaphoreType.DMA((3,)),
        ],
    )
    return pl.pallas_call(
        functools.partial(_expert_kernel, layer=layer),
        grid_spec=grid_spec,
        out_shape=jax.ShapeDtypeStruct((rows, D), _F32),
        compiler_params=_params(("arbitrary",), vmem=60 * 1024 * 1024),
        name="moe_experts",
    )(block_expert, next_expert, n_used, buf, wg, wu, wd)


ROW_CHUNK = 8


def _combine_kernel(src_ref, nch_ref, dst_ref, tot_ref,
                    x_ref, y_hbm, srow_ref, grow_ref, scol_ref, gm_ref, nf_ref, gn_ref, scn_ref, shn_ref,
                    *rest, final, nb, has_next):
    if has_next:
        o_ref, h_ref, ystage, ysb, sem = rest
    else:
        o_ref, ystage, ysb, sem = rest
        h_ref = None
    i = pl.program_id(0) * nb + pl.program_id(1)
    n_tiles = pl.num_programs(0) * nb
    cur = lax.rem(i, 2)
    tc = x_ref.shape[0]
    R = ystage.shape[1]

    def chunk_copy(src, dst, buf):
        return pltpu.make_async_copy(y_hbm.at[pl.ds(src, ROW_CHUNK), :],
                                     ystage.at[buf, pl.ds(dst, ROW_CHUNK), :], sem.at[buf])

    def start_tile(tile, buf):
        for e in range(N_EXPERTS):
            idx = tile * N_EXPERTS + e

            def body(c, carry, idx=idx):
                chunk_copy(pl.multiple_of(src_ref[idx] + c * ROW_CHUNK, ROW_CHUNK),
                           pl.multiple_of(dst_ref[idx] + c * ROW_CHUNK, ROW_CHUNK), buf).start()
                return carry

            lax.fori_loop(0, nch_ref[idx], body, 0)

    @pl.when(i == 0)
    def _():
        ystage[...] = jnp.zeros_like(ystage)
        start_tile(0, 0)

    @pl.when(i + 1 < n_tiles)
    def _():
        start_tile(i + 1, 1 - cur)

    def wait_one(c, carry):
        chunk_copy(0, 0, cur).wait()
        return carry

    lax.fori_loop(0, tot_ref[i], wait_one, 0)

    srow = srow_ref[...]
    grow = grow_ref[...]
    r_io = lax.broadcasted_iota(jnp.int32, (R, tc), 0)
    gate_of_row = jnp.sum(jnp.where(r_io == srow[0:1, :], grow[0:1, :], 0.0)
                          + jnp.where(r_io == srow[1:2, :], grow[1:2, :], 0.0),
                          axis=1, keepdims=True)
    step = LANES
    for r0 in range(0, R, step):
        ysb[r0:r0 + step, :] = (ystage[cur, r0:r0 + step, :] * gate_of_row[r0:r0 + step, :]).astype(_BF16)
    scol = scol_ref[...]
    c_io = lax.broadcasted_iota(jnp.int32, (tc, R), 1)
    g = jnp.where((c_io == scol[:, 0:1]) | (c_io == scol[:, 1:2]), 1.0, 0.0).astype(_BF16)
    moe = jnp.dot(g, ysb[...], preferred_element_type=_F32)
    xn = x_ref[...] + gm_ref[...] * moe
    if final:
        ms = jnp.mean(xn * xn, axis=-1, keepdims=True)
        xn = xn * lax.rsqrt(ms + EPS) * nf_ref[...]
    o_ref[...] = xn
    if has_next:
        h_ref[...] = _rms_mod(xn, gn_ref[...], scn_ref[...], shn_ref[...]).astype(h_ref.dtype)


def _combine(x, y, src, nch, dst, tot, slot_row, gate_row, slot_col, mod, i_gate, nf, final, g_next, mod_next):
    B, S, D = x.shape
    n_tiles, _, tc = slot_row.shape
    nb = S // tc
    R = _stage_rows(tc)
    has_next = g_next is not None
    if not has_next:
        g_next, mod_next = nf, mod
    tile = lambda b, s, *_: (b * nb + s, 0, 0)
    row = lambda b, s, *_: (b, s, 0)
    grid_spec = pltpu.PrefetchScalarGridSpec(
        num_scalar_prefetch=4,
        grid=(B, nb),
        in_specs=[
            pl.BlockSpec((None, tc, D), row),
            pl.BlockSpec(memory_space=pl.ANY),
            pl.BlockSpec((None, TOP_K, tc), tile),
            pl.BlockSpec((None, TOP_K, tc), tile),
            pl.BlockSpec((None, tc, TOP_K), tile),
            pl.BlockSpec((None, None, 1, D), lambda b, s, *_: (i_gate, b, 0, 0)),
            pl.BlockSpec((1, D), lambda b, s, *_: (0, 0)),
            pl.BlockSpec((1, D), lambda b, s, *_: (0, 0)),
            pl.BlockSpec((None, None, 1, D), lambda b, s, *_: (1, b, 0, 0)),
            pl.BlockSpec((None, None, 1, D), lambda b, s, *_: (0, b, 0, 0)),
        ],
        out_specs=([pl.BlockSpec((None, tc, D), row)] * 2) if has_next else pl.BlockSpec((None, tc, D), row),
        scratch_shapes=[pltpu.VMEM((2, R, D), _F32), pltpu.VMEM((R, D), _BF16),
                        pltpu.SemaphoreType.DMA((2,))],
    )
    out_shape = jax.ShapeDtypeStruct((B, S, D), _F32)
    if has_next:
        out_shape = [out_shape, jax.ShapeDtypeStruct((B, S, D), _BF16)]
    return pl.pallas_call(
        functools.partial(_combine_kernel, final=final, nb=nb, has_next=has_next),
        grid_spec=grid_spec,
        out_shape=out_shape,
        compiler_params=_params(("arbitrary", "arbitrary")),
        name="moe_combine",
    )(src, nch, dst, tot, x, y, slot_row, gate_row, slot_col, mod, nf.reshape(1, D), g_next.reshape(1, D),
      mod_next, mod_next)


def _stage_rows(tc):
    worst = TOP_K * tc + 2 * (ROW_CHUNK - 1) * N_EXPERTS
    return -(-worst // LANES) * LANES


def _combine_plan(eidx, dest, gate, pstarts, tc):
    T = eidx.shape[0] * eidx.shape[2]
    n_tiles = T // tc
    flat = lambda a: a.transpose(0, 2, 1).reshape(n_tiles, tc, TOP_K)
    e_f, d_f, g_f = flat(eidx), flat(dest), flat(gate)
    onehot = (e_f[..., None] == jnp.arange(N_EXPERTS, dtype=jnp.int32)).astype(jnp.int32)
    cnt = jnp.sum(onehot, axis=(1, 2))
    first = pstarts[None, :] + jnp.cumsum(cnt, axis=0) - cnt
    src = first // ROW_CHUNK * ROW_CHUNK
    nch = jnp.where(cnt > 0, (first + cnt - src + ROW_CHUNK - 1) // ROW_CHUNK, 0)
    dst = ROW_CHUNK * (jnp.cumsum(nch, axis=1) - nch)
    slot = d_f + jnp.sum(onehot * (dst - src)[:, None, None, :], axis=-1)
    return (src.reshape(-1), nch.reshape(-1), dst.reshape(-1), jnp.sum(nch, axis=1),
            slot.transpose(0, 2, 1), g_f.transpose(0, 2, 1), slot)


def kernel(x, c, w_ada, b_ada, norm_mix, norm_moe, w_in, b_forget, w_pool, pool_scale, w_branch,
           w_gate, b_gate, w_out, w_router, b_router, w_exp_gate, w_exp_up, w_exp_down, norm_final):
    B, S, D = x.shape
    L = w_ada.shape[0]
    W = D // 2
    nh = W // HEAD_DIM
    T = B * S
    A = T * TOP_K
    bm = min(256, A // N_EXPERTS)
    rows = (A // bm + N_EXPERTS) * bm

    mod_all = _ada(c, w_ada, b_ada)
    mod_all = mod_all.reshape(L, B, N_MOD, D).transpose(0, 2, 1, 3)[:, :, :, None, :]
    wr_t = w_router.T
    br = b_router.reshape(N_EXPERTS, 1)
    colscale = jnp.concatenate([jnp.ones((W,), _F32), jnp.full((W,), HEAD_DIM ** -0.5 * LOG2E, _F32),
                                jnp.ones((2 * W,), _F32)]).reshape(1, 4 * W)

    h = _norm_mod(x, norm_mix[0], mod_all[0], 0, 1)
    for l in range(L):
        mod = mod_all[l]
        wf =jnp.zeros((D, LANES), _F32).at[:, :nh].set(w_in[l][:, 4 * W:]).astype(_BF16)
        bf = jnp.zeros((1, LANES), _F32).at[0, :nh].set(b_forget[l])
        proj5 = _proj(h, w_in, l, colscale)
        fcum = _forget(h, wf, bf)
        pool_out = _pool(proj5, w_pool, l, pool_scale[l].reshape(1, W))
        attn = _attn(proj5, fcum, nh)
        merged = _merge(h, pool_out, attn, w_gate, b_gate, w_branch, l)
        x = _out_proj(x, merged, w_out, l, mod, 2)

        h_moe, eidx, pos, gate, counts = _route(x, norm_moe[l], mod, 3, 4, wr_t, br)
        counts = counts[:, 0].astype(jnp.int32)
        pcounts = (counts + bm - 1) // bm * bm
        pends = jnp.cumsum(pcounts)
        pstarts = pends - pcounts
        dest = pos
        for e in range(N_EXPERTS):
            dest = dest + jnp.where(eidx == e, pstarts[e], 0)
        n_used = (pends[-1] // bm).astype(jnp.int32)
        blk_ids = jnp.arange(rows // bm, dtype=jnp.int32)
        blk_ids = jnp.minimum(blk_ids, n_used - 1)
        block_expert = jnp.sum((pends[None, :] <= (blk_ids * bm)[:, None]).astype(jnp.int32), axis=1)
        block_expert = jnp.minimum(block_expert, N_EXPERTS - 1)
        ids = jnp.arange(N_EXPERTS, dtype=jnp.int32)
        later = (ids[None, :] > ids[:, None]) & (pcounts[None, :] > 0)
        next_of = jnp.min(jnp.where(later, ids[None, :], N_EXPERTS), axis=1)
        next_of = jnp.where(next_of == N_EXPERTS, -1, next_of)
        next_expert = jnp.sum(jnp.where(block_expert[:, None] == ids[None, :], next_of[None, :], 0), axis=1)
        n_used = n_used.reshape(1)
        buf = _dispatch(h_moe.reshape(T, D), dest, pstarts + counts, pcounts - counts, n_used, rows, bm)
        y = _experts(buf, block_expert, next_expert.astype(jnp.int32), n_used,
                     w_exp_gate, w_exp_up, w_exp_down, l, bm)
        plan = _combine_plan(eidx, dest, gate, pstarts, min(COMBINE_TILE, S))
        if l + 1 < L:
            x, h = _combine(x, y, *plan, mod, 5, norm_final, False, norm_mix[l + 1], mod_all[l + 1])
        else:
            x = _combine(x, y, *plan, mod, 5, norm_final, True, None, None)
    return x
```

```python
import functools

import jax
import jax.numpy as jnp
from jax import lax
from jax.experimental import pallas as pl
from jax.experimental.pallas import tpu as pltpu

N_MOD = 6
EPS = 1e-6
POOL_WINDOWS = (2, 4, 8, 16)
HEAD_DIM = 128
N_EXPERTS = 16
N_EXPERT_GROUPS = 4
EXPERTS_PER_GROUP = N_EXPERTS // N_EXPERT_GROUPS
TOP_K = 2
LOG2E = 1.4426950408889634
ROUTE_TILE = 1024
DISPATCH_TILE = 512
COMBINE_TILE = 256
LANES = 128
VMEM_LIMIT = 56 * 1024 * 1024

_F32 = jnp.float32
_BF16 = jnp.bfloat16


def _params(sem, vmem=VMEM_LIMIT):
    return pltpu.CompilerParams(dimension_semantics=sem, vmem_limit_bytes=vmem)


def _sigmoid(v):
    return 1.0 / (1.0 + jnp.exp(-v))


def _rms_mod(x, g, scale, shift):
    ms = jnp.mean(x * x, axis=-1, keepdims=True)
    return x * lax.rsqrt(ms + EPS) * g * (1.0 + scale) + shift


def _ada_kernel(c_ref, w_ref, b_ref, o_ref):
    c = c_ref[...]
    ca = (c * _sigmoid(c)).astype(_BF16)
    o_ref[...] = jnp.dot(ca, w_ref[...].astype(_BF16), preferred_element_type=_F32) + b_ref[...]


def _ada(c, w_ada, b_ada):
    L, D, M = w_ada.shape
    B = c.shape[0]
    tn = min(1024, M)
    return pl.pallas_call(
        _ada_kernel,
        grid=(L, M // tn),
        in_specs=[
            pl.BlockSpec((B, D), lambda l, n: (0, 0)),
            pl.BlockSpec((None, D, tn), lambda l, n: (l, 0, n)),
            pl.BlockSpec((None, 1, tn), lambda l, n: (l, 0, n)),
        ],
        out_specs=pl.BlockSpec((None, B, tn), lambda l, n: (l, 0, n)),
        out_shape=jax.ShapeDtypeStruct((L, B, M), _F32),
        compiler_params=_params(("arbitrary", "arbitrary")),
        name="ada_mod",
    )(c, w_ada, b_ada.reshape(L, 1, M))


def _normmod_kernel(x_ref, g_ref, sc_ref, sh_ref, h_ref):
    h_ref[...] = _rms_mod(x_ref[...], g_ref[...], sc_ref[...], sh_ref[...]).astype(h_ref.dtype)


def _norm_mod(x, g, mod, i_shift, i_scale):
    B, S, D = x.shape
    ts = min(512, S)
    return pl.pallas_call(
        _normmod_kernel,
        grid=(B, S // ts),
        in_specs=[
            pl.BlockSpec((None, ts, D), lambda b, s: (b, s, 0)),
            pl.BlockSpec((1, D), lambda b, s: (0, 0)),
            pl.BlockSpec((None, None, 1, D), lambda b, s: (i_scale, b, 0, 0)),
            pl.BlockSpec((None, None, 1, D), lambda b, s: (i_shift, b, 0, 0)),
        ],
        out_specs=pl.BlockSpec((None, ts, D), lambda b, s: (b, s, 0)),
        out_shape=jax.ShapeDtypeStruct((B, S, D), _BF16),
        compiler_params=_params(("arbitrary", "arbitrary")),
        name="norm_mod",
    )(x, g.reshape(1, D), mod, mod)


def _first_inner_step():
    return (pl.program_id(1) == 0) & (pl.program_id(2) == 0)


def _proj_kernel(h_ref, w_ref, cs_ref, o_ref, wb_ref):
    @pl.when(_first_inner_step())
    def _():
        wb_ref[...] = (w_ref[...] * cs_ref[...]).astype(_BF16)

    acc = jnp.dot(h_ref[...], wb_ref[...], preferred_element_type=_F32)
    for j in range(o_ref.shape[0]):
        o_ref[j] = acc[:, j * LANES:(j + 1) * LANES].astype(o_ref.dtype)


def _proj(h, w_in, layer, colscale):
    B, S, D = h.shape
    N = colscale.shape[1]
    tn = min(1024, N)
    tm = min(1024, S)
    return pl.pallas_call(
        _proj_kernel,
        grid=(N // tn, B, S // tm),
        in_specs=[
            pl.BlockSpec((None, tm, D), lambda n, b, m: (b, m, 0)),
            pl.BlockSpec((None, D, tn), lambda n, b, m: (layer, 0, n)),
            pl.BlockSpec((1, tn), lambda n, b, m: (0, n)),
        ],
        out_specs=pl.BlockSpec((None, tn // LANES, tm, LANES), lambda n, b, m: (b, n, m, 0)),
        out_shape=jax.ShapeDtypeStruct((B, N // LANES, S, LANES), _BF16),
        scratch_shapes=[pltpu.VMEM((D, tn), _BF16)],
        compiler_params=_params(("arbitrary", "arbitrary", "arbitrary")),
        name="in_proj",
    )(h, w_in, colscale)


def _forget_kernel(h_ref, w_ref, b_ref, o_ref):
    S = h_ref.shape[0]
    fl = jnp.dot(h_ref[...], w_ref[...], preferred_element_type=_F32) + b_ref[...]
    acc = jnp.minimum(fl, 0.0) - jnp.log(1.0 + jnp.exp(-jnp.abs(fl)))
    row = lax.broadcasted_iota(jnp.int32, acc.shape, 0)
    sh = 1
    while sh < S:
        acc = acc + jnp.where(row >= sh, pltpu.roll(acc, sh, axis=0), 0.0)
        sh *= 2
    o_ref[...] = acc


def _forget(h, wf, bf):
    B, S, D = h.shape
    return pl.pallas_call(
        _forget_kernel,
        grid=(B,),
        in_specs=[
            pl.BlockSpec((None, S, D), lambda b: (b, 0, 0)),
            pl.BlockSpec((D, LANES), lambda b: (0, 0)),
            pl.BlockSpec((1, LANES), lambda b: (0, 0)),
        ],
        out_specs=pl.BlockSpec((None, S, LANES), lambda b: (b, 0, 0)),
        out_shape=jax.ShapeDtypeStruct((B, S, LANES), _F32),
        compiler_params=_params(("arbitrary",)),
        name="forget_cumsum",
    )(h, wf, bf)


def _pool_kernel(u_ref, wp_ref, ps_ref, o_ref, *, cpg):
    S = u_ref.shape[1]
    gw = cpg * LANES
    row = lax.broadcasted_iota(jnp.int32, (S, gw), 0)
    for g, w in enumerate(POOL_WINDOWS):
        parts = [u_ref[g * cpg + j] for j in range(cpg)]
        u = (parts[0] if cpg == 1 else jnp.concatenate(parts, axis=-1)).astype(_F32)
        s = u
        sh = 1
        while sh < w:
            s = s + jnp.where(row >= sh, pltpu.roll(s, sh, axis=0), 0.0)
            sh *= 2
        cnt = jnp.minimum(row + 1, w).astype(_F32)
        pooled = s / cnt - u
        mixed = jnp.dot(pooled.astype(_BF16), wp_ref[g].astype(_BF16), preferred_element_type=_F32)
        o_ref[:, g * gw:(g + 1) * gw] = (mixed * ps_ref[:, g * gw:(g + 1) * gw]).astype(o_ref.dtype)


def _pool(proj5, w_pool, layer, ps):
    B, _, S, _ = proj5.shape
    _, G, gw, _ = w_pool.shape
    cpg = gw // LANES
    W = G * gw
    return pl.pallas_call(
        functools.partial(_pool_kernel, cpg=cpg),
        grid=(B,),
        in_specs=[
            pl.BlockSpec((None, G * cpg, S, LANES), lambda b: (b, 0, 0, 0)),
            pl.BlockSpec((None, G, gw, gw), lambda b: (layer, 0, 0, 0)),
            pl.BlockSpec((1, W), lambda b: (0, 0)),
        ],
        out_specs=pl.BlockSpec((None, S, W), lambda b: (b, 0, 0)),
        out_shape=jax.ShapeDtypeStruct((B, S, W), _BF16),
        compiler_params=_params(("arbitrary",)),
        name="pool_mixer",
    )(proj5, w_pool, ps)


def _attn_kernel(q_ref, k_ref, v_ref, f_ref, o_ref, qa_ref, ka_ref, va_ref, *, blk):
    S = q_ref.shape[0]
    h = pl.program_id(1)
    lane = lax.broadcasted_iota(jnp.int32, (S, LANES), 1)
    f = jnp.sum(jnp.where(lane == h, f_ref[...], 0.0), axis=-1, keepdims=True) * LOG2E
    f1 = f.astype(_BF16).astype(_F32)
    r1 = f - f1
    f2 = r1.astype(_BF16).astype(_F32)
    f3 = (r1 - f2).astype(_BF16).astype(_F32)
    qa_ref[:, :HEAD_DIM] = q_ref[...]
    qa_ref[:, HEAD_DIM:] = jnp.where(lane == 0, f1, jnp.where(lane == 1, f2, jnp.where(
        lane == 2, f3, jnp.where(lane < 6, 1.0, 0.0)))).astype(_BF16)
    ka_ref[:, :HEAD_DIM] = k_ref[...]
    ka_ref[:, HEAD_DIM:] = jnp.where(lane < 3, 1.0, jnp.where(lane == 3, -f1, jnp.where(
        lane == 4, -f2, jnp.where(lane == 5, -f3, 0.0)))).astype(_BF16)
    va_ref[:, :HEAD_DIM] = v_ref[...]
    va_ref[:, HEAD_DIM:] = jnp.where(lane == 0, 1.0, 0.0).astype(_BF16)

    nt = (((1,), (1,)), ((), ()))
    r_io = lax.broadcasted_iota(jnp.int32, (blk, blk), 0)
    c_io = lax.broadcasted_iota(jnp.int32, (blk, blk), 1)
    causal = c_io <= r_io
    nb = S // blk
    m = [jnp.full((blk, 1), -jnp.inf, _F32)] * nb
    acc = [jnp.zeros((blk, 2 * HEAD_DIM), _F32)] * nb
    for j in range(nb):
        ka = ka_ref[j * blk:(j + 1) * blk, :]
        va = va_ref[j * blk:(j + 1) * blk, :]
        for i in range(j, nb):
            s = lax.dot_general(qa_ref[i * blk:(i + 1) * blk, :], ka, nt, preferred_element_type=_F32)
            if i == j:
                s = jnp.where(causal, s, -jnp.inf)
            m_new = jnp.maximum(m[i], jnp.max(s, axis=-1, keepdims=True))
            p = jnp.exp2(s - m_new).astype(_BF16)
            acc[i] = jnp.exp2(m[i] - m_new) * acc[i] + jnp.dot(p, va, preferred_element_type=_F32)
            m[i] = m_new
    for i in range(nb):
        o_ref[i * blk:(i + 1) * blk, :] = (
            acc[i][:, :HEAD_DIM] / acc[i][:, HEAD_DIM:HEAD_DIM + 1]).astype(o_ref.dtype)


def _attn(proj5, fcum, nh):
    B, _, S, _ = proj5.shape
    blk = min(512, S)
    chunk = lambda o: pl.BlockSpec((None, None, S, HEAD_DIM), lambda b, h: (b, o + h, 0, 0))
    return pl.pallas_call(
        functools.partial(_attn_kernel, blk=blk),
        grid=(B, nh),
        in_specs=[
            chunk(nh), chunk(2 * nh), chunk(3 * nh),
            pl.BlockSpec((None, S, LANES), lambda b, h: (b, 0, 0)),
        ],
        out_specs=pl.BlockSpec((None, None, S, HEAD_DIM), lambda b, h: (b, h, 0, 0)),
        out_shape=jax.ShapeDtypeStruct((B, nh, S, HEAD_DIM), _BF16),
        scratch_shapes=[pltpu.VMEM((S, 2 * HEAD_DIM), _BF16)] * 3,
        compiler_params=_params(("arbitrary", "arbitrary")),
        name="forget_attn",
    )(proj5, proj5, proj5, fcum)


def _merge_kernel(h_ref, p_ref, a_ref, wg0_ref, wg1_ref, bg0_ref, bg1_ref, wb0_ref, wb1_ref, o_ref,
                  cg0_ref, cg1_ref, cb0_ref, cb1_ref):
    @pl.when(_first_inner_step())
    def _():
        cg0_ref[...] = wg0_ref[...].astype(_BF16)
        cg1_ref[...] = wg1_ref[...].astype(_BF16)
        cb0_ref[...] = wb0_ref[...].astype(_BF16)
        cb1_ref[...] = wb1_ref[...].astype(_BF16)

    h = h_ref[...]
    nh = a_ref.shape[0]
    a = jnp.concatenate([a_ref[j] for j in range(nh)], axis=-1)
    g0 = _sigmoid(jnp.dot(h, cg0_ref[...], preferred_element_type=_F32) + bg0_ref[...])
    y0 = jnp.dot(p_ref[...], cb0_ref[...], preferred_element_type=_F32)
    acc = g0 * y0
    g1 = _sigmoid(jnp.dot(h, cg1_ref[...], preferred_element_type=_F32) + bg1_ref[...])
    y1 = jnp.dot(a, cb1_ref[...], preferred_element_type=_F32)
    o_ref[...] = (acc + g1 * y1).astype(o_ref.dtype)


def _merge(h, pool_out, attn, w_gate, b_gate, w_branch, layer):
    B, S, D = h.shape
    W = pool_out.shape[-1]
    nh = attn.shape[1]
    L = w_gate.shape[0]
    tn = min(512, D)
    tm = min(512, S)
    nn = D // tn
    bg = b_gate.reshape(L, 1, 2 * D)
    return pl.pallas_call(
        _merge_kernel,
        grid=(nn, B, S // tm),
        in_specs=[
            pl.BlockSpec((None, tm, D), lambda n, b, m: (b, m, 0)),
            pl.BlockSpec((None, tm, W), lambda n, b, m: (b, m, 0)),
            pl.BlockSpec((None, nh, tm, HEAD_DIM), lambda n, b, m: (b, 0, m, 0)),
            pl.BlockSpec((None, D, tn), lambda n, b, m: (layer, 0, n)),
            pl.BlockSpec((None, D, tn), lambda n, b, m: (layer, 0, nn + n)),
            pl.BlockSpec((None, 1, tn), lambda n, b, m: (layer, 0, n)),
            pl.BlockSpec((None, 1, tn), lambda n, b, m: (layer, 0, nn + n)),
            pl.BlockSpec((None, None, W, tn), lambda n, b, m: (layer, 0, 0, n)),
            pl.BlockSpec((None, None, W, tn), lambda n, b, m: (layer, 1, 0, n)),
        ],
        out_specs=pl.BlockSpec((None, tm, tn), lambda n, b, m: (b, m, n)),
        out_shape=jax.ShapeDtypeStruct((B, S, D), _BF16),
        scratch_shapes=[pltpu.VMEM((D, tn), _BF16), pltpu.VMEM((D, tn), _BF16),
                        pltpu.VMEM((W, tn), _BF16), pltpu.VMEM((W, tn), _BF16)],
        compiler_params=_params(("arbitrary", "arbitrary", "arbitrary")),
        name="branch_merge",
    )(h, pool_out, attn, w_gate, w_gate, bg, bg, w_branch, w_branch)


def _out_kernel(x_ref, m_ref, w_ref, g_ref, o_ref, wb_ref):
    @pl.when(_first_inner_step())
    def _():
        wb_ref[...] = w_ref[...].astype(_BF16)

    y = jnp.dot(m_ref[...], wb_ref[...], preferred_element_type=_F32)
    o_ref[...] = x_ref[...] + g_ref[...] * y


def _out_proj(x, merged, w_out, layer, mod, i_gate):
    B, S, D = x.shape
    tm = min(1024, S)
    tn = min(1024, D)
    return pl.pallas_call(
        _out_kernel,
        grid=(D // tn, B, S // tm),
        in_specs=[
            pl.BlockSpec((None, tm, tn), lambda n, b, m: (b, m, n)),
            pl.BlockSpec((None, tm, D), lambda n, b, m: (b, m, 0)),
            pl.BlockSpec((None, D, tn), lambda n, b, m: (layer, 0, n)),
            pl.BlockSpec((None, None, 1, tn), lambda n, b, m: (i_gate, b, 0, n)),
        ],
        out_specs=pl.BlockSpec((None, tm, tn), lambda n, b, m: (b, m, n)),
        out_shape=jax.ShapeDtypeStruct((B, S, D), _F32),
        scratch_shapes=[pltpu.VMEM((D, tn), _BF16)],
        compiler_params=_params(("arbitrary", "arbitrary", "arbitrary")),
        name="out_proj",
    )(x, merged, w_out, mod)


def _route_kernel(x_ref, g_ref, sc_ref, sh_ref, wr_ref, br_ref,
                  e_ref, pos_ref, gate_ref, cnt_ref):
    first_step = (pl.program_id(0) == 0) & (pl.program_id(1) == 0)

    @pl.when(first_step)
    def _():
        cnt_ref[...] = jnp.zeros_like(cnt_ref)

    h = _rms_mod(x_ref[...], g_ref[...], sc_ref[...], sh_ref[...])
    tr, D = h.shape
    ns = tr // LANES

    sub = [lax.dot_general(wr_ref[...], h[s * LANES:(s + 1) * LANES, :], (((1,), (1,)), ((), ())),
                           precision=lax.Precision.HIGHEST, preferred_element_type=_F32)
           for s in range(ns)]
    rows = [jnp.concatenate([sub[s][e:e + 1, :] for s in range(ns)], axis=0) for e in range(N_EXPERTS)]
    mx = functools.reduce(jnp.maximum, rows)
    ex = [jnp.exp(r - mx) for r in rows]
    den = functools.reduce(lambda a, b: a + b, ex)
    probs = [v / den for v in ex]
    sel = [probs[e] + br_ref[e:e + 1, :] for e in range(N_EXPERTS)]

    in_top = [None] * N_EXPERTS
    gscore = []
    for g in range(N_EXPERT_GROUPS):
        ids = range(g * EXPERTS_PER_GROUP, (g + 1) * EXPERTS_PER_GROUP)
        score = None
        for i in ids:
            rank = None
            for j in ids:
                if j == i:
                    continue
                beats = (sel[j] > sel[i]) | ((sel[j] == sel[i]) if j < i else False)
                beats = beats.astype(_F32)
                rank = beats if rank is None else rank + beats
            in_top[i] = rank < float(TOP_K)
            term = jnp.where(in_top[i], sel[i], 0.0)
            score = term if score is None else score + term
        gscore.append(score)
    chosen = []
    for g in range(N_EXPERT_GROUPS):
        lose = None
        for g2 in range(N_EXPERT_GROUPS):
            if g2 == g:
                continue
            b = (gscore[g2] > gscore[g]) | ((gscore[g2] == gscore[g]) if g2 < g else False)
            lose = b if lose is None else (lose | b)
        chosen.append(jnp.logical_not(lose))
    picked = [in_top[e] & chosen[e // EXPERTS_PER_GROUP] for e in range(N_EXPERTS)]
    pf = [p.astype(_F32) for p in picked]

    n = N_EXPERTS * ns
    onehot = jnp.concatenate(pf, axis=0).astype(_BF16)
    upper = (lax.broadcasted_iota(jnp.int32, (LANES, LANES), 0)
             <= lax.broadcasted_iota(jnp.int32, (LANES, LANES), 1)).astype(_BF16)
    incl = jnp.dot(onehot, upper, preferred_element_type=_F32)
    r_io = lax.broadcasted_iota(jnp.int32, (n, n), 0)
    c_io = lax.broadcasted_iota(jnp.int32, (n, n), 1)
    shift = ns.bit_length() - 1
    same_expert = lax.shift_right_logical(r_io, shift) == lax.shift_right_logical(c_io, shift)
    earlier = jnp.where(same_expert & (c_io < r_io), 1.0, 0.0).astype(_BF16)
    carry = jnp.dot(earlier, incl.astype(_BF16), preferred_element_type=_F32)[:, LANES - 1:LANES]
    base = cnt_ref[:, 0:1]
    posm = [incl[e * ns:(e + 1) * ns, :] - 1.0 + carry[e * ns:(e + 1) * ns, :] + base[e:e + 1, :]
            for e in range(N_EXPERTS)]
    last = [(incl[(e + 1) * ns - 1:(e + 1) * ns, LANES - 1:LANES] + carry[(e + 1) * ns - 1:(e + 1) * ns, :])
            for e in range(N_EXPERTS)]
    cnt_ref[...] = jnp.broadcast_to(base + jnp.concatenate(last, axis=0), cnt_ref.shape)

    gnum = [pf[e] * probs[e] for e in range(N_EXPERTS)]
    gden = functools.reduce(lambda a, b: a + b, gnum)
    zero = jnp.zeros_like(pf[0])
    seen = zero
    e_out = [zero, zero]
    p_out = [zero, zero]
    g_out = [zero, zero]
    for e in range(N_EXPERTS):
        for k in range(TOP_K):
            hit = pf[e] * (seen == float(k)).astype(_F32)
            e_out[k] = e_out[k] + hit * float(e)
            p_out[k] = p_out[k] + hit * posm[e]
            g_out[k] = g_out[k] + hit * gnum[e]
        seen = seen + pf[e]
    for k in range(TOP_K):
        e_ref[k] = e_out[k].astype(jnp.int32)
        pos_ref[k] = p_out[k].astype(jnp.int32)
        gate_ref[k] = g_out[k] / gden


def _route(x, g, mod, i_shift, i_scale, wr_t, br):
    B, S, D = x.shape
    tr = min(ROUTE_TILE, S)
    ns = tr // LANES
    assert ns & (ns - 1) == 0
    nb = S // tr
    E = N_EXPERTS
    tok = lambda dt: jax.ShapeDtypeStruct((B * nb, TOP_K, ns, LANES), dt)
    tok_spec = pl.BlockSpec((None, TOP_K, ns, LANES), lambda b, s: (b * nb + s, 0, 0, 0))
    eidx, pos, gate, counts = pl.pallas_call(
        _route_kernel,
        grid=(B, nb),
        in_specs=[
            pl.BlockSpec((None, tr, D), lambda b, s: (b, s, 0)),
            pl.BlockSpec((1, D), lambda b, s: (0, 0)),
            pl.BlockSpec((None, None, 1, D), lambda b, s: (i_scale, b, 0, 0)),
            pl.BlockSpec((None, None, 1, D), lambda b, s: (i_shift, b, 0, 0)),
            pl.BlockSpec((E, D), lambda b, s: (0, 0)),
            pl.BlockSpec((E, 1), lambda b, s: (0, 0)),
        ],
        out_specs=[
            tok_spec, tok_spec, tok_spec,
            pl.BlockSpec((E, LANES), lambda b, s: (0, 0)),
        ],
        out_shape=[
            tok(jnp.int32), tok(jnp.int32), tok(_F32),
            jax.ShapeDtypeStruct((E, LANES), _F32),
        ],
        compiler_params=_params(("arbitrary", "arbitrary")),
        name="moe_route",
    )(x, g.reshape(1, D), mod, mod, wr_t, br)
    flat = lambda a: a.reshape(B * nb, TOP_K, tr)
    return flat(eidx), flat(pos), flat(gate), counts


def _dispatch_kernel(pad_start_ref, pad_len_ref, nu_ref, dest_ref, x_ref, g_ref, sc_ref, sh_ref,
                     buf_ref, h_ref, zblk, sem):
    tr = x_ref.shape[0]
    h_ref[...] = _rms_mod(x_ref[...], g_ref[...], sc_ref[...], sh_ref[...])

    for r in range(tr):
        for k in range(TOP_K):
            pltpu.make_async_copy(h_ref.at[pl.ds(r, 1), :],
                                  buf_ref.at[pl.ds(dest_ref[k, r], 1), :], sem).start(priority=k)

    def row_wait():
        pltpu.make_async_copy(h_ref.at[pl.ds(0, 1), :], buf_ref.at[pl.ds(0, 1), :], sem).wait()

    for _ in range(tr * TOP_K):
        row_wait()

    @pl.when(pl.program_id(0) == 0)
    def _():
        bm = zblk.shape[0]
        nblk = buf_ref.shape[0] // bm
        zblk[...] = jnp.zeros_like(zblk)
        for e in range(N_EXPERTS):
            def zissue(r, carry, e=e):
                pltpu.make_async_copy(zblk.at[pl.ds(0, 1), :],
                                      buf_ref.at[pl.ds(pad_start_ref[e] + r, 1), :], sem).start()
                return carry
            lax.fori_loop(0, pad_len_ref[e], zissue, 0)
        for e in range(N_EXPERTS):
            def zdrain(r, carry):
                row_wait()
                return carry
            lax.fori_loop(0, pad_len_ref[e], zdrain, 0)

        def blk_copy(i):
            return pltpu.make_async_copy(zblk, buf_ref.at[pl.ds(pl.multiple_of(i * bm, bm), bm), :], sem)

        def bissue(i, carry):
            blk_copy(i).start()
            return carry

        def bdrain(i, carry):
            blk_copy(i).wait()
            return carry

        lax.fori_loop(nu_ref[0], nblk, bissue, 0)
        lax.fori_loop(nu_ref[0], nblk, bdrain, 0)


def _dispatch(x, g, mod, i_shift, i_scale, dest, pad_start, pad_len, n_used, rows, bm):
    B, S, D = x.shape
    nbt, _, tr = dest.shape
    nb = S // tr
    grid_spec = pltpu.PrefetchScalarGridSpec(
        num_scalar_prefetch=3,
        grid=(nbt,),
        in_specs=[
            pl.BlockSpec((None, TOP_K, tr), lambda i, *_: (i, 0, 0), memory_space=pltpu.SMEM),
            pl.BlockSpec((None, tr, D), lambda i, *_: (i // nb, i % nb, 0)),
            pl.BlockSpec((1, D), lambda i, *_: (0, 0)),
            pl.BlockSpec((None, None, 1, D), lambda i, *_: (i_scale, i // nb, 0, 0)),
            pl.BlockSpec((None, None, 1, D), lambda i, *_: (i_shift, i // nb, 0, 0)),
        ],
        out_specs=pl.BlockSpec(memory_space=pl.ANY),
        scratch_shapes=[pltpu.VMEM((tr, D), _F32), pltpu.VMEM((bm, D), _F32), pltpu.SemaphoreType.DMA(())],
    )
    return pl.pallas_call(
        _dispatch_kernel,
        grid_spec=grid_spec,
        out_shape=jax.ShapeDtypeStruct((rows, D), _F32),
        compiler_params=_params(("arbitrary",)),
        name="moe_dispatch",
    )(pad_start, pad_len, n_used, dest, x, g.reshape(1, D), mod, mod)


def _expert_kernel(be_ref, nx_ref, nu_ref, x_ref, wg_hbm, wu_hbm, wd_hbm, o_ref,
                   wgb, wub, wdb, sg, su, sd, sem, *, layer):
    i = pl.program_id(0)
    e = be_ref[i]
    live = i < nu_ref[0]
    new_expert = (i == 0) | (e != be_ref[jnp.maximum(i - 1, 0)])

    def weight_copies(ex):
        return [pltpu.make_async_copy(w.at[layer, ex], s, sem.at[n])
                for n, (w, s) in enumerate(((wg_hbm, sg), (wu_hbm, su), (wd_hbm, sd)))]

    @pl.when(live & new_expert)
    def _():
        @pl.when(i == 0)
        def _():
            for cp in weight_copies(e):
                cp.start()

        for cp in weight_copies(e):
            cp.wait()
        wgb[...] = sg[...].astype(_BF16)
        wub[...] = su[...].astype(_BF16)
        wdb[...] = sd[...].astype(_BF16)

        @pl.when(nx_ref[i] >= 0)
        def _():
            for cp in weight_copies(nx_ref[i]):
                cp.start()

    @pl.when(live)
    def _():
        xb = x_ref[...].astype(_BF16)
        a = jnp.dot(xb, wgb[...], preferred_element_type=_F32)
        u = jnp.dot(xb, wub[...], preferred_element_type=_F32)
        act = (a * _sigmoid(a) * u).astype(_BF16)
        o_ref[...] = jnp.dot(act, wdb[...], preferred_element_type=_F32)

    @pl.when(jnp.logical_not(live))
    def _():
        o_ref[...] = jnp.zeros_like(o_ref)


def _experts(buf, block_expert, next_expert, n_used, wg, wu, wd, layer, bm):
    rows, D = buf.shape
    De = wg.shape[-1]
    nblk = rows // bm
    grid_spec = pltpu.PrefetchScalarGridSpec(
        num_scalar_prefetch=3,
        grid=(nblk,),
        in_specs=[
            pl.BlockSpec((bm, D), lambda i, be, nx, nu: (jnp.minimum(i, jnp.maximum(nu[0] - 1, 0)), 0)),
            pl.BlockSpec(memory_space=pl.ANY),
            pl.BlockSpec(memory_space=pl.ANY),
            pl.BlockSpec(memory_space=pl.ANY),
        ],
        out_specs=pl.BlockSpec((bm, D), lambda i, be, nx, nu: (i, 0)),
        scratch_shapes=[
            pltpu.VMEM((D, De), _BF16), pltpu.VMEM((D, De), _BF16), pltpu.VMEM((De, D), _BF16),
            pltpu.VMEM((D, De), _F32), pltpu.VMEM((D, De), _F32), pltpu.VMEM((De, D), _F32),
            pltpu.SemaphoreType.DMA((3,)),
        ],
    )
    return pl.pallas_call(
        functools.partial(_expert_kernel, layer=layer),
        grid_spec=grid_spec,
        out_shape=jax.ShapeDtypeStruct((rows, D), _F32),
        compiler_params=_params(("arbitrary",), vmem=60 * 1024 * 1024),
        name="moe_experts",
    )(block_expert, next_expert, n_used, buf, wg, wu, wd)


ROW_CHUNK = 8


def _combine_kernel(src_ref, nch_ref, dst_ref, tot_ref,
                    x_ref, y_hbm, srow_ref, grow_ref, scol_ref, gm_ref, nf_ref, gn_ref, scn_ref, shn_ref,
                    *rest, final, nb, has_next):
    if has_next:
        o_ref, h_ref, ystage, ysb, sem = rest
    else:
        o_ref, ystage, ysb, sem = rest
        h_ref = None
    i = pl.program_id(0) * nb + pl.program_id(1)
    n_tiles = pl.num_programs(0) * nb
    cur = lax.rem(i, 2)
    tc = x_ref.shape[0]
    R = ystage.shape[1]

    def chunk_copy(src, dst, buf):
        return pltpu.make_async_copy(y_hbm.at[pl.ds(src, ROW_CHUNK), :],
                                     ystage.at[buf, pl.ds(dst, ROW_CHUNK), :], sem.at[buf])

    def start_tile(tile, buf):
        for e in range(N_EXPERTS):
            idx = tile * N_EXPERTS + e

            def body(c, carry, idx=idx):
                chunk_copy(pl.multiple_of(src_ref[idx] + c * ROW_CHUNK, ROW_CHUNK),
                           pl.multiple_of(dst_ref[idx] + c * ROW_CHUNK, ROW_CHUNK), buf).start()
                return carry

            lax.fori_loop(0, nch_ref[idx], body, 0)

    @pl.when(i == 0)
    def _():
        ystage[...] = jnp.zeros_like(ystage)
        start_tile(0, 0)

    @pl.when(i + 1 < n_tiles)
    def _():
        start_tile(i + 1, 1 - cur)

    def wait_one(c, carry):
        chunk_copy(0, 0, cur).wait()
        return carry

    lax.fori_loop(0, tot_ref[i], wait_one, 0)

    srow = srow_ref[...]
    grow = grow_ref[...]
    r_io = lax.broadcasted_iota(jnp.int32, (R, tc), 0)
    gate_of_row = jnp.sum(jnp.where(r_io == srow[0:1, :], grow[0:1, :], 0.0)
                          + jnp.where(r_io == srow[1:2, :], grow[1:2, :], 0.0),
                          axis=1, keepdims=True)
    step = LANES
    for r0 in range(0, R, step):
        ysb[r0:r0 + step, :] = (ystage[cur, r0:r0 + step, :] * gate_of_row[r0:r0 + step, :]).astype(_BF16)
    scol = scol_ref[...]
    c_io = lax.broadcasted_iota(jnp.int32, (tc, R), 1)
    g = jnp.where((c_io == scol[:, 0:1]) | (c_io == scol[:, 1:2]), 1.0, 0.0).astype(_BF16)
    moe = jnp.dot(g, ysb[...], preferred_element_type=_F32)
    xn = x_ref[...] + gm_ref[...] * moe
    if final:
        ms = jnp.mean(xn * xn, axis=-1, keepdims=True)
        xn = xn * lax.rsqrt(ms + EPS) * nf_ref[...]
    o_ref[...] = xn
    if has_next:
        h_ref[...] = _rms_mod(xn, gn_ref[...], scn_ref[...], shn_ref[...]).astype(h_ref.dtype)


def _combine(x, y, src, nch, dst, tot, slot_row, gate_row, slot_col, mod, i_gate, nf, final, g_next, mod_next):
    B, S, D = x.shape
    n_tiles, _, tc = slot_row.shape
    nb = S // tc
    R = _stage_rows(tc)
    has_next = g_next is not None
    if not has_next:
        g_next, mod_next = nf, mod
    tile = lambda b, s, *_: (b * nb + s, 0, 0)
    row = lambda b, s, *_: (b, s, 0)
    grid_spec = pltpu.PrefetchScalarGridSpec(
        num_scalar_prefetch=4,
        grid=(B, nb),
        in_specs=[
            pl.BlockSpec((None, tc, D), row),
            pl.BlockSpec(memory_space=pl.ANY),
            pl.BlockSpec((None, TOP_K, tc), tile),
            pl.BlockSpec((None, TOP_K, tc), tile),
            pl.BlockSpec((None, tc, TOP_K), tile),
            pl.BlockSpec((None, None, 1, D), lambda b, s, *_: (i_gate, b, 0, 0)),
            pl.BlockSpec((1, D), lambda b, s, *_: (0, 0)),
            pl.BlockSpec((1, D), lambda b, s, *_: (0, 0)),
            pl.BlockSpec((None, None, 1, D), lambda b, s, *_: (1, b, 0, 0)),
            pl.BlockSpec((None, None, 1, D), lambda b, s, *_: (0, b, 0, 0)),
        ],
        out_specs=([pl.BlockSpec((None, tc, D), row)] * 2) if has_next else pl.BlockSpec((None, tc, D), row),
        scratch_shapes=[pltpu.VMEM((2, R, D), _F32), pltpu.VMEM((R, D), _BF16),
                        pltpu.SemaphoreType.DMA((2,))],
    )
    out_shape = jax.ShapeDtypeStruct((B, S, D), _F32)
    if has_next:
        out_shape = [out_shape, jax.ShapeDtypeStruct((B, S, D), _BF16)]
    return pl.pallas_call(
        functools.partial(_combine_kernel, final=final, nb=nb, has_next=has_next),
        grid_spec=grid_spec,
        out_shape=out_shape,
        compiler_params=_params(("arbitrary", "arbitrary")),
        name="moe_combine",
    )(src, nch, dst, tot, x, y, slot_row, gate_row, slot_col, mod, nf.reshape(1, D), g_next.reshape(1, D),
      mod_next, mod_next)


def _stage_rows(tc):
    worst = TOP_K * tc + 2 * (ROW_CHUNK - 1) * N_EXPERTS
    return -(-worst // LANES) * LANES


def _combine_plan(eidx, dest, gate, pstarts, tc):
    T = eidx.shape[0] * eidx.shape[2]
    n_tiles = T // tc
    flat = lambda a: a.transpose(0, 2, 1).reshape(n_tiles, tc, TOP_K)
    e_f, d_f, g_f = flat(eidx), flat(dest), flat(gate)
    onehot = (e_f[..., None] == jnp.arange(N_EXPERTS, dtype=jnp.int32)).astype(jnp.int32)
    cnt = jnp.sum(onehot, axis=(1, 2))
    first = pstarts[None, :] + jnp.cumsum(cnt, axis=0) - cnt
    src = first // ROW_CHUNK * ROW_CHUNK
    nch = jnp.where(cnt > 0, (first + cnt - src + ROW_CHUNK - 1) // ROW_CHUNK, 0)
    dst = ROW_CHUNK * (jnp.cumsum(nch, axis=1) - nch)
    slot = d_f + jnp.sum(onehot * (dst - src)[:, None, None, :], axis=-1)
    return (src.reshape(-1), nch.reshape(-1), dst.reshape(-1), jnp.sum(nch, axis=1),
            slot.transpose(0, 2, 1), g_f.transpose(0, 2, 1), slot)


def kernel(x, c, w_ada, b_ada, norm_mix, norm_moe, w_in, b_forget, w_pool, pool_scale, w_branch,
           w_gate, b_gate, w_out, w_router, b_router, w_exp_gate, w_exp_up, w_exp_down, norm_final):
    B, S, D = x.shape
    L = w_ada.shape[0]
    W = D // 2
    nh = W // HEAD_DIM
    T = B * S
    A = T * TOP_K
    bm = min(256, A // N_EXPERTS)
    rows = (A // bm + N_EXPERTS) * bm

    mod_all = _ada(c, w_ada, b_ada)
    mod_all = mod_all.reshape(L, B, N_MOD, D).transpose(0, 2, 1, 3)[:, :, :, None, :]
    wr_t = w_router.T
    br = b_router.reshape(N_EXPERTS, 1)
    colscale = jnp.concatenate([jnp.ones((W,), _F32), jnp.full((W,), HEAD_DIM ** -0.5 * LOG2E, _F32),
                                jnp.ones((2 * W,), _F32)]).reshape(1, 4 * W)

    h = _norm_mod(x, norm_mix[0], mod_all[0], 0, 1)
    for l in range(L):
        mod = mod_all[l]
        wf =jnp.zeros((D, LANES), _F32).at[:, :nh].set(w_in[l][:, 4 * W:]).astype(_BF16)
        bf = jnp.zeros((1, LANES), _F32).at[0, :nh].set(b_forget[l])
        proj5 = _proj(h, w_in, l, colscale)
        fcum = _forget(h, wf, bf)
        pool_out = _pool(proj5, w_pool, l, pool_scale[l].reshape(1, W))
        attn = _attn(proj5, fcum, nh)
        merged = _merge(h, pool_out, attn, w_gate, b_gate, w_branch, l)
        x = _out_proj(x, merged, w_out, l, mod, 2)

        eidx, pos, gate, counts = _route(x, norm_moe[l], mod, 3, 4, wr_t, br)
        counts = counts[:, 0].astype(jnp.int32)
        pcounts = (counts + bm - 1) // bm * bm
        pends = jnp.cumsum(pcounts)
        pstarts = pends - pcounts
        dest = pos
        for e in range(N_EXPERTS):
            dest = dest + jnp.where(eidx == e, pstarts[e], 0)
        n_used = (pends[-1] // bm).astype(jnp.int32)
        blk_ids = jnp.arange(rows // bm, dtype=jnp.int32)
        blk_ids = jnp.minimum(blk_ids, n_used - 1)
        block_expert = jnp.sum((pends[None, :] <= (blk_ids * bm)[:, None]).astype(jnp.int32), axis=1)
        block_expert = jnp.minimum(block_expert, N_EXPERTS - 1)
        ids = jnp.arange(N_EXPERTS, dtype=jnp.int32)
        later = (ids[None, :] > ids[:, None]) & (pcounts[None, :] > 0)
        next_of = jnp.min(jnp.where(later, ids[None, :], N_EXPERTS), axis=1)
        next_of = jnp.where(next_of == N_EXPERTS, -1, next_of)
        next_expert = jnp.sum(jnp.where(block_expert[:, None] == ids[None, :], next_of[None, :], 0), axis=1)
        n_used = n_used.reshape(1)
        td = min(DISPATCH_TILE, S)
        dest_d = dest.transpose(0, 2, 1).reshape(T // td, td, TOP_K).transpose(0, 2, 1)
        buf = _dispatch(x, norm_moe[l], mod, 3, 4, dest_d, pstarts + counts, pcounts - counts,
                        n_used, rows, bm)
        y = _experts(buf, block_expert, next_expert.astype(jnp.int32), n_used,
                     w_exp_gate, w_exp_up, w_exp_down, l, bm)
        plan = _combine_plan(eidx, dest, gate, pstarts, min(COMBINE_TILE, S))
        if l + 1 < L:
            x, h = _combine(x, y, *plan, mod, 5, norm_final, False, norm_mix[l + 1], mod_all[l + 1])
        else:
            x = _combine(x, y, *plan, mod, 5, norm_final, True, None, None)
    return x
```

```python
import functools

import jax
import jax.numpy as jnp
from jax import lax
from jax.experimental import pallas as pl
from jax.experimental.pallas import tpu as pltpu

N_MOD = 6
EPS = 1e-6
POOL_WINDOWS = (2, 4, 8, 16)
HEAD_DIM = 128
N_EXPERTS = 16
N_EXPERT_GROUPS = 4
EXPERTS_PER_GROUP = N_EXPERTS // N_EXPERT_GROUPS
TOP_K = 2
LOG2E = 1.4426950408889634
ROUTE_TILE = 1024
DISPATCH_TILE = 512
COMBINE_TILE = 256
LANES = 128
VMEM_LIMIT = 56 * 1024 * 1024

_F32 = jnp.float32
_BF16 = jnp.bfloat16


def _params(sem, vmem=VMEM_LIMIT):
    return pltpu.CompilerParams(dimension_semantics=sem, vmem_limit_bytes=vmem)


def _sigmoid(v):
    return 1.0 / (1.0 + jnp.exp(-v))


def _rms_mod(x, g, scale, shift):
    ms = jnp.mean(x * x, axis=-1, keepdims=True)
    return x * lax.rsqrt(ms + EPS) * g * (1.0 + scale) + shift


def _ada_kernel(c_ref, w_ref, b_ref, o_ref):
    c = c_ref[...]
    ca = (c * _sigmoid(c)).astype(_BF16)
    o_ref[...] = jnp.dot(ca, w_ref[...].astype(_BF16), preferred_element_type=_F32) + b_ref[...]


def _ada(c, w_ada, b_ada):
    L, D, M = w_ada.shape
    B = c.shape[0]
    tn = min(1024, M)
    return pl.pallas_call(
        _ada_kernel,
        grid=(L, M // tn),
        in_specs=[
            pl.BlockSpec((B, D), lambda l, n: (0, 0)),
            pl.BlockSpec((None, D, tn), lambda l, n: (l, 0, n)),
            pl.BlockSpec((None, 1, tn), lambda l, n: (l, 0, n)),
        ],
        out_specs=pl.BlockSpec((None, B, tn), lambda l, n: (l, 0, n)),
        out_shape=jax.ShapeDtypeStruct((L, B, M), _F32),
        compiler_params=_params(("arbitrary", "arbitrary")),
        name="ada_mod",
    )(c, w_ada, b_ada.reshape(L, 1, M))


def _normmod_kernel(x_ref, g_ref, sc_ref, sh_ref, h_ref):
    h_ref[...] = _rms_mod(x_ref[...], g_ref[...], sc_ref[...], sh_ref[...]).astype(h_ref.dtype)


def _norm_mod(x, g, mod, i_shift, i_scale):
    B, S, D = x.shape
    ts = min(512, S)
    return pl.pallas_call(
        _normmod_kernel,
        grid=(B, S // ts),
        in_specs=[
            pl.BlockSpec((None, ts, D), lambda b, s: (b, s, 0)),
            pl.BlockSpec((1, D), lambda b, s: (0, 0)),
            pl.BlockSpec((None, None, 1, D), lambda b, s: (i_scale, b, 0, 0)),
            pl.BlockSpec((None, None, 1, D), lambda b, s: (i_shift, b, 0, 0)),
        ],
        out_specs=pl.BlockSpec((None, ts, D), lambda b, s: (b, s, 0)),
        out_shape=jax.ShapeDtypeStruct((B, S, D), _BF16),
        compiler_params=_params(("arbitrary", "arbitrary")),
        name="norm_mod",
    )(x, g.reshape(1, D), mod, mod)


def _first_inner_step():
    return (pl.program_id(1) == 0) & (pl.program_id(2) == 0)


def _proj_kernel(h_ref, w_ref, cs_ref, o_ref, wb_ref):
    @pl.when(_first_inner_step())
    def _():
        wb_ref[...] = (w_ref[...].T * cs_ref[...]).astype(_BF16)

    acc = jnp.dot(h_ref[...], wb_ref[...], preferred_element_type=_F32)
    for j in range(o_ref.shape[0]):
        o_ref[j] = acc[:, j * LANES:(j + 1) * LANES].astype(o_ref.dtype)


def _proj(h, w_in_t, layer, colscale):
    B, S, D = h.shape
    N = colscale.shape[1]
    tn = min(1024, N)
    tm = min(1024, S)
    return pl.pallas_call(
        _proj_kernel,
        grid=(N // tn, B, S // tm),
        in_specs=[
            pl.BlockSpec((None, tm, D), lambda n, b, m: (b, m, 0)),
            pl.BlockSpec((None, tn, D), lambda n, b, m: (layer, n, 0)),
            pl.BlockSpec((1, tn), lambda n, b, m: (0, n)),
        ],
        out_specs=pl.BlockSpec((None, tn // LANES, tm, LANES), lambda n, b, m: (b, n, m, 0)),
        out_shape=jax.ShapeDtypeStruct((B, N // LANES, S, LANES), _BF16),
        scratch_shapes=[pltpu.VMEM((D, tn), _BF16)],
        compiler_params=_params(("arbitrary", "arbitrary", "arbitrary")),
        name="in_proj",
    )(h, w_in_t, colscale)


def _forget_kernel(h_ref, w_ref, b_ref, o_ref):
    S = h_ref.shape[0]
    fl = jnp.dot(h_ref[...], w_ref[...], preferred_element_type=_F32) + b_ref[...]
    acc = jnp.minimum(fl, 0.0) - jnp.log(1.0 + jnp.exp(-jnp.abs(fl)))
    row = lax.broadcasted_iota(jnp.int32, acc.shape, 0)
    sh = 1
    while sh < S:
        acc = acc + jnp.where(row >= sh, pltpu.roll(acc, sh, axis=0), 0.0)
        sh *= 2
    o_ref[...] = acc


def _forget(h, wf, bf):
    B, S, D = h.shape
    return pl.pallas_call(
        _forget_kernel,
        grid=(B,),
        in_specs=[
            pl.BlockSpec((None, S, D), lambda b: (b, 0, 0)),
            pl.BlockSpec((D, LANES), lambda b: (0, 0)),
            pl.BlockSpec((1, LANES), lambda b: (0, 0)),
        ],
        out_specs=pl.BlockSpec((None, S, LANES), lambda b: (b, 0, 0)),
        out_shape=jax.ShapeDtypeStruct((B, S, LANES), _F32),
        compiler_params=_params(("arbitrary",)),
        name="forget_cumsum",
    )(h, wf, bf)


def _pool_kernel(u_ref, wp_ref, ps_ref, o_ref, *, cpg):
    S = u_ref.shape[1]
    gw = cpg * LANES
    row = lax.broadcasted_iota(jnp.int32, (S, gw), 0)
    for g, w in enumerate(POOL_WINDOWS):
        parts = [u_ref[g * cpg + j] for j in range(cpg)]
        u = (parts[0] if cpg == 1 else jnp.concatenate(parts, axis=-1)).astype(_F32)
        s = u
        sh = 1
        while sh < w:
            s = s + jnp.where(row >= sh, pltpu.roll(s, sh, axis=0), 0.0)
            sh *= 2
        cnt = jnp.minimum(row + 1, w).astype(_F32)
        pooled = s / cnt - u
        mixed = jnp.dot(pooled.astype(_BF16), wp_ref[g].astype(_BF16), preferred_element_type=_F32)
        o_ref[:, g * gw:(g + 1) * gw] = (mixed * ps_ref[:, g * gw:(g + 1) * gw]).astype(o_ref.dtype)


def _pool(proj5, w_pool, layer, ps):
    B, _, S, _ = proj5.shape
    _, G, gw, _ = w_pool.shape
    cpg = gw // LANES
    W = G * gw
    return pl.pallas_call(
        functools.partial(_pool_kernel, cpg=cpg),
        grid=(B,),
        in_specs=[
            pl.BlockSpec((None, G * cpg, S, LANES), lambda b: (b, 0, 0, 0)),
            pl.BlockSpec((None, G, gw, gw), lambda b: (layer, 0, 0, 0)),
            pl.BlockSpec((1, W), lambda b: (0, 0)),
        ],
        out_specs=pl.BlockSpec((None, S, W), lambda b: (b, 0, 0)),
        out_shape=jax.ShapeDtypeStruct((B, S, W), _BF16),
        compiler_params=_params(("arbitrary",)),
        name="pool_mixer",
    )(proj5, w_pool, ps)


def _attn_kernel(q_ref, k_ref, v_ref, f_ref, o_ref, qa_ref, ka_ref, va_ref, *, blk):
    S = q_ref.shape[0]
    h = pl.program_id(1)
    lane = lax.broadcasted_iota(jnp.int32, (S, LANES), 1)
    f = jnp.sum(jnp.where(lane == h, f_ref[...], 0.0), axis=-1, keepdims=True) * LOG2E
    f1 = f.astype(_BF16).astype(_F32)
    r1 = f - f1
    f2 = r1.astype(_BF16).astype(_F32)
    f3 = (r1 - f2).astype(_BF16).astype(_F32)
    qa_ref[:, :HEAD_DIM] = q_ref[...]
    qa_ref[:, HEAD_DIM:] = jnp.where(lane == 0, f1, jnp.where(lane == 1, f2, jnp.where(
        lane == 2, f3, jnp.where(lane < 6, 1.0, 0.0)))).astype(_BF16)
    ka_ref[:, :HEAD_DIM] = k_ref[...]
    ka_ref[:, HEAD_DIM:] = jnp.where(lane < 3, 1.0, jnp.where(lane == 3, -f1, jnp.where(
        lane == 4, -f2, jnp.where(lane == 5, -f3, 0.0)))).astype(_BF16)
    va_ref[:, :HEAD_DIM] = v_ref[...]
    va_ref[:, HEAD_DIM:] = jnp.where(lane == 0, 1.0, 0.0).astype(_BF16)

    nt = (((1,), (1,)), ((), ()))
    r_io = lax.broadcasted_iota(jnp.int32, (blk, blk), 0)
    c_io = lax.broadcasted_iota(jnp.int32, (blk, blk), 1)
    causal = c_io <= r_io
    nb = S // blk
    m = [jnp.full((blk, 1), -jnp.inf, _F32)] * nb
    acc = [jnp.zeros((blk, 2 * HEAD_DIM), _F32)] * nb
    for j in range(nb):
        ka = ka_ref[j * blk:(j + 1) * blk, :]
        va = va_ref[j * blk:(j + 1) * blk, :]
        for i in range(j, nb):
            s = lax.dot_general(qa_ref[i * blk:(i + 1) * blk, :], ka, nt, preferred_element_type=_F32)
            if i == j:
                s = jnp.where(causal, s, -jnp.inf)
            m_new = jnp.maximum(m[i], jnp.max(s, axis=-1, keepdims=True))
            p = jnp.exp2(s - m_new).astype(_BF16)
            acc[i] = jnp.exp2(m[i] - m_new) * acc[i] + jnp.dot(p, va, preferred_element_type=_F32)
            m[i] = m_new
    for i in range(nb):
        o_ref[i * blk:(i + 1) * blk, :] = (
            acc[i][:, :HEAD_DIM] / acc[i][:, HEAD_DIM:HEAD_DIM + 1]).astype(o_ref.dtype)


def _attn(proj5, fcum, nh):
    B, _, S, _ = proj5.shape
    blk = min(512, S)
    chunk = lambda o: pl.BlockSpec((None, None, S, HEAD_DIM), lambda b, h: (b, o + h, 0, 0))
    return pl.pallas_call(
        functools.partial(_attn_kernel, blk=blk),
        grid=(B, nh),
        in_specs=[
            chunk(nh), chunk(2 * nh), chunk(3 * nh),
            pl.BlockSpec((None, S, LANES), lambda b, h: (b, 0, 0)),
        ],
        out_specs=pl.BlockSpec((None, None, S, HEAD_DIM), lambda b, h: (b, h, 0, 0)),
        out_shape=jax.ShapeDtypeStruct((B, nh, S, HEAD_DIM), _BF16),
        scratch_shapes=[pltpu.VMEM((S, 2 * HEAD_DIM), _BF16)] * 3,
        compiler_params=_params(("arbitrary", "arbitrary")),
        name="forget_attn",
    )(proj5, proj5, proj5, fcum)


def _merge_kernel(h_ref, p_ref, a_ref, wg0_ref, wg1_ref, bg0_ref, bg1_ref, wb0_ref, wb1_ref, o_ref,
                  cg0_ref, cg1_ref, cb0_ref, cb1_ref):
    @pl.when(_first_inner_step())
    def _():
        cg0_ref[...] = wg0_ref[...].astype(_BF16)
        cg1_ref[...] = wg1_ref[...].astype(_BF16)
        cb0_ref[...] = wb0_ref[...].astype(_BF16)
        cb1_ref[...] = wb1_ref[...].astype(_BF16)

    h = h_ref[...]
    nh = a_ref.shape[0]
    a = jnp.concatenate([a_ref[j] for j in range(nh)], axis=-1)
    g0 = _sigmoid(jnp.dot(h, cg0_ref[...], preferred_element_type=_F32) + bg0_ref[...])
    y0 = jnp.dot(p_ref[...], cb0_ref[...], preferred_element_type=_F32)
    acc = g0 * y0
    g1 = _sigmoid(jnp.dot(h, cg1_ref[...], preferred_element_type=_F32) + bg1_ref[...])
    y1 = jnp.dot(a, cb1_ref[...], preferred_element_type=_F32)
    o_ref[...] = (acc + g1 * y1).astype(o_ref.dtype)


def _merge(h, pool_out, attn, w_gate, b_gate, w_branch, layer):
    B, S, D = h.shape
    W = pool_out.shape[-1]
    nh = attn.shape[1]
    L = w_gate.shape[0]
    tn = min(512, D)
    tm = min(512, S)
    nn = D // tn
    bg = b_gate.reshape(L, 1, 2 * D)
    return pl.pallas_call(
        _merge_kernel,
        grid=(nn, B, S // tm),
        in_specs=[
            pl.BlockSpec((None, tm, D), lambda n, b, m: (b, m, 0)),
            pl.BlockSpec((None, tm, W), lambda n, b, m: (b, m, 0)),
            pl.BlockSpec((None, nh, tm, HEAD_DIM), lambda n, b, m: (b, 0, m, 0)),
            pl.BlockSpec((None, D, tn), lambda n, b, m: (layer, 0, n)),
            pl.BlockSpec((None, D, tn), lambda n, b, m: (layer, 0, nn + n)),
            pl.BlockSpec((None, 1, tn), lambda n, b, m: (layer, 0, n)),
            pl.BlockSpec((None, 1, tn), lambda n, b, m: (layer, 0, nn + n)),
            pl.BlockSpec((None, None, W, tn), lambda n, b, m: (layer, 0, 0, n)),
            pl.BlockSpec((None, None, W, tn), lambda n, b, m: (layer, 1, 0, n)),
        ],
        out_specs=pl.BlockSpec((None, tm, tn), lambda n, b, m: (b, m, n)),
        out_shape=jax.ShapeDtypeStruct((B, S, D), _BF16),
        scratch_shapes=[pltpu.VMEM((D, tn), _BF16), pltpu.VMEM((D, tn), _BF16),
                        pltpu.VMEM((W, tn), _BF16), pltpu.VMEM((W, tn), _BF16)],
        compiler_params=_params(("arbitrary", "arbitrary", "arbitrary")),
        name="branch_merge",
    )(h, pool_out, attn, w_gate, w_gate, bg, bg, w_branch, w_branch)


def _out_kernel(x_ref, m_ref, w_ref, g_ref, o_ref, wb_ref):
    @pl.when(_first_inner_step())
    def _():
        wb_ref[...] = w_ref[...].astype(_BF16)

    y = jnp.dot(m_ref[...], wb_ref[...], preferred_element_type=_F32)
    o_ref[...] = x_ref[...] + g_ref[...] * y


def _out_proj(x, merged, w_out, layer, mod, i_gate):
    B, S, D = x.shape
    tm = min(1024, S)
    tn = min(1024, D)
    return pl.pallas_call(
        _out_kernel,
        grid=(D // tn, B, S // tm),
        in_specs=[
            pl.BlockSpec((None, tm, tn), lambda n, b, m: (b, m, n)),
            pl.BlockSpec((None, tm, D), lambda n, b, m: (b, m, 0)),
            pl.BlockSpec((None, D, tn), lambda n, b, m: (layer, 0, n)),
            pl.BlockSpec((None, None, 1, tn), lambda n, b, m: (i_gate, b, 0, n)),
        ],
        out_specs=pl.BlockSpec((None, tm, tn), lambda n, b, m: (b, m, n)),
        out_shape=jax.ShapeDtypeStruct((B, S, D), _F32),
        scratch_shapes=[pltpu.VMEM((D, tn), _BF16)],
        compiler_params=_params(("arbitrary", "arbitrary", "arbitrary")),
        name="out_proj",
    )(x, merged, w_out, mod)


def _route_kernel(x_ref, g_ref, sc_ref, sh_ref, wr_ref, br_ref,
                  e_ref, pos_ref, gate_ref, cnt_ref):
    first_step = (pl.program_id(0) == 0) & (pl.program_id(1) == 0)

    @pl.when(first_step)
    def _():
        cnt_ref[...] = jnp.zeros_like(cnt_ref)

    h = _rms_mod(x_ref[...], g_ref[...], sc_ref[...], sh_ref[...])
    tr, D = h.shape
    ns = tr // LANES

    sub = [lax.dot_general(wr_ref[...], h[s * LANES:(s + 1) * LANES, :], (((1,), (1,)), ((), ())),
                           precision=lax.Precision.HIGHEST, preferred_element_type=_F32)
           for s in range(ns)]
    rows = [jnp.concatenate([sub[s][e:e + 1, :] for s in range(ns)], axis=0) for e in range(N_EXPERTS)]
    mx = functools.reduce(jnp.maximum, rows)
    ex = [jnp.exp(r - mx) for r in rows]
    den = functools.reduce(lambda a, b: a + b, ex)
    probs = [v / den for v in ex]
    sel = [probs[e] + br_ref[e:e + 1, :] for e in range(N_EXPERTS)]

    in_top = [None] * N_EXPERTS
    gscore = []
    for g in range(N_EXPERT_GROUPS):
        ids = range(g * EXPERTS_PER_GROUP, (g + 1) * EXPERTS_PER_GROUP)
        score = None
        for i in ids:
            rank = None
            for j in ids:
                if j == i:
                    continue
                beats = (sel[j] > sel[i]) | ((sel[j] == sel[i]) if j < i else False)
                beats = beats.astype(_F32)
                rank = beats if rank is None else rank + beats
            in_top[i] = rank < float(TOP_K)
            term = jnp.where(in_top[i], sel[i], 0.0)
            score = term if score is None else score + term
        gscore.append(score)
    chosen = []
    for g in range(N_EXPERT_GROUPS):
        lose = None
        for g2 in range(N_EXPERT_GROUPS):
            if g2 == g:
                continue
            b = (gscore[g2] > gscore[g]) | ((gscore[g2] == gscore[g]) if g2 < g else False)
            lose = b if lose is None else (lose | b)
        chosen.append(jnp.logical_not(lose))
    picked = [in_top[e] & chosen[e // EXPERTS_PER_GROUP] for e in range(N_EXPERTS)]
    pf = [p.astype(_F32) for p in picked]

    n = N_EXPERTS * ns
    onehot = jnp.concatenate(pf, axis=0).astype(_BF16)
    upper = (lax.broadcasted_iota(jnp.int32, (LANES, LANES), 0)
             <= lax.broadcasted_iota(jnp.int32, (LANES, LANES), 1)).astype(_BF16)
    incl = jnp.dot(onehot, upper, preferred_element_type=_F32)
    r_io = lax.broadcasted_iota(jnp.int32, (n, n), 0)
    c_io = lax.broadcasted_iota(jnp.int32, (n, n), 1)
    shift = ns.bit_length() - 1
    same_expert = lax.shift_right_logical(r_io, shift) == lax.shift_right_logical(c_io, shift)
    earlier = jnp.where(same_expert & (c_io < r_io), 1.0, 0.0).astype(_BF16)
    carry = jnp.dot(earlier, incl.astype(_BF16), preferred_element_type=_F32)[:, LANES - 1:LANES]
    base = cnt_ref[:, 0:1]
    posm = [incl[e * ns:(e + 1) * ns, :] - 1.0 + carry[e * ns:(e + 1) * ns, :] + base[e:e + 1, :]
            for e in range(N_EXPERTS)]
    last = [(incl[(e + 1) * ns - 1:(e + 1) * ns, LANES - 1:LANES] + carry[(e + 1) * ns - 1:(e + 1) * ns, :])
            for e in range(N_EXPERTS)]
    cnt_ref[...] = jnp.broadcast_to(base + jnp.concatenate(last, axis=0), cnt_ref.shape)

    gnum = [pf[e] * probs[e] for e in range(N_EXPERTS)]
    gden = functools.reduce(lambda a, b: a + b, gnum)
    zero = jnp.zeros_like(pf[0])
    seen = zero
    e_out = [zero, zero]
    p_out = [zero, zero]
    g_out = [zero, zero]
    for e in range(N_EXPERTS):
        for k in range(TOP_K):
            hit = pf[e] * (seen == float(k)).astype(_F32)
            e_out[k] = e_out[k] + hit * float(e)
            p_out[k] = p_out[k] + hit * posm[e]
            g_out[k] = g_out[k] + hit * gnum[e]
        seen = seen + pf[e]
    for k in range(TOP_K):
        e_ref[k] = e_out[k].astype(jnp.int32)
        pos_ref[k] = p_out[k].astype(jnp.int32)
        gate_ref[k] = g_out[k] / gden


def _route(x, g, mod, i_shift, i_scale, wr_t, br):
    B, S, D = x.shape
    tr = min(ROUTE_TILE, S)
    ns = tr // LANES
    assert ns & (ns - 1) == 0
    nb = S // tr
    E = N_EXPERTS
    tok = lambda dt: jax.ShapeDtypeStruct((B * nb, TOP_K, ns, LANES), dt)
    tok_spec = pl.BlockSpec((None, TOP_K, ns, LANES), lambda b, s: (b * nb + s, 0, 0, 0))
    eidx, pos, gate, counts = pl.pallas_call(
        _route_kernel,
        grid=(B, nb),
        in_specs=[
            pl.BlockSpec((None, tr, D), lambda b, s: (b, s, 0)),
            pl.BlockSpec((1, D), lambda b, s: (0, 0)),
            pl.BlockSpec((None, None, 1, D), lambda b, s: (i_scale, b, 0, 0)),
            pl.BlockSpec((None, None, 1, D), lambda b, s: (i_shift, b, 0, 0)),
            pl.BlockSpec((E, D), lambda b, s: (0, 0)),
            pl.BlockSpec((E, 1), lambda b, s: (0, 0)),
        ],
        out_specs=[
            tok_spec, tok_spec, tok_spec,
            pl.BlockSpec((E, LANES), lambda b, s: (0, 0)),
        ],
        out_shape=[
            tok(jnp.int32), tok(jnp.int32), tok(_F32),
            jax.ShapeDtypeStruct((E, LANES), _F32),
        ],
        compiler_params=_params(("arbitrary", "arbitrary")),
        name="moe_route",
    )(x, g.reshape(1, D), mod, mod, wr_t, br)
    flat = lambda a: a.reshape(B * nb, TOP_K, tr)
    return flat(eidx), flat(pos), flat(gate), counts


def _dispatch_kernel(pad_start_ref, pad_len_ref, nu_ref, dest_ref, x0_ref, xn_ref, g_ref,
                     sc0_ref, sh0_ref, scn_ref, shn_ref, buf_ref, h_ref, zblk, sem):
    i = pl.program_id(0)
    tr = xn_ref.shape[0]

    @pl.when(i == 0)
    def _():
        h_ref[0] = _rms_mod(x0_ref[...], g_ref[...], sc0_ref[...], sh0_ref[...])

    def row_wait():
        pltpu.make_async_copy(h_ref.at[0, pl.ds(0, 1), :], buf_ref.at[pl.ds(0, 1), :], sem).wait()

    def step(slot):
        for r in range(tr):
            for k in range(TOP_K):
                pltpu.make_async_copy(h_ref.at[slot, pl.ds(r, 1), :],
                                      buf_ref.at[pl.ds(dest_ref[k, r], 1), :], sem).start(priority=k)

        @pl.when(i + 1 < pl.num_programs(0))
        def _():
            h_ref[1 - slot] = _rms_mod(xn_ref[...], g_ref[...], scn_ref[...], shn_ref[...])

        for _ in range(tr * TOP_K):
            row_wait()

    for slot in range(2):
        pl.when(lax.rem(i, 2) == slot)(functools.partial(step, slot))

    @pl.when(pl.program_id(0) == 0)
    def _():
        bm = zblk.shape[0]
        nblk = buf_ref.shape[0] // bm
        zblk[...] = jnp.zeros_like(zblk)
        for e in range(N_EXPERTS):
            def zissue(r, carry, e=e):
                pltpu.make_async_copy(zblk.at[pl.ds(0, 1), :],
                                      buf_ref.at[pl.ds(pad_start_ref[e] + r, 1), :], sem).start()
                return carry
            lax.fori_loop(0, pad_len_ref[e], zissue, 0)
        for e in range(N_EXPERTS):
            def zdrain(r, carry):
                row_wait()
                return carry
            lax.fori_loop(0, pad_len_ref[e], zdrain, 0)

        def blk_copy(i):
            return pltpu.make_async_copy(zblk, buf_ref.at[pl.ds(pl.multiple_of(i * bm, bm), bm), :], sem)

        def bissue(i, carry):
            blk_copy(i).start()
            return carry

        def bdrain(i, carry):
            blk_copy(i).wait()
            return carry

        lax.fori_loop(nu_ref[0], nblk, bissue, 0)
        lax.fori_loop(nu_ref[0], nblk, bdrain, 0)


def _dispatch(x, g, mod, i_shift, i_scale, dest, pad_start, pad_len, n_used, rows, bm):
    B, S, D = x.shape
    nbt, _, tr = dest.shape
    nb = S // tr
    nxt = lambda i: jnp.minimum(i + 1, nbt - 1)
    grid_spec = pltpu.PrefetchScalarGridSpec(
        num_scalar_prefetch=3,
        grid=(nbt,),
        in_specs=[
            pl.BlockSpec((None, TOP_K, tr), lambda i, *_: (i, 0, 0), memory_space=pltpu.SMEM),
            pl.BlockSpec((None, tr, D), lambda i, *_: (0, 0, 0)),
            pl.BlockSpec((None, tr, D), lambda i, *_: (nxt(i) // nb, nxt(i) % nb, 0)),
            pl.BlockSpec((1, D), lambda i, *_: (0, 0)),
            pl.BlockSpec((None, None, 1, D), lambda i, *_: (i_scale, 0, 0, 0)),
            pl.BlockSpec((None, None, 1, D), lambda i, *_: (i_shift, 0, 0, 0)),
            pl.BlockSpec((None, None, 1, D), lambda i, *_: (i_scale, nxt(i) // nb, 0, 0)),
            pl.BlockSpec((None, None, 1, D), lambda i, *_: (i_shift, nxt(i) // nb, 0, 0)),
        ],
        out_specs=pl.BlockSpec(memory_space=pl.ANY),
        scratch_shapes=[pltpu.VMEM((2, tr, D), _F32), pltpu.VMEM((bm, D), _F32),
                        pltpu.SemaphoreType.DMA(())],
    )
    return pl.pallas_call(
        _dispatch_kernel,
        grid_spec=grid_spec,
        out_shape=jax.ShapeDtypeStruct((rows, D), _F32),
        compiler_params=_params(("arbitrary",)),
        name="moe_dispatch",
    )(pad_start, pad_len, n_used, dest, x, x, g.reshape(1, D), mod, mod, mod, mod)


def _expert_kernel(be_ref, nx_ref, nu_ref, x_ref, wg_hbm, wu_hbm, wd_hbm, o_ref,
                   wgb, wub, wdb, sg, su, sd, sem, *, layer):
    i = pl.program_id(0)
    e = be_ref[i]
    live = i < nu_ref[0]
    new_expert = (i == 0) | (e != be_ref[jnp.maximum(i - 1, 0)])

    def weight_copies(ex):
        return [pltpu.make_async_copy(w.at[layer, ex], s, sem.at[n])
                for n, (w, s) in enumerate(((wg_hbm, sg), (wu_hbm, su), (wd_hbm, sd)))]

    @pl.when(live & new_expert)
    def _():
        @pl.when(i == 0)
        def _():
            for cp in weight_copies(e):
                cp.start()

        for cp in weight_copies(e):
            cp.wait()
        wgb[...] = sg[...].astype(_BF16)
        wub[...] = su[...].astype(_BF16)
        wdb[...] = sd[...].astype(_BF16)

        @pl.when(nx_ref[i] >= 0)
        def _():
            for cp in weight_copies(nx_ref[i]):
                cp.start(priority=1)

    @pl.when(live)
    def _():
        xb = x_ref[...].astype(_BF16)
        a = jnp.dot(xb, wgb[...], preferred_element_type=_F32)
        u = jnp.dot(xb, wub[...], preferred_element_type=_F32)
        act = (a * _sigmoid(a) * u).astype(_BF16)
        o_ref[...] = jnp.dot(act, wdb[...], preferred_element_type=_F32)

    @pl.when(jnp.logical_not(live))
    def _():
        o_ref[...] = jnp.zeros_like(o_ref)


def _experts(buf, block_expert, next_expert, n_used, wg, wu, wd, layer, bm):
    rows, D = buf.shape
    De = wg.shape[-1]
    nblk = rows // bm
    grid_spec = pltpu.PrefetchScalarGridSpec(
        num_scalar_prefetch=3,
        grid=(nblk,),
        in_specs=[
            pl.BlockSpec((bm, D), lambda i, be, nx, nu: (jnp.minimum(i, jnp.maximum(nu[0] - 1, 0)), 0)),
            pl.BlockSpec(memory_space=pl.ANY),
            pl.BlockSpec(memory_space=pl.ANY),
            pl.BlockSpec(memory_space=pl.ANY),
        ],
        out_specs=pl.BlockSpec((bm, D), lambda i, be, nx, nu: (i, 0)),
        scratch_shapes=[
            pltpu.VMEM((D, De), _BF16), pltpu.VMEM((D, De), _BF16), pltpu.VMEM((De, D), _BF16),
            pltpu.VMEM((D, De), _F32), pltpu.VMEM((D, De), _F32), pltpu.VMEM((De, D), _F32),
            pltpu.SemaphoreType.DMA((3,)),
        ],
    )
    return pl.pallas_call(
        functools.partial(_expert_kernel, layer=layer),
        grid_spec=grid_spec,
        out_shape=jax.ShapeDtypeStruct((rows, D), _F32),
        compiler_params=_params(("arbitrary",), vmem=60 * 1024 * 1024),
        name="moe_experts",
    )(block_expert, next_expert, n_used, buf, wg, wu, wd)


ROW_CHUNK = 8


def _combine_kernel(src_ref, nch_ref, dst_ref, tot_ref,
                    x_ref, y_hbm, srow_ref, grow_ref, scol_ref, gm_ref, nf_ref, gn_ref, scn_ref, shn_ref,
                    *rest, final, nb, has_next):
    if has_next:
        o_ref, h_ref, ystage, ysb, sem = rest
    else:
        o_ref, ystage, ysb, sem = rest
        h_ref = None
    i = pl.program_id(0) * nb + pl.program_id(1)
    n_tiles = pl.num_programs(0) * nb
    cur = lax.rem(i, 2)
    tc = x_ref.shape[0]
    R = ystage.shape[1]

    def chunk_copy(src, dst, buf):
        return pltpu.make_async_copy(y_hbm.at[pl.ds(src, ROW_CHUNK), :],
                                     ystage.at[buf, pl.ds(dst, ROW_CHUNK), :], sem.at[buf])

    def start_tile(tile, buf):
        for e in range(N_EXPERTS):
            idx = tile * N_EXPERTS + e

            def body(c, carry, idx=idx):
                chunk_copy(pl.multiple_of(src_ref[idx] + c * ROW_CHUNK, ROW_CHUNK),
                           pl.multiple_of(dst_ref[idx] + c * ROW_CHUNK, ROW_CHUNK), buf).start()
                return carry

            lax.fori_loop(0, nch_ref[idx], body, 0)

    @pl.when(i == 0)
    def _():
        ystage[...] = jnp.zeros_like(ystage)
        start_tile(0, 0)

    @pl.when(i + 1 < n_tiles)
    def _():
        start_tile(i + 1, 1 - cur)

    def wait_one(c, carry):
        chunk_copy(0, 0, cur).wait()
        return carry

    lax.fori_loop(0, tot_ref[i], wait_one, 0)

    srow = srow_ref[...]
    grow = grow_ref[...]
    r_io = lax.broadcasted_iota(jnp.int32, (R, tc), 0)
    gate_of_row = jnp.sum(jnp.where(r_io == srow[0:1, :], grow[0:1, :], 0.0)
                          + jnp.where(r_io == srow[1:2, :], grow[1:2, :], 0.0),
                          axis=1, keepdims=True)
    step = LANES
    for r0 in range(0, R, step):
        ysb[r0:r0 + step, :] = (ystage[cur, r0:r0 + step, :] * gate_of_row[r0:r0 + step, :]).astype(_BF16)
    scol = scol_ref[...]
    c_io = lax.broadcasted_iota(jnp.int32, (tc, R), 1)
    g = jnp.where((c_io == scol[:, 0:1]) | (c_io == scol[:, 1:2]), 1.0, 0.0).astype(_BF16)
    moe = jnp.dot(g, ysb[...], preferred_element_type=_F32)
    xn = x_ref[...] + gm_ref[...] * moe
    if final:
        ms = jnp.mean(xn * xn, axis=-1, keepdims=True)
        xn = xn * lax.rsqrt(ms + EPS) * nf_ref[...]
    o_ref[...] = xn
    if has_next:
        h_ref[...] = _rms_mod(xn, gn_ref[...], scn_ref[...], shn_ref[...]).astype(h_ref.dtype)


def _combine(x, y, src, nch, dst, tot, slot_row, gate_row, slot_col, mod, i_gate, nf, final, g_next, mod_next):
    B, S, D = x.shape
    n_tiles, _, tc = slot_row.shape
    nb = S // tc
    R = _stage_rows(tc)
    has_next = g_next is not None
    if not has_next:
        g_next, mod_next = nf, mod
    tile = lambda b, s, *_: (b * nb + s, 0, 0)
    row = lambda b, s, *_: (b, s, 0)
    grid_spec = pltpu.PrefetchScalarGridSpec(
        num_scalar_prefetch=4,
        grid=(B, nb),
        in_specs=[
            pl.BlockSpec((None, tc, D), row),
            pl.BlockSpec(memory_space=pl.ANY),
            pl.BlockSpec((None, TOP_K, tc), tile),
            pl.BlockSpec((None, TOP_K, tc), tile),
            pl.BlockSpec((None, tc, TOP_K), tile),
            pl.BlockSpec((None, None, 1, D), lambda b, s, *_: (i_gate, b, 0, 0)),
            pl.BlockSpec((1, D), lambda b, s, *_: (0, 0)),
            pl.BlockSpec((1, D), lambda b, s, *_: (0, 0)),
            pl.BlockSpec((None, None, 1, D), lambda b, s, *_: (1, b, 0, 0)),
            pl.BlockSpec((None, None, 1, D), lambda b, s, *_: (0, b, 0, 0)),
        ],
        out_specs=([pl.BlockSpec((None, tc, D), row)] * 2) if has_next else pl.BlockSpec((None, tc, D), row),
        scratch_shapes=[pltpu.VMEM((2, R, D), _F32), pltpu.VMEM((R, D), _BF16),
                        pltpu.SemaphoreType.DMA((2,))],
    )
    out_shape = jax.ShapeDtypeStruct((B, S, D), _F32)
    if has_next:
        out_shape = [out_shape, jax.ShapeDtypeStruct((B, S, D), _BF16)]
    return pl.pallas_call(
        functools.partial(_combine_kernel, final=final, nb=nb, has_next=has_next),
        grid_spec=grid_spec,
        out_shape=out_shape,
        compiler_params=_params(("arbitrary", "arbitrary")),
        name="moe_combine",
    )(src, nch, dst, tot, x, y, slot_row, gate_row, slot_col, mod, nf.reshape(1, D), g_next.reshape(1, D),
      mod_next, mod_next)


def _stage_rows(tc):
    worst = TOP_K * tc + 2 * (ROW_CHUNK - 1) * N_EXPERTS
    return -(-worst // LANES) * LANES


def _combine_plan(eidx, dest, gate, pstarts, tc):
    T = eidx.shape[0] * eidx.shape[2]
    n_tiles = T // tc
    flat = lambda a: a.transpose(0, 2, 1).reshape(n_tiles, tc, TOP_K)
    e_f, d_f, g_f = flat(eidx), flat(dest), flat(gate)
    onehot = (e_f[..., None] == jnp.arange(N_EXPERTS, dtype=jnp.int32)).astype(jnp.int32)
    cnt = jnp.sum(onehot, axis=(1, 2))
    first = pstarts[None, :] + jnp.cumsum(cnt, axis=0) - cnt
    src = first // ROW_CHUNK * ROW_CHUNK
    nch = jnp.where(cnt > 0, (first + cnt - src + ROW_CHUNK - 1) // ROW_CHUNK, 0)
    dst = ROW_CHUNK * (jnp.cumsum(nch, axis=1) - nch)
    slot = d_f + jnp.sum(onehot * (dst - src)[:, None, None, :], axis=-1)
    return (src.reshape(-1), nch.reshape(-1), dst.reshape(-1), jnp.sum(nch, axis=1),
            slot.transpose(0, 2, 1), g_f.transpose(0, 2, 1), slot)


def kernel(x, c, w_ada, b_ada, norm_mix, norm_moe, w_in, b_forget, w_pool, pool_scale, w_branch,
           w_gate, b_gate, w_out, w_router, b_router, w_exp_gate, w_exp_up, w_exp_down, norm_final):
    B, S, D = x.shape
    L = w_ada.shape[0]
    W = D // 2
    nh = W // HEAD_DIM
    T = B * S
    A = T * TOP_K
    bm = min(256, A // N_EXPERTS)
    rows = (A // bm + N_EXPERTS) * bm

    mod_all = _ada(c, w_ada, b_ada)
    mod_all = mod_all.reshape(L, B, N_MOD, D).transpose(0, 2, 1, 3)[:, :, :, None, :]
    wr_t = w_router.T
    br = b_router.reshape(N_EXPERTS, 1)
    colscale = jnp.concatenate([jnp.ones((W,), _F32), jnp.full((W,), HEAD_DIM ** -0.5 * LOG2E, _F32),
                                jnp.ones((2 * W,), _F32)]).reshape(1, 4 * W)

    w_in_t = jnp.swapaxes(w_in, 1, 2)
    h = _norm_mod(x, norm_mix[0], mod_all[0], 0, 1)
    for l in range(L):
        mod = mod_all[l]
        wf = jnp.pad(w_in_t[l, 4 * W:, :], ((0, LANES - nh), (0, 0))).T.astype(_BF16)
        bf = jnp.zeros((1, LANES), _F32).at[0, :nh].set(b_forget[l])
        proj5 = _proj(h, w_in_t, l, colscale)
        fcum = _forget(h, wf, bf)
        pool_out = _pool(proj5, w_pool, l, pool_scale[l].reshape(1, W))
        attn = _attn(proj5, fcum, nh)
        merged = _merge(h, pool_out, attn, w_gate, b_gate, w_branch, l)
        x = _out_proj(x, merged, w_out, l, mod, 2)

        eidx, pos, gate, counts = _route(x, norm_moe[l], mod, 3, 4, wr_t, br)
        counts = counts[:, 0].astype(jnp.int32)
        pcounts = (counts + bm - 1) // bm * bm
        pends = jnp.cumsum(pcounts)
        pstarts = pends - pcounts
        dest = pos
        for e in range(N_EXPERTS):
            dest = dest + jnp.where(eidx == e, pstarts[e], 0)
        n_used = (pends[-1] // bm).astype(jnp.int32)
        blk_ids = jnp.arange(rows // bm, dtype=jnp.int32)
        blk_ids = jnp.minimum(blk_ids, n_used - 1)
        block_expert = jnp.sum((pends[None, :] <= (blk_ids * bm)[:, None]).astype(jnp.int32), axis=1)
        block_expert = jnp.minimum(block_expert, N_EXPERTS - 1)
        ids = jnp.arange(N_EXPERTS, dtype=jnp.int32)
        later = (ids[None, :] > ids[:, None]) & (pcounts[None, :] > 0)
        next_of = jnp.min(jnp.where(later, ids[None, :], N_EXPERTS), axis=1)
        next_of = jnp.where(next_of == N_EXPERTS, -1, next_of)
        next_expert = jnp.sum(jnp.where(block_expert[:, None] == ids[None, :], next_of[None, :], 0), axis=1)
        n_used = n_used.reshape(1)
        td = min(DISPATCH_TILE, S)
        dest_d = dest.transpose(0, 2, 1).reshape(T // td, td, TOP_K).transpose(0, 2, 1)
        buf = _dispatch(x, norm_moe[l], mod, 3, 4, dest_d, pstarts + counts, pcounts - counts,
                        n_used, rows, bm)
        y = _experts(buf, block_expert, next_expert.astype(jnp.int32), n_used,
                     w_exp_gate, w_exp_up, w_exp_down, l, bm)
        plan = _combine_plan(eidx, dest, gate, pstarts, min(COMBINE_TILE, S))
        if l + 1 < L:
            x, h = _combine(x, y, *plan, mod, 5, norm_final, False, norm_mix[l + 1], mod_all[l + 1])
        else:
            x = _combine(x, y, *plan, mod, 5, norm_final, True, None, None)
    return x
```

```python
import functools

import jax
import jax.numpy as jnp
from jax import lax
from jax.experimental import pallas as pl
from jax.experimental.pallas import tpu as pltpu

N_MOD = 6
EPS = 1e-6
POOL_WINDOWS = (2, 4, 8, 16)
HEAD_DIM = 128
N_EXPERTS = 16
N_EXPERT_GROUPS = 4
EXPERTS_PER_GROUP = N_EXPERTS // N_EXPERT_GROUPS
TOP_K = 2
LOG2E = 1.4426950408889634
ROUTE_TILE = 1024
DISPATCH_TILE = 512
COMBINE_TILE = 256
LANES = 128
VMEM_LIMIT = 56 * 1024 * 1024

_F32 = jnp.float32
_BF16 = jnp.bfloat16


def _params(sem, vmem=VMEM_LIMIT):
    return pltpu.CompilerParams(dimension_semantics=sem, vmem_limit_bytes=vmem)


def _sigmoid(v):
    return 1.0 / (1.0 + jnp.exp(-v))


def _rms_mod(x, g, scale, shift):
    ms = jnp.mean(x * x, axis=-1, keepdims=True)
    return x * lax.rsqrt(ms + EPS) * g * (1.0 + scale) + shift


def _ada_kernel(c_ref, w_ref, b_ref, o_ref):
    c = c_ref[...]
    ca = (c * _sigmoid(c)).astype(_BF16)
    o_ref[...] = jnp.dot(ca, w_ref[...].astype(_BF16), preferred_element_type=_F32) + b_ref[...]


def _ada(c, w_ada, b_ada):
    L, D, M = w_ada.shape
    B = c.shape[0]
    tn = min(1024, M)
    return pl.pallas_call(
        _ada_kernel,
        grid=(L, M // tn),
        in_specs=[
            pl.BlockSpec((B, D), lambda l, n: (0, 0)),
            pl.BlockSpec((None, D, tn), lambda l, n: (l, 0, n)),
            pl.BlockSpec((None, 1, tn), lambda l, n: (l, 0, n)),
        ],
        out_specs=pl.BlockSpec((None, B, tn), lambda l, n: (l, 0, n)),
        out_shape=jax.ShapeDtypeStruct((L, B, M), _F32),
        compiler_params=_params(("arbitrary", "arbitrary")),
        name="ada_mod",
    )(c, w_ada, b_ada.reshape(L, 1, M))


def _normmod_kernel(x_ref, g_ref, sc_ref, sh_ref, h_ref):
    h_ref[...] = _rms_mod(x_ref[...], g_ref[...], sc_ref[...], sh_ref[...]).astype(h_ref.dtype)


def _norm_mod(x, g, mod, i_shift, i_scale):
    B, S, D = x.shape
    ts = min(512, S)
    return pl.pallas_call(
        _normmod_kernel,
        grid=(B, S // ts),
        in_specs=[
            pl.BlockSpec((None, ts, D), lambda b, s: (b, s, 0)),
            pl.BlockSpec((1, D), lambda b, s: (0, 0)),
            pl.BlockSpec((None, None, 1, D), lambda b, s: (i_scale, b, 0, 0)),
            pl.BlockSpec((None, None, 1, D), lambda b, s: (i_shift, b, 0, 0)),
        ],
        out_specs=pl.BlockSpec((None, ts, D), lambda b, s: (b, s, 0)),
        out_shape=jax.ShapeDtypeStruct((B, S, D), _BF16),
        compiler_params=_params(("arbitrary", "arbitrary")),
        name="norm_mod",
    )(x, g.reshape(1, D), mod, mod)


def _first_inner_step():
    return (pl.program_id(1) == 0) & (pl.program_id(2) == 0)


def _proj_kernel(h_ref, w_ref, cs_ref, o_ref, wb_ref):
    @pl.when(_first_inner_step())
    def _():
        wb_ref[...] = (w_ref[...].T * cs_ref[...]).astype(_BF16)

    acc = jnp.dot(h_ref[...], wb_ref[...], preferred_element_type=_F32)
    for j in range(o_ref.shape[0]):
        o_ref[j] = acc[:, j * LANES:(j + 1) * LANES].astype(o_ref.dtype)


def _proj(h, w_in_t, layer, colscale):
    B, S, D = h.shape
    N = colscale.shape[1]
    tn = min(1024, N)
    tm = min(1024, S)
    return pl.pallas_call(
        _proj_kernel,
        grid=(N // tn, B, S // tm),
        in_specs=[
            pl.BlockSpec((None, tm, D), lambda n, b, m: (b, m, 0)),
            pl.BlockSpec((None, tn, D), lambda n, b, m: (layer, n, 0)),
            pl.BlockSpec((1, tn), lambda n, b, m: (0, n)),
        ],
        out_specs=pl.BlockSpec((None, tn // LANES, tm, LANES), lambda n, b, m: (b, n, m, 0)),
        out_shape=jax.ShapeDtypeStruct((B, N // LANES, S, LANES), _BF16),
        scratch_shapes=[pltpu.VMEM((D, tn), _BF16)],
        compiler_params=_params(("arbitrary", "arbitrary", "arbitrary")),
        name="in_proj",
    )(h, w_in_t, colscale)


def _forget_kernel(h_ref, w_ref, b_ref, o_ref):
    S, D = h_ref.shape
    nh = w_ref.shape[0]
    w = jnp.concatenate([w_ref[...], jnp.zeros((LANES - nh, D), _F32)], axis=0).astype(_BF16)
    fl = lax.dot_general(h_ref[...], w, (((1,), (1,)), ((), ())), preferred_element_type=_F32) + b_ref[...]
    acc = jnp.minimum(fl, 0.0) - jnp.log(1.0 + jnp.exp(-jnp.abs(fl)))
    row = lax.broadcasted_iota(jnp.int32, acc.shape, 0)
    sh = 1
    while sh < S:
        acc = acc + jnp.where(row >= sh, pltpu.roll(acc, sh, axis=0), 0.0)
        sh *= 2
    o_ref[...] = acc


def _forget(h, w_in_t, layer, nh, bf):
    B, S, D = h.shape
    first = (w_in_t.shape[1] - nh) // nh
    return pl.pallas_call(
        _forget_kernel,
        grid=(B,),
        in_specs=[
            pl.BlockSpec((None, S, D), lambda b: (b, 0, 0)),
            pl.BlockSpec((None, nh, D), lambda b: (layer, first, 0)),
            pl.BlockSpec((1, LANES), lambda b: (0, 0)),
        ],
        out_specs=pl.BlockSpec((None, S, LANES), lambda b: (b, 0, 0)),
        out_shape=jax.ShapeDtypeStruct((B, S, LANES), _F32),
        compiler_params=_params(("arbitrary",)),
        name="forget_cumsum",
    )(h, w_in_t, bf)


def _pool_kernel(u_ref, wp_ref, ps_ref, o_ref, *, cpg):
    S = u_ref.shape[1]
    gw = cpg * LANES
    row = lax.broadcasted_iota(jnp.int32, (S, gw), 0)
    for g, w in enumerate(POOL_WINDOWS):
        parts = [u_ref[g * cpg + j] for j in range(cpg)]
        u = (parts[0] if cpg == 1 else jnp.concatenate(parts, axis=-1)).astype(_F32)
        s = u
        sh = 1
        while sh < w:
            s = s + jnp.where(row >= sh, pltpu.roll(s, sh, axis=0), 0.0)
            sh *= 2
        cnt = jnp.minimum(row + 1, w).astype(_F32)
        pooled = s / cnt - u
        mixed = jnp.dot(pooled.astype(_BF16), wp_ref[g].astype(_BF16), preferred_element_type=_F32)
        o_ref[:, g * gw:(g + 1) * gw] = (mixed * ps_ref[:, g * gw:(g + 1) * gw]).astype(o_ref.dtype)


def _pool(proj5, w_pool, layer, ps):
    B, _, S, _ = proj5.shape
    _, G, gw, _ = w_pool.shape
    cpg = gw // LANES
    W = G * gw
    return pl.pallas_call(
        functools.partial(_pool_kernel, cpg=cpg),
        grid=(B,),
        in_specs=[
            pl.BlockSpec((None, G * cpg, S, LANES), lambda b: (b, 0, 0, 0)),
            pl.BlockSpec((None, G, gw, gw), lambda b: (layer, 0, 0, 0)),
            pl.BlockSpec((1, W), lambda b: (0, 0)),
        ],
        out_specs=pl.BlockSpec((None, S, W), lambda b: (b, 0, 0)),
        out_shape=jax.ShapeDtypeStruct((B, S, W), _BF16),
        compiler_params=_params(("arbitrary",)),
        name="pool_mixer",
    )(proj5, w_pool, ps)


def _attn_kernel(q_ref, k_ref, v_ref, f_ref, o_ref, qa_ref, ka_ref, va_ref, *, blk):
    S = q_ref.shape[0]
    h = pl.program_id(1)
    lane = lax.broadcasted_iota(jnp.int32, (S, LANES), 1)
    f = jnp.sum(jnp.where(lane == h, f_ref[...], 0.0), axis=-1, keepdims=True) * LOG2E
    f1 = f.astype(_BF16).astype(_F32)
    r1 = f - f1
    f2 = r1.astype(_BF16).astype(_F32)
    f3 = (r1 - f2).astype(_BF16).astype(_F32)
    qa_ref[:, :HEAD_DIM] = q_ref[...]
    qa_ref[:, HEAD_DIM:] = jnp.where(lane == 0, f1, jnp.where(lane == 1, f2, jnp.where(
        lane == 2, f3, jnp.where(lane < 6, 1.0, 0.0)))).astype(_BF16)
    ka_ref[:, :HEAD_DIM] = k_ref[...]
    ka_ref[:, HEAD_DIM:] = jnp.where(lane < 3, 1.0, jnp.where(lane == 3, -f1, jnp.where(
        lane == 4, -f2, jnp.where(lane == 5, -f3, 0.0)))).astype(_BF16)
    va_ref[:, :HEAD_DIM] = v_ref[...]
    va_ref[:, HEAD_DIM:] = jnp.where(lane == 0, 1.0, 0.0).astype(_BF16)

    nt = (((1,), (1,)), ((), ()))
    r_io = lax.broadcasted_iota(jnp.int32, (blk, blk), 0)
    c_io = lax.broadcasted_iota(jnp.int32, (blk, blk), 1)
    causal = c_io <= r_io
    nb = S // blk
    m = [jnp.full((blk, 1), -jnp.inf, _F32)] * nb
    acc = [jnp.zeros((blk, 2 * HEAD_DIM), _F32)] * nb
    for j in range(nb):
        ka = ka_ref[j * blk:(j + 1) * blk, :]
        va = va_ref[j * blk:(j + 1) * blk, :]
        for i in range(j, nb):
            s = lax.dot_general(qa_ref[i * blk:(i + 1) * blk, :], ka, nt, preferred_element_type=_F32)
            if i == j:
                s = jnp.where(causal, s, -jnp.inf)
            m_new = jnp.maximum(m[i], jnp.max(s, axis=-1, keepdims=True))
            p = jnp.exp2(s - m_new).astype(_BF16)
            acc[i] = jnp.exp2(m[i] - m_new) * acc[i] + jnp.dot(p, va, preferred_element_type=_F32)
            m[i] = m_new
    for i in range(nb):
        o_ref[i * blk:(i + 1) * blk, :] = (
            acc[i][:, :HEAD_DIM] / acc[i][:, HEAD_DIM:HEAD_DIM + 1]).astype(o_ref.dtype)


def _attn(proj5, fcum, nh):
    B, _, S, _ = proj5.shape
    blk = min(512, S)
    chunk = lambda o: pl.BlockSpec((None, None, S, HEAD_DIM), lambda b, h: (b, o + h, 0, 0))
    return pl.pallas_call(
        functools.partial(_attn_kernel, blk=blk),
        grid=(B, nh),
        in_specs=[
            chunk(nh), chunk(2 * nh), chunk(3 * nh),
            pl.BlockSpec((None, S, LANES), lambda b, h: (b, 0, 0)),
        ],
        out_specs=pl.BlockSpec((None, None, S, HEAD_DIM), lambda b, h: (b, h, 0, 0)),
        out_shape=jax.ShapeDtypeStruct((B, nh, S, HEAD_DIM), _BF16),
        scratch_shapes=[pltpu.VMEM((S, 2 * HEAD_DIM), _BF16)] * 3,
        compiler_params=_params(("arbitrary", "arbitrary")),
        name="forget_attn",
    )(proj5, proj5, proj5, fcum)


def _merge_kernel(h_ref, p_ref, a_ref, wg0_ref, wg1_ref, bg0_ref, bg1_ref, wb0_ref, wb1_ref, o_ref,
                  cg0_ref, cg1_ref, cb0_ref, cb1_ref):
    @pl.when(_first_inner_step())
    def _():
        cg0_ref[...] = wg0_ref[...].astype(_BF16)
        cg1_ref[...] = wg1_ref[...].astype(_BF16)
        cb0_ref[...] = wb0_ref[...].astype(_BF16)
        cb1_ref[...] = wb1_ref[...].astype(_BF16)

    h = h_ref[...]
    nh = a_ref.shape[0]
    a = jnp.concatenate([a_ref[j] for j in range(nh)], axis=-1)
    g0 = _sigmoid(jnp.dot(h, cg0_ref[...], preferred_element_type=_F32) + bg0_ref[...])
    y0 = jnp.dot(p_ref[...], cb0_ref[...], preferred_element_type=_F32)
    acc = g0 * y0
    g1 = _sigmoid(jnp.dot(h, cg1_ref[...], preferred_element_type=_F32) + bg1_ref[...])
    y1 = jnp.dot(a, cb1_ref[...], preferred_element_type=_F32)
    o_ref[...] = (acc + g1 * y1).astype(o_ref.dtype)


def _merge(h, pool_out, attn, w_gate, b_gate, w_branch, layer):
    B, S, D = h.shape
    W = pool_out.shape[-1]
    nh = attn.shape[1]
    L = w_gate.shape[0]
    tn = min(512, D)
    tm = min(1024, S)
    nn = D // tn
    bg = b_gate.reshape(L, 1, 2 * D)
    once = pl.Buffered(1)
    return pl.pallas_call(
        _merge_kernel,
        grid=(nn, B, S // tm),
        in_specs=[
            pl.BlockSpec((None, tm, D), lambda n, b, m: (b, m, 0)),
            pl.BlockSpec((None, tm, W), lambda n, b, m: (b, m, 0)),
            pl.BlockSpec((None, nh, tm, HEAD_DIM), lambda n, b, m: (b, 0, m, 0)),
            pl.BlockSpec((None, D, tn), lambda n, b, m: (layer, 0, n), pipeline_mode=once),
            pl.BlockSpec((None, D, tn), lambda n, b, m: (layer, 0, nn + n), pipeline_mode=once),
            pl.BlockSpec((None, 1, tn), lambda n, b, m: (layer, 0, n)),
            pl.BlockSpec((None, 1, tn), lambda n, b, m: (layer, 0, nn + n)),
            pl.BlockSpec((None, None, W, tn), lambda n, b, m: (layer, 0, 0, n), pipeline_mode=once),
            pl.BlockSpec((None, None, W, tn), lambda n, b, m: (layer, 1, 0, n), pipeline_mode=once),
        ],
        out_specs=pl.BlockSpec((None, tm, tn), lambda n, b, m: (b, m, n)),
        out_shape=jax.ShapeDtypeStruct((B, S, D), _BF16),
        scratch_shapes=[pltpu.VMEM((D, tn), _BF16), pltpu.VMEM((D, tn), _BF16),
                        pltpu.VMEM((W, tn), _BF16), pltpu.VMEM((W, tn), _BF16)],
        compiler_params=_params(("arbitrary", "arbitrary", "arbitrary")),
        name="branch_merge",
    )(h, pool_out, attn, w_gate, w_gate, bg, bg, w_branch, w_branch)


def _out_kernel(x_ref, m_ref, w_ref, g_ref, o_ref, wb_ref):
    @pl.when(_first_inner_step())
    def _():
        wb_ref[...] = w_ref[...].astype(_BF16)

    y = jnp.dot(m_ref[...], wb_ref[...], preferred_element_type=_F32)
    o_ref[...] = x_ref[...] + g_ref[...] * y


def _out_proj(x, merged, w_out, layer, mod, i_gate):
    B, S, D = x.shape
    tm = min(1024, S)
    tn = min(1024, D)
    return pl.pallas_call(
        _out_kernel,
        grid=(D // tn, B, S // tm),
        in_specs=[
            pl.BlockSpec((None, tm, tn), lambda n, b, m: (b, m, n)),
            pl.BlockSpec((None, tm, D), lambda n, b, m: (b, m, 0)),
            pl.BlockSpec((None, D, tn), lambda n, b, m: (layer, 0, n)),
            pl.BlockSpec((None, None, 1, tn), lambda n, b, m: (i_gate, b, 0, n)),
        ],
        out_specs=pl.BlockSpec((None, tm, tn), lambda n, b, m: (b, m, n)),
        out_shape=jax.ShapeDtypeStruct((B, S, D), _F32),
        scratch_shapes=[pltpu.VMEM((D, tn), _BF16)],
        compiler_params=_params(("arbitrary", "arbitrary", "arbitrary")),
        name="out_proj",
    )(x, merged, w_out, mod)


def _route_kernel(x_ref, g_ref, sc_ref, sh_ref, wr_ref, br_ref,
                  e_ref, pos_ref, gate_ref, cnt_ref):
    first_step = (pl.program_id(0) == 0) & (pl.program_id(1) == 0)

    @pl.when(first_step)
    def _():
        cnt_ref[...] = jnp.zeros_like(cnt_ref)

    h = _rms_mod(x_ref[...], g_ref[...], sc_ref[...], sh_ref[...])
    tr, D = h.shape
    ns = tr // LANES

    sub = [lax.dot_general(wr_ref[...], h[s * LANES:(s + 1) * LANES, :], (((1,), (1,)), ((), ())),
                           precision=lax.Precision.HIGHEST, preferred_element_type=_F32)
           for s in range(ns)]
    rows = [jnp.concatenate([sub[s][e:e + 1, :] for s in range(ns)], axis=0) for e in range(N_EXPERTS)]
    mx = functools.reduce(jnp.maximum, rows)
    ex = [jnp.exp(r - mx) for r in rows]
    den = functools.reduce(lambda a, b: a + b, ex)
    probs = [v / den for v in ex]
    sel = [probs[e] + br_ref[e:e + 1, :] for e in range(N_EXPERTS)]

    in_top = [None] * N_EXPERTS
    gscore = []
    for g in range(N_EXPERT_GROUPS):
        ids = range(g * EXPERTS_PER_GROUP, (g + 1) * EXPERTS_PER_GROUP)
        score = None
        for i in ids:
            rank = None
            for j in ids:
                if j == i:
                    continue
                beats = (sel[j] > sel[i]) | ((sel[j] == sel[i]) if j < i else False)
                beats = beats.astype(_F32)
                rank = beats if rank is None else rank + beats
            in_top[i] = rank < float(TOP_K)
            term = jnp.where(in_top[i], sel[i], 0.0)
            score = term if score is None else score + term
        gscore.append(score)
    chosen = []
    for g in range(N_EXPERT_GROUPS):
        lose = None
        for g2 in range(N_EXPERT_GROUPS):
            if g2 == g:
                continue
            b = (gscore[g2] > gscore[g]) | ((gscore[g2] == gscore[g]) if g2 < g else False)
            lose = b if lose is None else (lose | b)
        chosen.append(jnp.logical_not(lose))
    picked = [in_top[e] & chosen[e // EXPERTS_PER_GROUP] for e in range(N_EXPERTS)]
    pf = [p.astype(_F32) for p in picked]

    n = N_EXPERTS * ns
    onehot = jnp.concatenate(pf, axis=0).astype(_BF16)
    upper = (lax.broadcasted_iota(jnp.int32, (LANES, LANES), 0)
             <= lax.broadcasted_iota(jnp.int32, (LANES, LANES), 1)).astype(_BF16)
    incl = jnp.dot(onehot, upper, preferred_element_type=_F32)
    r_io = lax.broadcasted_iota(jnp.int32, (n, n), 0)
    c_io = lax.broadcasted_iota(jnp.int32, (n, n), 1)
    shift = ns.bit_length() - 1
    same_expert = lax.shift_right_logical(r_io, shift) == lax.shift_right_logical(c_io, shift)
    earlier = jnp.where(same_expert & (c_io < r_io), 1.0, 0.0).astype(_BF16)
    carry = jnp.dot(earlier, incl.astype(_BF16), preferred_element_type=_F32)[:, LANES - 1:LANES]
    base = cnt_ref[:, 0:1]
    posm = [incl[e * ns:(e + 1) * ns, :] - 1.0 + carry[e * ns:(e + 1) * ns, :] + base[e:e + 1, :]
            for e in range(N_EXPERTS)]
    last = [(incl[(e + 1) * ns - 1:(e + 1) * ns, LANES - 1:LANES] + carry[(e + 1) * ns - 1:(e + 1) * ns, :])
            for e in range(N_EXPERTS)]
    cnt_ref[...] = jnp.broadcast_to(base + jnp.concatenate(last, axis=0), cnt_ref.shape)

    gnum = [pf[e] * probs[e] for e in range(N_EXPERTS)]
    gden = functools.reduce(lambda a, b: a + b, gnum)
    zero = jnp.zeros_like(pf[0])
    seen = zero
    e_out = [zero, zero]
    p_out = [zero, zero]
    g_out = [zero, zero]
    for e in range(N_EXPERTS):
        for k in range(TOP_K):
            hit = pf[e] * (seen == float(k)).astype(_F32)
            e_out[k] = e_out[k] + hit * float(e)
            p_out[k] = p_out[k] + hit * posm[e]
            g_out[k] = g_out[k] + hit * gnum[e]
        seen = seen + pf[e]
    for k in range(TOP_K):
        e_ref[k] = e_out[k].astype(jnp.int32)
        pos_ref[k] = p_out[k].astype(jnp.int32)
        gate_ref[k] = g_out[k] / gden


def _route(x, g, mod, i_shift, i_scale, wr_t, br):
    B, S, D = x.shape
    tr = min(ROUTE_TILE, S)
    ns = tr // LANES
    assert ns & (ns - 1) == 0
    nb = S // tr
    E = N_EXPERTS
    tok = lambda dt: jax.ShapeDtypeStruct((B * nb, TOP_K, ns, LANES), dt)
    tok_spec = pl.BlockSpec((None, TOP_K, ns, LANES), lambda b, s: (b * nb + s, 0, 0, 0))
    eidx, pos, gate, counts = pl.pallas_call(
        _route_kernel,
        grid=(B, nb),
        in_specs=[
            pl.BlockSpec((None, tr, D), lambda b, s: (b, s, 0)),
            pl.BlockSpec((1, D), lambda b, s: (0, 0)),
            pl.BlockSpec((None, None, 1, D), lambda b, s: (i_scale, b, 0, 0)),
            pl.BlockSpec((None, None, 1, D), lambda b, s: (i_shift, b, 0, 0)),
            pl.BlockSpec((E, D), lambda b, s: (0, 0)),
            pl.BlockSpec((E, 1), lambda b, s: (0, 0)),
        ],
        out_specs=[
            tok_spec, tok_spec, tok_spec,
            pl.BlockSpec((E, LANES), lambda b, s: (0, 0)),
        ],
        out_shape=[
            tok(jnp.int32), tok(jnp.int32), tok(_F32),
            jax.ShapeDtypeStruct((E, LANES), _F32),
        ],
        compiler_params=_params(("arbitrary", "arbitrary")),
        name="moe_route",
    )(x, g.reshape(1, D), mod, mod, wr_t, br)
    flat = lambda a: a.reshape(B * nb, TOP_K, tr)
    return flat(eidx), flat(pos), flat(gate), counts


def _dispatch_kernel(pad_start_ref, pad_len_ref, nu_ref, dest_ref, x0_ref, xn_ref, g_ref,
                     sc0_ref, sh0_ref, scn_ref, shn_ref, buf_ref, h_ref, zblk, sem):
    i = pl.program_id(0)
    tr = xn_ref.shape[0]

    @pl.when(i == 0)
    def _():
        h_ref[0] = _rms_mod(x0_ref[...], g_ref[...], sc0_ref[...], sh0_ref[...])

    def row_wait():
        pltpu.make_async_copy(h_ref.at[0, pl.ds(0, 1), :], buf_ref.at[pl.ds(0, 1), :], sem).wait()

    def step(slot):
        for r in range(tr):
            for k in range(TOP_K):
                pltpu.make_async_copy(h_ref.at[slot, pl.ds(r, 1), :],
                                      buf_ref.at[pl.ds(dest_ref[k, r], 1), :], sem).start(priority=k)

        @pl.when(i + 1 < pl.num_programs(0))
        def _():
            h_ref[1 - slot] = _rms_mod(xn_ref[...], g_ref[...], scn_ref[...], shn_ref[...])

        for _ in range(tr * TOP_K):
            row_wait()

    for slot in range(2):
        pl.when(lax.rem(i, 2) == slot)(functools.partial(step, slot))

    @pl.when(pl.program_id(0) == 0)
    def _():
        bm = zblk.shape[0]
        nblk = buf_ref.shape[0] // bm
        zblk[...] = jnp.zeros_like(zblk)
        for e in range(N_EXPERTS):
            def zissue(r, carry, e=e):
                pltpu.make_async_copy(zblk.at[pl.ds(0, 1), :],
                                      buf_ref.at[pl.ds(pad_start_ref[e] + r, 1), :], sem).start()
                return carry
            lax.fori_loop(0, pad_len_ref[e], zissue, 0)
        for e in range(N_EXPERTS):
            def zdrain(r, carry):
                row_wait()
                return carry
            lax.fori_loop(0, pad_len_ref[e], zdrain, 0)

        def blk_copy(i):
            return pltpu.make_async_copy(zblk, buf_ref.at[pl.ds(pl.multiple_of(i * bm, bm), bm), :], sem)

        def bissue(i, carry):
            blk_copy(i).start()
            return carry

        def bdrain(i, carry):
            blk_copy(i).wait()
            return carry

        lax.fori_loop(nu_ref[0], nblk, bissue, 0)
        lax.fori_loop(nu_ref[0], nblk, bdrain, 0)


def _dispatch(x, g, mod, i_shift, i_scale, dest, pad_start, pad_len, n_used, rows, bm):
    B, S, D = x.shape
    nbt, _, tr = dest.shape
    nb = S // tr
    nxt = lambda i: jnp.minimum(i + 1, nbt - 1)
    grid_spec = pltpu.PrefetchScalarGridSpec(
        num_scalar_prefetch=3,
        grid=(nbt,),
        in_specs=[
            pl.BlockSpec((None, TOP_K, tr), lambda i, *_: (i, 0, 0), memory_space=pltpu.SMEM),
            pl.BlockSpec((None, tr, D), lambda i, *_: (0, 0, 0)),
            pl.BlockSpec((None, tr, D), lambda i, *_: (nxt(i) // nb, nxt(i) % nb, 0)),
            pl.BlockSpec((1, D), lambda i, *_: (0, 0)),
            pl.BlockSpec((None, None, 1, D), lambda i, *_: (i_scale, 0, 0, 0)),
            pl.BlockSpec((None, None, 1, D), lambda i, *_: (i_shift, 0, 0, 0)),
            pl.BlockSpec((None, None, 1, D), lambda i, *_: (i_scale, nxt(i) // nb, 0, 0)),
            pl.BlockSpec((None, None, 1, D), lambda i, *_: (i_shift, nxt(i) // nb, 0, 0)),
        ],
        out_specs=pl.BlockSpec(memory_space=pl.ANY),
        scratch_shapes=[pltpu.VMEM((2, tr, D), _F32), pltpu.VMEM((bm, D), _F32),
                        pltpu.SemaphoreType.DMA(())],
    )
    return pl.pallas_call(
        _dispatch_kernel,
        grid_spec=grid_spec,
        out_shape=jax.ShapeDtypeStruct((rows, D), _F32),
        compiler_params=_params(("arbitrary",)),
        name="moe_dispatch",
    )(pad_start, pad_len, n_used, dest, x, x, g.reshape(1, D), mod, mod, mod, mod)


def _expert_kernel(be_ref, nx_ref, nu_ref, x_ref, wg_hbm, wu_hbm, wd_hbm, o_ref,
                   wgb, wub, wdb, sg, su, sd, sem, *, layer):
    i = pl.program_id(0)
    e = be_ref[i]
    live = i < nu_ref[0]
    new_expert = (i == 0) | (e != be_ref[jnp.maximum(i - 1, 0)])

    def weight_copies(ex):
        return [pltpu.make_async_copy(w.at[layer, ex], s, sem.at[n])
                for n, (w, s) in enumerate(((wg_hbm, sg), (wu_hbm, su), (wd_hbm, sd)))]

    @pl.when(live & new_expert)
    def _():
        @pl.when(i == 0)
        def _():
            for cp in weight_copies(e):
                cp.start()

        for cp in weight_copies(e):
            cp.wait()
        wgb[...] = sg[...].astype(_BF16)
        wub[...] = su[...].astype(_BF16)
        wdb[...] = sd[...].astype(_BF16)

        @pl.when(nx_ref[i] >= 0)
        def _():
            for cp in weight_copies(nx_ref[i]):
                cp.start(priority=1)

    @pl.when(live)
    def _():
        xb = x_ref[...].astype(_BF16)
        a = jnp.dot(xb, wgb[...], preferred_element_type=_F32)
        u = jnp.dot(xb, wub[...], preferred_element_type=_F32)
        act = (a * _sigmoid(a) * u).astype(_BF16)
        o_ref[...] = jnp.dot(act, wdb[...], preferred_element_type=_F32)

    @pl.when(jnp.logical_not(live))
    def _():
        o_ref[...] = jnp.zeros_like(o_ref)


def _experts(buf, block_expert, next_expert, n_used, wg, wu, wd, layer, bm):
    rows, D = buf.shape
    De = wg.shape[-1]
    nblk = rows // bm
    grid_spec = pltpu.PrefetchScalarGridSpec(
        num_scalar_prefetch=3,
        grid=(nblk,),
        in_specs=[
            pl.BlockSpec((bm, D), lambda i, be, nx, nu: (jnp.minimum(i, jnp.maximum(nu[0] - 1, 0)), 0)),
            pl.BlockSpec(memory_space=pl.ANY),
            pl.BlockSpec(memory_space=pl.ANY),
            pl.BlockSpec(memory_space=pl.ANY),
        ],
        out_specs=pl.BlockSpec((bm, D), lambda i, be, nx, nu: (i, 0)),
        scratch_shapes=[
            pltpu.VMEM((D, De), _BF16), pltpu.VMEM((D, De), _BF16), pltpu.VMEM((De, D), _BF16),
            pltpu.VMEM((D, De), _F32), pltpu.VMEM((D, De), _F32), pltpu.VMEM((De, D), _F32),
            pltpu.SemaphoreType.DMA((3,)),
        ],
    )
    return pl.pallas_call(
        functools.partial(_expert_kernel, layer=layer),
        grid_spec=grid_spec,
        out_shape=jax.ShapeDtypeStruct((rows, D), _F32),
        compiler_params=_params(("arbitrary",), vmem=60 * 1024 * 1024),
        name="moe_experts",
    )(block_expert, next_expert, n_used, buf, wg, wu, wd)


ROW_CHUNK = 8


def _combine_kernel(src_ref, nch_ref, dst_ref, tot_ref,
                    x_ref, y_hbm, srow_ref, grow_ref, scol_ref, gm_ref, nf_ref, gn_ref, scn_ref, shn_ref,
                    *rest, final, nb, has_next):
    if has_next:
        o_ref, h_ref, ystage, ysb, sem = rest
    else:
        o_ref, ystage, ysb, sem = rest
        h_ref = None
    i = pl.program_id(0) * nb + pl.program_id(1)
    n_tiles = pl.num_programs(0) * nb
    cur = lax.rem(i, 2)
    tc = x_ref.shape[0]
    R = ystage.shape[1]

    def chunk_copy(src, dst, buf):
        return pltpu.make_async_copy(y_hbm.at[pl.ds(src, ROW_CHUNK), :],
                                     ystage.at[buf, pl.ds(dst, ROW_CHUNK), :], sem.at[buf])

    def start_tile(tile, buf):
        for e in range(N_EXPERTS):
            idx = tile * N_EXPERTS + e

            def body(c, carry, idx=idx):
                chunk_copy(pl.multiple_of(src_ref[idx] + c * ROW_CHUNK, ROW_CHUNK),
                           pl.multiple_of(dst_ref[idx] + c * ROW_CHUNK, ROW_CHUNK), buf).start()
                return carry

            lax.fori_loop(0, nch_ref[idx], body, 0)

    @pl.when(i == 0)
    def _():
        ystage[...] = jnp.zeros_like(ystage)
        start_tile(0, 0)

    @pl.when(i + 1 < n_tiles)
    def _():
        start_tile(i + 1, 1 - cur)

    def wait_one(c, carry):
        chunk_copy(0, 0, cur).wait()
        return carry

    lax.fori_loop(0, tot_ref[i], wait_one, 0)

    srow = srow_ref[...]
    grow = grow_ref[...]
    r_io = lax.broadcasted_iota(jnp.int32, (R, tc), 0)
    gate_of_row = jnp.sum(jnp.where(r_io == srow[0:1, :], grow[0:1, :], 0.0)
                          + jnp.where(r_io == srow[1:2, :], grow[1:2, :], 0.0),
                          axis=1, keepdims=True)
    step = LANES
    for r0 in range(0, R, step):
        ysb[r0:r0 + step, :] = (ystage[cur, r0:r0 + step, :] * gate_of_row[r0:r0 + step, :]).astype(_BF16)
    scol = scol_ref[...]
    c_io = lax.broadcasted_iota(jnp.int32, (tc, R), 1)
    g = jnp.where((c_io == scol[:, 0:1]) | (c_io == scol[:, 1:2]), 1.0, 0.0).astype(_BF16)
    moe = jnp.dot(g, ysb[...], preferred_element_type=_F32)
    xn = x_ref[...] + gm_ref[...] * moe
    if final:
        ms = jnp.mean(xn * xn, axis=-1, keepdims=True)
        xn = xn * lax.rsqrt(ms + EPS) * nf_ref[...]
    o_ref[...] = xn
    if has_next:
        h_ref[...] = _rms_mod(xn, gn_ref[...], scn_ref[...], shn_ref[...]).astype(h_ref.dtype)


def _combine(x, y, src, nch, dst, tot, slot_row, gate_row, slot_col, mod, i_gate, nf, final, g_next, mod_next):
    B, S, D = x.shape
    n_tiles, _, tc = slot_row.shape
    nb = S // tc
    R = _stage_rows(tc)
    has_next = g_next is not None
    if not has_next:
        g_next, mod_next = nf, mod
    tile = lambda b, s, *_: (b * nb + s, 0, 0)
    row = lambda b, s, *_: (b, s, 0)
    grid_spec = pltpu.PrefetchScalarGridSpec(
        num_scalar_prefetch=4,
        grid=(B, nb),
        in_specs=[
            pl.BlockSpec((None, tc, D), row),
            pl.BlockSpec(memory_space=pl.ANY),
            pl.BlockSpec((None, TOP_K, tc), tile),
            pl.BlockSpec((None, TOP_K, tc), tile),
            pl.BlockSpec((None, tc, TOP_K), tile),
            pl.BlockSpec((None, None, 1, D), lambda b, s, *_: (i_gate, b, 0, 0)),
            pl.BlockSpec((1, D), lambda b, s, *_: (0, 0)),
            pl.BlockSpec((1, D), lambda b, s, *_: (0, 0)),
            pl.BlockSpec((None, None, 1, D), lambda b, s, *_: (1, b, 0, 0)),
            pl.BlockSpec((None, None, 1, D), lambda b, s, *_: (0, b, 0, 0)),
        ],
        out_specs=([pl.BlockSpec((None, tc, D), row)] * 2) if has_next else pl.BlockSpec((None, tc, D), row),
        scratch_shapes=[pltpu.VMEM((2, R, D), _F32), pltpu.VMEM((R, D), _BF16),
                        pltpu.SemaphoreType.DMA((2,))],
    )
    out_shape = jax.ShapeDtypeStruct((B, S, D), _F32)
    if has_next:
        out_shape = [out_shape, jax.ShapeDtypeStruct((B, S, D), _BF16)]
    return pl.pallas_call(
        functools.partial(_combine_kernel, final=final, nb=nb, has_next=has_next),
        grid_spec=grid_spec,
        out_shape=out_shape,
        compiler_params=_params(("arbitrary", "arbitrary")),
        name="moe_combine",
    )(src, nch, dst, tot, x, y, slot_row, gate_row, slot_col, mod, nf.reshape(1, D), g_next.reshape(1, D),
      mod_next, mod_next)


def _stage_rows(tc):
    worst = TOP_K * tc + 2 * (ROW_CHUNK - 1) * N_EXPERTS
    return -(-worst // LANES) * LANES


def _combine_plan(eidx, dest, gate, pstarts, tc):
    T = eidx.shape[0] * eidx.shape[2]
    n_tiles = T // tc
    flat = lambda a: a.transpose(0, 2, 1).reshape(n_tiles, tc, TOP_K)
    e_f, d_f, g_f = flat(eidx), flat(dest), flat(gate)
    onehot = (e_f[..., None] == jnp.arange(N_EXPERTS, dtype=jnp.int32)).astype(jnp.int32)
    cnt = jnp.sum(onehot, axis=(1, 2))
    first = pstarts[None, :] + jnp.cumsum(cnt, axis=0) - cnt
    src = first // ROW_CHUNK * ROW_CHUNK
    nch = jnp.where(cnt > 0, (first + cnt - src + ROW_CHUNK - 1) // ROW_CHUNK, 0)
    dst = ROW_CHUNK * (jnp.cumsum(nch, axis=1) - nch)
    slot = d_f + jnp.sum(onehot * (dst - src)[:, None, None, :], axis=-1)
    return (src.reshape(-1), nch.reshape(-1), dst.reshape(-1), jnp.sum(nch, axis=1),
            slot.transpose(0, 2, 1), g_f.transpose(0, 2, 1), slot)


def kernel(x, c, w_ada, b_ada, norm_mix, norm_moe, w_in, b_forget, w_pool, pool_scale, w_branch,
           w_gate, b_gate, w_out, w_router, b_router, w_exp_gate, w_exp_up, w_exp_down, norm_final):
    B, S, D = x.shape
    L = w_ada.shape[0]
    W = D // 2
    nh = W // HEAD_DIM
    T = B * S
    A = T * TOP_K
    bm = min(256, A // N_EXPERTS)
    rows = (A // bm + N_EXPERTS) * bm

    mod_all = _ada(c, w_ada, b_ada)
    mod_all = mod_all.reshape(L, B, N_MOD, D).transpose(0, 2, 1, 3)[:, :, :, None, :]
    wr_t = w_router.T
    br = b_router.reshape(N_EXPERTS, 1)
    colscale = jnp.concatenate([jnp.ones((W,), _F32), jnp.full((W,), HEAD_DIM ** -0.5 * LOG2E, _F32),
                                jnp.ones((2 * W,), _F32)]).reshape(1, 4 * W)

    w_in_t = jnp.swapaxes(w_in, 1, 2)
    h = _norm_mod(x, norm_mix[0], mod_all[0], 0, 1)
    for l in range(L):
        mod = mod_all[l]
        bf = jnp.zeros((1, LANES), _F32).at[0, :nh].set(b_forget[l])
        proj5 = _proj(h, w_in_t, l, colscale)
        fcum = _forget(h, w_in_t, l, nh, bf)
        pool_out = _pool(proj5, w_pool, l, pool_scale[l].reshape(1, W))
        attn = _attn(proj5, fcum, nh)
        merged = _merge(h, pool_out, attn, w_gate, b_gate, w_branch, l)
        x = _out_proj(x, merged, w_out, l, mod, 2)

        eidx, pos, gate, counts = _route(x, norm_moe[l], mod, 3, 4, wr_t, br)
        counts = counts[:, 0].astype(jnp.int32)
        pcounts = (counts + bm - 1) // bm * bm
        pends = jnp.cumsum(pcounts)
        pstarts = pends - pcounts
        dest = pos
        for e in range(N_EXPERTS):
            dest = dest + jnp.where(eidx == e, pstarts[e], 0)
        n_used = (pends[-1] // bm).astype(jnp.int32)
        blk_ids = jnp.arange(rows // bm, dtype=jnp.int32)
        blk_ids = jnp.minimum(blk_ids, n_used - 1)
        block_expert = jnp.sum((pends[None, :] <= (blk_ids * bm)[:, None]).astype(jnp.int32), axis=1)
        block_expert = jnp.minimum(block_expert, N_EXPERTS - 1)
        ids = jnp.arange(N_EXPERTS, dtype=jnp.int32)
        later = (ids[None, :] > ids[:, None]) & (pcounts[None, :] > 0)
        next_of = jnp.min(jnp.where(later, ids[None, :], N_EXPERTS), axis=1)
        next_of = jnp.where(next_of == N_EXPERTS, -1, next_of)
        next_expert = jnp.sum(jnp.where(block_expert[:, None] == ids[None, :], next_of[None, :], 0), axis=1)
        n_used = n_used.reshape(1)
        td = min(DISPATCH_TILE, S)
        dest_d = dest.transpose(0, 2, 1).reshape(T // td, td, TOP_K).transpose(0, 2, 1)
        buf = _dispatch(x, norm_moe[l], mod, 3, 4, dest_d, pstarts + counts, pcounts - counts,
                        n_used, rows, bm)
        y = _experts(buf, block_expert, next_expert.astype(jnp.int32), n_used,
                     w_exp_gate, w_exp_up, w_exp_down, l, bm)
        plan = _combine_plan(eidx, dest, gate, pstarts, min(COMBINE_TILE, S))
        if l + 1 < L:
            x, h = _combine(x, y, *plan, mod, 5, norm_final, False, norm_mix[l + 1], mod_all[l + 1])
        else:
            x = _combine(x, y, *plan, mod, 5, norm_final, True, None, None)
    return x
```

```python
import functools

import jax
import jax.numpy as jnp
from jax import lax
from jax.experimental import pallas as pl
from jax.experimental.pallas import tpu as pltpu

N_MOD = 6
EPS = 1e-6
POOL_WINDOWS = (2, 4, 8, 16)
HEAD_DIM = 128
N_EXPERTS = 16
N_EXPERT_GROUPS = 4
EXPERTS_PER_GROUP = N_EXPERTS // N_EXPERT_GROUPS
TOP_K = 2
LOG2E = 1.4426950408889634
ROUTE_TILE = 1024
DISPATCH_TILE = 512
COMBINE_TILE = 256
LANES = 128
VMEM_LIMIT = 56 * 1024 * 1024

_F32 = jnp.float32
_BF16 = jnp.bfloat16


def _params(sem, vmem=VMEM_LIMIT):
    return pltpu.CompilerParams(dimension_semantics=sem, vmem_limit_bytes=vmem)


def _sigmoid(v):
    return 1.0 / (1.0 + jnp.exp(-v))


def _rms_mod(x, g, scale, shift):
    ms = jnp.mean(x * x, axis=-1, keepdims=True)
    return x * lax.rsqrt(ms + EPS) * g * (1.0 + scale) + shift


def _ada_kernel(c_ref, w_ref, b_ref, o_ref):
    c = c_ref[...]
    ca = (c * _sigmoid(c)).astype(_BF16)
    o_ref[...] = jnp.dot(ca, w_ref[...].astype(_BF16), preferred_element_type=_F32) + b_ref[...]


def _ada(c, w_ada, b_ada):
    L, D, M = w_ada.shape
    B = c.shape[0]
    tn = min(1024, M)
    return pl.pallas_call(
        _ada_kernel,
        grid=(L, M // tn),
        in_specs=[
            pl.BlockSpec((B, D), lambda l, n: (0, 0)),
            pl.BlockSpec((None, D, tn), lambda l, n: (l, 0, n)),
            pl.BlockSpec((None, 1, tn), lambda l, n: (l, 0, n)),
        ],
        out_specs=pl.BlockSpec((None, B, tn), lambda l, n: (l, 0, n)),
        out_shape=jax.ShapeDtypeStruct((L, B, M), _F32),
        compiler_params=_params(("arbitrary", "arbitrary")),
        name="ada_mod",
    )(c, w_ada, b_ada.reshape(L, 1, M))


def _first_inner_step():
    return (pl.program_id(1) == 0) & (pl.program_id(2) == 0)


def _proj_kernel(h_ref, w_ref, cs_ref, o_ref, wb_ref):
    @pl.when(_first_inner_step())
    def _():
        wb_ref[...] = (w_ref[...].T * cs_ref[...]).astype(_BF16)

    acc = jnp.dot(h_ref[...], wb_ref[...], preferred_element_type=_F32)
    for j in range(o_ref.shape[0]):
        o_ref[j] = acc[:, j * LANES:(j + 1) * LANES].astype(o_ref.dtype)


def _proj(h, w_in_t, layer, colscale):
    B, S, D = h.shape
    N = colscale.shape[1]
    tn = min(1024, N)
    tm = min(1024, S)
    return pl.pallas_call(
        _proj_kernel,
        grid=(N // tn, B, S // tm),
        in_specs=[
            pl.BlockSpec((None, tm, D), lambda n, b, m: (b, m, 0)),
            pl.BlockSpec((None, tn, D), lambda n, b, m: (layer, n, 0)),
            pl.BlockSpec((1, tn), lambda n, b, m: (0, n)),
        ],
        out_specs=pl.BlockSpec((None, tn // LANES, tm, LANES), lambda n, b, m: (b, n, m, 0)),
        out_shape=jax.ShapeDtypeStruct((B, N // LANES, S, LANES), _BF16),
        scratch_shapes=[pltpu.VMEM((D, tn), _BF16)],
        compiler_params=_params(("arbitrary", "arbitrary", "arbitrary")),
        name="in_proj",
    )(h, w_in_t, colscale)


def _forget_kernel(*refs, from_x):
    if from_x:
        x_ref, g_ref, sc_ref, sh_ref, w_ref, b_ref, o_ref, h_ref, carry = refs
        h = _rms_mod(x_ref[...], g_ref[...], sc_ref[...], sh_ref[...]).astype(_BF16)
        h_ref[...] = h
    else:
        hin_ref, w_ref, b_ref, o_ref, carry = refs
        h = hin_ref[...]
    ts, D = h.shape
    nh = w_ref.shape[0]

    @pl.when(pl.program_id(1) == 0)
    def _():
        carry[...] = jnp.zeros_like(carry)

    w = jnp.concatenate([w_ref[...], jnp.zeros((LANES - nh, D), _F32)], axis=0).astype(_BF16)
    fl = lax.dot_general(h, w, (((1,), (1,)), ((), ())), preferred_element_type=_F32) + b_ref[...]
    acc = jnp.minimum(fl, 0.0) - jnp.log(1.0 + jnp.exp(-jnp.abs(fl)))
    row = lax.broadcasted_iota(jnp.int32, acc.shape, 0)
    sh = 1
    while sh < ts:
        acc = acc + jnp.where(row >= sh, pltpu.roll(acc, sh, axis=0), 0.0)
        sh *= 2
    acc = acc + carry[...]
    o_ref[...] = acc
    carry[...] = acc[ts - 1:ts, :]


def _forget(h_or_x, w_in_t, layer, nh, bf, norm=None):
    B, S, D = h_or_x.shape
    ts = min(1024, S)
    first = (w_in_t.shape[1] - nh) // nh
    rows = pl.BlockSpec((None, ts, D), lambda b, s: (b, s, 0))
    tail_specs = [pl.BlockSpec((None, nh, D), lambda b, s: (layer, first, 0)),
                  pl.BlockSpec((1, LANES), lambda b, s: (0, 0))]
    f_spec = pl.BlockSpec((None, ts, LANES), lambda b, s: (b, s, 0))
    f_shape = jax.ShapeDtypeStruct((B, S, LANES), _F32)
    if norm is None:
        in_specs, args = [rows] + tail_specs, (h_or_x, w_in_t, bf)
        out_specs, out_shape = f_spec, f_shape
    else:
        g, mod, i_shift, i_scale = norm
        in_specs = [rows, pl.BlockSpec((1, D), lambda b, s: (0, 0)),
                    pl.BlockSpec((None, None, 1, D), lambda b, s: (i_scale, b, 0, 0)),
                    pl.BlockSpec((None, None, 1, D), lambda b, s: (i_shift, b, 0, 0))] + tail_specs
        args = (h_or_x, g.reshape(1, D), mod, mod, w_in_t, bf)
        out_specs, out_shape = [f_spec, rows], [f_shape, jax.ShapeDtypeStruct((B, S, D), _BF16)]
    return pl.pallas_call(
        functools.partial(_forget_kernel, from_x=norm is not None),
        grid=(B, S // ts),
        in_specs=in_specs,
        out_specs=out_specs,
        out_shape=out_shape,
        scratch_shapes=[pltpu.VMEM((1, LANES), _F32)],
        compiler_params=_params(("arbitrary", "arbitrary")),
        name="forget_cumsum",
    )(*args)


def _pool_kernel(u_ref, wp_ref, ps_ref, o_ref, *, cpg):
    S = u_ref.shape[1]
    gw = cpg * LANES
    row = lax.broadcasted_iota(jnp.int32, (S, gw), 0)
    for g, w in enumerate(POOL_WINDOWS):
        parts = [u_ref[g * cpg + j] for j in range(cpg)]
        u = (parts[0] if cpg == 1 else jnp.concatenate(parts, axis=-1)).astype(_F32)
        s = u
        sh = 1
        while sh < w:
            s = s + jnp.where(row >= sh, pltpu.roll(s, sh, axis=0), 0.0)
            sh *= 2
        cnt = jnp.minimum(row + 1, w).astype(_F32)
        pooled = s / cnt - u
        mixed = jnp.dot(pooled.astype(_BF16), wp_ref[g].astype(_BF16), preferred_element_type=_F32)
        o_ref[:, g * gw:(g + 1) * gw] = (mixed * ps_ref[:, g * gw:(g + 1) * gw]).astype(o_ref.dtype)


def _pool(proj5, w_pool, layer, ps):
    B, _, S, _ = proj5.shape
    _, G, gw, _ = w_pool.shape
    cpg = gw // LANES
    W = G * gw
    return pl.pallas_call(
        functools.partial(_pool_kernel, cpg=cpg),
        grid=(B,),
        in_specs=[
            pl.BlockSpec((None, G * cpg, S, LANES), lambda b: (b, 0, 0, 0)),
            pl.BlockSpec((None, G, gw, gw), lambda b: (layer, 0, 0, 0)),
            pl.BlockSpec((1, W), lambda b: (0, 0)),
        ],
        out_specs=pl.BlockSpec((None, S, W), lambda b: (b, 0, 0)),
        out_shape=jax.ShapeDtypeStruct((B, S, W), _BF16),
        compiler_params=_params(("arbitrary",)),
        name="pool_mixer",
    )(proj5, w_pool, ps)


def _attn_kernel(q_ref, k_ref, v_ref, f_ref, o_ref, qa_ref, ka_ref, va_ref, *, blk):
    S = q_ref.shape[0]
    h = pl.program_id(1)
    lane = lax.broadcasted_iota(jnp.int32, (S, LANES), 1)
    f = jnp.sum(jnp.where(lane == h, f_ref[...], 0.0), axis=-1, keepdims=True) * LOG2E
    f1 = f.astype(_BF16).astype(_F32)
    r1 = f - f1
    f2 = r1.astype(_BF16).astype(_F32)
    f3 = (r1 - f2).astype(_BF16).astype(_F32)
    qa_ref[:, :HEAD_DIM] = q_ref[...]
    qa_ref[:, HEAD_DIM:] = jnp.where(lane == 0, f1, jnp.where(lane == 1, f2, jnp.where(
        lane == 2, f3, jnp.where(lane < 6, 1.0, 0.0)))).astype(_BF16)
    ka_ref[:, :HEAD_DIM] = k_ref[...]
    ka_ref[:, HEAD_DIM:] = jnp.where(lane < 3, 1.0, jnp.where(lane == 3, -f1, jnp.where(
        lane == 4, -f2, jnp.where(lane == 5, -f3, 0.0)))).astype(_BF16)
    va_ref[:, :HEAD_DIM] = v_ref[...]
    va_ref[:, HEAD_DIM:] = jnp.where(lane == 0, 1.0, 0.0).astype(_BF16)

    nt = (((1,), (1,)), ((), ()))
    r_io = lax.broadcasted_iota(jnp.int32, (blk, blk), 0)
    c_io = lax.broadcasted_iota(jnp.int32, (blk, blk), 1)
    causal = c_io <= r_io
    nb = S // blk
    m = [jnp.full((blk, 1), -jnp.inf, _F32)] * nb
    acc = [jnp.zeros((blk, 2 * HEAD_DIM), _F32)] * nb
    for j in range(nb):
        ka = ka_ref[j * blk:(j + 1) * blk, :]
        va = va_ref[j * blk:(j + 1) * blk, :]
        for i in range(j, nb):
            s = lax.dot_general(qa_ref[i * blk:(i + 1) * blk, :], ka, nt, preferred_element_type=_F32)
            if i == j:
                s = jnp.where(causal, s, -jnp.inf)
            m_new = jnp.maximum(m[i], jnp.max(s, axis=-1, keepdims=True))
            p = jnp.exp2(s - m_new).astype(_BF16)
            acc[i] = jnp.exp2(m[i] - m_new) * acc[i] + jnp.dot(p, va, preferred_element_type=_F32)
            m[i] = m_new
    for i in range(nb):
        o_ref[i * blk:(i + 1) * blk, :] = (
            acc[i][:, :HEAD_DIM] / acc[i][:, HEAD_DIM:HEAD_DIM + 1]).astype(o_ref.dtype)


def _attn(proj5, fcum, nh):
    B, _, S, _ = proj5.shape
    blk = min(512, S)
    chunk = lambda o: pl.BlockSpec((None, None, S, HEAD_DIM), lambda b, h: (b, o + h, 0, 0))
    return pl.pallas_call(
        functools.partial(_attn_kernel, blk=blk),
        grid=(B, nh),
        in_specs=[
            chunk(nh), chunk(2 * nh), chunk(3 * nh),
            pl.BlockSpec((None, S, LANES), lambda b, h: (b, 0, 0)),
        ],
        out_specs=pl.BlockSpec((None, None, S, HEAD_DIM), lambda b, h: (b, h, 0, 0)),
        out_shape=jax.ShapeDtypeStruct((B, nh, S, HEAD_DIM), _BF16),
        scratch_shapes=[pltpu.VMEM((S, 2 * HEAD_DIM), _BF16)] * 3,
        compiler_params=_params(("arbitrary", "arbitrary")),
        name="forget_attn",
    )(proj5, proj5, proj5, fcum)


def _merge_kernel(h_ref, p_ref, a_ref, wg0_ref, wg1_ref, bg0_ref, bg1_ref, wb0_ref, wb1_ref, o_ref,
                  cg0_ref, cg1_ref, cb0_ref, cb1_ref):
    @pl.when(_first_inner_step())
    def _():
        cg0_ref[...] = wg0_ref[...].astype(_BF16)
        cg1_ref[...] = wg1_ref[...].astype(_BF16)
        cb0_ref[...] = wb0_ref[...].astype(_BF16)
        cb1_ref[...] = wb1_ref[...].astype(_BF16)

    h = h_ref[...]
    nh = a_ref.shape[0]
    a = jnp.concatenate([a_ref[j] for j in range(nh)], axis=-1)
    g0 = _sigmoid(jnp.dot(h, cg0_ref[...], preferred_element_type=_F32) + bg0_ref[...])
    y0 = jnp.dot(p_ref[...], cb0_ref[...], preferred_element_type=_F32)
    acc = g0 * y0
    g1 = _sigmoid(jnp.dot(h, cg1_ref[...], preferred_element_type=_F32) + bg1_ref[...])
    y1 = jnp.dot(a, cb1_ref[...], preferred_element_type=_F32)
    o_ref[...] = (acc + g1 * y1).astype(o_ref.dtype)


def _merge(h, pool_out, attn, w_gate, b_gate, w_branch, layer):
    B, S, D = h.shape
    W = pool_out.shape[-1]
    nh = attn.shape[1]
    L = w_gate.shape[0]
    tn = min(512, D)
    tm = min(1024, S)
    nn = D // tn
    bg = b_gate.reshape(L, 1, 2 * D)
    once = pl.Buffered(1)
    return pl.pallas_call(
        _merge_kernel,
        grid=(nn, B, S // tm),
        in_specs=[
            pl.BlockSpec((None, tm, D), lambda n, b, m: (b, m, 0)),
            pl.BlockSpec((None, tm, W), lambda n, b, m: (b, m, 0)),
            pl.BlockSpec((None, nh, tm, HEAD_DIM), lambda n, b, m: (b, 0, m, 0)),
            pl.BlockSpec((None, D, tn), lambda n, b, m: (layer, 0, n), pipeline_mode=once),
            pl.BlockSpec((None, D, tn), lambda n, b, m: (layer, 0, nn + n), pipeline_mode=once),
            pl.BlockSpec((None, 1, tn), lambda n, b, m: (layer, 0, n)),
            pl.BlockSpec((None, 1, tn), lambda n, b, m: (layer, 0, nn + n)),
            pl.BlockSpec((None, None, W, tn), lambda n, b, m: (layer, 0, 0, n), pipeline_mode=once),
            pl.BlockSpec((None, None, W, tn), lambda n, b, m: (layer, 1, 0, n), pipeline_mode=once),
        ],
        out_specs=pl.BlockSpec((None, tm, tn), lambda n, b, m: (b, m, n)),
        out_shape=jax.ShapeDtypeStruct((B, S, D), _BF16),
        scratch_shapes=[pltpu.VMEM((D, tn), _BF16), pltpu.VMEM((D, tn), _BF16),
                        pltpu.VMEM((W, tn), _BF16), pltpu.VMEM((W, tn), _BF16)],
        compiler_params=_params(("arbitrary", "arbitrary", "arbitrary")),
        name="branch_merge",
    )(h, pool_out, attn, w_gate, w_gate, bg, bg, w_branch, w_branch)


def _out_kernel(x_ref, m_ref, w_ref, g_ref, o_ref, wb_ref):
    @pl.when(_first_inner_step())
    def _():
        wb_ref[...] = w_ref[...].astype(_BF16)

    y = jnp.dot(m_ref[...], wb_ref[...], preferred_element_type=_F32)
    o_ref[...] = x_ref[...] + g_ref[...] * y


def _out_proj(x, merged, w_out, layer, mod, i_gate):
    B, S, D = x.shape
    tm = min(1024, S)
    tn = min(1024, D)
    return pl.pallas_call(
        _out_kernel,
        grid=(D // tn, B, S // tm),
        in_specs=[
            pl.BlockSpec((None, tm, tn), lambda n, b, m: (b, m, n)),
            pl.BlockSpec((None, tm, D), lambda n, b, m: (b, m, 0)),
            pl.BlockSpec((None, D, tn), lambda n, b, m: (layer, 0, n)),
            pl.BlockSpec((None, None, 1, tn), lambda n, b, m: (i_gate, b, 0, n)),
        ],
        out_specs=pl.BlockSpec((None, tm, tn), lambda n, b, m: (b, m, n)),
        out_shape=jax.ShapeDtypeStruct((B, S, D), _F32),
        scratch_shapes=[pltpu.VMEM((D, tn), _BF16)],
        compiler_params=_params(("arbitrary", "arbitrary", "arbitrary")),
        name="out_proj",
    )(x, merged, w_out, mod)


def _route_tile(x, g_ref, sc_ref, sh_ref, wr_ref, br_ref, e_ref, pos_ref, gate_ref, cnt_ref):
    first_step = (pl.program_id(0) == 0) & (pl.program_id(1) == 0)

    @pl.when(first_step)
    def _():
        cnt_ref[...] = jnp.zeros_like(cnt_ref)

    h = _rms_mod(x, g_ref[...], sc_ref[...], sh_ref[...])
    tr, D = h.shape
    ns = tr // LANES

    sub = [lax.dot_general(wr_ref[...], h[s * LANES:(s + 1) * LANES, :], (((1,), (1,)), ((), ())),
                           precision=lax.Precision.HIGHEST, preferred_element_type=_F32)
           for s in range(ns)]
    rows = [jnp.concatenate([sub[s][e:e + 1, :] for s in range(ns)], axis=0) for e in range(N_EXPERTS)]
    mx = functools.reduce(jnp.maximum, rows)
    ex = [jnp.exp(r - mx) for r in rows]
    den = functools.reduce(lambda a, b: a + b, ex)
    probs = [v / den for v in ex]
    sel = [probs[e] + br_ref[e:e + 1, :] for e in range(N_EXPERTS)]

    in_top = [None] * N_EXPERTS
    gscore = []
    for g in range(N_EXPERT_GROUPS):
        ids = range(g * EXPERTS_PER_GROUP, (g + 1) * EXPERTS_PER_GROUP)
        score = None
        for i in ids:
            rank = None
            for j in ids:
                if j == i:
                    continue
                beats = (sel[j] > sel[i]) | ((sel[j] == sel[i]) if j < i else False)
                beats = beats.astype(_F32)
                rank = beats if rank is None else rank + beats
            in_top[i] = rank < float(TOP_K)
            term = jnp.where(in_top[i], sel[i], 0.0)
            score = term if score is None else score + term
        gscore.append(score)
    chosen = []
    for g in range(N_EXPERT_GROUPS):
        lose = None
        for g2 in range(N_EXPERT_GROUPS):
            if g2 == g:
                continue
            b = (gscore[g2] > gscore[g]) | ((gscore[g2] == gscore[g]) if g2 < g else False)
            lose = b if lose is None else (lose | b)
        chosen.append(jnp.logical_not(lose))
    picked = [in_top[e] & chosen[e // EXPERTS_PER_GROUP] for e in range(N_EXPERTS)]
    pf = [p.astype(_F32) for p in picked]

    n = N_EXPERTS * ns
    onehot = jnp.concatenate(pf, axis=0).astype(_BF16)
    upper = (lax.broadcasted_iota(jnp.int32, (LANES, LANES), 0)
             <= lax.broadcasted_iota(jnp.int32, (LANES, LANES), 1)).astype(_BF16)
    incl = jnp.dot(onehot, upper, preferred_element_type=_F32)
    r_io = lax.broadcasted_iota(jnp.int32, (n, n), 0)
    c_io = lax.broadcasted_iota(jnp.int32, (n, n), 1)
    shift = ns.bit_length() - 1
    same_expert = lax.shift_right_logical(r_io, shift) == lax.shift_right_logical(c_io, shift)
    earlier = jnp.where(same_expert & (c_io < r_io), 1.0, 0.0).astype(_BF16)
    carry = jnp.dot(earlier, incl.astype(_BF16), preferred_element_type=_F32)[:, LANES - 1:LANES]
    base = cnt_ref[:, 0:1]
    posm = [incl[e * ns:(e + 1) * ns, :] - 1.0 + carry[e * ns:(e + 1) * ns, :] + base[e:e + 1, :]
            for e in range(N_EXPERTS)]
    last = [(incl[(e + 1) * ns - 1:(e + 1) * ns, LANES - 1:LANES] + carry[(e + 1) * ns - 1:(e + 1) * ns, :])
            for e in range(N_EXPERTS)]
    cnt_ref[...] = jnp.broadcast_to(base + jnp.concatenate(last, axis=0), cnt_ref.shape)

    gnum = [pf[e] * probs[e] for e in range(N_EXPERTS)]
    gden = functools.reduce(lambda a, b: a + b, gnum)
    zero = jnp.zeros_like(pf[0])
    seen = zero
    e_out = [zero, zero]
    p_out = [zero, zero]
    g_out = [zero, zero]
    for e in range(N_EXPERTS):
        for k in range(TOP_K):
            hit = pf[e] * (seen == float(k)).astype(_F32)
            e_out[k] = e_out[k] + hit * float(e)
            p_out[k] = p_out[k] + hit * posm[e]
            g_out[k] = g_out[k] + hit * gnum[e]
        seen = seen + pf[e]
    for k in range(TOP_K):
        e_ref[k] = e_out[k].astype(jnp.int32)
        pos_ref[k] = p_out[k].astype(jnp.int32)
        gate_ref[k] = g_out[k] / gden


def _route_kernel(x_ref, *refs):
    _route_tile(x_ref[...], *refs)


def _route(x, g, mod, i_shift, i_scale, wr_t, br):
    B, S, D = x.shape
    tr = min(ROUTE_TILE, S)
    ns = tr // LANES
    assert ns & (ns - 1) == 0
    nb = S // tr
    E = N_EXPERTS
    tok = lambda dt: jax.ShapeDtypeStruct((B * nb, TOP_K, ns, LANES), dt)
    tok_spec = pl.BlockSpec((None, TOP_K, ns, LANES), lambda b, s: (b * nb + s, 0, 0, 0))
    eidx, pos, gate, counts = pl.pallas_call(
        _route_kernel,
        grid=(B, nb),
        in_specs=[
            pl.BlockSpec((None, tr, D), lambda b, s: (b, s, 0)),
            pl.BlockSpec((1, D), lambda b, s: (0, 0)),
            pl.BlockSpec((None, None, 1, D), lambda b, s: (i_scale, b, 0, 0)),
            pl.BlockSpec((None, None, 1, D), lambda b, s: (i_shift, b, 0, 0)),
            pl.BlockSpec((E, D), lambda b, s: (0, 0)),
            pl.BlockSpec((E, 1), lambda b, s: (0, 0)),
        ],
        out_specs=[
            tok_spec, tok_spec, tok_spec,
            pl.BlockSpec((E, LANES), lambda b, s: (0, 0)),
        ],
        out_shape=[
            tok(jnp.int32), tok(jnp.int32), tok(_F32),
            jax.ShapeDtypeStruct((E, LANES), _F32),
        ],
        compiler_params=_params(("arbitrary", "arbitrary")),
        name="moe_route",
    )(x, g.reshape(1, D), mod, mod, wr_t, br)
    flat = lambda a: a.reshape(B * nb, TOP_K, tr)
    return flat(eidx), flat(pos), flat(gate), counts


def _dispatch_kernel(pad_start_ref, pad_len_ref, nu_ref, dest_ref, x0_ref, xn_ref, g_ref,
                     sc0_ref, sh0_ref, scn_ref, shn_ref, buf_ref, h_ref, zblk, sem):
    i = pl.program_id(0)
    tr = xn_ref.shape[0]

    @pl.when(i == 0)
    def _():
        h_ref[0] = _rms_mod(x0_ref[...], g_ref[...], sc0_ref[...], sh0_ref[...])

    def row_wait():
        pltpu.make_async_copy(h_ref.at[0, pl.ds(0, 1), :], buf_ref.at[pl.ds(0, 1), :], sem).wait()

    def step(slot):
        for r in range(tr):
            for k in range(TOP_K):
                pltpu.make_async_copy(h_ref.at[slot, pl.ds(r, 1), :],
                                      buf_ref.at[pl.ds(dest_ref[k, r], 1), :], sem).start(priority=k)

        @pl.when(i + 1 < pl.num_programs(0))
        def _():
            h_ref[1 - slot] = _rms_mod(xn_ref[...], g_ref[...], scn_ref[...], shn_ref[...])

        for _ in range(tr * TOP_K):
            row_wait()

    for slot in range(2):
        pl.when(lax.rem(i, 2) == slot)(functools.partial(step, slot))

    @pl.when(pl.program_id(0) == 0)
    def _():
        bm = zblk.shape[0]
        nblk = buf_ref.shape[0] // bm
        zblk[...] = jnp.zeros_like(zblk)
        for e in range(N_EXPERTS):
            def zissue(r, carry, e=e):
                pltpu.make_async_copy(zblk.at[pl.ds(0, 1), :],
                                      buf_ref.at[pl.ds(pad_start_ref[e] + r, 1), :], sem).start()
                return carry
            lax.fori_loop(0, pad_len_ref[e], zissue, 0)
        for e in range(N_EXPERTS):
            def zdrain(r, carry):
                row_wait()
                return carry
            lax.fori_loop(0, pad_len_ref[e], zdrain, 0)

        def blk_copy(i):
            return pltpu.make_async_copy(zblk, buf_ref.at[pl.ds(pl.multiple_of(i * bm, bm), bm), :], sem)

        def bissue(i, carry):
            blk_copy(i).start()
            return carry

        def bdrain(i, carry):
            blk_copy(i).wait()
            return carry

        lax.fori_loop(nu_ref[0], nblk, bissue, 0)
        lax.fori_loop(nu_ref[0], nblk, bdrain, 0)


def _dispatch(x, g, mod, i_shift, i_scale, dest, pad_start, pad_len, n_used, rows, bm):
    B, S, D = x.shape
    nbt, _, tr = dest.shape
    nb = S // tr
    nxt = lambda i: jnp.minimum(i + 1, nbt - 1)
    grid_spec = pltpu.PrefetchScalarGridSpec(
        num_scalar_prefetch=3,
        grid=(nbt,),
        in_specs=[
            pl.BlockSpec((None, TOP_K, tr), lambda i, *_: (i, 0, 0), memory_space=pltpu.SMEM),
            pl.BlockSpec((None, tr, D), lambda i, *_: (0, 0, 0)),
            pl.BlockSpec((None, tr, D), lambda i, *_: (nxt(i) // nb, nxt(i) % nb, 0)),
            pl.BlockSpec((1, D), lambda i, *_: (0, 0)),
            pl.BlockSpec((None, None, 1, D), lambda i, *_: (i_scale, 0, 0, 0)),
            pl.BlockSpec((None, None, 1, D), lambda i, *_: (i_shift, 0, 0, 0)),
            pl.BlockSpec((None, None, 1, D), lambda i, *_: (i_scale, nxt(i) // nb, 0, 0)),
            pl.BlockSpec((None, None, 1, D), lambda i, *_: (i_shift, nxt(i) // nb, 0, 0)),
        ],
        out_specs=pl.BlockSpec(memory_space=pl.ANY),
        scratch_shapes=[pltpu.VMEM((2, tr, D), _F32), pltpu.VMEM((bm, D), _F32),
                        pltpu.SemaphoreType.DMA(())],
    )
    return pl.pallas_call(
        _dispatch_kernel,
        grid_spec=grid_spec,
        out_shape=jax.ShapeDtypeStruct((rows, D), _F32),
        compiler_params=_params(("arbitrary",)),
        name="moe_dispatch",
    )(pad_start, pad_len, n_used, dest, x, x, g.reshape(1, D), mod, mod, mod, mod)


def _expert_kernel(be_ref, nx_ref, nu_ref, x_ref, wg_hbm, wu_hbm, wd_hbm, o_ref,
                   wgb, wub, wdb, sg, su, sd, sem, *, layer):
    i = pl.program_id(0)
    e = be_ref[i]
    live = i < nu_ref[0]
    new_expert = (i == 0) | (e != be_ref[jnp.maximum(i - 1, 0)])

    def weight_copies(ex):
        return [pltpu.make_async_copy(w.at[layer, ex], s, sem.at[n])
                for n, (w, s) in enumerate(((wg_hbm, sg), (wu_hbm, su), (wd_hbm, sd)))]

    @pl.when(live & new_expert)
    def _():
        @pl.when(i == 0)
        def _():
            for cp in weight_copies(e):
                cp.start()

        for cp in weight_copies(e):
            cp.wait()
        wgb[...] = sg[...].astype(_BF16)
        wub[...] = su[...].astype(_BF16)
        wdb[...] = sd[...].astype(_BF16)

        @pl.when(nx_ref[i] >= 0)
        def _():
            for cp in weight_copies(nx_ref[i]):
                cp.start(priority=1)

    @pl.when(live)
    def _():
        xb = x_ref[...].astype(_BF16)
        a = jnp.dot(xb, wgb[...], preferred_element_type=_F32)
        u = jnp.dot(xb, wub[...], preferred_element_type=_F32)
        act = (a * _sigmoid(a) * u).astype(_BF16)
        o_ref[...] = jnp.dot(act, wdb[...], preferred_element_type=_F32)

    @pl.when(jnp.logical_not(live))
    def _():
        o_ref[...] = jnp.zeros_like(o_ref)


def _experts(buf, block_expert, next_expert, n_used, wg, wu, wd, layer, bm):
    rows, D = buf.shape
    De = wg.shape[-1]
    nblk = rows // bm
    grid_spec = pltpu.PrefetchScalarGridSpec(
        num_scalar_prefetch=3,
        grid=(nblk,),
        in_specs=[
            pl.BlockSpec((bm, D), lambda i, be, nx, nu: (jnp.minimum(i, jnp.maximum(nu[0] - 1, 0)), 0)),
            pl.BlockSpec(memory_space=pl.ANY),
            pl.BlockSpec(memory_space=pl.ANY),
            pl.BlockSpec(memory_space=pl.ANY),
        ],
        out_specs=pl.BlockSpec((bm, D), lambda i, be, nx, nu: (i, 0)),
        scratch_shapes=[
            pltpu.VMEM((D, De), _BF16), pltpu.VMEM((D, De), _BF16), pltpu.VMEM((De, D), _BF16),
            pltpu.VMEM((D, De), _F32), pltpu.VMEM((D, De), _F32), pltpu.VMEM((De, D), _F32),
            pltpu.SemaphoreType.DMA((3,)),
        ],
    )
    return pl.pallas_call(
        functools.partial(_expert_kernel, layer=layer),
        grid_spec=grid_spec,
        out_shape=jax.ShapeDtypeStruct((rows, D), _F32),
        compiler_params=_params(("arbitrary",), vmem=60 * 1024 * 1024),
        name="moe_experts",
    )(block_expert, next_expert, n_used, buf, wg, wu, wd)


ROW_CHUNK = 8


def _combine_kernel(src_ref, nch_ref, dst_ref, tot_ref,
                    x_ref, y_hbm, srow_ref, grow_ref, scol_ref, gm_ref, nf_ref, gn_ref, scn_ref, shn_ref,
                    *rest, final, nb, has_next):
    if has_next:
        o_ref, h_ref, ystage, ysb, sem = rest
    else:
        o_ref, ystage, ysb, sem = rest
        h_ref = None
    i = pl.program_id(0) * nb + pl.program_id(1)
    n_tiles = pl.num_programs(0) * nb
    cur = lax.rem(i, 2)
    tc = x_ref.shape[0]
    R = ystage.shape[1]

    def chunk_copy(src, dst, buf):
        return pltpu.make_async_copy(y_hbm.at[pl.ds(src, ROW_CHUNK), :],
                                     ystage.at[buf, pl.ds(dst, ROW_CHUNK), :], sem.at[buf])

    def start_tile(tile, buf):
        for e in range(N_EXPERTS):
            idx = tile * N_EXPERTS + e

            def body(c, carry, idx=idx):
                chunk_copy(pl.multiple_of(src_ref[idx] + c * ROW_CHUNK, ROW_CHUNK),
                           pl.multiple_of(dst_ref[idx] + c * ROW_CHUNK, ROW_CHUNK), buf).start()
                return carry

            lax.fori_loop(0, nch_ref[idx], body, 0)

    @pl.when(i == 0)
    def _():
        ystage[...] = jnp.zeros_like(ystage)
        start_tile(0, 0)

    @pl.when(i + 1 < n_tiles)
    def _():
        start_tile(i + 1, 1 - cur)

    def wait_one(c, carry):
        chunk_copy(0, 0, cur).wait()
        return carry

    lax.fori_loop(0, tot_ref[i], wait_one, 0)

    srow = srow_ref[...]
    grow = grow_ref[...]
    r_io = lax.broadcasted_iota(jnp.int32, (R, tc), 0)
    gate_of_row = jnp.sum(jnp.where(r_io == srow[0:1, :], grow[0:1, :], 0.0)
                          + jnp.where(r_io == srow[1:2, :], grow[1:2, :], 0.0),
                          axis=1, keepdims=True)
    step = LANES
    for r0 in range(0, R, step):
        ysb[r0:r0 + step, :] = (ystage[cur, r0:r0 + step, :] * gate_of_row[r0:r0 + step, :]).astype(_BF16)
    scol = scol_ref[...]
    c_io = lax.broadcasted_iota(jnp.int32, (tc, R), 1)
    g = jnp.where((c_io == scol[:, 0:1]) | (c_io == scol[:, 1:2]), 1.0, 0.0).astype(_BF16)
    moe = jnp.dot(g, ysb[...], preferred_element_type=_F32)
    xn = x_ref[...] + gm_ref[...] * moe
    if final:
        ms = jnp.mean(xn * xn, axis=-1, keepdims=True)
        xn = xn * lax.rsqrt(ms + EPS) * nf_ref[...]
    o_ref[...] = xn
    if has_next:
        h_ref[...] = _rms_mod(xn, gn_ref[...], scn_ref[...], shn_ref[...]).astype(h_ref.dtype)


def _combine(x, y, src, nch, dst, tot, slot_row, gate_row, slot_col, mod, i_gate, nf, final, g_next, mod_next):
    B, S, D = x.shape
    n_tiles, _, tc = slot_row.shape
    nb = S // tc
    R = _stage_rows(tc)
    has_next = g_next is not None
    if not has_next:
        g_next, mod_next = nf, mod
    tile = lambda b, s, *_: (b * nb + s, 0, 0)
    row = lambda b, s, *_: (b, s, 0)
    grid_spec = pltpu.PrefetchScalarGridSpec(
        num_scalar_prefetch=4,
        grid=(B, nb),
        in_specs=[
            pl.BlockSpec((None, tc, D), row),
            pl.BlockSpec(memory_space=pl.ANY),
            pl.BlockSpec((None, TOP_K, tc), tile),
            pl.BlockSpec((None, TOP_K, tc), tile),
            pl.BlockSpec((None, tc, TOP_K), tile),
            pl.BlockSpec((None, None, 1, D), lambda b, s, *_: (i_gate, b, 0, 0)),
            pl.BlockSpec((1, D), lambda b, s, *_: (0, 0)),
            pl.BlockSpec((1, D), lambda b, s, *_: (0, 0)),
            pl.BlockSpec((None, None, 1, D), lambda b, s, *_: (1, b, 0, 0)),
            pl.BlockSpec((None, None, 1, D), lambda b, s, *_: (0, b, 0, 0)),
        ],
        out_specs=([pl.BlockSpec((None, tc, D), row)] * 2) if has_next else pl.BlockSpec((None, tc, D), row),
        scratch_shapes=[pltpu.VMEM((2, R, D), _F32), pltpu.VMEM((R, D), _BF16),
                        pltpu.SemaphoreType.DMA((2,))],
    )
    out_shape = jax.ShapeDtypeStruct((B, S, D), _F32)
    if has_next:
        out_shape = [out_shape, jax.ShapeDtypeStruct((B, S, D), _BF16)]
    return pl.pallas_call(
        functools.partial(_combine_kernel, final=final, nb=nb, has_next=has_next),
        grid_spec=grid_spec,
        out_shape=out_shape,
        compiler_params=_params(("arbitrary", "arbitrary")),
        name="moe_combine",
    )(src, nch, dst, tot, x, y, slot_row, gate_row, slot_col, mod, nf.reshape(1, D), g_next.reshape(1, D),
      mod_next, mod_next)


def _stage_rows(tc):
    worst = TOP_K * tc + 2 * (ROW_CHUNK - 1) * N_EXPERTS
    return -(-worst // LANES) * LANES


def _combine_plan(eidx, dest, gate, pstarts, tc):
    T = eidx.shape[0] * eidx.shape[2]
    n_tiles = T // tc
    flat = lambda a: a.transpose(0, 2, 1).reshape(n_tiles, tc, TOP_K)
    e_f, d_f, g_f = flat(eidx), flat(dest), flat(gate)
    onehot = (e_f[..., None] == jnp.arange(N_EXPERTS, dtype=jnp.int32)).astype(jnp.int32)
    cnt = jnp.sum(onehot, axis=(1, 2))
    first = pstarts[None, :] + jnp.cumsum(cnt, axis=0) - cnt
    src = first // ROW_CHUNK * ROW_CHUNK
    nch = jnp.where(cnt > 0, (first + cnt - src + ROW_CHUNK - 1) // ROW_CHUNK, 0)
    dst = ROW_CHUNK * (jnp.cumsum(nch, axis=1) - nch)
    slot = d_f + jnp.sum(onehot * (dst - src)[:, None, None, :], axis=-1)
    return (src.reshape(-1), nch.reshape(-1), dst.reshape(-1), jnp.sum(nch, axis=1),
            slot.transpose(0, 2, 1), g_f.transpose(0, 2, 1), slot)


def kernel(x, c, w_ada, b_ada, norm_mix, norm_moe, w_in, b_forget, w_pool, pool_scale, w_branch,
           w_gate, b_gate, w_out, w_router, b_router, w_exp_gate, w_exp_up, w_exp_down, norm_final):
    B, S, D = x.shape
    L = w_ada.shape[0]
    W = D // 2
    nh = W // HEAD_DIM
    T = B * S
    A = T * TOP_K
    bm = min(256, A // N_EXPERTS)
    rows = (A // bm + N_EXPERTS) * bm

    mod_all = _ada(c, w_ada, b_ada)
    mod_all = mod_all.reshape(L, B, N_MOD, D).transpose(0, 2, 1, 3)[:, :, :, None, :]
    wr_t = w_router.T
    br = b_router.reshape(N_EXPERTS, 1)
    colscale = jnp.concatenate([jnp.ones((W,), _F32), jnp.full((W,), HEAD_DIM ** -0.5 * LOG2E, _F32),
                                jnp.ones((2 * W,), _F32)]).reshape(1, 4 * W)

    w_in_t = jnp.swapaxes(w_in, 1, 2)
    h = None
    for l in range(L):
        mod = mod_all[l]
        bf = jnp.zeros((1, LANES), _F32).at[0, :nh].set(b_forget[l])
        if h is None:
            fcum, h = _forget(x, w_in_t, l, nh, bf, norm=(norm_mix[l], mod, 0, 1))
        else:
            fcum = _forget(h, w_in_t, l, nh, bf)
        proj5 = _proj(h, w_in_t, l, colscale)
        pool_out = _pool(proj5, w_pool, l, pool_scale[l].reshape(1, W))
        attn = _attn(proj5, fcum, nh)
        merged = _merge(h, pool_out, attn, w_gate, b_gate, w_branch, l)
        x = _out_proj(x, merged, w_out, l, mod, 2)

        eidx, pos, gate, counts = _route(x, norm_moe[l], mod, 3, 4, wr_t, br)
        counts = counts[:, 0].astype(jnp.int32)
        pcounts = (counts + bm - 1) // bm * bm
        pends = jnp.cumsum(pcounts)
        pstarts = pends - pcounts
        dest = pos
        for e in range(N_EXPERTS):
            dest = dest + jnp.where(eidx == e, pstarts[e], 0)
        n_used = (pends[-1] // bm).astype(jnp.int32)
        blk_ids = jnp.arange(rows // bm, dtype=jnp.int32)
        blk_ids = jnp.minimum(blk_ids, n_used - 1)
        block_expert = jnp.sum((pends[None, :] <= (blk_ids * bm)[:, None]).astype(jnp.int32), axis=1)
        block_expert = jnp.minimum(block_expert, N_EXPERTS - 1)
        ids = jnp.arange(N_EXPERTS, dtype=jnp.int32)
        later = (ids[None, :] > ids[:, None]) & (pcounts[None, :] > 0)
        next_of = jnp.min(jnp.where(later, ids[None, :], N_EXPERTS), axis=1)
        next_of = jnp.where(next_of == N_EXPERTS, -1, next_of)
        next_expert = jnp.sum(jnp.where(block_expert[:, None] == ids[None, :], next_of[None, :], 0), axis=1)
        n_used = n_used.reshape(1)
        td = min(DISPATCH_TILE, S)
        dest_d = dest.transpose(0, 2, 1).reshape(T // td, td, TOP_K).transpose(0, 2, 1)
        buf = _dispatch(x, norm_moe[l], mod, 3, 4, dest_d, pstarts + counts, pcounts - counts,
                        n_used, rows, bm)
        y = _experts(buf, block_expert, next_expert.astype(jnp.int32), n_used,
                     w_exp_gate, w_exp_up, w_exp_down, l, bm)
        plan = _combine_plan(eidx, dest, gate, pstarts, min(COMBINE_TILE, S))
        if l + 1 < L:
            x, h = _combine(x, y, *plan, mod, 5, norm_final, False, norm_mix[l + 1], mod_all[l + 1])
        else:
            x = _combine(x, y, *plan, mod, 5, norm_final, True, None, None)
    return x
```

```python
import functools

import jax
import jax.numpy as jnp
from jax import lax
from jax.experimental import pallas as pl
from jax.experimental.pallas import tpu as pltpu

N_MOD = 6
EPS = 1e-6
POOL_WINDOWS = (2, 4, 8, 16)
HEAD_DIM = 128
N_EXPERTS = 16
N_EXPERT_GROUPS = 4
EXPERTS_PER_GROUP = N_EXPERTS // N_EXPERT_GROUPS
TOP_K = 2
LOG2E = 1.4426950408889634
ROUTE_TILE = 1024
DISPATCH_TILE = 512
COMBINE_TILE = 256
LANES = 128
VMEM_LIMIT = 56 * 1024 * 1024

_F32 = jnp.float32
_BF16 = jnp.bfloat16


def _params(sem, vmem=VMEM_LIMIT):
    return pltpu.CompilerParams(dimension_semantics=sem, vmem_limit_bytes=vmem)


def _sigmoid(v):
    return 1.0 / (1.0 + jnp.exp(-v))


def _rms_mod(x, g, scale, shift):
    ms = jnp.mean(x * x, axis=-1, keepdims=True)
    return x * lax.rsqrt(ms + EPS) * g * (1.0 + scale) + shift


def _ada_kernel(c_ref, w_ref, b_ref, o_ref):
    c = c_ref[...]
    ca = (c * _sigmoid(c)).astype(_BF16)
    o_ref[...] = jnp.dot(ca, w_ref[...].astype(_BF16), preferred_element_type=_F32) + b_ref[...]


def _ada(c, w_ada, b_ada):
    L, D, M = w_ada.shape
    B = c.shape[0]
    tn = min(1024, M)
    return pl.pallas_call(
        _ada_kernel,
        grid=(L, M // tn),
        in_specs=[
            pl.BlockSpec((B, D), lambda l, n: (0, 0)),
            pl.BlockSpec((None, D, tn), lambda l, n: (l, 0, n)),
            pl.BlockSpec((None, 1, tn), lambda l, n: (l, 0, n)),
        ],
        out_specs=pl.BlockSpec((None, B, tn), lambda l, n: (l, 0, n)),
        out_shape=jax.ShapeDtypeStruct((L, B, M), _F32),
        compiler_params=_params(("arbitrary", "arbitrary")),
        name="ada_mod",
    )(c, w_ada, b_ada.reshape(L, 1, M))


def _first_inner_step():
    return (pl.program_id(1) == 0) & (pl.program_id(2) == 0)


def _proj_kernel(h_ref, w_ref, cs_ref, o_ref, wb_ref):
    @pl.when(_first_inner_step())
    def _():
        wb_ref[...] = (w_ref[...].T * cs_ref[...]).astype(_BF16)

    acc = jnp.dot(h_ref[...], wb_ref[...], preferred_element_type=_F32)
    for j in range(o_ref.shape[0]):
        o_ref[j] = acc[:, j * LANES:(j + 1) * LANES].astype(o_ref.dtype)


def _proj(h, w_in_t, layer, colscale):
    B, S, D = h.shape
    N = colscale.shape[1]
    tn = min(1024, N)
    tm = min(1024, S)
    return pl.pallas_call(
        _proj_kernel,
        grid=(N // tn, B, S // tm),
        in_specs=[
            pl.BlockSpec((None, tm, D), lambda n, b, m: (b, m, 0)),
            pl.BlockSpec((None, tn, D), lambda n, b, m: (layer, n, 0)),
            pl.BlockSpec((1, tn), lambda n, b, m: (0, n)),
        ],
        out_specs=pl.BlockSpec((None, tn // LANES, tm, LANES), lambda n, b, m: (b, n, m, 0)),
        out_shape=jax.ShapeDtypeStruct((B, N // LANES, S, LANES), _BF16),
        scratch_shapes=[pltpu.VMEM((D, tn), _BF16)],
        compiler_params=_params(("arbitrary", "arbitrary", "arbitrary")),
        name="in_proj",
    )(h, w_in_t, colscale)


def _forget_kernel(*refs, from_x):
    if from_x:
        x_ref, g_ref, sc_ref, sh_ref, w_ref, b_ref, o_ref, h_ref, carry = refs
        h = _rms_mod(x_ref[...], g_ref[...], sc_ref[...], sh_ref[...]).astype(_BF16)
        h_ref[...] = h
    else:
        hin_ref, w_ref, b_ref, o_ref, carry = refs
        h = hin_ref[...]
    ts, D = h.shape
    nh = w_ref.shape[0]

    @pl.when(pl.program_id(1) == 0)
    def _():
        carry[...] = jnp.zeros_like(carry)

    w = jnp.concatenate([w_ref[...], jnp.zeros((LANES - nh, D), _F32)], axis=0).astype(_BF16)
    fl = lax.dot_general(h, w, (((1,), (1,)), ((), ())), preferred_element_type=_F32) + b_ref[...]
    acc = jnp.minimum(fl, 0.0) - jnp.log(1.0 + jnp.exp(-jnp.abs(fl)))
    row = lax.broadcasted_iota(jnp.int32, acc.shape, 0)
    sh = 1
    while sh < ts:
        acc = acc + jnp.where(row >= sh, pltpu.roll(acc, sh, axis=0), 0.0)
        sh *= 2
    acc = acc + carry[...]
    o_ref[...] = acc
    carry[...] = acc[ts - 1:ts, :]


def _forget(h_or_x, w_in_t, layer, nh, bf, norm=None):
    B, S, D = h_or_x.shape
    ts = min(1024, S)
    first = (w_in_t.shape[1] - nh) // nh
    rows = pl.BlockSpec((None, ts, D), lambda b, s: (b, s, 0))
    tail_specs = [pl.BlockSpec((None, nh, D), lambda b, s: (layer, first, 0)),
                  pl.BlockSpec((1, LANES), lambda b, s: (0, 0))]
    f_spec = pl.BlockSpec((None, ts, LANES), lambda b, s: (b, s, 0))
    f_shape = jax.ShapeDtypeStruct((B, S, LANES), _F32)
    if norm is None:
        in_specs, args = [rows] + tail_specs, (h_or_x, w_in_t, bf)
        out_specs, out_shape = f_spec, f_shape
    else:
        g, mod, i_shift, i_scale = norm
        in_specs = [rows, pl.BlockSpec((1, D), lambda b, s: (0, 0)),
                    pl.BlockSpec((None, None, 1, D), lambda b, s: (i_scale, b, 0, 0)),
                    pl.BlockSpec((None, None, 1, D), lambda b, s: (i_shift, b, 0, 0))] + tail_specs
        args = (h_or_x, g.reshape(1, D), mod, mod, w_in_t, bf)
        out_specs, out_shape = [f_spec, rows], [f_shape, jax.ShapeDtypeStruct((B, S, D), _BF16)]
    return pl.pallas_call(
        functools.partial(_forget_kernel, from_x=norm is not None),
        grid=(B, S // ts),
        in_specs=in_specs,
        out_specs=out_specs,
        out_shape=out_shape,
        scratch_shapes=[pltpu.VMEM((1, LANES), _F32)],
        compiler_params=_params(("arbitrary", "arbitrary")),
        name="forget_cumsum",
    )(*args)


def _pool_kernel(u_ref, wp_ref, ps_ref, o_ref, *, cpg):
    S = u_ref.shape[1]
    gw = cpg * LANES
    row = lax.broadcasted_iota(jnp.int32, (S, gw), 0)
    for g, w in enumerate(POOL_WINDOWS):
        parts = [u_ref[g * cpg + j] for j in range(cpg)]
        u = (parts[0] if cpg == 1 else jnp.concatenate(parts, axis=-1)).astype(_F32)
        s = u
        sh = 1
        while sh < w:
            s = s + jnp.where(row >= sh, pltpu.roll(s, sh, axis=0), 0.0)
            sh *= 2
        cnt = jnp.minimum(row + 1, w).astype(_F32)
        pooled = s / cnt - u
        mixed = jnp.dot(pooled.astype(_BF16), wp_ref[g].astype(_BF16), preferred_element_type=_F32)
        o_ref[:, g * gw:(g + 1) * gw] = (mixed * ps_ref[:, g * gw:(g + 1) * gw]).astype(o_ref.dtype)


def _pool(proj5, w_pool, layer, ps):
    B, _, S, _ = proj5.shape
    _, G, gw, _ = w_pool.shape
    cpg = gw // LANES
    W = G * gw
    return pl.pallas_call(
        functools.partial(_pool_kernel, cpg=cpg),
        grid=(B,),
        in_specs=[
            pl.BlockSpec((None, G * cpg, S, LANES), lambda b: (b, 0, 0, 0)),
            pl.BlockSpec((None, G, gw, gw), lambda b: (layer, 0, 0, 0)),
            pl.BlockSpec((1, W), lambda b: (0, 0)),
        ],
        out_specs=pl.BlockSpec((None, S, W), lambda b: (b, 0, 0)),
        out_shape=jax.ShapeDtypeStruct((B, S, W), _BF16),
        compiler_params=_params(("arbitrary",)),
        name="pool_mixer",
    )(proj5, w_pool, ps)


def _attn_kernel(q_ref, k_ref, v_ref, f_ref, o_ref, qa_ref, ka_ref, va_ref, *, blk):
    S = q_ref.shape[0]
    h = pl.program_id(1)
    lane = lax.broadcasted_iota(jnp.int32, (S, LANES), 1)
    f = jnp.sum(jnp.where(lane == h, f_ref[...], 0.0), axis=-1, keepdims=True) * LOG2E
    f1 = f.astype(_BF16).astype(_F32)
    r1 = f - f1
    f2 = r1.astype(_BF16).astype(_F32)
    f3 = (r1 - f2).astype(_BF16).astype(_F32)
    qa_ref[:, :HEAD_DIM] = q_ref[...]
    qa_ref[:, HEAD_DIM:] = jnp.where(lane == 0, f1, jnp.where(lane == 1, f2, jnp.where(
        lane == 2, f3, jnp.where(lane < 6, 1.0, 0.0)))).astype(_BF16)
    ka_ref[:, :HEAD_DIM] = k_ref[...]
    ka_ref[:, HEAD_DIM:] = jnp.where(lane < 3, 1.0, jnp.where(lane == 3, -f1, jnp.where(
        lane == 4, -f2, jnp.where(lane == 5, -f3, 0.0)))).astype(_BF16)
    va_ref[:, :HEAD_DIM] = v_ref[...]
    va_ref[:, HEAD_DIM:] = jnp.where(lane == 0, 1.0, 0.0).astype(_BF16)

    nt = (((1,), (1,)), ((), ()))
    r_io = lax.broadcasted_iota(jnp.int32, (blk, blk), 0)
    c_io = lax.broadcasted_iota(jnp.int32, (blk, blk), 1)
    causal = c_io <= r_io
    nb = S // blk
    m = [jnp.full((blk, 1), -jnp.inf, _F32)] * nb
    acc = [jnp.zeros((blk, 2 * HEAD_DIM), _F32)] * nb
    for j in range(nb):
        ka = ka_ref[j * blk:(j + 1) * blk, :]
        va = va_ref[j * blk:(j + 1) * blk, :]
        for i in range(j, nb):
            s = lax.dot_general(qa_ref[i * blk:(i + 1) * blk, :], ka, nt, preferred_element_type=_F32)
            if i == j:
                s = jnp.where(causal, s, -jnp.inf)
            m_new = jnp.maximum(m[i], jnp.max(s, axis=-1, keepdims=True))
            p = jnp.exp2(s - m_new).astype(_BF16)
            acc[i] = jnp.exp2(m[i] - m_new) * acc[i] + jnp.dot(p, va, preferred_element_type=_F32)
            m[i] = m_new
    for i in range(nb):
        o_ref[i * blk:(i + 1) * blk, :] = (
            acc[i][:, :HEAD_DIM] / acc[i][:, HEAD_DIM:HEAD_DIM + 1]).astype(o_ref.dtype)


def _attn(proj5, fcum, nh):
    B, _, S, _ = proj5.shape
    blk = min(512, S)
    chunk = lambda o: pl.BlockSpec((None, None, S, HEAD_DIM), lambda b, h: (b, o + h, 0, 0))
    return pl.pallas_call(
        functools.partial(_attn_kernel, blk=blk),
        grid=(B, nh),
        in_specs=[
            chunk(nh), chunk(2 * nh), chunk(3 * nh),
            pl.BlockSpec((None, S, LANES), lambda b, h: (b, 0, 0)),
        ],
        out_specs=pl.BlockSpec((None, None, S, HEAD_DIM), lambda b, h: (b, h, 0, 0)),
        out_shape=jax.ShapeDtypeStruct((B, nh, S, HEAD_DIM), _BF16),
        scratch_shapes=[pltpu.VMEM((S, 2 * HEAD_DIM), _BF16)] * 3,
        compiler_params=_params(("arbitrary", "arbitrary")),
        name="forget_attn",
    )(proj5, proj5, proj5, fcum)


def _merge_kernel(h_ref, p_ref, a_ref, wg0_ref, wg1_ref, bg0_ref, bg1_ref, wb0_ref, wb1_ref, o_ref,
                  cg0_ref, cg1_ref, cb0_ref, cb1_ref):
    @pl.when(_first_inner_step())
    def _():
        cg0_ref[...] = wg0_ref[...].astype(_BF16)
        cg1_ref[...] = wg1_ref[...].astype(_BF16)
        cb0_ref[...] = wb0_ref[...].astype(_BF16)
        cb1_ref[...] = wb1_ref[...].astype(_BF16)

    h = h_ref[...]
    nh = a_ref.shape[0]
    a = jnp.concatenate([a_ref[j] for j in range(nh)], axis=-1)
    g0 = _sigmoid(jnp.dot(h, cg0_ref[...], preferred_element_type=_F32) + bg0_ref[...])
    y0 = jnp.dot(p_ref[...], cb0_ref[...], preferred_element_type=_F32)
    acc = g0 * y0
    g1 = _sigmoid(jnp.dot(h, cg1_ref[...], preferred_element_type=_F32) + bg1_ref[...])
    y1 = jnp.dot(a, cb1_ref[...], preferred_element_type=_F32)
    o_ref[...] = (acc + g1 * y1).astype(o_ref.dtype)


def _merge(h, pool_out, attn, w_gate, b_gate, w_branch, layer):
    B, S, D = h.shape
    W = pool_out.shape[-1]
    nh = attn.shape[1]
    L = w_gate.shape[0]
    tn = min(512, D)
    tm = min(1024, S)
    nn = D // tn
    bg = b_gate.reshape(L, 1, 2 * D)
    once = pl.Buffered(1)
    return pl.pallas_call(
        _merge_kernel,
        grid=(nn, B, S // tm),
        in_specs=[
            pl.BlockSpec((None, tm, D), lambda n, b, m: (b, m, 0)),
            pl.BlockSpec((None, tm, W), lambda n, b, m: (b, m, 0)),
            pl.BlockSpec((None, nh, tm, HEAD_DIM), lambda n, b, m: (b, 0, m, 0)),
            pl.BlockSpec((None, D, tn), lambda n, b, m: (layer, 0, n), pipeline_mode=once),
            pl.BlockSpec((None, D, tn), lambda n, b, m: (layer, 0, nn + n), pipeline_mode=once),
            pl.BlockSpec((None, 1, tn), lambda n, b, m: (layer, 0, n)),
            pl.BlockSpec((None, 1, tn), lambda n, b, m: (layer, 0, nn + n)),
            pl.BlockSpec((None, None, W, tn), lambda n, b, m: (layer, 0, 0, n), pipeline_mode=once),
            pl.BlockSpec((None, None, W, tn), lambda n, b, m: (layer, 1, 0, n), pipeline_mode=once),
        ],
        out_specs=pl.BlockSpec((None, tm, tn), lambda n, b, m: (b, m, n)),
        out_shape=jax.ShapeDtypeStruct((B, S, D), _BF16),
        scratch_shapes=[pltpu.VMEM((D, tn), _BF16), pltpu.VMEM((D, tn), _BF16),
                        pltpu.VMEM((W, tn), _BF16), pltpu.VMEM((W, tn), _BF16)],
        compiler_params=_params(("arbitrary", "arbitrary", "arbitrary")),
        name="branch_merge",
    )(h, pool_out, attn, w_gate, w_gate, bg, bg, w_branch, w_branch)


def _out_kernel(x_ref, m_ref, w_ref, g_ref, o_ref, wb_ref):
    @pl.when(_first_inner_step())
    def _():
        wb_ref[...] = w_ref[...].astype(_BF16)

    y = jnp.dot(m_ref[...], wb_ref[...], preferred_element_type=_F32)
    o_ref[...] = x_ref[...] + g_ref[...] * y


def _out_proj(x, merged, w_out, layer, mod, i_gate):
    B, S, D = x.shape
    tm = min(1024, S)
    tn = min(1024, D)
    return pl.pallas_call(
        _out_kernel,
        grid=(D // tn, B, S // tm),
        in_specs=[
            pl.BlockSpec((None, tm, tn), lambda n, b, m: (b, m, n)),
            pl.BlockSpec((None, tm, D), lambda n, b, m: (b, m, 0)),
            pl.BlockSpec((None, D, tn), lambda n, b, m: (layer, 0, n)),
            pl.BlockSpec((None, None, 1, tn), lambda n, b, m: (i_gate, b, 0, n)),
        ],
        out_specs=pl.BlockSpec((None, tm, tn), lambda n, b, m: (b, m, n)),
        out_shape=jax.ShapeDtypeStruct((B, S, D), _F32),
        scratch_shapes=[pltpu.VMEM((D, tn), _BF16)],
        compiler_params=_params(("arbitrary", "arbitrary", "arbitrary")),
        name="out_proj",
    )(x, merged, w_out, mod)


def _route_tile(x, g_ref, sc_ref, sh_ref, wr_ref, br_ref, e_ref, pos_ref, gate_ref, cnt_ref):
    first_step = (pl.program_id(0) == 0) & (pl.program_id(1) == 0)

    @pl.when(first_step)
    def _():
        cnt_ref[...] = jnp.zeros_like(cnt_ref)

    h = _rms_mod(x, g_ref[...], sc_ref[...], sh_ref[...])
    tr, D = h.shape
    ns = tr // LANES

    nt = (((1,), (1,)), ((), ()))
    h_hi = h.astype(_BF16)
    h_lo = (h - h_hi.astype(_F32)).astype(_BF16)
    w = wr_ref[...]
    w_hi = w.astype(_BF16)
    w_lo = (w - w_hi.astype(_F32)).astype(_BF16)
    both = lax.dot_general(jnp.concatenate([w_hi, w_lo], axis=0), h_hi, nt, preferred_element_type=_F32)
    logits = (both[:N_EXPERTS] + both[N_EXPERTS:]
              + lax.dot_general(w_hi, h_lo, nt, preferred_element_type=_F32))
    rows = [jnp.concatenate([logits[e:e + 1, s * LANES:(s + 1) * LANES] for s in range(ns)], axis=0)
            for e in range(N_EXPERTS)]
    mx = functools.reduce(jnp.maximum, rows)
    ex = [jnp.exp(r - mx) for r in rows]
    den = functools.reduce(lambda a, b: a + b, ex)
    probs = [v / den for v in ex]
    sel = [probs[e] + br_ref[e:e + 1, :] for e in range(N_EXPERTS)]

    in_top = [None] * N_EXPERTS
    gscore = []
    for g in range(N_EXPERT_GROUPS):
        ids = range(g * EXPERTS_PER_GROUP, (g + 1) * EXPERTS_PER_GROUP)
        score = None
        for i in ids:
            rank = None
            for j in ids:
                if j == i:
                    continue
                beats = (sel[j] > sel[i]) | ((sel[j] == sel[i]) if j < i else False)
                beats = beats.astype(_F32)
                rank = beats if rank is None else rank + beats
            in_top[i] = rank < float(TOP_K)
            term = jnp.where(in_top[i], sel[i], 0.0)
            score = term if score is None else score + term
        gscore.append(score)
    chosen = []
    for g in range(N_EXPERT_GROUPS):
        lose = None
        for g2 in range(N_EXPERT_GROUPS):
            if g2 == g:
                continue
            b = (gscore[g2] > gscore[g]) | ((gscore[g2] == gscore[g]) if g2 < g else False)
            lose = b if lose is None else (lose | b)
        chosen.append(jnp.logical_not(lose))
    picked = [in_top[e] & chosen[e // EXPERTS_PER_GROUP] for e in range(N_EXPERTS)]
    pf = [p.astype(_F32) for p in picked]

    n = N_EXPERTS * ns
    onehot = jnp.concatenate(pf, axis=0).astype(_BF16)
    upper = (lax.broadcasted_iota(jnp.int32, (LANES, LANES), 0)
             <= lax.broadcasted_iota(jnp.int32, (LANES, LANES), 1)).astype(_BF16)
    incl = jnp.dot(onehot, upper, preferred_element_type=_F32)
    r_io = lax.broadcasted_iota(jnp.int32, (n, n), 0)
    c_io = lax.broadcasted_iota(jnp.int32, (n, n), 1)
    shift = ns.bit_length() - 1
    same_expert = lax.shift_right_logical(r_io, shift) == lax.shift_right_logical(c_io, shift)
    earlier = jnp.where(same_expert & (c_io < r_io), 1.0, 0.0).astype(_BF16)
    carry = jnp.dot(earlier, incl.astype(_BF16), preferred_element_type=_F32)[:, LANES - 1:LANES]
    base = cnt_ref[:, 0:1]
    posm = [incl[e * ns:(e + 1) * ns, :] - 1.0 + carry[e * ns:(e + 1) * ns, :] + base[e:e + 1, :]
            for e in range(N_EXPERTS)]
    last = [(incl[(e + 1) * ns - 1:(e + 1) * ns, LANES - 1:LANES] + carry[(e + 1) * ns - 1:(e + 1) * ns, :])
            for e in range(N_EXPERTS)]
    cnt_ref[...] = jnp.broadcast_to(base + jnp.concatenate(last, axis=0), cnt_ref.shape)

    gnum = [pf[e] * probs[e] for e in range(N_EXPERTS)]
    gden = functools.reduce(lambda a, b: a + b, gnum)
    zero = jnp.zeros_like(pf[0])
    seen = zero
    e_out = [zero, zero]
    p_out = [zero, zero]
    g_out = [zero, zero]
    for e in range(N_EXPERTS):
        for k in range(TOP_K):
            hit = pf[e] * (seen == float(k)).astype(_F32)
            e_out[k] = e_out[k] + hit * float(e)
            p_out[k] = p_out[k] + hit * posm[e]
            g_out[k] = g_out[k] + hit * gnum[e]
        seen = seen + pf[e]
    for k in range(TOP_K):
        e_ref[k] = e_out[k].astype(jnp.int32)
        pos_ref[k] = p_out[k].astype(jnp.int32)
        gate_ref[k] = g_out[k] / gden


def _route_kernel(x_ref, *refs):
    _route_tile(x_ref[...], *refs)


def _route(x, g, mod, i_shift, i_scale, wr_t, br):
    B, S, D = x.shape
    tr = min(ROUTE_TILE, S)
    ns = tr // LANES
    assert ns & (ns - 1) == 0
    nb = S // tr
    E = N_EXPERTS
    tok = lambda dt: jax.ShapeDtypeStruct((B * nb, TOP_K, ns, LANES), dt)
    tok_spec = pl.BlockSpec((None, TOP_K, ns, LANES), lambda b, s: (b * nb + s, 0, 0, 0))
    eidx, pos, gate, counts = pl.pallas_call(
        _route_kernel,
        grid=(B, nb),
        in_specs=[
            pl.BlockSpec((None, tr, D), lambda b, s: (b, s, 0)),
            pl.BlockSpec((1, D), lambda b, s: (0, 0)),
            pl.BlockSpec((None, None, 1, D), lambda b, s: (i_scale, b, 0, 0)),
            pl.BlockSpec((None, None, 1, D), lambda b, s: (i_shift, b, 0, 0)),
            pl.BlockSpec((E, D), lambda b, s: (0, 0)),
            pl.BlockSpec((E, 1), lambda b, s: (0, 0)),
        ],
        out_specs=[
            tok_spec, tok_spec, tok_spec,
            pl.BlockSpec((E, LANES), lambda b, s: (0, 0)),
        ],
        out_shape=[
            tok(jnp.int32), tok(jnp.int32), tok(_F32),
            jax.ShapeDtypeStruct((E, LANES), _F32),
        ],
        compiler_params=_params(("arbitrary", "arbitrary")),
        name="moe_route",
    )(x, g.reshape(1, D), mod, mod, wr_t, br)
    flat = lambda a: a.reshape(B * nb, TOP_K, tr)
    return flat(eidx), flat(pos), flat(gate), counts


def _dispatch_kernel(pad_start_ref, pad_len_ref, nu_ref, dest_ref, x0_ref, xn_ref, g_ref,
                     sc0_ref, sh0_ref, scn_ref, shn_ref, buf_ref, h_ref, zblk, sem):
    i = pl.program_id(0)
    tr = xn_ref.shape[0]

    @pl.when(i == 0)
    def _():
        h_ref[0] = _rms_mod(x0_ref[...], g_ref[...], sc0_ref[...], sh0_ref[...])

    def row_wait():
        pltpu.make_async_copy(h_ref.at[0, pl.ds(0, 1), :], buf_ref.at[pl.ds(0, 1), :], sem).wait()

    def step(slot):
        for r in range(tr):
            for k in range(TOP_K):
                pltpu.make_async_copy(h_ref.at[slot, pl.ds(r, 1), :],
                                      buf_ref.at[pl.ds(dest_ref[k, r], 1), :], sem).start(priority=k)

        @pl.when(i + 1 < pl.num_programs(0))
        def _():
            h_ref[1 - slot] = _rms_mod(xn_ref[...], g_ref[...], scn_ref[...], shn_ref[...])

        for _ in range(tr * TOP_K):
            row_wait()

    for slot in range(2):
        pl.when(lax.rem(i, 2) == slot)(functools.partial(step, slot))

    @pl.when(pl.program_id(0) == 0)
    def _():
        bm = zblk.shape[0]
        nblk = buf_ref.shape[0] // bm
        zblk[...] = jnp.zeros_like(zblk)
        for e in range(N_EXPERTS):
            def zissue(r, carry, e=e):
                pltpu.make_async_copy(zblk.at[pl.ds(0, 1), :],
                                      buf_ref.at[pl.ds(pad_start_ref[e] + r, 1), :], sem).start()
                return carry
            lax.fori_loop(0, pad_len_ref[e], zissue, 0)
        for e in range(N_EXPERTS):
            def zdrain(r, carry):
                row_wait()
                return carry
            lax.fori_loop(0, pad_len_ref[e], zdrain, 0)

        def blk_copy(i):
            return pltpu.make_async_copy(zblk, buf_ref.at[pl.ds(pl.multiple_of(i * bm, bm), bm), :], sem)

        def bissue(i, carry):
            blk_copy(i).start()
            return carry

        def bdrain(i, carry):
            blk_copy(i).wait()
            return carry

        lax.fori_loop(nu_ref[0], nblk, bissue, 0)
        lax.fori_loop(nu_ref[0], nblk, bdrain, 0)


def _dispatch(x, g, mod, i_shift, i_scale, dest, pad_start, pad_len, n_used, rows, bm):
    B, S, D = x.shape
    nbt, _, tr = dest.shape
    nb = S // tr
    nxt = lambda i: jnp.minimum(i + 1, nbt - 1)
    grid_spec = pltpu.PrefetchScalarGridSpec(
        num_scalar_prefetch=3,
        grid=(nbt,),
        in_specs=[
            pl.BlockSpec((None, TOP_K, tr), lambda i, *_: (i, 0, 0), memory_space=pltpu.SMEM),
            pl.BlockSpec((None, tr, D), lambda i, *_: (0, 0, 0)),
            pl.BlockSpec((None, tr, D), lambda i, *_: (nxt(i) // nb, nxt(i) % nb, 0)),
            pl.BlockSpec((1, D), lambda i, *_: (0, 0)),
            pl.BlockSpec((None, None, 1, D), lambda i, *_: (i_scale, 0, 0, 0)),
            pl.BlockSpec((None, None, 1, D), lambda i, *_: (i_shift, 0, 0, 0)),
            pl.BlockSpec((None, None, 1, D), lambda i, *_: (i_scale, nxt(i) // nb, 0, 0)),
            pl.BlockSpec((None, None, 1, D), lambda i, *_: (i_shift, nxt(i) // nb, 0, 0)),
        ],
        out_specs=pl.BlockSpec(memory_space=pl.ANY),
        scratch_shapes=[pltpu.VMEM((2, tr, D), _F32), pltpu.VMEM((bm, D), _F32),
                        pltpu.SemaphoreType.DMA(())],
    )
    return pl.pallas_call(
        _dispatch_kernel,
        grid_spec=grid_spec,
        out_shape=jax.ShapeDtypeStruct((rows, D), _F32),
        compiler_params=_params(("arbitrary",)),
        name="moe_dispatch",
    )(pad_start, pad_len, n_used, dest, x, x, g.reshape(1, D), mod, mod, mod, mod)


def _expert_kernel(be_ref, nx_ref, nu_ref, x_ref, wg_hbm, wu_hbm, wd_hbm, o_ref,
                   wgb, wub, wdb, sg, su, sd, sem, *, layer):
    i = pl.program_id(0)
    e = be_ref[i]
    live = i < nu_ref[0]
    new_expert = (i == 0) | (e != be_ref[jnp.maximum(i - 1, 0)])

    def weight_copies(ex):
        return [pltpu.make_async_copy(w.at[layer, ex], s, sem.at[n])
                for n, (w, s) in enumerate(((wg_hbm, sg), (wu_hbm, su), (wd_hbm, sd)))]

    @pl.when(live & new_expert)
    def _():
        @pl.when(i == 0)
        def _():
            for cp in weight_copies(e):
                cp.start()

        for cp in weight_copies(e):
            cp.wait()
        wgb[...] = sg[...].astype(_BF16)
        wub[...] = su[...].astype(_BF16)
        wdb[...] = sd[...].astype(_BF16)

        @pl.when(nx_ref[i] >= 0)
        def _():
            for cp in weight_copies(nx_ref[i]):
                cp.start(priority=1)

    @pl.when(live)
    def _():
        xb = x_ref[...].astype(_BF16)
        a = jnp.dot(xb, wgb[...], preferred_element_type=_F32)
        u = jnp.dot(xb, wub[...], preferred_element_type=_F32)
        act = (a * _sigmoid(a) * u).astype(_BF16)
        o_ref[...] = jnp.dot(act, wdb[...], preferred_element_type=_F32)

    @pl.when(jnp.logical_not(live))
    def _():
        o_ref[...] = jnp.zeros_like(o_ref)


def _experts(buf, block_expert, next_expert, n_used, wg, wu, wd, layer, bm):
    rows, D = buf.shape
    De = wg.shape[-1]
    nblk = rows // bm
    grid_spec = pltpu.PrefetchScalarGridSpec(
        num_scalar_prefetch=3,
        grid=(nblk,),
        in_specs=[
            pl.BlockSpec((bm, D), lambda i, be, nx, nu: (jnp.minimum(i, jnp.maximum(nu[0] - 1, 0)), 0)),
            pl.BlockSpec(memory_space=pl.ANY),
            pl.BlockSpec(memory_space=pl.ANY),
            pl.BlockSpec(memory_space=pl.ANY),
        ],
        out_specs=pl.BlockSpec((bm, D), lambda i, be, nx, nu: (i, 0)),
        scratch_shapes=[
            pltpu.VMEM((D, De), _BF16), pltpu.VMEM((D, De), _BF16), pltpu.VMEM((De, D), _BF16),
            pltpu.VMEM((D, De), _F32), pltpu.VMEM((D, De), _F32), pltpu.VMEM((De, D), _F32),
            pltpu.SemaphoreType.DMA((3,)),
        ],
    )
    return pl.pallas_call(
        functools.partial(_expert_kernel, layer=layer),
        grid_spec=grid_spec,
        out_shape=jax.ShapeDtypeStruct((rows, D), _F32),
        compiler_params=_params(("arbitrary",), vmem=60 * 1024 * 1024),
        name="moe_experts",
    )(block_expert, next_expert, n_used, buf, wg, wu, wd)


ROW_CHUNK = 8


def _combine_kernel(src_ref, nch_ref, dst_ref, tot_ref,
                    x_ref, y_hbm, srow_ref, grow_ref, scol_ref, gm_ref, nf_ref, gn_ref, scn_ref, shn_ref,
                    *rest, final, nb, has_next):
    if has_next:
        o_ref, h_ref, ystage, ysb, sem = rest
    else:
        o_ref, ystage, ysb, sem = rest
        h_ref = None
    i = pl.program_id(0) * nb + pl.program_id(1)
    n_tiles = pl.num_programs(0) * nb
    cur = lax.rem(i, 2)
    tc = x_ref.shape[0]
    R = ystage.shape[1]

    def chunk_copy(src, dst, buf):
        return pltpu.make_async_copy(y_hbm.at[pl.ds(src, ROW_CHUNK), :],
                                     ystage.at[buf, pl.ds(dst, ROW_CHUNK), :], sem.at[buf])

    def start_tile(tile, buf):
        for e in range(N_EXPERTS):
            idx = tile * N_EXPERTS + e

            def body(c, carry, idx=idx):
                chunk_copy(pl.multiple_of(src_ref[idx] + c * ROW_CHUNK, ROW_CHUNK),
                           pl.multiple_of(dst_ref[idx] + c * ROW_CHUNK, ROW_CHUNK), buf).start()
                return carry

            lax.fori_loop(0, nch_ref[idx], body, 0)

    @pl.when(i == 0)
    def _():
        ystage[...] = jnp.zeros_like(ystage)
        start_tile(0, 0)

    @pl.when(i + 1 < n_tiles)
    def _():
        start_tile(i + 1, 1 - cur)

    def wait_one(c, carry):
        chunk_copy(0, 0, cur).wait()
        return carry

    lax.fori_loop(0, tot_ref[i], wait_one, 0)

    srow = srow_ref[...]
    grow = grow_ref[...]
    r_io = lax.broadcasted_iota(jnp.int32, (R, tc), 0)
    gate_of_row = jnp.sum(jnp.where(r_io == srow[0:1, :], grow[0:1, :], 0.0)
                          + jnp.where(r_io == srow[1:2, :], grow[1:2, :], 0.0),
                          axis=1, keepdims=True)
    step = LANES
    for r0 in range(0, R, step):
        ysb[r0:r0 + step, :] = (ystage[cur, r0:r0 + step, :] * gate_of_row[r0:r0 + step, :]).astype(_BF16)
    scol = scol_ref[...]
    c_io = lax.broadcasted_iota(jnp.int32, (tc, R), 1)
    g = jnp.where((c_io == scol[:, 0:1]) | (c_io == scol[:, 1:2]), 1.0, 0.0).astype(_BF16)
    moe = jnp.dot(g, ysb[...], preferred_element_type=_F32)
    xn = x_ref[...] + gm_ref[...] * moe
    if final:
        ms = jnp.mean(xn * xn, axis=-1, keepdims=True)
        xn = xn * lax.rsqrt(ms + EPS) * nf_ref[...]
    o_ref[...] = xn
    if has_next:
        h_ref[...] = _rms_mod(xn, gn_ref[...], scn_ref[...], shn_ref[...]).astype(h_ref.dtype)


def _combine(x, y, src, nch, dst, tot, slot_row, gate_row, slot_col, mod, i_gate, nf, final, g_next, mod_next):
    B, S, D = x.shape
    n_tiles, _, tc = slot_row.shape
    nb = S // tc
    R = _stage_rows(tc)
    has_next = g_next is not None
    if not has_next:
        g_next, mod_next = nf, mod
    tile = lambda b, s, *_: (b * nb + s, 0, 0)
    row = lambda b, s, *_: (b, s, 0)
    grid_spec = pltpu.PrefetchScalarGridSpec(
        num_scalar_prefetch=4,
        grid=(B, nb),
        in_specs=[
            pl.BlockSpec((None, tc, D), row),
            pl.BlockSpec(memory_space=pl.ANY),
            pl.BlockSpec((None, TOP_K, tc), tile),
            pl.BlockSpec((None, TOP_K, tc), tile),
            pl.BlockSpec((None, tc, TOP_K), tile),
            pl.BlockSpec((None, None, 1, D), lambda b, s, *_: (i_gate, b, 0, 0)),
            pl.BlockSpec((1, D), lambda b, s, *_: (0, 0)),
            pl.BlockSpec((1, D), lambda b, s, *_: (0, 0)),
            pl.BlockSpec((None, None, 1, D), lambda b, s, *_: (1, b, 0, 0)),
            pl.BlockSpec((None, None, 1, D), lambda b, s, *_: (0, b, 0, 0)),
        ],
        out_specs=([pl.BlockSpec((None, tc, D), row)] * 2) if has_next else pl.BlockSpec((None, tc, D), row),
        scratch_shapes=[pltpu.VMEM((2, R, D), _F32), pltpu.VMEM((R, D), _BF16),
                        pltpu.SemaphoreType.DMA((2,))],
    )
    out_shape = jax.ShapeDtypeStruct((B, S, D), _F32)
    if has_next:
        out_shape = [out_shape, jax.ShapeDtypeStruct((B, S, D), _BF16)]
    return pl.pallas_call(
        functools.partial(_combine_kernel, final=final, nb=nb, has_next=has_next),
        grid_spec=grid_spec,
        out_shape=out_shape,
        compiler_params=_params(("arbitrary", "arbitrary")),
        name="moe_combine",
    )(src, nch, dst, tot, x, y, slot_row, gate_row, slot_col, mod, nf.reshape(1, D), g_next.reshape(1, D),
      mod_next, mod_next)


def _stage_rows(tc):
    worst = TOP_K * tc + 2 * (ROW_CHUNK - 1) * N_EXPERTS
    return -(-worst // LANES) * LANES


def _combine_plan(eidx, dest, gate, pstarts, tc):
    T = eidx.shape[0] * eidx.shape[2]
    n_tiles = T // tc
    flat = lambda a: a.transpose(0, 2, 1).reshape(n_tiles, tc, TOP_K)
    e_f, d_f, g_f = flat(eidx), flat(dest), flat(gate)
    onehot = (e_f[..., None] == jnp.arange(N_EXPERTS, dtype=jnp.int32)).astype(jnp.int32)
    cnt = jnp.sum(onehot, axis=(1, 2))
    first = pstarts[None, :] + jnp.cumsum(cnt, axis=0) - cnt
    src = first // ROW_CHUNK * ROW_CHUNK
    nch = jnp.where(cnt > 0, (first + cnt - src + ROW_CHUNK - 1) // ROW_CHUNK, 0)
    dst = ROW_CHUNK * (jnp.cumsum(nch, axis=1) - nch)
    slot = d_f + jnp.sum(onehot * (dst - src)[:, None, None, :], axis=-1)
    return (src.reshape(-1), nch.reshape(-1), dst.reshape(-1), jnp.sum(nch, axis=1),
            slot.transpose(0, 2, 1), g_f.transpose(0, 2, 1), slot)


def kernel(x, c, w_ada, b_ada, norm_mix, norm_moe, w_in, b_forget, w_pool, pool_scale, w_branch,
           w_gate, b_gate, w_out, w_router, b_router, w_exp_gate, w_exp_up, w_exp_down, norm_final):
    B, S, D = x.shape
    L = w_ada.shape[0]
    W = D // 2
    nh = W // HEAD_DIM
    T = B * S
    A = T * TOP_K
    bm = min(256, A // N_EXPERTS)
    rows = (A // bm + N_EXPERTS) * bm

    mod_all = _ada(c, w_ada, b_ada)
    mod_all = mod_all.reshape(L, B, N_MOD, D).transpose(0, 2, 1, 3)[:, :, :, None, :]
    wr_t = w_router.T
    br = b_router.reshape(N_EXPERTS, 1)
    colscale = jnp.concatenate([jnp.ones((W,), _F32), jnp.full((W,), HEAD_DIM ** -0.5 * LOG2E, _F32),
                                jnp.ones((2 * W,), _F32)]).reshape(1, 4 * W)

    w_in_t = jnp.swapaxes(w_in, 1, 2)
    h = None
    for l in range(L):
        mod = mod_all[l]
        bf = jnp.zeros((1, LANES), _F32).at[0, :nh].set(b_forget[l])
        if h is None:
            fcum, h = _forget(x, w_in_t, l, nh, bf, norm=(norm_mix[l], mod, 0, 1))
        else:
            fcum = _forget(h, w_in_t, l, nh, bf)
        proj5 = _proj(h, w_in_t, l, colscale)
        pool_out = _pool(proj5, w_pool, l, pool_scale[l].reshape(1, W))
        attn = _attn(proj5, fcum, nh)
        merged = _merge(h, pool_out, attn, w_gate, b_gate, w_branch, l)
        x = _out_proj(x, merged, w_out, l, mod, 2)

        eidx, pos, gate, counts = _route(x, norm_moe[l], mod, 3, 4, wr_t, br)
        counts = counts[:, 0].astype(jnp.int32)
        pcounts = (counts + bm - 1) // bm * bm
        pends = jnp.cumsum(pcounts)
        pstarts = pends - pcounts
        dest = pos
        for e in range(N_EXPERTS):
            dest = dest + jnp.where(eidx == e, pstarts[e], 0)
        n_used = (pends[-1] // bm).astype(jnp.int32)
        blk_ids = jnp.arange(rows // bm, dtype=jnp.int32)
        blk_ids = jnp.minimum(blk_ids, n_used - 1)
        block_expert = jnp.sum((pends[None, :] <= (blk_ids * bm)[:, None]).astype(jnp.int32), axis=1)
        block_expert = jnp.minimum(block_expert, N_EXPERTS - 1)
        ids = jnp.arange(N_EXPERTS, dtype=jnp.int32)
        later = (ids[None, :] > ids[:, None]) & (pcounts[None, :] > 0)
        next_of = jnp.min(jnp.where(later, ids[None, :], N_EXPERTS), axis=1)
        next_of = jnp.where(next_of == N_EXPERTS, -1, next_of)
        next_expert = jnp.sum(jnp.where(block_expert[:, None] == ids[None, :], next_of[None, :], 0), axis=1)
        n_used = n_used.reshape(1)
        td = min(DISPATCH_TILE, S)
        dest_d = dest.transpose(0, 2, 1).reshape(T // td, td, TOP_K).transpose(0, 2, 1)
        buf = _dispatch(x, norm_moe[l], mod, 3, 4, dest_d, pstarts + counts, pcounts - counts,
                        n_used, rows, bm)
        y = _experts(buf, block_expert, next_expert.astype(jnp.int32), n_used,
                     w_exp_gate, w_exp_up, w_exp_down, l, bm)
        plan = _combine_plan(eidx, dest, gate, pstarts, min(COMBINE_TILE, S))
        if l + 1 < L:
            x, h = _combine(x, y, *plan, mod, 5, norm_final, False, norm_mix[l + 1], mod_all[l + 1])
        else:
            x = _combine(x, y, *plan, mod, 5, norm_final, True, None, None)
    return x
```

```python
import functools

import jax
import jax.numpy as jnp
from jax import lax
from jax.experimental import pallas as pl
from jax.experimental.pallas import tpu as pltpu

N_MOD = 6
EPS = 1e-6
POOL_WINDOWS = (2, 4, 8, 16)
HEAD_DIM = 128
N_EXPERTS = 16
N_EXPERT_GROUPS = 4
EXPERTS_PER_GROUP = N_EXPERTS // N_EXPERT_GROUPS
TOP_K = 2
LOG2E = 1.4426950408889634
ROUTE_TILE = 512
DISPATCH_TILE = 512
COMBINE_TILE = 256
LANES = 128
VMEM_LIMIT = 56 * 1024 * 1024

_F32 = jnp.float32
_BF16 = jnp.bfloat16


def _params(sem, vmem=VMEM_LIMIT):
    return pltpu.CompilerParams(dimension_semantics=sem, vmem_limit_bytes=vmem)


def _sigmoid(v):
    return 1.0 / (1.0 + jnp.exp(-v))


def _rms_mod(x, g, scale, shift):
    ms = jnp.mean(x * x, axis=-1, keepdims=True)
    return x * lax.rsqrt(ms + EPS) * g * (1.0 + scale) + shift


def _ada_kernel(c_ref, w_ref, b_ref, o_ref):
    c = c_ref[...]
    ca = (c * _sigmoid(c)).astype(_BF16)
    o_ref[...] = jnp.dot(ca, w_ref[...].astype(_BF16), preferred_element_type=_F32) + b_ref[...]


def _ada(c, w_ada, b_ada):
    L, D, M = w_ada.shape
    B = c.shape[0]
    tn = min(1024, M)
    return pl.pallas_call(
        _ada_kernel,
        grid=(L, M // tn),
        in_specs=[
            pl.BlockSpec((B, D), lambda l, n: (0, 0)),
            pl.BlockSpec((None, D, tn), lambda l, n: (l, 0, n)),
            pl.BlockSpec((None, 1, tn), lambda l, n: (l, 0, n)),
        ],
        out_specs=pl.BlockSpec((None, B, tn), lambda l, n: (l, 0, n)),
        out_shape=jax.ShapeDtypeStruct((L, B, M), _F32),
        compiler_params=_params(("arbitrary", "arbitrary")),
        name="ada_mod",
    )(c, w_ada, b_ada.reshape(L, 1, M))


def _first_inner_step():
    return (pl.program_id(1) == 0) & (pl.program_id(2) == 0)


def _proj_kernel(h_ref, w_ref, cs_ref, o_ref, wb_ref):
    @pl.when(_first_inner_step())
    def _():
        wb_ref[...] = (w_ref[...].T * cs_ref[...]).astype(_BF16)

    acc = jnp.dot(h_ref[...], wb_ref[...], preferred_element_type=_F32)
    for j in range(o_ref.shape[0]):
        o_ref[j] = acc[:, j * LANES:(j + 1) * LANES].astype(o_ref.dtype)


def _proj(h, w_in_t, layer, colscale):
    B, S, D = h.shape
    N = colscale.shape[1]
    tn = min(1024, N)
    tm = min(1024, S)
    return pl.pallas_call(
        _proj_kernel,
        grid=(N // tn, B, S // tm),
        in_specs=[
            pl.BlockSpec((None, tm, D), lambda n, b, m: (b, m, 0)),
            pl.BlockSpec((None, tn, D), lambda n, b, m: (layer, n, 0)),
            pl.BlockSpec((1, tn), lambda n, b, m: (0, n)),
        ],
        out_specs=pl.BlockSpec((None, tn // LANES, tm, LANES), lambda n, b, m: (b, n, m, 0)),
        out_shape=jax.ShapeDtypeStruct((B, N // LANES, S, LANES), _BF16),
        scratch_shapes=[pltpu.VMEM((D, tn), _BF16)],
        compiler_params=_params(("arbitrary", "arbitrary", "arbitrary")),
        name="in_proj",
    )(h, w_in_t, colscale)


def _forget_kernel(*refs, from_x):
    if from_x:
        x_ref, g_ref, sc_ref, sh_ref, w_ref, b_ref, o_ref, h_ref, carry = refs
        h = _rms_mod(x_ref[...], g_ref[...], sc_ref[...], sh_ref[...]).astype(_BF16)
        h_ref[...] = h
    else:
        hin_ref, w_ref, b_ref, o_ref, carry = refs
        h = hin_ref[...]
    ts, D = h.shape
    nh = w_ref.shape[0]

    @pl.when(pl.program_id(1) == 0)
    def _():
        carry[...] = jnp.zeros_like(carry)

    w = jnp.concatenate([w_ref[...], jnp.zeros((LANES - nh, D), _F32)], axis=0).astype(_BF16)
    fl = lax.dot_general(h, w, (((1,), (1,)), ((), ())), preferred_element_type=_F32) + b_ref[...]
    acc = jnp.minimum(fl, 0.0) - jnp.log(1.0 + jnp.exp(-jnp.abs(fl)))
    row = lax.broadcasted_iota(jnp.int32, acc.shape, 0)
    sh = 1
    while sh < ts:
        acc = acc + jnp.where(row >= sh, pltpu.roll(acc, sh, axis=0), 0.0)
        sh *= 2
    acc = acc + carry[...]
    o_ref[...] = acc
    carry[...] = acc[ts - 1:ts, :]


def _forget(h_or_x, w_in_t, layer, nh, bf, norm=None):
    B, S, D = h_or_x.shape
    ts = min(1024, S)
    first = (w_in_t.shape[1] - nh) // nh
    rows = pl.BlockSpec((None, ts, D), lambda b, s: (b, s, 0))
    tail_specs = [pl.BlockSpec((None, nh, D), lambda b, s: (layer, first, 0)),
                  pl.BlockSpec((1, LANES), lambda b, s: (0, 0))]
    f_spec = pl.BlockSpec((None, ts, LANES), lambda b, s: (b, s, 0))
    f_shape = jax.ShapeDtypeStruct((B, S, LANES), _F32)
    if norm is None:
        in_specs, args = [rows] + tail_specs, (h_or_x, w_in_t, bf)
        out_specs, out_shape = f_spec, f_shape
    else:
        g, mod, i_shift, i_scale = norm
        in_specs = [rows, pl.BlockSpec((1, D), lambda b, s: (0, 0)),
                    pl.BlockSpec((None, None, 1, D), lambda b, s: (i_scale, b, 0, 0)),
                    pl.BlockSpec((None, None, 1, D), lambda b, s: (i_shift, b, 0, 0))] + tail_specs
        args = (h_or_x, g.reshape(1, D), mod, mod, w_in_t, bf)
        out_specs, out_shape = [f_spec, rows], [f_shape, jax.ShapeDtypeStruct((B, S, D), _BF16)]
    return pl.pallas_call(
        functools.partial(_forget_kernel, from_x=norm is not None),
        grid=(B, S // ts),
        in_specs=in_specs,
        out_specs=out_specs,
        out_shape=out_shape,
        scratch_shapes=[pltpu.VMEM((1, LANES), _F32)],
        compiler_params=_params(("arbitrary", "arbitrary")),
        name="forget_cumsum",
    )(*args)


def _pool_kernel(u_ref, wp_ref, ps_ref, o_ref, *, cpg):
    S = u_ref.shape[1]
    gw = cpg * LANES
    row = lax.broadcasted_iota(jnp.int32, (S, gw), 0)
    for g, w in enumerate(POOL_WINDOWS):
        parts = [u_ref[g * cpg + j] for j in range(cpg)]
        u = (parts[0] if cpg == 1 else jnp.concatenate(parts, axis=-1)).astype(_F32)
        s = u
        sh = 1
        while sh < w:
            s = s + jnp.where(row >= sh, pltpu.roll(s, sh, axis=0), 0.0)
            sh *= 2
        cnt = jnp.minimum(row + 1, w).astype(_F32)
        pooled = s / cnt - u
        mixed = jnp.dot(pooled.astype(_BF16), wp_ref[g].astype(_BF16), preferred_element_type=_F32)
        o_ref[:, g * gw:(g + 1) * gw] = (mixed * ps_ref[:, g * gw:(g + 1) * gw]).astype(o_ref.dtype)


def _pool(proj5, w_pool, layer, ps):
    B, _, S, _ = proj5.shape
    _, G, gw, _ = w_pool.shape
    cpg = gw // LANES
    W = G * gw
    return pl.pallas_call(
        functools.partial(_pool_kernel, cpg=cpg),
        grid=(B,),
        in_specs=[
            pl.BlockSpec((None, G * cpg, S, LANES), lambda b: (b, 0, 0, 0)),
            pl.BlockSpec((None, G, gw, gw), lambda b: (layer, 0, 0, 0)),
            pl.BlockSpec((1, W), lambda b: (0, 0)),
        ],
        out_specs=pl.BlockSpec((None, S, W), lambda b: (b, 0, 0)),
        out_shape=jax.ShapeDtypeStruct((B, S, W), _BF16),
        compiler_params=_params(("arbitrary",)),
        name="pool_mixer",
    )(proj5, w_pool, ps)


def _attn_kernel(q_ref, k_ref, v_ref, f_ref, o_ref, qa_ref, ka_ref, va_ref, *, blk):
    S = q_ref.shape[0]
    h = pl.program_id(1)
    lane = lax.broadcasted_iota(jnp.int32, (S, LANES), 1)
    f = jnp.sum(jnp.where(lane == h, f_ref[...], 0.0), axis=-1, keepdims=True) * LOG2E
    f1 = f.astype(_BF16).astype(_F32)
    r1 = f - f1
    f2 = r1.astype(_BF16).astype(_F32)
    f3 = (r1 - f2).astype(_BF16).astype(_F32)
    qa_ref[:, :HEAD_DIM] = q_ref[...]
    qa_ref[:, HEAD_DIM:] = jnp.where(lane == 0, f1, jnp.where(lane == 1, f2, jnp.where(
        lane == 2, f3, jnp.where(lane < 6, 1.0, 0.0)))).astype(_BF16)
    ka_ref[:, :HEAD_DIM] = k_ref[...]
    ka_ref[:, HEAD_DIM:] = jnp.where(lane < 3, 1.0, jnp.where(lane == 3, -f1, jnp.where(
        lane == 4, -f2, jnp.where(lane == 5, -f3, 0.0)))).astype(_BF16)
    va_ref[:, :HEAD_DIM] = v_ref[...]
    va_ref[:, HEAD_DIM:] = jnp.where(lane == 0, 1.0, 0.0).astype(_BF16)

    nt = (((1,), (1,)), ((), ()))
    r_io = lax.broadcasted_iota(jnp.int32, (blk, blk), 0)
    c_io = lax.broadcasted_iota(jnp.int32, (blk, blk), 1)
    causal = c_io <= r_io
    nb = S // blk
    m = [jnp.full((blk, 1), -jnp.inf, _F32)] * nb
    acc = [jnp.zeros((blk, 2 * HEAD_DIM), _F32)] * nb
    for j in range(nb):
        ka = ka_ref[j * blk:(j + 1) * blk, :]
        va = va_ref[j * blk:(j + 1) * blk, :]
        for i in range(j, nb):
            s = lax.dot_general(qa_ref[i * blk:(i + 1) * blk, :], ka, nt, preferred_element_type=_F32)
            if i == j:
                s = jnp.where(causal, s, -jnp.inf)
            m_new = jnp.maximum(m[i], jnp.max(s, axis=-1, keepdims=True))
            p = jnp.exp2(s - m_new).astype(_BF16)
            acc[i] = jnp.exp2(m[i] - m_new) * acc[i] + jnp.dot(p, va, preferred_element_type=_F32)
            m[i] = m_new
    for i in range(nb):
        o_ref[i * blk:(i + 1) * blk, :] = (
            acc[i][:, :HEAD_DIM] / acc[i][:, HEAD_DIM:HEAD_DIM + 1]).astype(o_ref.dtype)


def _attn(proj5, fcum, nh):
    B, _, S, _ = proj5.shape
    blk = min(512, S)
    chunk = lambda o: pl.BlockSpec((None, None, S, HEAD_DIM), lambda b, h: (b, o + h, 0, 0))
    return pl.pallas_call(
        functools.partial(_attn_kernel, blk=blk),
        grid=(B, nh),
        in_specs=[
            chunk(nh), chunk(2 * nh), chunk(3 * nh),
            pl.BlockSpec((None, S, LANES), lambda b, h: (b, 0, 0)),
        ],
        out_specs=pl.BlockSpec((None, None, S, HEAD_DIM), lambda b, h: (b, h, 0, 0)),
        out_shape=jax.ShapeDtypeStruct((B, nh, S, HEAD_DIM), _BF16),
        scratch_shapes=[pltpu.VMEM((S, 2 * HEAD_DIM), _BF16)] * 3,
        compiler_params=_params(("arbitrary", "arbitrary")),
        name="forget_attn",
    )(proj5, proj5, proj5, fcum)


def _merge_kernel(h_ref, p_ref, a_ref, wg0_ref, wg1_ref, bg0_ref, bg1_ref, wb0_ref, wb1_ref, o_ref,
                  cg0_ref, cg1_ref, cb0_ref, cb1_ref):
    @pl.when(_first_inner_step())
    def _():
        cg0_ref[...] = wg0_ref[...].astype(_BF16)
        cg1_ref[...] = wg1_ref[...].astype(_BF16)
        cb0_ref[...] = wb0_ref[...].astype(_BF16)
        cb1_ref[...] = wb1_ref[...].astype(_BF16)

    h = h_ref[...]
    nh = a_ref.shape[0]
    a = jnp.concatenate([a_ref[j] for j in range(nh)], axis=-1)
    g0 = _sigmoid(jnp.dot(h, cg0_ref[...], preferred_element_type=_F32) + bg0_ref[...])
    y0 = jnp.dot(p_ref[...], cb0_ref[...], preferred_element_type=_F32)
    acc = g0 * y0
    g1 = _sigmoid(jnp.dot(h, cg1_ref[...], preferred_element_type=_F32) + bg1_ref[...])
    y1 = jnp.dot(a, cb1_ref[...], preferred_element_type=_F32)
    o_ref[...] = (acc + g1 * y1).astype(o_ref.dtype)


def _merge(h, pool_out, attn, w_gate, b_gate, w_branch, layer):
    B, S, D = h.shape
    W = pool_out.shape[-1]
    nh = attn.shape[1]
    L = w_gate.shape[0]
    tn = min(512, D)
    tm = min(1024, S)
    nn = D // tn
    bg = b_gate.reshape(L, 1, 2 * D)
    once = pl.Buffered(1)
    return pl.pallas_call(
        _merge_kernel,
        grid=(nn, B, S // tm),
        in_specs=[
            pl.BlockSpec((None, tm, D), lambda n, b, m: (b, m, 0)),
            pl.BlockSpec((None, tm, W), lambda n, b, m: (b, m, 0)),
            pl.BlockSpec((None, nh, tm, HEAD_DIM), lambda n, b, m: (b, 0, m, 0)),
            pl.BlockSpec((None, D, tn), lambda n, b, m: (layer, 0, n), pipeline_mode=once),
            pl.BlockSpec((None, D, tn), lambda n, b, m: (layer, 0, nn + n), pipeline_mode=once),
            pl.BlockSpec((None, 1, tn), lambda n, b, m: (layer, 0, n)),
            pl.BlockSpec((None, 1, tn), lambda n, b, m: (layer, 0, nn + n)),
            pl.BlockSpec((None, None, W, tn), lambda n, b, m: (layer, 0, 0, n), pipeline_mode=once),
            pl.BlockSpec((None, None, W, tn), lambda n, b, m: (layer, 1, 0, n), pipeline_mode=once),
        ],
        out_specs=pl.BlockSpec((None, tm, tn), lambda n, b, m: (b, m, n)),
        out_shape=jax.ShapeDtypeStruct((B, S, D), _BF16),
        scratch_shapes=[pltpu.VMEM((D, tn), _BF16), pltpu.VMEM((D, tn), _BF16),
                        pltpu.VMEM((W, tn), _BF16), pltpu.VMEM((W, tn), _BF16)],
        compiler_params=_params(("arbitrary", "arbitrary", "arbitrary")),
        name="branch_merge",
    )(h, pool_out, attn, w_gate, w_gate, bg, bg, w_branch, w_branch)


def _route_tile(x, g_ref, sc_ref, sh_ref, wr_ref, br_ref, e_ref, pos_ref, gate_ref, cnt_ref):
    first_step = (pl.program_id(0) == 0) & (pl.program_id(1) == 0)

    @pl.when(first_step)
    def _():
        cnt_ref[...] = jnp.zeros_like(cnt_ref)

    h = _rms_mod(x, g_ref[...], sc_ref[...], sh_ref[...])
    tr, D = h.shape
    ns = tr // LANES

    nt = (((1,), (1,)), ((), ()))
    h_hi = h.astype(_BF16)
    h_lo = (h - h_hi.astype(_F32)).astype(_BF16)
    w = wr_ref[...]
    w_hi = w.astype(_BF16)
    w_lo = (w - w_hi.astype(_F32)).astype(_BF16)
    both = lax.dot_general(jnp.concatenate([w_hi, w_lo], axis=0), h_hi, nt, preferred_element_type=_F32)
    logits = (both[:N_EXPERTS] + both[N_EXPERTS:]
              + lax.dot_general(w_hi, h_lo, nt, preferred_element_type=_F32))
    rows = [jnp.concatenate([logits[e:e + 1, s * LANES:(s + 1) * LANES] for s in range(ns)], axis=0)
            for e in range(N_EXPERTS)]
    mx = functools.reduce(jnp.maximum, rows)
    ex = [jnp.exp(r - mx) for r in rows]
    den = functools.reduce(lambda a, b: a + b, ex)
    probs = [v / den for v in ex]
    sel = [probs[e] + br_ref[e:e + 1, :] for e in range(N_EXPERTS)]

    in_top = [None] * N_EXPERTS
    gscore = []
    for g in range(N_EXPERT_GROUPS):
        ids = range(g * EXPERTS_PER_GROUP, (g + 1) * EXPERTS_PER_GROUP)
        score = None
        for i in ids:
            rank = None
            for j in ids:
                if j == i:
                    continue
                beats = (sel[j] > sel[i]) | ((sel[j] == sel[i]) if j < i else False)
                beats = beats.astype(_F32)
                rank = beats if rank is None else rank + beats
            in_top[i] = rank < float(TOP_K)
            term = jnp.where(in_top[i], sel[i], 0.0)
            score = term if score is None else score + term
        gscore.append(score)
    chosen = []
    for g in range(N_EXPERT_GROUPS):
        lose = None
        for g2 in range(N_EXPERT_GROUPS):
            if g2 == g:
                continue
            b = (gscore[g2] > gscore[g]) | ((gscore[g2] == gscore[g]) if g2 < g else False)
            lose = b if lose is None else (lose | b)
        chosen.append(jnp.logical_not(lose))
    picked = [in_top[e] & chosen[e // EXPERTS_PER_GROUP] for e in range(N_EXPERTS)]
    pf = [p.astype(_F32) for p in picked]

    n = N_EXPERTS * ns
    onehot = jnp.concatenate(pf, axis=0).astype(_BF16)
    upper = (lax.broadcasted_iota(jnp.int32, (LANES, LANES), 0)
             <= lax.broadcasted_iota(jnp.int32, (LANES, LANES), 1)).astype(_BF16)
    incl = jnp.dot(onehot, upper, preferred_element_type=_F32)
    r_io = lax.broadcasted_iota(jnp.int32, (n, n), 0)
    c_io = lax.broadcasted_iota(jnp.int32, (n, n), 1)
    shift = ns.bit_length() - 1
    same_expert = lax.shift_right_logical(r_io, shift) == lax.shift_right_logical(c_io, shift)
    earlier = jnp.where(same_expert & (c_io < r_io), 1.0, 0.0).astype(_BF16)
    carry = jnp.dot(earlier, incl.astype(_BF16), preferred_element_type=_F32)[:, LANES - 1:LANES]
    base = cnt_ref[:, 0:1]
    posm = [incl[e * ns:(e + 1) * ns, :] - 1.0 + carry[e * ns:(e + 1) * ns, :] + base[e:e + 1, :]
            for e in range(N_EXPERTS)]
    last = [(incl[(e + 1) * ns - 1:(e + 1) * ns, LANES - 1:LANES] + carry[(e + 1) * ns - 1:(e + 1) * ns, :])
            for e in range(N_EXPERTS)]
    cnt_ref[...] = jnp.broadcast_to(base + jnp.concatenate(last, axis=0), cnt_ref.shape)

    gnum = [pf[e] * probs[e] for e in range(N_EXPERTS)]
    gden = functools.reduce(lambda a, b: a + b, gnum)
    zero = jnp.zeros_like(pf[0])
    seen = zero
    e_out = [zero, zero]
    p_out = [zero, zero]
    g_out = [zero, zero]
    for e in range(N_EXPERTS):
        for k in range(TOP_K):
            hit = pf[e] * (seen == float(k)).astype(_F32)
            e_out[k] = e_out[k] + hit * float(e)
            p_out[k] = p_out[k] + hit * posm[e]
            g_out[k] = g_out[k] + hit * gnum[e]
        seen = seen + pf[e]
    for k in range(TOP_K):
        e_ref[k] = e_out[k].astype(jnp.int32)
        pos_ref[k] = p_out[k].astype(jnp.int32)
        gate_ref[k] = g_out[k] / gden


def _out_route_kernel(x_ref, m_ref, w_ref, ga_ref, g_ref, sc_ref, sh_ref, wr_ref, br_ref,
                      o_ref, e_ref, pos_ref, gate_ref, cnt_ref, wb_ref):
    @pl.when((pl.program_id(0) == 0) & (pl.program_id(1) == 0))
    def _():
        wb_ref[...] = w_ref[...].astype(_BF16)

    xn = x_ref[...] + ga_ref[...] * jnp.dot(m_ref[...], wb_ref[...], preferred_element_type=_F32)
    o_ref[...] = xn
    _route_tile(xn, g_ref, sc_ref, sh_ref, wr_ref, br_ref, e_ref, pos_ref, gate_ref, cnt_ref)


def _out_proj_route(x, merged, w_out, layer, mod, g_moe, wr_t, br):
    B, S, D = x.shape
    tr = min(ROUTE_TILE, S)
    ns = tr // LANES
    assert ns & (ns - 1) == 0
    nb = S // tr
    E = N_EXPERTS
    tok = lambda dt: jax.ShapeDtypeStruct((B * nb, TOP_K, ns, LANES), dt)
    tok_spec = pl.BlockSpec((None, TOP_K, ns, LANES), lambda b, s: (b * nb + s, 0, 0, 0))
    rows = pl.BlockSpec((None, tr, D), lambda b, s: (b, s, 0))
    mod_row = lambda i: pl.BlockSpec((None, None, 1, D), lambda b, s: (i, b, 0, 0))
    xn, eidx, pos, gate, counts = pl.pallas_call(
        _out_route_kernel,
        grid=(B, nb),
        in_specs=[
            rows, rows,
            pl.BlockSpec((None, D, D), lambda b, s: (layer, 0, 0), pipeline_mode=pl.Buffered(1)),
            mod_row(2),
            pl.BlockSpec((1, D), lambda b, s: (0, 0)),
            mod_row(4), mod_row(3),
            pl.BlockSpec((E, D), lambda b, s: (0, 0)),
            pl.BlockSpec((E, 1), lambda b, s: (0, 0)),
        ],
        out_specs=[
            rows, tok_spec, tok_spec, tok_spec,
            pl.BlockSpec((E, LANES), lambda b, s: (0, 0)),
        ],
        out_shape=[
            jax.ShapeDtypeStruct((B, S, D), _F32),
            tok(jnp.int32), tok(jnp.int32), tok(_F32),
            jax.ShapeDtypeStruct((E, LANES), _F32),
        ],
        scratch_shapes=[pltpu.VMEM((D, D), _BF16)],
        compiler_params=_params(("arbitrary", "arbitrary")),
        name="out_proj_route",
    )(x, merged, w_out, mod, g_moe.reshape(1, D), mod, mod, wr_t, br)
    flat = lambda a: a.reshape(B * nb, TOP_K, tr)
    return xn, flat(eidx), flat(pos), flat(gate), counts


def _dispatch_kernel(pad_start_ref, pad_len_ref, nu_ref, dest_ref, x0_ref, xn_ref, g_ref,
                     sc0_ref, sh0_ref, scn_ref, shn_ref, buf_ref, h_ref, zblk, sem):
    i = pl.program_id(0)
    tr = xn_ref.shape[0]

    @pl.when(i == 0)
    def _():
        h_ref[0] = _rms_mod(x0_ref[...], g_ref[...], sc0_ref[...], sh0_ref[...])

    def row_wait():
        pltpu.make_async_copy(h_ref.at[0, pl.ds(0, 1), :], buf_ref.at[pl.ds(0, 1), :], sem).wait()

    def step(slot):
        for r in range(tr):
            for k in range(TOP_K):
                pltpu.make_async_copy(h_ref.at[slot, pl.ds(r, 1), :],
                                      buf_ref.at[pl.ds(dest_ref[k, r], 1), :], sem).start(priority=k)

        @pl.when(i + 1 < pl.num_programs(0))
        def _():
            h_ref[1 - slot] = _rms_mod(xn_ref[...], g_ref[...], scn_ref[...], shn_ref[...])

        for _ in range(tr * TOP_K):
            row_wait()

    for slot in range(2):
        pl.when(lax.rem(i, 2) == slot)(functools.partial(step, slot))

    @pl.when(pl.program_id(0) == 0)
    def _():
        bm = zblk.shape[0]
        nblk = buf_ref.shape[0] // bm
        zblk[...] = jnp.zeros_like(zblk)
        for e in range(N_EXPERTS):
            def zissue(r, carry, e=e):
                pltpu.make_async_copy(zblk.at[pl.ds(0, 1), :],
                                      buf_ref.at[pl.ds(pad_start_ref[e] + r, 1), :], sem).start()
                return carry
            lax.fori_loop(0, pad_len_ref[e], zissue, 0)
        for e in range(N_EXPERTS):
            def zdrain(r, carry):
                row_wait()
                return carry
            lax.fori_loop(0, pad_len_ref[e], zdrain, 0)

        def blk_copy(i):
            return pltpu.make_async_copy(zblk, buf_ref.at[pl.ds(pl.multiple_of(i * bm, bm), bm), :], sem)

        def bissue(i, carry):
            blk_copy(i).start()
            return carry

        def bdrain(i, carry):
            blk_copy(i).wait()
            return carry

        lax.fori_loop(nu_ref[0], nblk, bissue, 0)
        lax.fori_loop(nu_ref[0], nblk, bdrain, 0)


def _dispatch(x, g, mod, i_shift, i_scale, dest, pad_start, pad_len, n_used, rows, bm):
    B, S, D = x.shape
    nbt, _, tr = dest.shape
    nb = S // tr
    nxt = lambda i: jnp.minimum(i + 1, nbt - 1)
    grid_spec = pltpu.PrefetchScalarGridSpec(
        num_scalar_prefetch=3,
        grid=(nbt,),
        in_specs=[
            pl.BlockSpec((None, TOP_K, tr), lambda i, *_: (i, 0, 0), memory_space=pltpu.SMEM),
            pl.BlockSpec((None, tr, D), lambda i, *_: (0, 0, 0)),
            pl.BlockSpec((None, tr, D), lambda i, *_: (nxt(i) // nb, nxt(i) % nb, 0)),
            pl.BlockSpec((1, D), lambda i, *_: (0, 0)),
            pl.BlockSpec((None, None, 1, D), lambda i, *_: (i_scale, 0, 0, 0)),
            pl.BlockSpec((None, None, 1, D), lambda i, *_: (i_shift, 0, 0, 0)),
            pl.BlockSpec((None, None, 1, D), lambda i, *_: (i_scale, nxt(i) // nb, 0, 0)),
            pl.BlockSpec((None, None, 1, D), lambda i, *_: (i_shift, nxt(i) // nb, 0, 0)),
        ],
        out_specs=pl.BlockSpec(memory_space=pl.ANY),
        scratch_shapes=[pltpu.VMEM((2, tr, D), _F32), pltpu.VMEM((bm, D), _F32),
                        pltpu.SemaphoreType.DMA(())],
    )
    return pl.pallas_call(
        _dispatch_kernel,
        grid_spec=grid_spec,
        out_shape=jax.ShapeDtypeStruct((rows, D), _F32),
        compiler_params=_params(("arbitrary",)),
        name="moe_dispatch",
    )(pad_start, pad_len, n_used, dest, x, x, g.reshape(1, D), mod, mod, mod, mod)


def _expert_kernel(be_ref, nx_ref, nu_ref, x_ref, wg_hbm, wu_hbm, wd_hbm, o_ref,
                   wgb, wub, wdb, sg, su, sd, sem, *, layer):
    i = pl.program_id(0)
    e = be_ref[i]
    live = i < nu_ref[0]
    new_expert = (i == 0) | (e != be_ref[jnp.maximum(i - 1, 0)])

    def weight_copies(ex):
        return [pltpu.make_async_copy(w.at[layer, ex], s, sem.at[n])
                for n, (w, s) in enumerate(((wg_hbm, sg), (wu_hbm, su), (wd_hbm, sd)))]

    @pl.when(live & new_expert)
    def _():
        @pl.when(i == 0)
        def _():
            for cp in weight_copies(e):
                cp.start()

        for cp in weight_copies(e):
            cp.wait()
        wgb[...] = sg[...].astype(_BF16)
        wub[...] = su[...].astype(_BF16)
        wdb[...] = sd[...].astype(_BF16)

        @pl.when(nx_ref[i] >= 0)
        def _():
            for cp in weight_copies(nx_ref[i]):
                cp.start(priority=1)

    @pl.when(live)
    def _():
        xb = x_ref[...].astype(_BF16)
        a = jnp.dot(xb, wgb[...], preferred_element_type=_F32)
        u = jnp.dot(xb, wub[...], preferred_element_type=_F32)
        act = (a * _sigmoid(a) * u).astype(_BF16)
        o_ref[...] = jnp.dot(act, wdb[...], preferred_element_type=_F32).astype(o_ref.dtype)

    @pl.when(jnp.logical_not(live))
    def _():
        o_ref[...] = jnp.zeros_like(o_ref)


def _experts(buf, block_expert, next_expert, n_used, wg, wu, wd, layer, bm):
    rows, D = buf.shape
    De = wg.shape[-1]
    nblk = rows // bm
    grid_spec = pltpu.PrefetchScalarGridSpec(
        num_scalar_prefetch=3,
        grid=(nblk,),
        in_specs=[
            pl.BlockSpec((bm, D), lambda i, be, nx, nu: (jnp.minimum(i, jnp.maximum(nu[0] - 1, 0)), 0)),
            pl.BlockSpec(memory_space=pl.ANY),
            pl.BlockSpec(memory_space=pl.ANY),
            pl.BlockSpec(memory_space=pl.ANY),
        ],
        out_specs=pl.BlockSpec((bm, D), lambda i, be, nx, nu: (i, 0)),
        scratch_shapes=[
            pltpu.VMEM((D, De), _BF16), pltpu.VMEM((D, De), _BF16), pltpu.VMEM((De, D), _BF16),
            pltpu.VMEM((D, De), _F32), pltpu.VMEM((D, De), _F32), pltpu.VMEM((De, D), _F32),
            pltpu.SemaphoreType.DMA((3,)),
        ],
    )
    return pl.pallas_call(
        functools.partial(_expert_kernel, layer=layer),
        grid_spec=grid_spec,
        out_shape=jax.ShapeDtypeStruct((rows, D), _BF16),
        compiler_params=_params(("arbitrary",), vmem=60 * 1024 * 1024),
        name="moe_experts",
    )(block_expert, next_expert, n_used, buf, wg, wu, wd)


ROW_CHUNK = 16


def _combine_kernel(src_ref, nch_ref, dst_ref, tot_ref,
                    x_ref, y_hbm, scol_ref, gcol_ref, gm_ref, nf_ref, gn_ref, scn_ref, shn_ref,
                    *rest, final, nb, has_next):
    if has_next:
        o_ref, h_ref, ystage, sem = rest
    else:
        o_ref, ystage, sem = rest
        h_ref = None
    i = pl.program_id(0) * nb + pl.program_id(1)
    n_tiles = pl.num_programs(0) * nb
    cur = lax.rem(i, 2)
    tc = x_ref.shape[0]
    R = ystage.shape[1]

    def chunk_copy(src, dst, buf):
        return pltpu.make_async_copy(y_hbm.at[pl.ds(src, ROW_CHUNK), :],
                                     ystage.at[buf, pl.ds(dst, ROW_CHUNK), :], sem.at[buf])

    def start_tile(tile, buf):
        for e in range(N_EXPERTS):
            idx = tile * N_EXPERTS + e

            def body(c, carry, idx=idx):
                chunk_copy(pl.multiple_of(src_ref[idx] + c * ROW_CHUNK, ROW_CHUNK),
                           pl.multiple_of(dst_ref[idx] + c * ROW_CHUNK, ROW_CHUNK), buf).start()
                return carry

            lax.fori_loop(0, nch_ref[idx], body, 0)

    @pl.when(i == 0)
    def _():
        ystage[...] = jnp.zeros_like(ystage)
        start_tile(0, 0)

    @pl.when(i + 1 < n_tiles)
    def _():
        start_tile(i + 1, 1 - cur)

    def wait_one(c, carry):
        chunk_copy(0, 0, cur).wait()
        return carry

    lax.fori_loop(0, tot_ref[i], wait_one, 0)

    scol = scol_ref[...]
    gcol = gcol_ref[...]
    c_io = lax.broadcasted_iota(jnp.int32, (tc, R), 1)
    g = jnp.where(c_io == scol[:, 0:1], gcol[:, 0:1],
                  jnp.where(c_io == scol[:, 1:2], gcol[:, 1:2], 0.0)).astype(_BF16)
    moe = jnp.dot(g, ystage[cur], preferred_element_type=_F32)
    xn = x_ref[...] + gm_ref[...] * moe
    if final:
        ms = jnp.mean(xn * xn, axis=-1, keepdims=True)
        xn = xn * lax.rsqrt(ms + EPS) * nf_ref[...]
    o_ref[...] = xn
    if has_next:
        h_ref[...] = _rms_mod(xn, gn_ref[...], scn_ref[...], shn_ref[...]).astype(h_ref.dtype)


def _combine(x, y, src, nch, dst, tot, slot_col, gate_col, mod, i_gate, nf, final, g_next, mod_next):
    B, S, D = x.shape
    n_tiles, tc, _ = slot_col.shape
    nb = S // tc
    R = _stage_rows(tc)
    has_next = g_next is not None
    if not has_next:
        g_next, mod_next = nf, mod
    tile = lambda b, s, *_: (b * nb + s, 0, 0)
    row = lambda b, s, *_: (b, s, 0)
    grid_spec = pltpu.PrefetchScalarGridSpec(
        num_scalar_prefetch=4,
        grid=(B, nb),
        in_specs=[
            pl.BlockSpec((None, tc, D), row),
            pl.BlockSpec(memory_space=pl.ANY),
            pl.BlockSpec((None, tc, TOP_K), tile),
            pl.BlockSpec((None, tc, TOP_K), tile),
            pl.BlockSpec((None, None, 1, D), lambda b, s, *_: (i_gate, b, 0, 0)),
            pl.BlockSpec((1, D), lambda b, s, *_: (0, 0)),
            pl.BlockSpec((1, D), lambda b, s, *_: (0, 0)),
            pl.BlockSpec((None, None, 1, D), lambda b, s, *_: (1, b, 0, 0)),
            pl.BlockSpec((None, None, 1, D), lambda b, s, *_: (0, b, 0, 0)),
        ],
        out_specs=([pl.BlockSpec((None, tc, D), row)] * 2) if has_next else pl.BlockSpec((None, tc, D), row),
        scratch_shapes=[pltpu.VMEM((2, R, D), _BF16), pltpu.SemaphoreType.DMA((2,))],
    )
    out_shape = jax.ShapeDtypeStruct((B, S, D), _F32)
    if has_next:
        out_shape = [out_shape, jax.ShapeDtypeStruct((B, S, D), _BF16)]
    return pl.pallas_call(
        functools.partial(_combine_kernel, final=final, nb=nb, has_next=has_next),
        grid_spec=grid_spec,
        out_shape=out_shape,
        compiler_params=_params(("arbitrary", "arbitrary")),
        name="moe_combine",
    )(src, nch, dst, tot, x, y, slot_col, gate_col, mod, nf.reshape(1, D), g_next.reshape(1, D),
      mod_next, mod_next)


def _stage_rows(tc):
    worst = TOP_K * tc + 2 * (ROW_CHUNK - 1) * N_EXPERTS
    return -(-worst // LANES) * LANES


def _combine_plan(eidx, dest, gate, pstarts, tc):
    T = eidx.shape[0] * eidx.shape[2]
    n_tiles = T // tc
    flat = lambda a: a.transpose(0, 2, 1).reshape(n_tiles, tc, TOP_K)
    e_f, d_f, g_f = flat(eidx), flat(dest), flat(gate)
    onehot = (e_f[..., None] == jnp.arange(N_EXPERTS, dtype=jnp.int32)).astype(jnp.int32)
    cnt = jnp.sum(onehot, axis=(1, 2))
    first = pstarts[None, :] + jnp.cumsum(cnt, axis=0) - cnt
    src = first // ROW_CHUNK * ROW_CHUNK
    nch = jnp.where(cnt > 0, (first + cnt - src + ROW_CHUNK - 1) // ROW_CHUNK, 0)
    dst = ROW_CHUNK * (jnp.cumsum(nch, axis=1) - nch)
    slot = d_f + jnp.sum(onehot * (dst - src)[:, None, None, :], axis=-1)
    return (src.reshape(-1), nch.reshape(-1), dst.reshape(-1), jnp.sum(nch, axis=1),
            slot, g_f)


def kernel(x, c, w_ada, b_ada, norm_mix, norm_moe, w_in, b_forget, w_pool, pool_scale, w_branch,
           w_gate, b_gate, w_out, w_router, b_router, w_exp_gate, w_exp_up, w_exp_down, norm_final):
    B, S, D = x.shape
    L = w_ada.shape[0]
    W = D // 2
    nh = W // HEAD_DIM
    T = B * S
    A = T * TOP_K
    bm = min(256, A // N_EXPERTS)
    rows = (A // bm + N_EXPERTS) * bm

    mod_all = _ada(c, w_ada, b_ada)
    mod_all = mod_all.reshape(L, B, N_MOD, D).transpose(0, 2, 1, 3)[:, :, :, None, :]
    wr_t = w_router.T
    br = b_router.reshape(N_EXPERTS, 1)
    colscale = jnp.concatenate([jnp.ones((W,), _F32), jnp.full((W,), HEAD_DIM ** -0.5 * LOG2E, _F32),
                                jnp.ones((2 * W,), _F32)]).reshape(1, 4 * W)

    w_in_t = jnp.swapaxes(w_in, 1, 2)
    h = None
    for l in range(L):
        mod = mod_all[l]
        bf = jnp.zeros((1, LANES), _F32).at[0, :nh].set(b_forget[l])
        if h is None:
            fcum, h = _forget(x, w_in_t, l, nh, bf, norm=(norm_mix[l], mod, 0, 1))
        else:
            fcum = _forget(h, w_in_t, l, nh, bf)
        proj5 = _proj(h, w_in_t, l, colscale)
        pool_out = _pool(proj5, w_pool, l, pool_scale[l].reshape(1, W))
        attn = _attn(proj5, fcum, nh)
        merged = _merge(h, pool_out, attn, w_gate, b_gate, w_branch, l)
        x, eidx, pos, gate, counts = _out_proj_route(x, merged, w_out, l, mod, norm_moe[l], wr_t, br)
        counts = counts[:, 0].astype(jnp.int32)
        pcounts = (counts + bm - 1) // bm * bm
        pends = jnp.cumsum(pcounts)
        pstarts = pends - pcounts
        dest = pos
        for e in range(N_EXPERTS):
            dest = dest + jnp.where(eidx == e, pstarts[e], 0)
        n_used = (pends[-1] // bm).astype(jnp.int32)
        blk_ids = jnp.arange(rows // bm, dtype=jnp.int32)
        blk_ids = jnp.minimum(blk_ids, n_used - 1)
        block_expert = jnp.sum((pends[None, :] <= (blk_ids * bm)[:, None]).astype(jnp.int32), axis=1)
        block_expert = jnp.minimum(block_expert, N_EXPERTS - 1)
        ids = jnp.arange(N_EXPERTS, dtype=jnp.int32)
        later = (ids[None, :] > ids[:, None]) & (pcounts[None, :] > 0)
        next_of = jnp.min(jnp.where(later, ids[None, :], N_EXPERTS), axis=1)
        next_of = jnp.where(next_of == N_EXPERTS, -1, next_of)
        next_expert = jnp.sum(jnp.where(block_expert[:, None] == ids[None, :], next_of[None, :], 0), axis=1)
        n_used = n_used.reshape(1)
        td = min(DISPATCH_TILE, S)
        dest_d = dest.transpose(0, 2, 1).reshape(T // td, td, TOP_K).transpose(0, 2, 1)
        buf = _dispatch(x, norm_moe[l], mod, 3, 4, dest_d, pstarts + counts, pcounts - counts,
                        n_used, rows, bm)
        y = _experts(buf, block_expert, next_expert.astype(jnp.int32), n_used,
                     w_exp_gate, w_exp_up, w_exp_down, l, bm)
        plan = _combine_plan(eidx, dest, gate, pstarts, min(COMBINE_TILE, S))
        if l + 1 < L:
            x, h = _combine(x, y, *plan, mod, 5, norm_final, False, norm_mix[l + 1], mod_all[l + 1])
        else:
            x = _combine(x, y, *plan, mod, 5, norm_final, True, None, None)
    return x
```

```python
import functools

import jax
import jax.numpy as jnp
from jax import lax
from jax.experimental import pallas as pl
from jax.experimental.pallas import tpu as pltpu

N_MOD = 6
EPS = 1e-6
POOL_WINDOWS = (2, 4, 8, 16)
HEAD_DIM = 128
N_EXPERTS = 16
N_EXPERT_GROUPS = 4
EXPERTS_PER_GROUP = N_EXPERTS // N_EXPERT_GROUPS
TOP_K = 2
LOG2E = 1.4426950408889634
ROUTE_TILE = 512
DISPATCH_TILE = 512
COMBINE_TILE = 256
LANES = 128
VMEM_LIMIT = 56 * 1024 * 1024

_F32 = jnp.float32
_BF16 = jnp.bfloat16


def _params(sem, vmem=VMEM_LIMIT):
    return pltpu.CompilerParams(dimension_semantics=sem, vmem_limit_bytes=vmem)


def _sigmoid(v):
    return 1.0 / (1.0 + jnp.exp(-v))


def _rms_mod(x, g, scale, shift):
    ms = jnp.mean(x * x, axis=-1, keepdims=True)
    return x * lax.rsqrt(ms + EPS) * g * (1.0 + scale) + shift


def _ada_kernel(c_ref, w_ref, b_ref, o_ref):
    c = c_ref[...]
    ca = (c * _sigmoid(c)).astype(_BF16)
    o_ref[...] = jnp.dot(ca, w_ref[...].astype(_BF16), preferred_element_type=_F32) + b_ref[...]


def _ada(c, w_ada, b_ada):
    L, D, M = w_ada.shape
    B = c.shape[0]
    tn = min(1024, M)
    return pl.pallas_call(
        _ada_kernel,
        grid=(L, M // tn),
        in_specs=[
            pl.BlockSpec((B, D), lambda l, n: (0, 0)),
            pl.BlockSpec((None, D, tn), lambda l, n: (l, 0, n)),
            pl.BlockSpec((None, 1, tn), lambda l, n: (l, 0, n)),
        ],
        out_specs=pl.BlockSpec((None, B, tn), lambda l, n: (l, 0, n)),
        out_shape=jax.ShapeDtypeStruct((L, B, M), _F32),
        compiler_params=_params(("arbitrary", "arbitrary")),
        name="ada_mod",
    )(c, w_ada, b_ada.reshape(L, 1, M))


def _first_inner_step():
    return (pl.program_id(1) == 0) & (pl.program_id(2) == 0)


def _proj_kernel(h_ref, w_ref, cs_ref, o_ref, wb_ref):
    @pl.when(_first_inner_step())
    def _():
        wb_ref[...] = (w_ref[...].T * cs_ref[...]).astype(_BF16)

    acc = jnp.dot(h_ref[...], wb_ref[...], preferred_element_type=_F32)
    for j in range(o_ref.shape[0]):
        o_ref[j] = acc[:, j * LANES:(j + 1) * LANES].astype(o_ref.dtype)


def _proj(h, w_in_t, layer, colscale):
    B, S, D = h.shape
    N = colscale.shape[1]
    tn = min(2048, N)
    tm = min(1024, S)
    return pl.pallas_call(
        _proj_kernel,
        grid=(N // tn, B, S // tm),
        in_specs=[
            pl.BlockSpec((None, tm, D), lambda n, b, m: (b, m, 0)),
            pl.BlockSpec((None, tn, D), lambda n, b, m: (layer, n, 0), pipeline_mode=pl.Buffered(1)),
            pl.BlockSpec((1, tn), lambda n, b, m: (0, n)),
        ],
        out_specs=pl.BlockSpec((None, tn // LANES, tm, LANES), lambda n, b, m: (b, n, m, 0)),
        out_shape=jax.ShapeDtypeStruct((B, N // LANES, S, LANES), _BF16),
        scratch_shapes=[pltpu.VMEM((D, tn), _BF16)],
        compiler_params=_params(("arbitrary", "arbitrary", "arbitrary")),
        name="in_proj",
    )(h, w_in_t, colscale)


def _forget_kernel(*refs, from_x):
    if from_x:
        x_ref, g_ref, sc_ref, sh_ref, w_ref, b_ref, o_ref, h_ref, carry = refs
        h = _rms_mod(x_ref[...], g_ref[...], sc_ref[...], sh_ref[...]).astype(_BF16)
        h_ref[...] = h
    else:
        hin_ref, w_ref, b_ref, o_ref, carry = refs
        h = hin_ref[...]
    ts, D = h.shape
    nh = w_ref.shape[0]

    @pl.when(pl.program_id(1) == 0)
    def _():
        carry[...] = jnp.zeros_like(carry)

    w = jnp.concatenate([w_ref[...], jnp.zeros((LANES - nh, D), _F32)], axis=0).astype(_BF16)
    fl = lax.dot_general(h, w, (((1,), (1,)), ((), ())), preferred_element_type=_F32) + b_ref[...]
    acc = jnp.minimum(fl, 0.0) - jnp.log(1.0 + jnp.exp(-jnp.abs(fl)))
    row = lax.broadcasted_iota(jnp.int32, acc.shape, 0)
    sh = 1
    while sh < ts:
        acc = acc + jnp.where(row >= sh, pltpu.roll(acc, sh, axis=0), 0.0)
        sh *= 2
    acc = acc + carry[...]
    o_ref[...] = acc
    carry[...] = acc[ts - 1:ts, :]


def _forget(h_or_x, w_in_t, layer, nh, bf, norm=None):
    B, S, D = h_or_x.shape
    ts = min(1024, S)
    first = (w_in_t.shape[1] - nh) // nh
    rows = pl.BlockSpec((None, ts, D), lambda b, s: (b, s, 0))
    tail_specs = [pl.BlockSpec((None, nh, D), lambda b, s: (layer, first, 0)),
                  pl.BlockSpec((1, LANES), lambda b, s: (0, 0))]
    f_spec = pl.BlockSpec((None, ts, LANES), lambda b, s: (b, s, 0))
    f_shape = jax.ShapeDtypeStruct((B, S, LANES), _F32)
    if norm is None:
        in_specs, args = [rows] + tail_specs, (h_or_x, w_in_t, bf)
        out_specs, out_shape = f_spec, f_shape
    else:
        g, mod, i_shift, i_scale = norm
        in_specs = [rows, pl.BlockSpec((1, D), lambda b, s: (0, 0)),
                    pl.BlockSpec((None, None, 1, D), lambda b, s: (i_scale, b, 0, 0)),
                    pl.BlockSpec((None, None, 1, D), lambda b, s: (i_shift, b, 0, 0))] + tail_specs
        args = (h_or_x, g.reshape(1, D), mod, mod, w_in_t, bf)
        out_specs, out_shape = [f_spec, rows], [f_shape, jax.ShapeDtypeStruct((B, S, D), _BF16)]
    return pl.pallas_call(
        functools.partial(_forget_kernel, from_x=norm is not None),
        grid=(B, S // ts),
        in_specs=in_specs,
        out_specs=out_specs,
        out_shape=out_shape,
        scratch_shapes=[pltpu.VMEM((1, LANES), _F32)],
        compiler_params=_params(("arbitrary", "arbitrary")),
        name="forget_cumsum",
    )(*args)


def _pool_kernel(u_ref, wp_ref, ps_ref, o_ref, *, cpg):
    S = u_ref.shape[1]
    gw = cpg * LANES
    row = lax.broadcasted_iota(jnp.int32, (S, gw), 0)
    for g, w in enumerate(POOL_WINDOWS):
        parts = [u_ref[g * cpg + j] for j in range(cpg)]
        u = (parts[0] if cpg == 1 else jnp.concatenate(parts, axis=-1)).astype(_F32)
        s = u
        sh = 1
        while sh < w:
            s = s + jnp.where(row >= sh, pltpu.roll(s, sh, axis=0), 0.0)
            sh *= 2
        cnt = jnp.minimum(row + 1, w).astype(_F32)
        pooled = s / cnt - u
        mixed = jnp.dot(pooled.astype(_BF16), wp_ref[g].astype(_BF16), preferred_element_type=_F32)
        o_ref[:, g * gw:(g + 1) * gw] = (mixed * ps_ref[:, g * gw:(g + 1) * gw]).astype(o_ref.dtype)


def _pool(proj5, w_pool, layer, ps):
    B, _, S, _ = proj5.shape
    _, G, gw, _ = w_pool.shape
    cpg = gw // LANES
    W = G * gw
    return pl.pallas_call(
        functools.partial(_pool_kernel, cpg=cpg),
        grid=(B,),
        in_specs=[
            pl.BlockSpec((None, G * cpg, S, LANES), lambda b: (b, 0, 0, 0)),
            pl.BlockSpec((None, G, gw, gw), lambda b: (layer, 0, 0, 0)),
            pl.BlockSpec((1, W), lambda b: (0, 0)),
        ],
        out_specs=pl.BlockSpec((None, S, W), lambda b: (b, 0, 0)),
        out_shape=jax.ShapeDtypeStruct((B, S, W), _BF16),
        compiler_params=_params(("arbitrary",)),
        name="pool_mixer",
    )(proj5, w_pool, ps)


def _attn_kernel(q_ref, k_ref, v_ref, f_ref, o_ref, qa_ref, ka_ref, va_ref, *, blk):
    S = q_ref.shape[0]
    h = pl.program_id(1)
    lane = lax.broadcasted_iota(jnp.int32, (S, LANES), 1)
    f = jnp.sum(jnp.where(lane == h, f_ref[...], 0.0), axis=-1, keepdims=True) * LOG2E
    f1 = f.astype(_BF16).astype(_F32)
    r1 = f - f1
    f2 = r1.astype(_BF16).astype(_F32)
    f3 = (r1 - f2).astype(_BF16).astype(_F32)
    qa_ref[:, :HEAD_DIM] = q_ref[...]
    qa_ref[:, HEAD_DIM:] = jnp.where(lane == 0, f1, jnp.where(lane == 1, f2, jnp.where(
        lane == 2, f3, jnp.where(lane < 6, 1.0, 0.0)))).astype(_BF16)
    ka_ref[:, :HEAD_DIM] = k_ref[...]
    ka_ref[:, HEAD_DIM:] = jnp.where(lane < 3, 1.0, jnp.where(lane == 3, -f1, jnp.where(
        lane == 4, -f2, jnp.where(lane == 5, -f3, 0.0)))).astype(_BF16)
    va_ref[:, :HEAD_DIM] = v_ref[...]
    va_ref[:, HEAD_DIM:] = jnp.where(lane == 0, 1.0, 0.0).astype(_BF16)

    nt = (((1,), (1,)), ((), ()))
    r_io = lax.broadcasted_iota(jnp.int32, (blk, blk), 0)
    c_io = lax.broadcasted_iota(jnp.int32, (blk, blk), 1)
    causal = c_io <= r_io
    nb = S // blk
    m = [jnp.full((blk, 1), -jnp.inf, _F32)] * nb
    acc = [jnp.zeros((blk, 2 * HEAD_DIM), _F32)] * nb
    for j in range(nb):
        ka = ka_ref[j * blk:(j + 1) * blk, :]
        va = va_ref[j * blk:(j + 1) * blk, :]
        for i in range(j, nb):
            s = lax.dot_general(qa_ref[i * blk:(i + 1) * blk, :], ka, nt, preferred_element_type=_F32)
            if i == j:
                s = jnp.where(causal, s, -jnp.inf)
            m_new = jnp.maximum(m[i], jnp.max(s, axis=-1, keepdims=True))
            p = jnp.exp2(s - m_new).astype(_BF16)
            acc[i] = jnp.exp2(m[i] - m_new) * acc[i] + jnp.dot(p, va, preferred_element_type=_F32)
            m[i] = m_new
    for i in range(nb):
        o_ref[i * blk:(i + 1) * blk, :] = (
            acc[i][:, :HEAD_DIM] / acc[i][:, HEAD_DIM:HEAD_DIM + 1]).astype(o_ref.dtype)


def _attn(proj5, fcum, nh):
    B, _, S, _ = proj5.shape
    blk = min(512, S)
    chunk = lambda o: pl.BlockSpec((None, None, S, HEAD_DIM), lambda b, h: (b, o + h, 0, 0))
    return pl.pallas_call(
        functools.partial(_attn_kernel, blk=blk),
        grid=(B, nh),
        in_specs=[
            chunk(nh), chunk(2 * nh), chunk(3 * nh),
            pl.BlockSpec((None, S, LANES), lambda b, h: (b, 0, 0)),
        ],
        out_specs=pl.BlockSpec((None, None, S, HEAD_DIM), lambda b, h: (b, h, 0, 0)),
        out_shape=jax.ShapeDtypeStruct((B, nh, S, HEAD_DIM), _BF16),
        scratch_shapes=[pltpu.VMEM((S, 2 * HEAD_DIM), _BF16)] * 3,
        compiler_params=_params(("arbitrary", "arbitrary")),
        name="forget_attn",
    )(proj5, proj5, proj5, fcum)


def _merge_kernel(h_ref, p_ref, a_ref, wg0_ref, wg1_ref, bg0_ref, bg1_ref, wb0_ref, wb1_ref, o_ref,
                  cg0_ref, cg1_ref, cb0_ref, cb1_ref):
    @pl.when(_first_inner_step())
    def _():
        cg0_ref[...] = wg0_ref[...].astype(_BF16)
        cg1_ref[...] = wg1_ref[...].astype(_BF16)
        cb0_ref[...] = wb0_ref[...].astype(_BF16)
        cb1_ref[...] = wb1_ref[...].astype(_BF16)

    h = h_ref[...]
    nh = a_ref.shape[0]
    a = jnp.concatenate([a_ref[j] for j in range(nh)], axis=-1)
    g0 = _sigmoid(jnp.dot(h, cg0_ref[...], preferred_element_type=_F32) + bg0_ref[...])
    y0 = jnp.dot(p_ref[...], cb0_ref[...], preferred_element_type=_F32)
    acc = g0 * y0
    g1 = _sigmoid(jnp.dot(h, cg1_ref[...], preferred_element_type=_F32) + bg1_ref[...])
    y1 = jnp.dot(a, cb1_ref[...], preferred_element_type=_F32)
    o_ref[...] = (acc + g1 * y1).astype(o_ref.dtype)


def _merge(h, pool_out, attn, w_gate, b_gate, w_branch, layer):
    B, S, D = h.shape
    W = pool_out.shape[-1]
    nh = attn.shape[1]
    L = w_gate.shape[0]
    tn = min(512, D)
    tm = min(1024, S)
    nn = D // tn
    bg = b_gate.reshape(L, 1, 2 * D)
    once = pl.Buffered(1)
    return pl.pallas_call(
        _merge_kernel,
        grid=(nn, B, S // tm),
        in_specs=[
            pl.BlockSpec((None, tm, D), lambda n, b, m: (b, m, 0)),
            pl.BlockSpec((None, tm, W), lambda n, b, m: (b, m, 0)),
            pl.BlockSpec((None, nh, tm, HEAD_DIM), lambda n, b, m: (b, 0, m, 0)),
            pl.BlockSpec((None, D, tn), lambda n, b, m: (layer, 0, n), pipeline_mode=once),
            pl.BlockSpec((None, D, tn), lambda n, b, m: (layer, 0, nn + n), pipeline_mode=once),
            pl.BlockSpec((None, 1, tn), lambda n, b, m: (layer, 0, n)),
            pl.BlockSpec((None, 1, tn), lambda n, b, m: (layer, 0, nn + n)),
            pl.BlockSpec((None, None, W, tn), lambda n, b, m: (layer, 0, 0, n), pipeline_mode=once),
            pl.BlockSpec((None, None, W, tn), lambda n, b, m: (layer, 1, 0, n), pipeline_mode=once),
        ],
        out_specs=pl.BlockSpec((None, tm, tn), lambda n, b, m: (b, m, n)),
        out_shape=jax.ShapeDtypeStruct((B, S, D), _BF16),
        scratch_shapes=[pltpu.VMEM((D, tn), _BF16), pltpu.VMEM((D, tn), _BF16),
                        pltpu.VMEM((W, tn), _BF16), pltpu.VMEM((W, tn), _BF16)],
        compiler_params=_params(("arbitrary", "arbitrary", "arbitrary")),
        name="branch_merge",
    )(h, pool_out, attn, w_gate, w_gate, bg, bg, w_branch, w_branch)


def _route_tile(x, g_ref, sc_ref, sh_ref, wr_ref, br_ref, e_ref, pos_ref, gate_ref, cnt_ref):
    first_step = (pl.program_id(0) == 0) & (pl.program_id(1) == 0)

    @pl.when(first_step)
    def _():
        cnt_ref[...] = jnp.zeros_like(cnt_ref)

    h = _rms_mod(x, g_ref[...], sc_ref[...], sh_ref[...])
    tr, D = h.shape
    ns = tr // LANES

    nt = (((1,), (1,)), ((), ()))
    h_hi = h.astype(_BF16)
    h_lo = (h - h_hi.astype(_F32)).astype(_BF16)
    w = wr_ref[...]
    w_hi = w.astype(_BF16)
    w_lo = (w - w_hi.astype(_F32)).astype(_BF16)
    both = lax.dot_general(jnp.concatenate([w_hi, w_lo], axis=0), h_hi, nt, preferred_element_type=_F32)
    logits = (both[:N_EXPERTS] + both[N_EXPERTS:]
              + lax.dot_general(w_hi, h_lo, nt, preferred_element_type=_F32))
    rows = [jnp.concatenate([logits[e:e + 1, s * LANES:(s + 1) * LANES] for s in range(ns)], axis=0)
            for e in range(N_EXPERTS)]
    mx = functools.reduce(jnp.maximum, rows)
    ex = [jnp.exp(r - mx) for r in rows]
    den = functools.reduce(lambda a, b: a + b, ex)
    probs = [v / den for v in ex]
    sel = [probs[e] + br_ref[e:e + 1, :] for e in range(N_EXPERTS)]

    in_top = [None] * N_EXPERTS
    gscore = []
    for g in range(N_EXPERT_GROUPS):
        ids = range(g * EXPERTS_PER_GROUP, (g + 1) * EXPERTS_PER_GROUP)
        score = None
        for i in ids:
            rank = None
            for j in ids:
                if j == i:
                    continue
                beats = (sel[j] > sel[i]) | ((sel[j] == sel[i]) if j < i else False)
                beats = beats.astype(_F32)
                rank = beats if rank is None else rank + beats
            in_top[i] = rank < float(TOP_K)
            term = jnp.where(in_top[i], sel[i], 0.0)
            score = term if score is None else score + term
        gscore.append(score)
    chosen = []
    for g in range(N_EXPERT_GROUPS):
        lose = None
        for g2 in range(N_EXPERT_GROUPS):
            if g2 == g:
                continue
            b = (gscore[g2] > gscore[g]) | ((gscore[g2] == gscore[g]) if g2 < g else False)
            lose = b if lose is None else (lose | b)
        chosen.append(jnp.logical_not(lose))
    picked = [in_top[e] & chosen[e // EXPERTS_PER_GROUP] for e in range(N_EXPERTS)]
    pf = [p.astype(_F32) for p in picked]

    n = N_EXPERTS * ns
    onehot = jnp.concatenate(pf, axis=0).astype(_BF16)
    upper = (lax.broadcasted_iota(jnp.int32, (LANES, LANES), 0)
             <= lax.broadcasted_iota(jnp.int32, (LANES, LANES), 1)).astype(_BF16)
    incl = jnp.dot(onehot, upper, preferred_element_type=_F32)
    r_io = lax.broadcasted_iota(jnp.int32, (n, n), 0)
    c_io = lax.broadcasted_iota(jnp.int32, (n, n), 1)
    shift = ns.bit_length() - 1
    same_expert = lax.shift_right_logical(r_io, shift) == lax.shift_right_logical(c_io, shift)
    earlier = jnp.where(same_expert & (c_io < r_io), 1.0, 0.0).astype(_BF16)
    carry = jnp.dot(earlier, incl.astype(_BF16), preferred_element_type=_F32)[:, LANES - 1:LANES]
    base = cnt_ref[:, 0:1]
    posm = [incl[e * ns:(e + 1) * ns, :] - 1.0 + carry[e * ns:(e + 1) * ns, :] + base[e:e + 1, :]
            for e in range(N_EXPERTS)]
    last = [(incl[(e + 1) * ns - 1:(e + 1) * ns, LANES - 1:LANES] + carry[(e + 1) * ns - 1:(e + 1) * ns, :])
            for e in range(N_EXPERTS)]
    cnt_ref[...] = jnp.broadcast_to(base + jnp.concatenate(last, axis=0), cnt_ref.shape)

    gnum = [pf[e] * probs[e] for e in range(N_EXPERTS)]
    gden = functools.reduce(lambda a, b: a + b, gnum)
    zero = jnp.zeros_like(pf[0])
    seen = zero
    e_out = [zero, zero]
    p_out = [zero, zero]
    g_out = [zero, zero]
    for e in range(N_EXPERTS):
        for k in range(TOP_K):
            hit = pf[e] * (seen == float(k)).astype(_F32)
            e_out[k] = e_out[k] + hit * float(e)
            p_out[k] = p_out[k] + hit * posm[e]
            g_out[k] = g_out[k] + hit * gnum[e]
        seen = seen + pf[e]
    for k in range(TOP_K):
        e_ref[k] = e_out[k].astype(jnp.int32)
        pos_ref[k] = p_out[k].astype(jnp.int32)
        gate_ref[k] = g_out[k] / gden


def _out_route_kernel(x_ref, m_ref, w_ref, ga_ref, g_ref, sc_ref, sh_ref, wr_ref, br_ref,
                      o_ref, e_ref, pos_ref, gate_ref, cnt_ref, wb_ref):
    @pl.when((pl.program_id(0) == 0) & (pl.program_id(1) == 0))
    def _():
        wb_ref[...] = w_ref[...].astype(_BF16)

    xn = x_ref[...] + ga_ref[...] * jnp.dot(m_ref[...], wb_ref[...], preferred_element_type=_F32)
    o_ref[...] = xn
    _route_tile(xn, g_ref, sc_ref, sh_ref, wr_ref, br_ref, e_ref, pos_ref, gate_ref, cnt_ref)


def _out_proj_route(x, merged, w_out, layer, mod, g_moe, wr_t, br):
    B, S, D = x.shape
    tr = min(ROUTE_TILE, S)
    ns = tr // LANES
    assert ns & (ns - 1) == 0
    nb = S // tr
    E = N_EXPERTS
    tok = lambda dt: jax.ShapeDtypeStruct((B * nb, TOP_K, ns, LANES), dt)
    tok_spec = pl.BlockSpec((None, TOP_K, ns, LANES), lambda b, s: (b * nb + s, 0, 0, 0))
    rows = pl.BlockSpec((None, tr, D), lambda b, s: (b, s, 0))
    mod_row = lambda i: pl.BlockSpec((None, None, 1, D), lambda b, s: (i, b, 0, 0))
    xn, eidx, pos, gate, counts = pl.pallas_call(
        _out_route_kernel,
        grid=(B, nb),
        in_specs=[
            rows, rows,
            pl.BlockSpec((None, D, D), lambda b, s: (layer, 0, 0), pipeline_mode=pl.Buffered(1)),
            mod_row(2),
            pl.BlockSpec((1, D), lambda b, s: (0, 0)),
            mod_row(4), mod_row(3),
            pl.BlockSpec((E, D), lambda b, s: (0, 0)),
            pl.BlockSpec((E, 1), lambda b, s: (0, 0)),
        ],
        out_specs=[
            rows, tok_spec, tok_spec, tok_spec,
            pl.BlockSpec((E, LANES), lambda b, s: (0, 0)),
        ],
        out_shape=[
            jax.ShapeDtypeStruct((B, S, D), _F32),
            tok(jnp.int32), tok(jnp.int32), tok(_F32),
            jax.ShapeDtypeStruct((E, LANES), _F32),
        ],
        scratch_shapes=[pltpu.VMEM((D, D), _BF16)],
        compiler_params=_params(("arbitrary", "arbitrary")),
        name="out_proj_route",
    )(x, merged, w_out, mod, g_moe.reshape(1, D), mod, mod, wr_t, br)
    flat = lambda a: a.reshape(B * nb, TOP_K, tr)
    return xn, flat(eidx), flat(pos), flat(gate), counts


def _dispatch_kernel(pad_start_ref, pad_len_ref, nu_ref, dest_ref, x0_ref, xn_ref, g_ref,
                     sc0_ref, sh0_ref, scn_ref, shn_ref, buf_ref, h_ref, zblk, sem):
    i = pl.program_id(0)
    tr = xn_ref.shape[0]

    @pl.when(i == 0)
    def _():
        h_ref[0] = _rms_mod(x0_ref[...], g_ref[...], sc0_ref[...], sh0_ref[...])

    def row_wait():
        pltpu.make_async_copy(h_ref.at[0, pl.ds(0, 1), :], buf_ref.at[pl.ds(0, 1), :], sem).wait()

    def step(slot):
        for r in range(tr):
            for k in range(TOP_K):
                pltpu.make_async_copy(h_ref.at[slot, pl.ds(r, 1), :],
                                      buf_ref.at[pl.ds(dest_ref[k, r], 1), :], sem).start(priority=k)

        @pl.when(i + 1 < pl.num_programs(0))
        def _():
            h_ref[1 - slot] = _rms_mod(xn_ref[...], g_ref[...], scn_ref[...], shn_ref[...])

        for _ in range(tr * TOP_K):
            row_wait()

    for slot in range(2):
        pl.when(lax.rem(i, 2) == slot)(functools.partial(step, slot))

    @pl.when(pl.program_id(0) == 0)
    def _():
        bm = zblk.shape[0]
        nblk = buf_ref.shape[0] // bm
        zblk[...] = jnp.zeros_like(zblk)
        for e in range(N_EXPERTS):
            def zissue(r, carry, e=e):
                pltpu.make_async_copy(zblk.at[pl.ds(0, 1), :],
                                      buf_ref.at[pl.ds(pad_start_ref[e] + r, 1), :], sem).start()
                return carry
            lax.fori_loop(0, pad_len_ref[e], zissue, 0)
        for e in range(N_EXPERTS):
            def zdrain(r, carry):
                row_wait()
                return carry
            lax.fori_loop(0, pad_len_ref[e], zdrain, 0)

        def blk_copy(i):
            return pltpu.make_async_copy(zblk, buf_ref.at[pl.ds(pl.multiple_of(i * bm, bm), bm), :], sem)

        def bissue(i, carry):
            blk_copy(i).start()
            return carry

        def bdrain(i, carry):
            blk_copy(i).wait()
            return carry

        lax.fori_loop(nu_ref[0], nblk, bissue, 0)
        lax.fori_loop(nu_ref[0], nblk, bdrain, 0)


def _dispatch(x, g, mod, i_shift, i_scale, dest, pad_start, pad_len, n_used, rows, bm):
    B, S, D = x.shape
    nbt, _, tr = dest.shape
    nb = S // tr
    nxt = lambda i: jnp.minimum(i + 1, nbt - 1)
    grid_spec = pltpu.PrefetchScalarGridSpec(
        num_scalar_prefetch=3,
        grid=(nbt,),
        in_specs=[
            pl.BlockSpec((None, TOP_K, tr), lambda i, *_: (i, 0, 0), memory_space=pltpu.SMEM),
            pl.BlockSpec((None, tr, D), lambda i, *_: (0, 0, 0)),
            pl.BlockSpec((None, tr, D), lambda i, *_: (nxt(i) // nb, nxt(i) % nb, 0)),
            pl.BlockSpec((1, D), lambda i, *_: (0, 0)),
            pl.BlockSpec((None, None, 1, D), lambda i, *_: (i_scale, 0, 0, 0)),
            pl.BlockSpec((None, None, 1, D), lambda i, *_: (i_shift, 0, 0, 0)),
            pl.BlockSpec((None, None, 1, D), lambda i, *_: (i_scale, nxt(i) // nb, 0, 0)),
            pl.BlockSpec((None, None, 1, D), lambda i, *_: (i_shift, nxt(i) // nb, 0, 0)),
        ],
        out_specs=pl.BlockSpec(memory_space=pl.ANY),
        scratch_shapes=[pltpu.VMEM((2, tr, D), _F32), pltpu.VMEM((bm, D), _F32),
                        pltpu.SemaphoreType.DMA(())],
    )
    return pl.pallas_call(
        _dispatch_kernel,
        grid_spec=grid_spec,
        out_shape=jax.ShapeDtypeStruct((rows, D), _F32),
        compiler_params=_params(("arbitrary",)),
        name="moe_dispatch",
    )(pad_start, pad_len, n_used, dest, x, x, g.reshape(1, D), mod, mod, mod, mod)


def _expert_kernel(be_ref, nx_ref, nu_ref, x_ref, wg_hbm, wu_hbm, wd_hbm, o_ref,
                   wgb, wub, wdb, sg, su, sd, sem, *, layer):
    i = pl.program_id(0)
    e = be_ref[i]
    live = i < nu_ref[0]
    new_expert = (i == 0) | (e != be_ref[jnp.maximum(i - 1, 0)])

    def weight_copies(ex):
        return [pltpu.make_async_copy(w.at[layer, ex], s, sem.at[n])
                for n, (w, s) in enumerate(((wg_hbm, sg), (wu_hbm, su), (wd_hbm, sd)))]

    @pl.when(live & new_expert)
    def _():
        @pl.when(i == 0)
        def _():
            for cp in weight_copies(e):
                cp.start()

        for cp in weight_copies(e):
            cp.wait()
        wgb[...] = sg[...].astype(_BF16)
        wub[...] = su[...].astype(_BF16)
        wdb[...] = sd[...].astype(_BF16)

        @pl.when(nx_ref[i] >= 0)
        def _():
            for cp in weight_copies(nx_ref[i]):
                cp.start(priority=1)

    @pl.when(live)
    def _():
        xb = x_ref[...].astype(_BF16)
        a = jnp.dot(xb, wgb[...], preferred_element_type=_F32)
        u = jnp.dot(xb, wub[...], preferred_element_type=_F32)
        act = (a * _sigmoid(a) * u).astype(_BF16)
        o_ref[...] = jnp.dot(act, wdb[...], preferred_element_type=_F32).astype(o_ref.dtype)

    @pl.when(jnp.logical_not(live))
    def _():
        o_ref[...] = jnp.zeros_like(o_ref)


def _experts(buf, block_expert, next_expert, n_used, wg, wu, wd, layer, bm):
    rows, D = buf.shape
    De = wg.shape[-1]
    nblk = rows // bm
    grid_spec = pltpu.PrefetchScalarGridSpec(
        num_scalar_prefetch=3,
        grid=(nblk,),
        in_specs=[
            pl.BlockSpec((bm, D), lambda i, be, nx, nu: (jnp.minimum(i, jnp.maximum(nu[0] - 1, 0)), 0)),
            pl.BlockSpec(memory_space=pl.ANY),
            pl.BlockSpec(memory_space=pl.ANY),
            pl.BlockSpec(memory_space=pl.ANY),
        ],
        out_specs=pl.BlockSpec((bm, D), lambda i, be, nx, nu: (i, 0)),
        scratch_shapes=[
            pltpu.VMEM((D, De), _BF16), pltpu.VMEM((D, De), _BF16), pltpu.VMEM((De, D), _BF16),
            pltpu.VMEM((D, De), _F32), pltpu.VMEM((D, De), _F32), pltpu.VMEM((De, D), _F32),
            pltpu.SemaphoreType.DMA((3,)),
        ],
    )
    return pl.pallas_call(
        functools.partial(_expert_kernel, layer=layer),
        grid_spec=grid_spec,
        out_shape=jax.ShapeDtypeStruct((rows, D), _BF16),
        compiler_params=_params(("arbitrary",), vmem=60 * 1024 * 1024),
        name="moe_experts",
    )(block_expert, next_expert, n_used, buf, wg, wu, wd)


ROW_CHUNK = 16


def _combine_kernel(src_ref, nch_ref, dst_ref, tot_ref,
                    x_ref, y_hbm, scol_ref, gcol_ref, gm_ref, nf_ref, gn_ref, scn_ref, shn_ref,
                    *rest, final, nb, has_next):
    if has_next:
        o_ref, h_ref, ystage, sem = rest
    else:
        o_ref, ystage, sem = rest
        h_ref = None
    i = pl.program_id(0) * nb + pl.program_id(1)
    n_tiles = pl.num_programs(0) * nb
    cur = lax.rem(i, 2)
    tc = x_ref.shape[0]
    R = ystage.shape[1]

    def chunk_copy(src, dst, buf):
        return pltpu.make_async_copy(y_hbm.at[pl.ds(src, ROW_CHUNK), :],
                                     ystage.at[buf, pl.ds(dst, ROW_CHUNK), :], sem.at[buf])

    def start_tile(tile, buf):
        for e in range(N_EXPERTS):
            idx = tile * N_EXPERTS + e

            def body(c, carry, idx=idx):
                chunk_copy(pl.multiple_of(src_ref[idx] + c * ROW_CHUNK, ROW_CHUNK),
                           pl.multiple_of(dst_ref[idx] + c * ROW_CHUNK, ROW_CHUNK), buf).start()
                return carry

            lax.fori_loop(0, nch_ref[idx], body, 0)

    @pl.when(i == 0)
    def _():
        ystage[...] = jnp.zeros_like(ystage)
        start_tile(0, 0)

    @pl.when(i + 1 < n_tiles)
    def _():
        start_tile(i + 1, 1 - cur)

    def wait_one(c, carry):
        chunk_copy(0, 0, cur).wait()
        return carry

    lax.fori_loop(0, tot_ref[i], wait_one, 0)

    scol = scol_ref[...]
    gcol = gcol_ref[...]
    c_io = lax.broadcasted_iota(jnp.int32, (tc, R), 1)
    g = jnp.where(c_io == scol[:, 0:1], gcol[:, 0:1],
                  jnp.where(c_io == scol[:, 1:2], gcol[:, 1:2], 0.0)).astype(_BF16)
    moe = jnp.dot(g, ystage[cur], preferred_element_type=_F32)
    xn = x_ref[...] + gm_ref[...] * moe
    if final:
        ms = jnp.mean(xn * xn, axis=-1, keepdims=True)
        xn = xn * lax.rsqrt(ms + EPS) * nf_ref[...]
    o_ref[...] = xn
    if has_next:
        h_ref[...] = _rms_mod(xn, gn_ref[...], scn_ref[...], shn_ref[...]).astype(h_ref.dtype)


def _combine(x, y, src, nch, dst, tot, slot_col, gate_col, mod, i_gate, nf, final, g_next, mod_next):
    B, S, D = x.shape
    n_tiles, tc, _ = slot_col.shape
    nb = S // tc
    R = _stage_rows(tc)
    has_next = g_next is not None
    if not has_next:
        g_next, mod_next = nf, mod
    tile = lambda b, s, *_: (b * nb + s, 0, 0)
    row = lambda b, s, *_: (b, s, 0)
    grid_spec = pltpu.PrefetchScalarGridSpec(
        num_scalar_prefetch=4,
        grid=(B, nb),
        in_specs=[
            pl.BlockSpec((None, tc, D), row),
            pl.BlockSpec(memory_space=pl.ANY),
            pl.BlockSpec((None, tc, TOP_K), tile),
            pl.BlockSpec((None, tc, TOP_K), tile),
            pl.BlockSpec((None, None, 1, D), lambda b, s, *_: (i_gate, b, 0, 0)),
            pl.BlockSpec((1, D), lambda b, s, *_: (0, 0)),
            pl.BlockSpec((1, D), lambda b, s, *_: (0, 0)),
            pl.BlockSpec((None, None, 1, D), lambda b, s, *_: (1, b, 0, 0)),
            pl.BlockSpec((None, None, 1, D), lambda b, s, *_: (0, b, 0, 0)),
        ],
        out_specs=([pl.BlockSpec((None, tc, D), row)] * 2) if has_next else pl.BlockSpec((None, tc, D), row),
        scratch_shapes=[pltpu.VMEM((2, R, D), _BF16), pltpu.SemaphoreType.DMA((2,))],
    )
    out_shape = jax.ShapeDtypeStruct((B, S, D), _F32)
    if has_next:
        out_shape = [out_shape, jax.ShapeDtypeStruct((B, S, D), _BF16)]
    return pl.pallas_call(
        functools.partial(_combine_kernel, final=final, nb=nb, has_next=has_next),
        grid_spec=grid_spec,
        out_shape=out_shape,
        compiler_params=_params(("arbitrary", "arbitrary")),
        name="moe_combine",
    )(src, nch, dst, tot, x, y, slot_col, gate_col, mod, nf.reshape(1, D), g_next.reshape(1, D),
      mod_next, mod_next)


def _stage_rows(tc):
    worst = TOP_K * tc + 2 * (ROW_CHUNK - 1) * N_EXPERTS
    return -(-worst // LANES) * LANES


def _combine_plan(eidx, dest, gate, pstarts, tc):
    T = eidx.shape[0] * eidx.shape[2]
    n_tiles = T // tc
    flat = lambda a: a.transpose(0, 2, 1).reshape(n_tiles, tc, TOP_K)
    e_f, d_f, g_f = flat(eidx), flat(dest), flat(gate)
    onehot = (e_f[..., None] == jnp.arange(N_EXPERTS, dtype=jnp.int32)).astype(jnp.int32)
    cnt = jnp.sum(onehot, axis=(1, 2))
    first = pstarts[None, :] + jnp.cumsum(cnt, axis=0) - cnt
    src = first // ROW_CHUNK * ROW_CHUNK
    nch = jnp.where(cnt > 0, (first + cnt - src + ROW_CHUNK - 1) // ROW_CHUNK, 0)
    dst = ROW_CHUNK * (jnp.cumsum(nch, axis=1) - nch)
    slot = d_f + jnp.sum(onehot * (dst - src)[:, None, None, :], axis=-1)
    return (src.reshape(-1), nch.reshape(-1), dst.reshape(-1), jnp.sum(nch, axis=1),
            slot, g_f)


def kernel(x, c, w_ada, b_ada, norm_mix, norm_moe, w_in, b_forget, w_pool, pool_scale, w_branch,
           w_gate, b_gate, w_out, w_router, b_router, w_exp_gate, w_exp_up, w_exp_down, norm_final):
    B, S, D = x.shape
    L = w_ada.shape[0]
    W = D // 2
    nh = W // HEAD_DIM
    T = B * S
    A = T * TOP_K
    bm = min(256, A // N_EXPERTS)
    rows = (A // bm + N_EXPERTS) * bm

    mod_all = _ada(c, w_ada, b_ada)
    mod_all = mod_all.reshape(L, B, N_MOD, D).transpose(0, 2, 1, 3)[:, :, :, None, :]
    wr_t = w_router.T
    br = b_router.reshape(N_EXPERTS, 1)
    colscale = jnp.concatenate([jnp.ones((W,), _F32), jnp.full((W,), HEAD_DIM ** -0.5 * LOG2E, _F32),
                                jnp.ones((2 * W,), _F32)]).reshape(1, 4 * W)

    w_in_t = jnp.swapaxes(w_in, 1, 2)
    h = None
    for l in range(L):
        mod = mod_all[l]
        bf = jnp.zeros((1, LANES), _F32).at[0, :nh].set(b_forget[l])
        if h is None:
            fcum, h = _forget(x, w_in_t, l, nh, bf, norm=(norm_mix[l], mod, 0, 1))
        else:
            fcum = _forget(h, w_in_t, l, nh, bf)
        proj5 = _proj(h, w_in_t, l, colscale)
        pool_out = _pool(proj5, w_pool, l, pool_scale[l].reshape(1, W))
        attn = _attn(proj5, fcum, nh)
        merged = _merge(h, pool_out, attn, w_gate, b_gate, w_branch, l)
        x, eidx, pos, gate, counts = _out_proj_route(x, merged, w_out, l, mod, norm_moe[l], wr_t, br)
        counts = counts[:, 0].astype(jnp.int32)
        pcounts = (counts + bm - 1) // bm * bm
        pends = jnp.cumsum(pcounts)
        pstarts = pends - pcounts
        dest = pos
        for e in range(N_EXPERTS):
            dest = dest + jnp.where(eidx == e, pstarts[e], 0)
        n_used = (pends[-1] // bm).astype(jnp.int32)
        blk_ids = jnp.arange(rows // bm, dtype=jnp.int32)
        blk_ids = jnp.minimum(blk_ids, n_used - 1)
        block_expert = jnp.sum((pends[None, :] <= (blk_ids * bm)[:, None]).astype(jnp.int32), axis=1)
        block_expert = jnp.minimum(block_expert, N_EXPERTS - 1)
        ids = jnp.arange(N_EXPERTS, dtype=jnp.int32)
        later = (ids[None, :] > ids[:, None]) & (pcounts[None, :] > 0)
        next_of = jnp.min(jnp.where(later, ids[None, :], N_EXPERTS), axis=1)
        next_of = jnp.where(next_of == N_EXPERTS, -1, next_of)
        next_expert = jnp.sum(jnp.where(block_expert[:, None] == ids[None, :], next_of[None, :], 0), axis=1)
        n_used = n_used.reshape(1)
        td = min(DISPATCH_TILE, S)
        dest_d = dest.transpose(0, 2, 1).reshape(T // td, td, TOP_K).transpose(0, 2, 1)
        buf = _dispatch(x, norm_moe[l], mod, 3, 4, dest_d, pstarts + counts, pcounts - counts,
                        n_used, rows, bm)
        y = _experts(buf, block_expert, next_expert.astype(jnp.int32), n_used,
                     w_exp_gate, w_exp_up, w_exp_down, l, bm)
        plan = _combine_plan(eidx, dest, gate, pstarts, min(COMBINE_TILE, S))
        if l + 1 < L:
            x, h = _combine(x, y, *plan, mod, 5, norm_final, False, norm_mix[l + 1], mod_all[l + 1])
        else:
            x = _combine(x, y, *plan, mod, 5, norm_final, True, None, None)
    return x
```

```python
import functools

import jax
import jax.numpy as jnp
from jax import lax
from jax.experimental import pallas as pl
from jax.experimental.pallas import tpu as pltpu

N_MOD = 6
EPS = 1e-6
POOL_WINDOWS = (2, 4, 8, 16)
HEAD_DIM = 128
N_EXPERTS = 16
N_EXPERT_GROUPS = 4
EXPERTS_PER_GROUP = N_EXPERTS // N_EXPERT_GROUPS
TOP_K = 2
LOG2E = 1.4426950408889634
ROUTE_TILE = 512
DISPATCH_TILE = 512
COMBINE_TILE = 256
HEADS_PER_STEP = 2
LANES = 128
VMEM_LIMIT = 56 * 1024 * 1024

_F32 = jnp.float32
_BF16 = jnp.bfloat16


def _params(sem, vmem=VMEM_LIMIT):
    return pltpu.CompilerParams(dimension_semantics=sem, vmem_limit_bytes=vmem)


def _sigmoid(v):
    return 1.0 / (1.0 + jnp.exp(-v))


def _rms_mod(x, g, scale, shift):
    ms = jnp.mean(x * x, axis=-1, keepdims=True)
    return x * lax.rsqrt(ms + EPS) * g * (1.0 + scale) + shift


def _ada_kernel(c_ref, w_ref, b_ref, o_ref):
    c = c_ref[...]
    ca = (c * _sigmoid(c)).astype(_BF16)
    o_ref[...] = jnp.dot(ca, w_ref[...].astype(_BF16), preferred_element_type=_F32) + b_ref[...]


def _ada(c, w_ada, b_ada):
    L, D, M = w_ada.shape
    B = c.shape[0]
    tn = min(1024, M)
    return pl.pallas_call(
        _ada_kernel,
        grid=(L, M // tn),
        in_specs=[
            pl.BlockSpec((B, D), lambda l, n: (0, 0)),
            pl.BlockSpec((None, D, tn), lambda l, n: (l, 0, n)),
            pl.BlockSpec((None, 1, tn), lambda l, n: (l, 0, n)),
        ],
        out_specs=pl.BlockSpec((None, B, tn), lambda l, n: (l, 0, n)),
        out_shape=jax.ShapeDtypeStruct((L, B, M), _F32),
        compiler_params=_params(("arbitrary", "arbitrary")),
        name="ada_mod",
    )(c, w_ada, b_ada.reshape(L, 1, M))


def _first_inner_step():
    return (pl.program_id(1) == 0) & (pl.program_id(2) == 0)


def _proj_kernel(h_ref, w_ref, cs_ref, o_ref, wb_ref):
    @pl.when(_first_inner_step())
    def _():
        wb_ref[...] = (w_ref[...].T * cs_ref[...]).astype(_BF16)

    acc = jnp.dot(h_ref[...], wb_ref[...], preferred_element_type=_F32)
    for j in range(o_ref.shape[0]):
        o_ref[j] = acc[:, j * LANES:(j + 1) * LANES].astype(o_ref.dtype)


def _proj(h, w_in_t, layer, colscale):
    B, S, D = h.shape
    N = colscale.shape[1]
    tn = min(2048, N)
    tm = min(1024, S)
    return pl.pallas_call(
        _proj_kernel,
        grid=(N // tn, B, S // tm),
        in_specs=[
            pl.BlockSpec((None, tm, D), lambda n, b, m: (b, m, 0)),
            pl.BlockSpec((None, tn, D), lambda n, b, m: (layer, n, 0), pipeline_mode=pl.Buffered(1)),
            pl.BlockSpec((1, tn), lambda n, b, m: (0, n)),
        ],
        out_specs=pl.BlockSpec((None, tn // LANES, tm, LANES), lambda n, b, m: (b, n, m, 0)),
        out_shape=jax.ShapeDtypeStruct((B, N // LANES, S, LANES), _BF16),
        scratch_shapes=[pltpu.VMEM((D, tn), _BF16)],
        compiler_params=_params(("arbitrary", "arbitrary", "arbitrary")),
        name="in_proj",
    )(h, w_in_t, colscale)


def _forget_kernel(*refs, from_x):
    if from_x:
        x_ref, g_ref, sc_ref, sh_ref, w_ref, b_ref, o_ref, h_ref, carry = refs
        h = _rms_mod(x_ref[...], g_ref[...], sc_ref[...], sh_ref[...]).astype(_BF16)
        h_ref[...] = h
    else:
        hin_ref, w_ref, b_ref, o_ref, carry = refs
        h = hin_ref[...]
    ts, D = h.shape
    nh = w_ref.shape[0]

    @pl.when(pl.program_id(1) == 0)
    def _():
        carry[...] = jnp.zeros_like(carry)

    w = jnp.concatenate([w_ref[...], jnp.zeros((LANES - nh, D), _F32)], axis=0).astype(_BF16)
    fl = lax.dot_general(h, w, (((1,), (1,)), ((), ())), preferred_element_type=_F32) + b_ref[...]
    acc = jnp.minimum(fl, 0.0) - jnp.log(1.0 + jnp.exp(-jnp.abs(fl)))
    row = lax.broadcasted_iota(jnp.int32, acc.shape, 0)
    sh = 1
    while sh < ts:
        acc = acc + jnp.where(row >= sh, pltpu.roll(acc, sh, axis=0), 0.0)
        sh *= 2
    acc = acc + carry[...]
    o_ref[...] = acc
    carry[...] = acc[ts - 1:ts, :]


def _forget(h_or_x, w_in_t, layer, nh, bf, norm=None):
    B, S, D = h_or_x.shape
    ts = min(1024, S)
    first = (w_in_t.shape[1] - nh) // nh
    rows = pl.BlockSpec((None, ts, D), lambda b, s: (b, s, 0))
    tail_specs = [pl.BlockSpec((None, nh, D), lambda b, s: (layer, first, 0)),
                  pl.BlockSpec((1, LANES), lambda b, s: (0, 0))]
    f_spec = pl.BlockSpec((None, ts, LANES), lambda b, s: (b, s, 0))
    f_shape = jax.ShapeDtypeStruct((B, S, LANES), _F32)
    if norm is None:
        in_specs, args = [rows] + tail_specs, (h_or_x, w_in_t, bf)
        out_specs, out_shape = f_spec, f_shape
    else:
        g, mod, i_shift, i_scale = norm
        in_specs = [rows, pl.BlockSpec((1, D), lambda b, s: (0, 0)),
                    pl.BlockSpec((None, None, 1, D), lambda b, s: (i_scale, b, 0, 0)),
                    pl.BlockSpec((None, None, 1, D), lambda b, s: (i_shift, b, 0, 0))] + tail_specs
        args = (h_or_x, g.reshape(1, D), mod, mod, w_in_t, bf)
        out_specs, out_shape = [f_spec, rows], [f_shape, jax.ShapeDtypeStruct((B, S, D), _BF16)]
    return pl.pallas_call(
        functools.partial(_forget_kernel, from_x=norm is not None),
        grid=(B, S // ts),
        in_specs=in_specs,
        out_specs=out_specs,
        out_shape=out_shape,
        scratch_shapes=[pltpu.VMEM((1, LANES), _F32)],
        compiler_params=_params(("arbitrary", "arbitrary")),
        name="forget_cumsum",
    )(*args)


def _pool_kernel(u_ref, wp_ref, ps_ref, o_ref, *, cpg):
    S = u_ref.shape[1]
    gw = cpg * LANES
    row = lax.broadcasted_iota(jnp.int32, (S, gw), 0)
    for g, w in enumerate(POOL_WINDOWS):
        parts = [u_ref[g * cpg + j] for j in range(cpg)]
        u = (parts[0] if cpg == 1 else jnp.concatenate(parts, axis=-1)).astype(_F32)
        s = u
        sh = 1
        while sh < w:
            s = s + jnp.where(row >= sh, pltpu.roll(s, sh, axis=0), 0.0)
            sh *= 2
        cnt = jnp.minimum(row + 1, w).astype(_F32)
        pooled = s / cnt - u
        mixed = jnp.dot(pooled.astype(_BF16), wp_ref[g].astype(_BF16), preferred_element_type=_F32)
        o_ref[:, g * gw:(g + 1) * gw] = (mixed * ps_ref[:, g * gw:(g + 1) * gw]).astype(o_ref.dtype)


def _pool(proj5, w_pool, layer, ps):
    B, _, S, _ = proj5.shape
    _, G, gw, _ = w_pool.shape
    cpg = gw // LANES
    W = G * gw
    return pl.pallas_call(
        functools.partial(_pool_kernel, cpg=cpg),
        grid=(B,),
        in_specs=[
            pl.BlockSpec((None, G * cpg, S, LANES), lambda b: (b, 0, 0, 0)),
            pl.BlockSpec((None, G, gw, gw), lambda b: (layer, 0, 0, 0)),
            pl.BlockSpec((1, W), lambda b: (0, 0)),
        ],
        out_specs=pl.BlockSpec((None, S, W), lambda b: (b, 0, 0)),
        out_shape=jax.ShapeDtypeStruct((B, S, W), _BF16),
        compiler_params=_params(("arbitrary",)),
        name="pool_mixer",
    )(proj5, w_pool, ps)


def _attn_kernel(q_ref, k_ref, v_ref, f_ref, o_ref, qa_ref, ka_ref, va_ref, *, blk):
    hps, S, _ = q_ref.shape
    lane = lax.broadcasted_iota(jnp.int32, (S, LANES), 1)
    for g in range(hps):
        h = pl.program_id(1) * hps + g
        f = jnp.sum(jnp.where(lane == h, f_ref[...], 0.0), axis=-1, keepdims=True) * LOG2E
        f1 = f.astype(_BF16).astype(_F32)
        r1 = f - f1
        f2 = r1.astype(_BF16).astype(_F32)
        f3 = (r1 - f2).astype(_BF16).astype(_F32)
        qa_ref[g, :, :HEAD_DIM] = q_ref[g]
        qa_ref[g, :, HEAD_DIM:] = jnp.where(lane == 0, f1, jnp.where(lane == 1, f2, jnp.where(
            lane == 2, f3, jnp.where(lane < 6, 1.0, 0.0)))).astype(_BF16)
        ka_ref[g, :, :HEAD_DIM] = k_ref[g]
        ka_ref[g, :, HEAD_DIM:] = jnp.where(lane < 3, 1.0, jnp.where(lane == 3, -f1, jnp.where(
            lane == 4, -f2, jnp.where(lane == 5, -f3, 0.0)))).astype(_BF16)
        va_ref[g, :, :HEAD_DIM] = v_ref[g]
        va_ref[g, :, HEAD_DIM:] = jnp.where(lane == 0, 1.0, 0.0).astype(_BF16)

    nt = (((1,), (1,)), ((), ()))
    r_io = lax.broadcasted_iota(jnp.int32, (blk, blk), 0)
    c_io = lax.broadcasted_iota(jnp.int32, (blk, blk), 1)
    causal = c_io <= r_io
    nb = S // blk
    m = [[jnp.full((blk, 1), -jnp.inf, _F32)] * nb for _ in range(hps)]
    acc = [[jnp.zeros((blk, 2 * HEAD_DIM), _F32)] * nb for _ in range(hps)]
    for j in range(nb):
        for i in range(j, nb):
            for g in range(hps):
                s = lax.dot_general(qa_ref[g, i * blk:(i + 1) * blk, :], ka_ref[g, j * blk:(j + 1) * blk, :],
                                    nt, preferred_element_type=_F32)
                if i == j:
                    s = jnp.where(causal, s, -jnp.inf)
                m_new = jnp.maximum(m[g][i], jnp.max(s, axis=-1, keepdims=True))
                p = jnp.exp2(s - m_new).astype(_BF16)
                acc[g][i] = (jnp.exp2(m[g][i] - m_new) * acc[g][i]
                             + jnp.dot(p, va_ref[g, j * blk:(j + 1) * blk, :], preferred_element_type=_F32))
                m[g][i] = m_new
    for g in range(hps):
        for i in range(nb):
            o_ref[g, i * blk:(i + 1) * blk, :] = (
                acc[g][i][:, :HEAD_DIM] / acc[g][i][:, HEAD_DIM:HEAD_DIM + 1]).astype(o_ref.dtype)


def _attn(proj5, fcum, nh):
    B, _, S, _ = proj5.shape
    blk = min(512, S)
    hps = HEADS_PER_STEP if nh % HEADS_PER_STEP == 0 else 1
    chunk = lambda o: pl.BlockSpec((None, hps, S, HEAD_DIM), lambda b, h: (b, o // hps + h, 0, 0))
    return pl.pallas_call(
        functools.partial(_attn_kernel, blk=blk),
        grid=(B, nh // hps),
        in_specs=[
            chunk(nh), chunk(2 * nh), chunk(3 * nh),
            pl.BlockSpec((None, S, LANES), lambda b, h: (b, 0, 0)),
        ],
        out_specs=pl.BlockSpec((None, hps, S, HEAD_DIM), lambda b, h: (b, h, 0, 0)),
        out_shape=jax.ShapeDtypeStruct((B, nh, S, HEAD_DIM), _BF16),
        scratch_shapes=[pltpu.VMEM((hps, S, 2 * HEAD_DIM), _BF16)] * 3,
        compiler_params=_params(("arbitrary", "arbitrary")),
        name="forget_attn",
    )(proj5, proj5, proj5, fcum)


def _merge_kernel(h_ref, p_ref, a_ref, wg0_ref, wg1_ref, bg0_ref, bg1_ref, wb0_ref, wb1_ref, o_ref,
                  cg0_ref, cg1_ref, cb0_ref, cb1_ref):
    @pl.when(_first_inner_step())
    def _():
        cg0_ref[...] = wg0_ref[...].astype(_BF16)
        cg1_ref[...] = wg1_ref[...].astype(_BF16)
        cb0_ref[...] = wb0_ref[...].astype(_BF16)
        cb1_ref[...] = wb1_ref[...].astype(_BF16)

    h = h_ref[...]
    nh = a_ref.shape[0]
    a = jnp.concatenate([a_ref[j] for j in range(nh)], axis=-1)
    g0 = _sigmoid(jnp.dot(h, cg0_ref[...], preferred_element_type=_F32) + bg0_ref[...])
    y0 = jnp.dot(p_ref[...], cb0_ref[...], preferred_element_type=_F32)
    acc = g0 * y0
    g1 = _sigmoid(jnp.dot(h, cg1_ref[...], preferred_element_type=_F32) + bg1_ref[...])
    y1 = jnp.dot(a, cb1_ref[...], preferred_element_type=_F32)
    o_ref[...] = (acc + g1 * y1).astype(o_ref.dtype)


def _merge(h, pool_out, attn, w_gate, b_gate, w_branch, layer):
    B, S, D = h.shape
    W = pool_out.shape[-1]
    nh = attn.shape[1]
    L = w_gate.shape[0]
    tn = min(512, D)
    tm = min(1024, S)
    nn = D // tn
    bg = b_gate.reshape(L, 1, 2 * D)
    once = pl.Buffered(1)
    return pl.pallas_call(
        _merge_kernel,
        grid=(nn, B, S // tm),
        in_specs=[
            pl.BlockSpec((None, tm, D), lambda n, b, m: (b, m, 0)),
            pl.BlockSpec((None, tm, W), lambda n, b, m: (b, m, 0)),
            pl.BlockSpec((None, nh, tm, HEAD_DIM), lambda n, b, m: (b, 0, m, 0)),
            pl.BlockSpec((None, D, tn), lambda n, b, m: (layer, 0, n), pipeline_mode=once),
            pl.BlockSpec((None, D, tn), lambda n, b, m: (layer, 0, nn + n), pipeline_mode=once),
            pl.BlockSpec((None, 1, tn), lambda n, b, m: (layer, 0, n)),
            pl.BlockSpec((None, 1, tn), lambda n, b, m: (layer, 0, nn + n)),
            pl.BlockSpec((None, None, W, tn), lambda n, b, m: (layer, 0, 0, n), pipeline_mode=once),
            pl.BlockSpec((None, None, W, tn), lambda n, b, m: (layer, 1, 0, n), pipeline_mode=once),
        ],
        out_specs=pl.BlockSpec((None, tm, tn), lambda n, b, m: (b, m, n)),
        out_shape=jax.ShapeDtypeStruct((B, S, D), _BF16),
        scratch_shapes=[pltpu.VMEM((D, tn), _BF16), pltpu.VMEM((D, tn), _BF16),
                        pltpu.VMEM((W, tn), _BF16), pltpu.VMEM((W, tn), _BF16)],
        compiler_params=_params(("arbitrary", "arbitrary", "arbitrary")),
        name="branch_merge",
    )(h, pool_out, attn, w_gate, w_gate, bg, bg, w_branch, w_branch)


def _route_tile(x, g_ref, sc_ref, sh_ref, wr_ref, br_ref, e_ref, pos_ref, gate_ref, cnt_ref):
    first_step = (pl.program_id(0) == 0) & (pl.program_id(1) == 0)

    @pl.when(first_step)
    def _():
        cnt_ref[...] = jnp.zeros_like(cnt_ref)

    h = _rms_mod(x, g_ref[...], sc_ref[...], sh_ref[...])
    tr, D = h.shape
    ns = tr // LANES

    nt = (((1,), (1,)), ((), ()))
    h_hi = h.astype(_BF16)
    h_lo = (h - h_hi.astype(_F32)).astype(_BF16)
    w = wr_ref[...]
    w_hi = w.astype(_BF16)
    w_lo = (w - w_hi.astype(_F32)).astype(_BF16)
    both = lax.dot_general(jnp.concatenate([w_hi, w_lo], axis=0), h_hi, nt, preferred_element_type=_F32)
    logits = (both[:N_EXPERTS] + both[N_EXPERTS:]
              + lax.dot_general(w_hi, h_lo, nt, preferred_element_type=_F32))
    rows = [jnp.concatenate([logits[e:e + 1, s * LANES:(s + 1) * LANES] for s in range(ns)], axis=0)
            for e in range(N_EXPERTS)]
    mx = functools.reduce(jnp.maximum, rows)
    ex = [jnp.exp(r - mx) for r in rows]
    den = functools.reduce(lambda a, b: a + b, ex)
    probs = [v / den for v in ex]
    sel = [probs[e] + br_ref[e:e + 1, :] for e in range(N_EXPERTS)]

    in_top = [None] * N_EXPERTS
    gscore = []
    for g in range(N_EXPERT_GROUPS):
        ids = range(g * EXPERTS_PER_GROUP, (g + 1) * EXPERTS_PER_GROUP)
        score = None
        for i in ids:
            rank = None
            for j in ids:
                if j == i:
                    continue
                beats = (sel[j] > sel[i]) | ((sel[j] == sel[i]) if j < i else False)
                beats = beats.astype(_F32)
                rank = beats if rank is None else rank + beats
            in_top[i] = rank < float(TOP_K)
            term = jnp.where(in_top[i], sel[i], 0.0)
            score = term if score is None else score + term
        gscore.append(score)
    chosen = []
    for g in range(N_EXPERT_GROUPS):
        lose = None
        for g2 in range(N_EXPERT_GROUPS):
            if g2 == g:
                continue
            b = (gscore[g2] > gscore[g]) | ((gscore[g2] == gscore[g]) if g2 < g else False)
            lose = b if lose is None else (lose | b)
        chosen.append(jnp.logical_not(lose))
    picked = [in_top[e] & chosen[e // EXPERTS_PER_GROUP] for e in range(N_EXPERTS)]
    pf = [p.astype(_F32) for p in picked]

    n = N_EXPERTS * ns
    onehot = jnp.concatenate(pf, axis=0).astype(_BF16)
    upper = (lax.broadcasted_iota(jnp.int32, (LANES, LANES), 0)
             <= lax.broadcasted_iota(jnp.int32, (LANES, LANES), 1)).astype(_BF16)
    incl = jnp.dot(onehot, upper, preferred_element_type=_F32)
    r_io = lax.broadcasted_iota(jnp.int32, (n, n), 0)
    c_io = lax.broadcasted_iota(jnp.int32, (n, n), 1)
    shift = ns.bit_length() - 1
    same_expert = lax.shift_right_logical(r_io, shift) == lax.shift_right_logical(c_io, shift)
    earlier = jnp.where(same_expert & (c_io < r_io), 1.0, 0.0).astype(_BF16)
    carry = jnp.dot(earlier, incl.astype(_BF16), preferred_element_type=_F32)[:, LANES - 1:LANES]
    base = cnt_ref[:, 0:1]
    posm = [incl[e * ns:(e + 1) * ns, :] - 1.0 + carry[e * ns:(e + 1) * ns, :] + base[e:e + 1, :]
            for e in range(N_EXPERTS)]
    last = [(incl[(e + 1) * ns - 1:(e + 1) * ns, LANES - 1:LANES] + carry[(e + 1) * ns - 1:(e + 1) * ns, :])
            for e in range(N_EXPERTS)]
    cnt_ref[...] = jnp.broadcast_to(base + jnp.concatenate(last, axis=0), cnt_ref.shape)

    gnum = [pf[e] * probs[e] for e in range(N_EXPERTS)]
    gden = functools.reduce(lambda a, b: a + b, gnum)
    zero = jnp.zeros_like(pf[0])
    seen = zero
    e_out = [zero, zero]
    p_out = [zero, zero]
    g_out = [zero, zero]
    for e in range(N_EXPERTS):
        for k in range(TOP_K):
            hit = pf[e] * (seen == float(k)).astype(_F32)
            e_out[k] = e_out[k] + hit * float(e)
            p_out[k] = p_out[k] + hit * posm[e]
            g_out[k] = g_out[k] + hit * gnum[e]
        seen = seen + pf[e]
    for k in range(TOP_K):
        e_ref[k] = e_out[k].astype(jnp.int32)
        pos_ref[k] = p_out[k].astype(jnp.int32)
        gate_ref[k] = g_out[k] / gden


def _out_route_kernel(x_ref, m_ref, w_ref, ga_ref, g_ref, sc_ref, sh_ref, wr_ref, br_ref,
                      o_ref, e_ref, pos_ref, gate_ref, cnt_ref, wb_ref):
    @pl.when((pl.program_id(0) == 0) & (pl.program_id(1) == 0))
    def _():
        wb_ref[...] = w_ref[...].astype(_BF16)

    xn = x_ref[...] + ga_ref[...] * jnp.dot(m_ref[...], wb_ref[...], preferred_element_type=_F32)
    o_ref[...] = xn
    _route_tile(xn, g_ref, sc_ref, sh_ref, wr_ref, br_ref, e_ref, pos_ref, gate_ref, cnt_ref)


def _out_proj_route(x, merged, w_out, layer, mod, g_moe, wr_t, br):
    B, S, D = x.shape
    tr = min(ROUTE_TILE, S)
    ns = tr // LANES
    assert ns & (ns - 1) == 0
    nb = S // tr
    E = N_EXPERTS
    tok = lambda dt: jax.ShapeDtypeStruct((B * nb, TOP_K, ns, LANES), dt)
    tok_spec = pl.BlockSpec((None, TOP_K, ns, LANES), lambda b, s: (b * nb + s, 0, 0, 0))
    rows = pl.BlockSpec((None, tr, D), lambda b, s: (b, s, 0))
    mod_row = lambda i: pl.BlockSpec((None, None, 1, D), lambda b, s: (i, b, 0, 0))
    xn, eidx, pos, gate, counts = pl.pallas_call(
        _out_route_kernel,
        grid=(B, nb),
        in_specs=[
            rows, rows,
            pl.BlockSpec((None, D, D), lambda b, s: (layer, 0, 0), pipeline_mode=pl.Buffered(1)),
            mod_row(2),
            pl.BlockSpec((1, D), lambda b, s: (0, 0)),
            mod_row(4), mod_row(3),
            pl.BlockSpec((E, D), lambda b, s: (0, 0)),
            pl.BlockSpec((E, 1), lambda b, s: (0, 0)),
        ],
        out_specs=[
            rows, tok_spec, tok_spec, tok_spec,
            pl.BlockSpec((E, LANES), lambda b, s: (0, 0)),
        ],
        out_shape=[
            jax.ShapeDtypeStruct((B, S, D), _F32),
            tok(jnp.int32), tok(jnp.int32), tok(_F32),
            jax.ShapeDtypeStruct((E, LANES), _F32),
        ],
        scratch_shapes=[pltpu.VMEM((D, D), _BF16)],
        compiler_params=_params(("arbitrary", "arbitrary")),
        name="out_proj_route",
    )(x, merged, w_out, mod, g_moe.reshape(1, D), mod, mod, wr_t, br)
    flat = lambda a: a.reshape(B * nb, TOP_K, tr)
    return xn, flat(eidx), flat(pos), flat(gate), counts


def _dispatch_kernel(pad_start_ref, pad_len_ref, nu_ref, dest_ref, x0_ref, xn_ref, g_ref,
                     sc0_ref, sh0_ref, scn_ref, shn_ref, buf_ref, h_ref, zblk, sem):
    i = pl.program_id(0)
    tr = xn_ref.shape[0]

    @pl.when(i == 0)
    def _():
        h_ref[0] = _rms_mod(x0_ref[...], g_ref[...], sc0_ref[...], sh0_ref[...])

    def row_wait():
        pltpu.make_async_copy(h_ref.at[0, pl.ds(0, 1), :], buf_ref.at[pl.ds(0, 1), :], sem).wait()

    def step(slot):
        for r in range(tr):
            for k in range(TOP_K):
                pltpu.make_async_copy(h_ref.at[slot, pl.ds(r, 1), :],
                                      buf_ref.at[pl.ds(dest_ref[k, r], 1), :], sem).start(priority=k)

        @pl.when(i + 1 < pl.num_programs(0))
        def _():
            h_ref[1 - slot] = _rms_mod(xn_ref[...], g_ref[...], scn_ref[...], shn_ref[...])

        for _ in range(tr * TOP_K):
            row_wait()

    for slot in range(2):
        pl.when(lax.rem(i, 2) == slot)(functools.partial(step, slot))

    @pl.when(pl.program_id(0) == 0)
    def _():
        bm = zblk.shape[0]
        nblk = buf_ref.shape[0] // bm
        zblk[...] = jnp.zeros_like(zblk)
        for e in range(N_EXPERTS):
            def zissue(r, carry, e=e):
                pltpu.make_async_copy(zblk.at[pl.ds(0, 1), :],
                                      buf_ref.at[pl.ds(pad_start_ref[e] + r, 1), :], sem).start()
                return carry
            lax.fori_loop(0, pad_len_ref[e], zissue, 0)
        for e in range(N_EXPERTS):
            def zdrain(r, carry):
                row_wait()
                return carry
            lax.fori_loop(0, pad_len_ref[e], zdrain, 0)

        def blk_copy(i):
            return pltpu.make_async_copy(zblk, buf_ref.at[pl.ds(pl.multiple_of(i * bm, bm), bm), :], sem)

        def bissue(i, carry):
            blk_copy(i).start()
            return carry

        def bdrain(i, carry):
            blk_copy(i).wait()
            return carry

        lax.fori_loop(nu_ref[0], nblk, bissue, 0)
        lax.fori_loop(nu_ref[0], nblk, bdrain, 0)


def _dispatch(x, g, mod, i_shift, i_scale, dest, pad_start, pad_len, n_used, rows, bm):
    B, S, D = x.shape
    nbt, _, tr = dest.shape
    nb = S // tr
    nxt = lambda i: jnp.minimum(i + 1, nbt - 1)
    grid_spec = pltpu.PrefetchScalarGridSpec(
        num_scalar_prefetch=3,
        grid=(nbt,),
        in_specs=[
            pl.BlockSpec((None, TOP_K, tr), lambda i, *_: (i, 0, 0), memory_space=pltpu.SMEM),
            pl.BlockSpec((None, tr, D), lambda i, *_: (0, 0, 0)),
            pl.BlockSpec((None, tr, D), lambda i, *_: (nxt(i) // nb, nxt(i) % nb, 0)),
            pl.BlockSpec((1, D), lambda i, *_: (0, 0)),
            pl.BlockSpec((None, None, 1, D), lambda i, *_: (i_scale, 0, 0, 0)),
            pl.BlockSpec((None, None, 1, D), lambda i, *_: (i_shift, 0, 0, 0)),
            pl.BlockSpec((None, None, 1, D), lambda i, *_: (i_scale, nxt(i) // nb, 0, 0)),
            pl.BlockSpec((None, None, 1, D), lambda i, *_: (i_shift, nxt(i) // nb, 0, 0)),
        ],
        out_specs=pl.BlockSpec(memory_space=pl.ANY),
        scratch_shapes=[pltpu.VMEM((2, tr, D), _F32), pltpu.VMEM((bm, D), _F32),
                        pltpu.SemaphoreType.DMA(())],
    )
    return pl.pallas_call(
        _dispatch_kernel,
        grid_spec=grid_spec,
        out_shape=jax.ShapeDtypeStruct((rows, D), _F32),
        compiler_params=_params(("arbitrary",)),
        name="moe_dispatch",
    )(pad_start, pad_len, n_used, dest, x, x, g.reshape(1, D), mod, mod, mod, mod)


def _expert_kernel(be_ref, nx_ref, nu_ref, x_ref, wg_hbm, wu_hbm, wd_hbm, o_ref,
                   wgb, wub, wdb, sg, su, sd, sem, *, layer):
    i = pl.program_id(0)
    e = be_ref[i]
    live = i < nu_ref[0]
    new_expert = (i == 0) | (e != be_ref[jnp.maximum(i - 1, 0)])

    def weight_copies(ex):
        return [pltpu.make_async_copy(w.at[layer, ex], s, sem.at[n])
                for n, (w, s) in enumerate(((wg_hbm, sg), (wu_hbm, su), (wd_hbm, sd)))]

    @pl.when(live & new_expert)
    def _():
        @pl.when(i == 0)
        def _():
            for cp in weight_copies(e):
                cp.start()

        for cp in weight_copies(e):
            cp.wait()
        wgb[...] = sg[...].astype(_BF16)
        wub[...] = su[...].astype(_BF16)
        wdb[...] = sd[...].astype(_BF16)

        @pl.when(nx_ref[i] >= 0)
        def _():
            for cp in weight_copies(nx_ref[i]):
                cp.start(priority=1)

    @pl.when(live)
    def _():
        xb = x_ref[...].astype(_BF16)
        a = jnp.dot(xb, wgb[...], preferred_element_type=_F32)
        u = jnp.dot(xb, wub[...], preferred_element_type=_F32)
        act = (a * _sigmoid(a) * u).astype(_BF16)
        o_ref[...] = jnp.dot(act, wdb[...], preferred_element_type=_F32).astype(o_ref.dtype)

    @pl.when(jnp.logical_not(live))
    def _():
        o_ref[...] = jnp.zeros_like(o_ref)


def _experts(buf, block_expert, next_expert, n_used, wg, wu, wd, layer, bm):
    rows, D = buf.shape
    De = wg.shape[-1]
    nblk = rows // bm
    grid_spec = pltpu.PrefetchScalarGridSpec(
        num_scalar_prefetch=3,
        grid=(nblk,),
        in_specs=[
            pl.BlockSpec((bm, D), lambda i, be, nx, nu: (jnp.minimum(i, jnp.maximum(nu[0] - 1, 0)), 0)),
            pl.BlockSpec(memory_space=pl.ANY),
            pl.BlockSpec(memory_space=pl.ANY),
            pl.BlockSpec(memory_space=pl.ANY),
        ],
        out_specs=pl.BlockSpec((bm, D), lambda i, be, nx, nu: (i, 0)),
        scratch_shapes=[
            pltpu.VMEM((D, De), _BF16), pltpu.VMEM((D, De), _BF16), pltpu.VMEM((De, D), _BF16),
            pltpu.VMEM((D, De), _F32), pltpu.VMEM((D, De), _F32), pltpu.VMEM((De, D), _F32),
            pltpu.SemaphoreType.DMA((3,)),
        ],
    )
    return pl.pallas_call(
        functools.partial(_expert_kernel, layer=layer),
        grid_spec=grid_spec,
        out_shape=jax.ShapeDtypeStruct((rows, D), _BF16),
        compiler_params=_params(("arbitrary",), vmem=60 * 1024 * 1024),
        name="moe_experts",
    )(block_expert, next_expert, n_used, buf, wg, wu, wd)


ROW_CHUNK = 16


def _combine_kernel(src_ref, nch_ref, dst_ref, tot_ref,
                    x_ref, y_hbm, scol_ref, gcol_ref, gm_ref, nf_ref, gn_ref, scn_ref, shn_ref,
                    *rest, final, nb, has_next):
    if has_next:
        o_ref, h_ref, ystage, sem = rest
    else:
        o_ref, ystage, sem = rest
        h_ref = None
    i = pl.program_id(0) * nb + pl.program_id(1)
    n_tiles = pl.num_programs(0) * nb
    cur = lax.rem(i, 2)
    tc = x_ref.shape[0]
    R = ystage.shape[1]

    def chunk_copy(src, dst, buf):
        return pltpu.make_async_copy(y_hbm.at[pl.ds(src, ROW_CHUNK), :],
                                     ystage.at[buf, pl.ds(dst, ROW_CHUNK), :], sem.at[buf])

    def start_tile(tile, buf):
        for e in range(N_EXPERTS):
            idx = tile * N_EXPERTS + e

            def body(c, carry, idx=idx):
                chunk_copy(pl.multiple_of(src_ref[idx] + c * ROW_CHUNK, ROW_CHUNK),
                           pl.multiple_of(dst_ref[idx] + c * ROW_CHUNK, ROW_CHUNK), buf).start()
                return carry

            lax.fori_loop(0, nch_ref[idx], body, 0)

    @pl.when(i == 0)
    def _():
        ystage[...] = jnp.zeros_like(ystage)
        start_tile(0, 0)

    @pl.when(i + 1 < n_tiles)
    def _():
        start_tile(i + 1, 1 - cur)

    def wait_one(c, carry):
        chunk_copy(0, 0, cur).wait()
        return carry

    lax.fori_loop(0, tot_ref[i], wait_one, 0)

    scol = scol_ref[...]
    gcol = gcol_ref[...]
    c_io = lax.broadcasted_iota(jnp.int32, (tc, R), 1)
    g = jnp.where(c_io == scol[:, 0:1], gcol[:, 0:1],
                  jnp.where(c_io == scol[:, 1:2], gcol[:, 1:2], 0.0)).astype(_BF16)
    moe = jnp.dot(g, ystage[cur], preferred_element_type=_F32)
    xn = x_ref[...] + gm_ref[...] * moe
    if final:
        ms = jnp.mean(xn * xn, axis=-1, keepdims=True)
        xn = xn * lax.rsqrt(ms + EPS) * nf_ref[...]
    o_ref[...] = xn
    if has_next:
        h_ref[...] = _rms_mod(xn, gn_ref[...], scn_ref[...], shn_ref[...]).astype(h_ref.dtype)


def _combine(x, y, src, nch, dst, tot, slot_col, gate_col, mod, i_gate, nf, final, g_next, mod_next):
    B, S, D = x.shape
    n_tiles, tc, _ = slot_col.shape
    nb = S // tc
    R = _stage_rows(tc)
    has_next = g_next is not None
    if not has_next:
        g_next, mod_next = nf, mod
    tile = lambda b, s, *_: (b * nb + s, 0, 0)
    row = lambda b, s, *_: (b, s, 0)
    grid_spec = pltpu.PrefetchScalarGridSpec(
        num_scalar_prefetch=4,
        grid=(B, nb),
        in_specs=[
            pl.BlockSpec((None, tc, D), row),
            pl.BlockSpec(memory_space=pl.ANY),
            pl.BlockSpec((None, tc, TOP_K), tile),
            pl.BlockSpec((None, tc, TOP_K), tile),
            pl.BlockSpec((None, None, 1, D), lambda b, s, *_: (i_gate, b, 0, 0)),
            pl.BlockSpec((1, D), lambda b, s, *_: (0, 0)),
            pl.BlockSpec((1, D), lambda b, s, *_: (0, 0)),
            pl.BlockSpec((None, None, 1, D), lambda b, s, *_: (1, b, 0, 0)),
            pl.BlockSpec((None, None, 1, D), lambda b, s, *_: (0, b, 0, 0)),
        ],
        out_specs=([pl.BlockSpec((None, tc, D), row)] * 2) if has_next else pl.BlockSpec((None, tc, D), row),
        scratch_shapes=[pltpu.VMEM((2, R, D), _BF16), pltpu.SemaphoreType.DMA((2,))],
    )
    out_shape = jax.ShapeDtypeStruct((B, S, D), _F32)
    if has_next:
        out_shape = [out_shape, jax.ShapeDtypeStruct((B, S, D), _BF16)]
    return pl.pallas_call(
        functools.partial(_combine_kernel, final=final, nb=nb, has_next=has_next),
        grid_spec=grid_spec,
        out_shape=out_shape,
        compiler_params=_params(("arbitrary", "arbitrary")),
        name="moe_combine",
    )(src, nch, dst, tot, x, y, slot_col, gate_col, mod, nf.reshape(1, D), g_next.reshape(1, D),
      mod_next, mod_next)


def _stage_rows(tc):
    worst = TOP_K * tc + 2 * (ROW_CHUNK - 1) * N_EXPERTS
    return -(-worst // LANES) * LANES


def _combine_plan(eidx, dest, gate, pstarts, tc):
    T = eidx.shape[0] * eidx.shape[2]
    n_tiles = T // tc
    flat = lambda a: a.transpose(0, 2, 1).reshape(n_tiles, tc, TOP_K)
    e_f, d_f, g_f = flat(eidx), flat(dest), flat(gate)
    onehot = (e_f[..., None] == jnp.arange(N_EXPERTS, dtype=jnp.int32)).astype(jnp.int32)
    cnt = jnp.sum(onehot, axis=(1, 2))
    first = pstarts[None, :] + jnp.cumsum(cnt, axis=0) - cnt
    src = first // ROW_CHUNK * ROW_CHUNK
    nch = jnp.where(cnt > 0, (first + cnt - src + ROW_CHUNK - 1) // ROW_CHUNK, 0)
    dst = ROW_CHUNK * (jnp.cumsum(nch, axis=1) - nch)
    slot = d_f + jnp.sum(onehot * (dst - src)[:, None, None, :], axis=-1)
    return (src.reshape(-1), nch.reshape(-1), dst.reshape(-1), jnp.sum(nch, axis=1),
            slot, g_f)


def kernel(x, c, w_ada, b_ada, norm_mix, norm_moe, w_in, b_forget, w_pool, pool_scale, w_branch,
           w_gate, b_gate, w_out, w_router, b_router, w_exp_gate, w_exp_up, w_exp_down, norm_final):
    B, S, D = x.shape
    L = w_ada.shape[0]
    W = D // 2
    nh = W // HEAD_DIM
    T = B * S
    A = T * TOP_K
    bm = min(256, A // N_EXPERTS)
    rows = (A // bm + N_EXPERTS) * bm

    mod_all = _ada(c, w_ada, b_ada)
    mod_all = mod_all.reshape(L, B, N_MOD, D).transpose(0, 2, 1, 3)[:, :, :, None, :]
    wr_t = w_router.T
    br = b_router.reshape(N_EXPERTS, 1)
    colscale = jnp.concatenate([jnp.ones((W,), _F32), jnp.full((W,), HEAD_DIM ** -0.5 * LOG2E, _F32),
                                jnp.ones((2 * W,), _F32)]).reshape(1, 4 * W)

    w_in_t = jnp.swapaxes(w_in, 1, 2)
    h = None
    for l in range(L):
        mod = mod_all[l]
        bf = jnp.zeros((1, LANES), _F32).at[0, :nh].set(b_forget[l])
        if h is None:
            fcum, h = _forget(x, w_in_t, l, nh, bf, norm=(norm_mix[l], mod, 0, 1))
        else:
            fcum = _forget(h, w_in_t, l, nh, bf)
        proj5 = _proj(h, w_in_t, l, colscale)
        pool_out = _pool(proj5, w_pool, l, pool_scale[l].reshape(1, W))
        attn = _attn(proj5, fcum, nh)
        merged = _merge(h, pool_out, attn, w_gate, b_gate, w_branch, l)
        x, eidx, pos, gate, counts = _out_proj_route(x, merged, w_out, l, mod, norm_moe[l], wr_t, br)
        counts = counts[:, 0].astype(jnp.int32)
        pcounts = (counts + bm - 1) // bm * bm
        pends = jnp.cumsum(pcounts)
        pstarts = pends - pcounts
        dest = pos
        for e in range(N_EXPERTS):
            dest = dest + jnp.where(eidx == e, pstarts[e], 0)
        n_used = (pends[-1] // bm).astype(jnp.int32)
        blk_ids = jnp.arange(rows // bm, dtype=jnp.int32)
        blk_ids = jnp.minimum(blk_ids, n_used - 1)
        block_expert = jnp.sum((pends[None, :] <= (blk_ids * bm)[:, None]).astype(jnp.int32), axis=1)
        block_expert = jnp.minimum(block_expert, N_EXPERTS - 1)
        ids = jnp.arange(N_EXPERTS, dtype=jnp.int32)
        later = (ids[None, :] > ids[:, None]) & (pcounts[None, :] > 0)
        next_of = jnp.min(jnp.where(later, ids[None, :], N_EXPERTS), axis=1)
        next_of = jnp.where(next_of == N_EXPERTS, -1, next_of)
        next_expert = jnp.sum(jnp.where(block_expert[:, None] == ids[None, :], next_of[None, :], 0), axis=1)
        n_used = n_used.reshape(1)
        td = min(DISPATCH_TILE, S)
        dest_d = dest.transpose(0, 2, 1).reshape(T // td, td, TOP_K).transpose(0, 2, 1)
        buf = _dispatch(x, norm_moe[l], mod, 3, 4, dest_d, pstarts + counts, pcounts - counts,
                        n_used, rows, bm)
        y = _experts(buf, block_expert, next_expert.astype(jnp.int32), n_used,
                     w_exp_gate, w_exp_up, w_exp_down, l, bm)
        plan = _combine_plan(eidx, dest, gate, pstarts, min(COMBINE_TILE, S))
        if l + 1 < L:
            x, h = _combine(x, y, *plan, mod, 5, norm_final, False, norm_mix[l + 1], mod_all[l + 1])
        else:
            x = _combine(x, y, *plan, mod, 5, norm_final, True, None, None)
    return x
```

```python
import functools

import jax
import jax.numpy as jnp
from jax import lax
from jax.experimental import pallas as pl
from jax.experimental.pallas import tpu as pltpu

N_MOD = 6
EPS = 1e-6
POOL_WINDOWS = (2, 4, 8, 16)
HEAD_DIM = 128
N_EXPERTS = 16
N_EXPERT_GROUPS = 4
EXPERTS_PER_GROUP = N_EXPERTS // N_EXPERT_GROUPS
TOP_K = 2
LOG2E = 1.4426950408889634
ROUTE_TILE = 512
DISPATCH_TILE = 512
COMBINE_TILE = 256
LANES = 128
VMEM_LIMIT = 56 * 1024 * 1024

_F32 = jnp.float32
_BF16 = jnp.bfloat16


def _params(sem, vmem=VMEM_LIMIT):
    return pltpu.CompilerParams(dimension_semantics=sem, vmem_limit_bytes=vmem)


def _sigmoid(v):
    return 1.0 / (1.0 + jnp.exp(-v))


def _rms_mod(x, g, scale, shift):
    ms = jnp.mean(x * x, axis=-1, keepdims=True)
    return x * lax.rsqrt(ms + EPS) * g * (1.0 + scale) + shift


def _ada_kernel(c_ref, w_ref, b_ref, o_ref):
    c = c_ref[...]
    ca = (c * _sigmoid(c)).astype(_BF16)
    o_ref[...] = jnp.dot(ca, w_ref[...].astype(_BF16), preferred_element_type=_F32) + b_ref[...]


def _ada(c, w_ada, b_ada):
    L, D, M = w_ada.shape
    B = c.shape[0]
    tn = min(1024, M)
    return pl.pallas_call(
        _ada_kernel,
        grid=(L, M // tn),
        in_specs=[
            pl.BlockSpec((B, D), lambda l, n: (0, 0)),
            pl.BlockSpec((None, D, tn), lambda l, n: (l, 0, n)),
            pl.BlockSpec((None, 1, tn), lambda l, n: (l, 0, n)),
        ],
        out_specs=pl.BlockSpec((None, B, tn), lambda l, n: (l, 0, n)),
        out_shape=jax.ShapeDtypeStruct((L, B, M), _F32),
        compiler_params=_params(("arbitrary", "arbitrary")),
        name="ada_mod",
    )(c, w_ada, b_ada.reshape(L, 1, M))


def _first_inner_step():
    return (pl.program_id(1) == 0) & (pl.program_id(2) == 0)


def _proj_kernel(h_ref, w_ref, cs_ref, o_ref, wb_ref):
    @pl.when(_first_inner_step())
    def _():
        wb_ref[...] = (w_ref[...].T * cs_ref[...]).astype(_BF16)

    acc = jnp.dot(h_ref[...], wb_ref[...], preferred_element_type=_F32)
    for j in range(o_ref.shape[0]):
        o_ref[j] = acc[:, j * LANES:(j + 1) * LANES].astype(o_ref.dtype)


def _proj(h, w_in_t, layer, colscale):
    B, S, D = h.shape
    N = colscale.shape[1]
    tn = min(2048, N)
    tm = min(1024, S)
    return pl.pallas_call(
        _proj_kernel,
        grid=(N // tn, B, S // tm),
        in_specs=[
            pl.BlockSpec((None, tm, D), lambda n, b, m: (b, m, 0)),
            pl.BlockSpec((None, tn, D), lambda n, b, m: (layer, n, 0), pipeline_mode=pl.Buffered(1)),
            pl.BlockSpec((1, tn), lambda n, b, m: (0, n)),
        ],
        out_specs=pl.BlockSpec((None, tn // LANES, tm, LANES), lambda n, b, m: (b, n, m, 0)),
        out_shape=jax.ShapeDtypeStruct((B, N // LANES, S, LANES), _BF16),
        scratch_shapes=[pltpu.VMEM((D, tn), _BF16)],
        compiler_params=_params(("arbitrary", "arbitrary", "arbitrary")),
        name="in_proj",
    )(h, w_in_t, colscale)


def _forget_kernel(*refs, from_x):
    if from_x:
        x_ref, g_ref, sc_ref, sh_ref, w_ref, b_ref, o_ref, h_ref, carry = refs
        h = _rms_mod(x_ref[...], g_ref[...], sc_ref[...], sh_ref[...]).astype(_BF16)
        h_ref[...] = h
    else:
        hin_ref, w_ref, b_ref, o_ref, carry = refs
        h = hin_ref[...]
    _forget_tile(h, w_ref, b_ref, o_ref, carry)


def _forget_tile(h, w_ref, b_ref, o_ref, carry):
    ts, D = h.shape
    nh = w_ref.shape[0]

    @pl.when(pl.program_id(1) == 0)
    def _():
        carry[...] = jnp.zeros_like(carry)

    w = jnp.concatenate([w_ref[...], jnp.zeros((LANES - nh, D), _F32)], axis=0).astype(_BF16)
    fl = lax.dot_general(h, w, (((1,), (1,)), ((), ())), preferred_element_type=_F32) + b_ref[...]
    acc = jnp.minimum(fl, 0.0) - jnp.log(1.0 + jnp.exp(-jnp.abs(fl)))
    row = lax.broadcasted_iota(jnp.int32, acc.shape, 0)
    sh = 1
    while sh < ts:
        acc = acc + jnp.where(row >= sh, pltpu.roll(acc, sh, axis=0), 0.0)
        sh *= 2
    acc = acc + carry[...]
    o_ref[...] = acc
    carry[...] = acc[ts - 1:ts, :]


def _forget(h_or_x, w_in_t, layer, nh, bf, norm=None):
    B, S, D = h_or_x.shape
    ts = min(1024, S)
    first = (w_in_t.shape[1] - nh) // nh
    rows = pl.BlockSpec((None, ts, D), lambda b, s: (b, s, 0))
    tail_specs = [pl.BlockSpec((None, nh, D), lambda b, s: (layer, first, 0)),
                  pl.BlockSpec((1, LANES), lambda b, s: (0, 0))]
    f_spec = pl.BlockSpec((None, ts, LANES), lambda b, s: (b, s, 0))
    f_shape = jax.ShapeDtypeStruct((B, S, LANES), _F32)
    if norm is None:
        in_specs, args = [rows] + tail_specs, (h_or_x, w_in_t, bf)
        out_specs, out_shape = f_spec, f_shape
    else:
        g, mod, i_shift, i_scale = norm
        in_specs = [rows, pl.BlockSpec((1, D), lambda b, s: (0, 0)),
                    pl.BlockSpec((None, None, 1, D), lambda b, s: (i_scale, b, 0, 0)),
                    pl.BlockSpec((None, None, 1, D), lambda b, s: (i_shift, b, 0, 0))] + tail_specs
        args = (h_or_x, g.reshape(1, D), mod, mod, w_in_t, bf)
        out_specs, out_shape = [f_spec, rows], [f_shape, jax.ShapeDtypeStruct((B, S, D), _BF16)]
    return pl.pallas_call(
        functools.partial(_forget_kernel, from_x=norm is not None),
        grid=(B, S // ts),
        in_specs=in_specs,
        out_specs=out_specs,
        out_shape=out_shape,
        scratch_shapes=[pltpu.VMEM((1, LANES), _F32)],
        compiler_params=_params(("arbitrary", "arbitrary")),
        name="forget_cumsum",
    )(*args)


def _pool_kernel(u_ref, wp_ref, ps_ref, o_ref, *, cpg):
    S = u_ref.shape[1]
    gw = cpg * LANES
    row = lax.broadcasted_iota(jnp.int32, (S, gw), 0)
    for g, w in enumerate(POOL_WINDOWS):
        parts = [u_ref[g * cpg + j] for j in range(cpg)]
        u = (parts[0] if cpg == 1 else jnp.concatenate(parts, axis=-1)).astype(_F32)
        s = u
        sh = 1
        while sh < w:
            s = s + jnp.where(row >= sh, pltpu.roll(s, sh, axis=0), 0.0)
            sh *= 2
        cnt = jnp.minimum(row + 1, w).astype(_F32)
        pooled = s / cnt - u
        mixed = jnp.dot(pooled.astype(_BF16), wp_ref[g].astype(_BF16), preferred_element_type=_F32)
        o_ref[:, g * gw:(g + 1) * gw] = (mixed * ps_ref[:, g * gw:(g + 1) * gw]).astype(o_ref.dtype)


def _pool(proj5, w_pool, layer, ps):
    B, _, S, _ = proj5.shape
    _, G, gw, _ = w_pool.shape
    cpg = gw // LANES
    W = G * gw
    return pl.pallas_call(
        functools.partial(_pool_kernel, cpg=cpg),
        grid=(B,),
        in_specs=[
            pl.BlockSpec((None, G * cpg, S, LANES), lambda b: (b, 0, 0, 0)),
            pl.BlockSpec((None, G, gw, gw), lambda b: (layer, 0, 0, 0)),
            pl.BlockSpec((1, W), lambda b: (0, 0)),
        ],
        out_specs=pl.BlockSpec((None, S, W), lambda b: (b, 0, 0)),
        out_shape=jax.ShapeDtypeStruct((B, S, W), _BF16),
        compiler_params=_params(("arbitrary",)),
        name="pool_mixer",
    )(proj5, w_pool, ps)


def _attn_kernel(q_ref, k_ref, v_ref, f_ref, o_ref, qa_ref, ka_ref, va_ref, *, blk):
    S = q_ref.shape[0]
    h = pl.program_id(1)
    lane = lax.broadcasted_iota(jnp.int32, (S, LANES), 1)
    f = jnp.sum(jnp.where(lane == h, f_ref[...], 0.0), axis=-1, keepdims=True) * LOG2E
    f1 = f.astype(_BF16).astype(_F32)
    r1 = f - f1
    f2 = r1.astype(_BF16).astype(_F32)
    f3 = (r1 - f2).astype(_BF16).astype(_F32)
    qa_ref[:, :HEAD_DIM] = q_ref[...]
    qa_ref[:, HEAD_DIM:] = jnp.where(lane == 0, f1, jnp.where(lane == 1, f2, jnp.where(
        lane == 2, f3, jnp.where(lane < 6, 1.0, 0.0)))).astype(_BF16)
    ka_ref[:, :HEAD_DIM] = k_ref[...]
    ka_ref[:, HEAD_DIM:] = jnp.where(lane < 3, 1.0, jnp.where(lane == 3, -f1, jnp.where(
        lane == 4, -f2, jnp.where(lane == 5, -f3, 0.0)))).astype(_BF16)
    va_ref[:, :HEAD_DIM] = v_ref[...]
    va_ref[:, HEAD_DIM:] = jnp.where(lane == 0, 1.0, 0.0).astype(_BF16)

    nt = (((1,), (1,)), ((), ()))
    r_io = lax.broadcasted_iota(jnp.int32, (blk, blk), 0)
    c_io = lax.broadcasted_iota(jnp.int32, (blk, blk), 1)
    causal = c_io <= r_io
    nb = S // blk
    m = [jnp.full((blk, 1), -jnp.inf, _F32)] * nb
    acc = [jnp.zeros((blk, 2 * HEAD_DIM), _F32)] * nb
    for j in range(nb):
        ka = ka_ref[j * blk:(j + 1) * blk, :]
        va = va_ref[j * blk:(j + 1) * blk, :]
        for i in range(j, nb):
            s = lax.dot_general(qa_ref[i * blk:(i + 1) * blk, :], ka, nt, preferred_element_type=_F32)
            if i == j:
                s = jnp.where(causal, s, -jnp.inf)
            m_new = jnp.maximum(m[i], jnp.max(s, axis=-1, keepdims=True))
            p = jnp.exp2(s - m_new).astype(_BF16)
            acc[i] = jnp.exp2(m[i] - m_new) * acc[i] + jnp.dot(p, va, preferred_element_type=_F32)
            m[i] = m_new
    for i in range(nb):
        o_ref[i * blk:(i + 1) * blk, :] = (
            acc[i][:, :HEAD_DIM] / acc[i][:, HEAD_DIM:HEAD_DIM + 1]).astype(o_ref.dtype)


def _attn(proj5, fcum, nh):
    B, _, S, _ = proj5.shape
    blk = min(512, S)
    chunk = lambda o: pl.BlockSpec((None, None, S, HEAD_DIM), lambda b, h: (b, o + h, 0, 0))
    return pl.pallas_call(
        functools.partial(_attn_kernel, blk=blk),
        grid=(B, nh),
        in_specs=[
            chunk(nh), chunk(2 * nh), chunk(3 * nh),
            pl.BlockSpec((None, S, LANES), lambda b, h: (b, 0, 0)),
        ],
        out_specs=pl.BlockSpec((None, None, S, HEAD_DIM), lambda b, h: (b, h, 0, 0)),
        out_shape=jax.ShapeDtypeStruct((B, nh, S, HEAD_DIM), _BF16),
        scratch_shapes=[pltpu.VMEM((S, 2 * HEAD_DIM), _BF16)] * 3,
        compiler_params=_params(("arbitrary", "arbitrary")),
        name="forget_attn",
    )(proj5, proj5, proj5, fcum)


def _merge_kernel(h_ref, p_ref, a_ref, wg0_ref, wg1_ref, bg0_ref, bg1_ref, wb0_ref, wb1_ref, o_ref,
                  cg0_ref, cg1_ref, cb0_ref, cb1_ref):
    @pl.when(_first_inner_step())
    def _():
        cg0_ref[...] = wg0_ref[...].astype(_BF16)
        cg1_ref[...] = wg1_ref[...].astype(_BF16)
        cb0_ref[...] = wb0_ref[...].astype(_BF16)
        cb1_ref[...] = wb1_ref[...].astype(_BF16)

    h = h_ref[...]
    nh = a_ref.shape[0]
    a = jnp.concatenate([a_ref[j] for j in range(nh)], axis=-1)
    g0 = _sigmoid(jnp.dot(h, cg0_ref[...], preferred_element_type=_F32) + bg0_ref[...])
    y0 = jnp.dot(p_ref[...], cb0_ref[...], preferred_element_type=_F32)
    acc = g0 * y0
    g1 = _sigmoid(jnp.dot(h, cg1_ref[...], preferred_element_type=_F32) + bg1_ref[...])
    y1 = jnp.dot(a, cb1_ref[...], preferred_element_type=_F32)
    o_ref[...] = (acc + g1 * y1).astype(o_ref.dtype)


def _merge(h, pool_out, attn, w_gate, b_gate, w_branch, layer):
    B, S, D = h.shape
    W = pool_out.shape[-1]
    nh = attn.shape[1]
    L = w_gate.shape[0]
    tn = min(512, D)
    tm = min(1024, S)
    nn = D // tn
    bg = b_gate.reshape(L, 1, 2 * D)
    once = pl.Buffered(1)
    return pl.pallas_call(
        _merge_kernel,
        grid=(nn, B, S // tm),
        in_specs=[
            pl.BlockSpec((None, tm, D), lambda n, b, m: (b, m, 0)),
            pl.BlockSpec((None, tm, W), lambda n, b, m: (b, m, 0)),
            pl.BlockSpec((None, nh, tm, HEAD_DIM), lambda n, b, m: (b, 0, m, 0)),
            pl.BlockSpec((None, D, tn), lambda n, b, m: (layer, 0, n), pipeline_mode=once),
            pl.BlockSpec((None, D, tn), lambda n, b, m: (layer, 0, nn + n), pipeline_mode=once),
            pl.BlockSpec((None, 1, tn), lambda n, b, m: (layer, 0, n)),
            pl.BlockSpec((None, 1, tn), lambda n, b, m: (layer, 0, nn + n)),
            pl.BlockSpec((None, None, W, tn), lambda n, b, m: (layer, 0, 0, n), pipeline_mode=once),
            pl.BlockSpec((None, None, W, tn), lambda n, b, m: (layer, 1, 0, n), pipeline_mode=once),
        ],
        out_specs=pl.BlockSpec((None, tm, tn), lambda n, b, m: (b, m, n)),
        out_shape=jax.ShapeDtypeStruct((B, S, D), _BF16),
        scratch_shapes=[pltpu.VMEM((D, tn), _BF16), pltpu.VMEM((D, tn), _BF16),
                        pltpu.VMEM((W, tn), _BF16), pltpu.VMEM((W, tn), _BF16)],
        compiler_params=_params(("arbitrary", "arbitrary", "arbitrary")),
        name="branch_merge",
    )(h, pool_out, attn, w_gate, w_gate, bg, bg, w_branch, w_branch)


def _route_tile(x, g_ref, sc_ref, sh_ref, wr_ref, br_ref, e_ref, pos_ref, gate_ref, cnt_ref):
    first_step = (pl.program_id(0) == 0) & (pl.program_id(1) == 0)

    @pl.when(first_step)
    def _():
        cnt_ref[...] = jnp.zeros_like(cnt_ref)

    h = _rms_mod(x, g_ref[...], sc_ref[...], sh_ref[...])
    tr, D = h.shape
    ns = tr // LANES

    nt = (((1,), (1,)), ((), ()))
    h_hi = h.astype(_BF16)
    h_lo = (h - h_hi.astype(_F32)).astype(_BF16)
    w = wr_ref[...]
    w_hi = w.astype(_BF16)
    w_lo = (w - w_hi.astype(_F32)).astype(_BF16)
    both = lax.dot_general(jnp.concatenate([w_hi, w_lo], axis=0), h_hi, nt, preferred_element_type=_F32)
    logits = (both[:N_EXPERTS] + both[N_EXPERTS:]
              + lax.dot_general(w_hi, h_lo, nt, preferred_element_type=_F32))
    rows = [jnp.concatenate([logits[e:e + 1, s * LANES:(s + 1) * LANES] for s in range(ns)], axis=0)
            for e in range(N_EXPERTS)]
    mx = functools.reduce(jnp.maximum, rows)
    ex = [jnp.exp(r - mx) for r in rows]
    den = functools.reduce(lambda a, b: a + b, ex)
    probs = [v / den for v in ex]
    sel = [probs[e] + br_ref[e:e + 1, :] for e in range(N_EXPERTS)]

    in_top = [None] * N_EXPERTS
    gscore = []
    for g in range(N_EXPERT_GROUPS):
        ids = range(g * EXPERTS_PER_GROUP, (g + 1) * EXPERTS_PER_GROUP)
        score = None
        for i in ids:
            rank = None
            for j in ids:
                if j == i:
                    continue
                beats = (sel[j] > sel[i]) | ((sel[j] == sel[i]) if j < i else False)
                beats = beats.astype(_F32)
                rank = beats if rank is None else rank + beats
            in_top[i] = rank < float(TOP_K)
            term = jnp.where(in_top[i], sel[i], 0.0)
            score = term if score is None else score + term
        gscore.append(score)
    chosen = []
    for g in range(N_EXPERT_GROUPS):
        lose = None
        for g2 in range(N_EXPERT_GROUPS):
            if g2 == g:
                continue
            b = (gscore[g2] > gscore[g]) | ((gscore[g2] == gscore[g]) if g2 < g else False)
            lose = b if lose is None else (lose | b)
        chosen.append(jnp.logical_not(lose))
    picked = [in_top[e] & chosen[e // EXPERTS_PER_GROUP] for e in range(N_EXPERTS)]
    pf = [p.astype(_F32) for p in picked]

    n = N_EXPERTS * ns
    onehot = jnp.concatenate(pf, axis=0).astype(_BF16)
    upper = (lax.broadcasted_iota(jnp.int32, (LANES, LANES), 0)
             <= lax.broadcasted_iota(jnp.int32, (LANES, LANES), 1)).astype(_BF16)
    incl = jnp.dot(onehot, upper, preferred_element_type=_F32)
    r_io = lax.broadcasted_iota(jnp.int32, (n, n), 0)
    c_io = lax.broadcasted_iota(jnp.int32, (n, n), 1)
    shift = ns.bit_length() - 1
    same_expert = lax.shift_right_logical(r_io, shift) == lax.shift_right_logical(c_io, shift)
    earlier = jnp.where(same_expert & (c_io < r_io), 1.0, 0.0).astype(_BF16)
    carry = jnp.dot(earlier, incl.astype(_BF16), preferred_element_type=_F32)[:, LANES - 1:LANES]
    base = cnt_ref[:, 0:1]
    posm = [incl[e * ns:(e + 1) * ns, :] - 1.0 + carry[e * ns:(e + 1) * ns, :] + base[e:e + 1, :]
            for e in range(N_EXPERTS)]
    last = [(incl[(e + 1) * ns - 1:(e + 1) * ns, LANES - 1:LANES] + carry[(e + 1) * ns - 1:(e + 1) * ns, :])
            for e in range(N_EXPERTS)]
    cnt_ref[...] = jnp.broadcast_to(base + jnp.concatenate(last, axis=0), cnt_ref.shape)

    gnum = [pf[e] * probs[e] for e in range(N_EXPERTS)]
    gden = functools.reduce(lambda a, b: a + b, gnum)
    zero = jnp.zeros_like(pf[0])
    seen = zero
    e_out = [zero, zero]
    p_out = [zero, zero]
    g_out = [zero, zero]
    for e in range(N_EXPERTS):
        for k in range(TOP_K):
            hit = pf[e] * (seen == float(k)).astype(_F32)
            e_out[k] = e_out[k] + hit * float(e)
            p_out[k] = p_out[k] + hit * posm[e]
            g_out[k] = g_out[k] + hit * gnum[e]
        seen = seen + pf[e]
    for k in range(TOP_K):
        e_ref[k] = e_out[k].astype(jnp.int32)
        pos_ref[k] = p_out[k].astype(jnp.int32)
        gate_ref[k] = g_out[k] / gden


def _out_route_kernel(x_ref, m_ref, w_ref, ga_ref, g_ref, sc_ref, sh_ref, wr_ref, br_ref,
                      o_ref, e_ref, pos_ref, gate_ref, cnt_ref, wb_ref):
    @pl.when((pl.program_id(0) == 0) & (pl.program_id(1) == 0))
    def _():
        wb_ref[...] = w_ref[...].astype(_BF16)

    xn = x_ref[...] + ga_ref[...] * jnp.dot(m_ref[...], wb_ref[...], preferred_element_type=_F32)
    o_ref[...] = xn
    _route_tile(xn, g_ref, sc_ref, sh_ref, wr_ref, br_ref, e_ref, pos_ref, gate_ref, cnt_ref)


def _out_proj_route(x, merged, w_out, layer, mod, g_moe, wr_t, br):
    B, S, D = x.shape
    tr = min(ROUTE_TILE, S)
    ns = tr // LANES
    assert ns & (ns - 1) == 0
    nb = S // tr
    E = N_EXPERTS
    tok = lambda dt: jax.ShapeDtypeStruct((B * nb, TOP_K, ns, LANES), dt)
    tok_spec = pl.BlockSpec((None, TOP_K, ns, LANES), lambda b, s: (b * nb + s, 0, 0, 0))
    rows = pl.BlockSpec((None, tr, D), lambda b, s: (b, s, 0))
    mod_row = lambda i: pl.BlockSpec((None, None, 1, D), lambda b, s: (i, b, 0, 0))
    xn, eidx, pos, gate, counts = pl.pallas_call(
        _out_route_kernel,
        grid=(B, nb),
        in_specs=[
            rows, rows,
            pl.BlockSpec((None, D, D), lambda b, s: (layer, 0, 0), pipeline_mode=pl.Buffered(1)),
            mod_row(2),
            pl.BlockSpec((1, D), lambda b, s: (0, 0)),
            mod_row(4), mod_row(3),
            pl.BlockSpec((E, D), lambda b, s: (0, 0)),
            pl.BlockSpec((E, 1), lambda b, s: (0, 0)),
        ],
        out_specs=[
            rows, tok_spec, tok_spec, tok_spec,
            pl.BlockSpec((E, LANES), lambda b, s: (0, 0)),
        ],
        out_shape=[
            jax.ShapeDtypeStruct((B, S, D), _F32),
            tok(jnp.int32), tok(jnp.int32), tok(_F32),
            jax.ShapeDtypeStruct((E, LANES), _F32),
        ],
        scratch_shapes=[pltpu.VMEM((D, D), _BF16)],
        compiler_params=_params(("arbitrary", "arbitrary")),
        name="out_proj_route",
    )(x, merged, w_out, mod, g_moe.reshape(1, D), mod, mod, wr_t, br)
    flat = lambda a: a.reshape(B * nb, TOP_K, tr)
    return xn, flat(eidx), flat(pos), flat(gate), counts


def _dispatch_kernel(pad_start_ref, pad_len_ref, nu_ref, dest_ref, x0_ref, xn_ref, g_ref,
                     sc0_ref, sh0_ref, scn_ref, shn_ref, buf_ref, h_ref, zblk, sem):
    i = pl.program_id(0)
    tr = xn_ref.shape[0]

    @pl.when(i == 0)
    def _():
        h_ref[0] = _rms_mod(x0_ref[...], g_ref[...], sc0_ref[...], sh0_ref[...])

    def row_wait():
        pltpu.make_async_copy(h_ref.at[0, pl.ds(0, 1), :], buf_ref.at[pl.ds(0, 1), :], sem).wait()

    def step(slot):
        for r in range(tr):
            for k in range(TOP_K):
                pltpu.make_async_copy(h_ref.at[slot, pl.ds(r, 1), :],
                                      buf_ref.at[pl.ds(dest_ref[k, r], 1), :], sem).start(priority=k)

        @pl.when(i + 1 < pl.num_programs(0))
        def _():
            h_ref[1 - slot] = _rms_mod(xn_ref[...], g_ref[...], scn_ref[...], shn_ref[...])

        for _ in range(tr * TOP_K):
            row_wait()

    for slot in range(2):
        pl.when(lax.rem(i, 2) == slot)(functools.partial(step, slot))

    @pl.when(pl.program_id(0) == 0)
    def _():
        bm = zblk.shape[0]
        nblk = buf_ref.shape[0] // bm
        zblk[...] = jnp.zeros_like(zblk)
        for e in range(N_EXPERTS):
            def zissue(r, carry, e=e):
                pltpu.make_async_copy(zblk.at[pl.ds(0, 1), :],
                                      buf_ref.at[pl.ds(pad_start_ref[e] + r, 1), :], sem).start()
                return carry
            lax.fori_loop(0, pad_len_ref[e], zissue, 0)
        for e in range(N_EXPERTS):
            def zdrain(r, carry):
                row_wait()
                return carry
            lax.fori_loop(0, pad_len_ref[e], zdrain, 0)

        def blk_copy(i):
            return pltpu.make_async_copy(zblk, buf_ref.at[pl.ds(pl.multiple_of(i * bm, bm), bm), :], sem)

        def bissue(i, carry):
            blk_copy(i).start()
            return carry

        def bdrain(i, carry):
            blk_copy(i).wait()
            return carry

        lax.fori_loop(nu_ref[0], nblk, bissue, 0)
        lax.fori_loop(nu_ref[0], nblk, bdrain, 0)


def _dispatch(x, g, mod, i_shift, i_scale, dest, pad_start, pad_len, n_used, rows, bm):
    B, S, D = x.shape
    nbt, _, tr = dest.shape
    nb = S // tr
    nxt = lambda i: jnp.minimum(i + 1, nbt - 1)
    grid_spec = pltpu.PrefetchScalarGridSpec(
        num_scalar_prefetch=3,
        grid=(nbt,),
        in_specs=[
            pl.BlockSpec((None, TOP_K, tr), lambda i, *_: (i, 0, 0), memory_space=pltpu.SMEM),
            pl.BlockSpec((None, tr, D), lambda i, *_: (0, 0, 0)),
            pl.BlockSpec((None, tr, D), lambda i, *_: (nxt(i) // nb, nxt(i) % nb, 0)),
            pl.BlockSpec((1, D), lambda i, *_: (0, 0)),
            pl.BlockSpec((None, None, 1, D), lambda i, *_: (i_scale, 0, 0, 0)),
            pl.BlockSpec((None, None, 1, D), lambda i, *_: (i_shift, 0, 0, 0)),
            pl.BlockSpec((None, None, 1, D), lambda i, *_: (i_scale, nxt(i) // nb, 0, 0)),
            pl.BlockSpec((None, None, 1, D), lambda i, *_: (i_shift, nxt(i) // nb, 0, 0)),
        ],
        out_specs=pl.BlockSpec(memory_space=pl.ANY),
        scratch_shapes=[pltpu.VMEM((2, tr, D), _F32), pltpu.VMEM((bm, D), _F32),
                        pltpu.SemaphoreType.DMA(())],
    )
    return pl.pallas_call(
        _dispatch_kernel,
        grid_spec=grid_spec,
        out_shape=jax.ShapeDtypeStruct((rows, D), _F32),
        compiler_params=_params(("arbitrary",)),
        name="moe_dispatch",
    )(pad_start, pad_len, n_used, dest, x, x, g.reshape(1, D), mod, mod, mod, mod)


def _expert_kernel(be_ref, nx_ref, nu_ref, x_ref, wg_hbm, wu_hbm, wd_hbm, o_ref,
                   wgb, wub, wdb, sg, su, sd, sem, *, layer):
    i = pl.program_id(0)
    e = be_ref[i]
    live = i < nu_ref[0]
    new_expert = (i == 0) | (e != be_ref[jnp.maximum(i - 1, 0)])

    def weight_copies(ex):
        return [pltpu.make_async_copy(w.at[layer, ex], s, sem.at[n])
                for n, (w, s) in enumerate(((wg_hbm, sg), (wu_hbm, su), (wd_hbm, sd)))]

    @pl.when(live & new_expert)
    def _():
        @pl.when(i == 0)
        def _():
            for cp in weight_copies(e):
                cp.start()

        for cp in weight_copies(e):
            cp.wait()
        wgb[...] = sg[...].astype(_BF16)
        wub[...] = su[...].astype(_BF16)
        wdb[...] = sd[...].astype(_BF16)

        @pl.when(nx_ref[i] >= 0)
        def _():
            for cp in weight_copies(nx_ref[i]):
                cp.start(priority=1)

    @pl.when(live)
    def _():
        xb = x_ref[...].astype(_BF16)
        a = jnp.dot(xb, wgb[...], preferred_element_type=_F32)
        u = jnp.dot(xb, wub[...], preferred_element_type=_F32)
        act = (a * _sigmoid(a) * u).astype(_BF16)
        o_ref[...] = jnp.dot(act, wdb[...], preferred_element_type=_F32).astype(o_ref.dtype)

    @pl.when(jnp.logical_not(live))
    def _():
        o_ref[...] = jnp.zeros_like(o_ref)


def _experts(buf, block_expert, next_expert, n_used, wg, wu, wd, layer, bm):
    rows, D = buf.shape
    De = wg.shape[-1]
    nblk = rows // bm
    grid_spec = pltpu.PrefetchScalarGridSpec(
        num_scalar_prefetch=3,
        grid=(nblk,),
        in_specs=[
            pl.BlockSpec((bm, D), lambda i, be, nx, nu: (jnp.minimum(i, jnp.maximum(nu[0] - 1, 0)), 0)),
            pl.BlockSpec(memory_space=pl.ANY),
            pl.BlockSpec(memory_space=pl.ANY),
            pl.BlockSpec(memory_space=pl.ANY),
        ],
        out_specs=pl.BlockSpec((bm, D), lambda i, be, nx, nu: (i, 0)),
        scratch_shapes=[
            pltpu.VMEM((D, De), _BF16), pltpu.VMEM((D, De), _BF16), pltpu.VMEM((De, D), _BF16),
            pltpu.VMEM((D, De), _F32), pltpu.VMEM((D, De), _F32), pltpu.VMEM((De, D), _F32),
            pltpu.SemaphoreType.DMA((3,)),
        ],
    )
    return pl.pallas_call(
        functools.partial(_expert_kernel, layer=layer),
        grid_spec=grid_spec,
        out_shape=jax.ShapeDtypeStruct((rows, D), _BF16),
        compiler_params=_params(("arbitrary",), vmem=60 * 1024 * 1024),
        name="moe_experts",
    )(block_expert, next_expert, n_used, buf, wg, wu, wd)


ROW_CHUNK = 16


def _combine_kernel(src_ref, nch_ref, dst_ref, tot_ref,
                    x_ref, y_hbm, scol_ref, gcol_ref, gm_ref, nf_ref, gn_ref, scn_ref, shn_ref, wfn_ref, bfn_ref,
                    *rest, final, nb, has_next):
    if has_next:
        o_ref, h_ref, f_ref, ystage, carry, sem = rest
    else:
        o_ref, ystage, carry, sem = rest
    i = pl.program_id(0) * nb + pl.program_id(1)
    n_tiles = pl.num_programs(0) * nb
    cur = lax.rem(i, 2)
    tc = x_ref.shape[0]
    R = ystage.shape[1]

    def chunk_copy(src, dst, buf):
        return pltpu.make_async_copy(y_hbm.at[pl.ds(src, ROW_CHUNK), :],
                                     ystage.at[buf, pl.ds(dst, ROW_CHUNK), :], sem.at[buf])

    def start_tile(tile, buf):
        for e in range(N_EXPERTS):
            idx = tile * N_EXPERTS + e

            def body(c, carry, idx=idx):
                chunk_copy(pl.multiple_of(src_ref[idx] + c * ROW_CHUNK, ROW_CHUNK),
                           pl.multiple_of(dst_ref[idx] + c * ROW_CHUNK, ROW_CHUNK), buf).start()
                return carry

            lax.fori_loop(0, nch_ref[idx], body, 0)

    @pl.when(i == 0)
    def _():
        ystage[...] = jnp.zeros_like(ystage)
        start_tile(0, 0)

    @pl.when(i + 1 < n_tiles)
    def _():
        start_tile(i + 1, 1 - cur)

    def wait_one(c, carry):
        chunk_copy(0, 0, cur).wait()
        return carry

    lax.fori_loop(0, tot_ref[i], wait_one, 0)

    scol = scol_ref[...]
    gcol = gcol_ref[...]
    c_io = lax.broadcasted_iota(jnp.int32, (tc, R), 1)
    g = jnp.where(c_io == scol[:, 0:1], gcol[:, 0:1],
                  jnp.where(c_io == scol[:, 1:2], gcol[:, 1:2], 0.0)).astype(_BF16)
    moe = jnp.dot(g, ystage[cur], preferred_element_type=_F32)
    xn = x_ref[...] + gm_ref[...] * moe
    if final:
        ms = jnp.mean(xn * xn, axis=-1, keepdims=True)
        xn = xn * lax.rsqrt(ms + EPS) * nf_ref[...]
    o_ref[...] = xn
    if has_next:
        hn = _rms_mod(xn, gn_ref[...], scn_ref[...], shn_ref[...]).astype(h_ref.dtype)
        h_ref[...] = hn
        _forget_tile(hn, wfn_ref, bfn_ref, f_ref, carry)


def _combine(x, y, src, nch, dst, tot, slot_col, gate_col, mod, i_gate, nf, final, w_in_t, nh, nxt):
    B, S, D = x.shape
    n_tiles, tc, _ = slot_col.shape
    nb = S // tc
    R = _stage_rows(tc)
    has_next = nxt is not None
    l_next, g_next, mod_next, bf_next = nxt if has_next else (0, nf, mod, jnp.zeros((1, LANES), _F32))
    first = (w_in_t.shape[1] - nh) // nh
    tile = lambda b, s, *_: (b * nb + s, 0, 0)
    row = lambda b, s, *_: (b, s, 0)
    grid_spec = pltpu.PrefetchScalarGridSpec(
        num_scalar_prefetch=4,
        grid=(B, nb),
        in_specs=[
            pl.BlockSpec((None, tc, D), row),
            pl.BlockSpec(memory_space=pl.ANY),
            pl.BlockSpec((None, tc, TOP_K), tile),
            pl.BlockSpec((None, tc, TOP_K), tile),
            pl.BlockSpec((None, None, 1, D), lambda b, s, *_: (i_gate, b, 0, 0)),
            pl.BlockSpec((1, D), lambda b, s, *_: (0, 0)),
            pl.BlockSpec((1, D), lambda b, s, *_: (0, 0)),
            pl.BlockSpec((None, None, 1, D), lambda b, s, *_: (1, b, 0, 0)),
            pl.BlockSpec((None, None, 1, D), lambda b, s, *_: (0, b, 0, 0)),
            pl.BlockSpec((None, nh, D), lambda b, s, *_: (l_next, first, 0)),
            pl.BlockSpec((1, LANES), lambda b, s, *_: (0, 0)),
        ],
        out_specs=([pl.BlockSpec((None, tc, D), row)] * 2 + [pl.BlockSpec((None, tc, LANES), row)]
                   if has_next else pl.BlockSpec((None, tc, D), row)),
        scratch_shapes=[pltpu.VMEM((2, R, D), _BF16), pltpu.VMEM((1, LANES), _F32),
                        pltpu.SemaphoreType.DMA((2,))],
    )
    out_shape = jax.ShapeDtypeStruct((B, S, D), _F32)
    if has_next:
        out_shape = [out_shape, jax.ShapeDtypeStruct((B, S, D), _BF16),
                     jax.ShapeDtypeStruct((B, S, LANES), _F32)]
    return pl.pallas_call(
        functools.partial(_combine_kernel, final=final, nb=nb, has_next=has_next),
        grid_spec=grid_spec,
        out_shape=out_shape,
        compiler_params=_params(("arbitrary", "arbitrary")),
        name="moe_combine",
    )(src, nch, dst, tot, x, y, slot_col, gate_col, mod, nf.reshape(1, D), g_next.reshape(1, D),
      mod_next, mod_next, w_in_t, bf_next)


def _stage_rows(tc):
    worst = TOP_K * tc + 2 * (ROW_CHUNK - 1) * N_EXPERTS
    return -(-worst // LANES) * LANES


def _combine_plan(eidx, dest, gate, pstarts, tc):
    T = eidx.shape[0] * eidx.shape[2]
    n_tiles = T // tc
    flat = lambda a: a.transpose(0, 2, 1).reshape(n_tiles, tc, TOP_K)
    e_f, d_f, g_f = flat(eidx), flat(dest), flat(gate)
    onehot = (e_f[..., None] == jnp.arange(N_EXPERTS, dtype=jnp.int32)).astype(jnp.int32)
    cnt = jnp.sum(onehot, axis=(1, 2))
    first = pstarts[None, :] + jnp.cumsum(cnt, axis=0) - cnt
    src = first // ROW_CHUNK * ROW_CHUNK
    nch = jnp.where(cnt > 0, (first + cnt - src + ROW_CHUNK - 1) // ROW_CHUNK, 0)
    dst = ROW_CHUNK * (jnp.cumsum(nch, axis=1) - nch)
    slot = d_f + jnp.sum(onehot * (dst - src)[:, None, None, :], axis=-1)
    return (src.reshape(-1), nch.reshape(-1), dst.reshape(-1), jnp.sum(nch, axis=1),
            slot, g_f)


def kernel(x, c, w_ada, b_ada, norm_mix, norm_moe, w_in, b_forget, w_pool, pool_scale, w_branch,
           w_gate, b_gate, w_out, w_router, b_router, w_exp_gate, w_exp_up, w_exp_down, norm_final):
    B, S, D = x.shape
    L = w_ada.shape[0]
    W = D // 2
    nh = W // HEAD_DIM
    T = B * S
    A = T * TOP_K
    bm = min(256, A // N_EXPERTS)
    rows = (A // bm + N_EXPERTS) * bm

    mod_all = _ada(c, w_ada, b_ada)
    mod_all = mod_all.reshape(L, B, N_MOD, D).transpose(0, 2, 1, 3)[:, :, :, None, :]
    wr_t = w_router.T
    br = b_router.reshape(N_EXPERTS, 1)
    colscale = jnp.concatenate([jnp.ones((W,), _F32), jnp.full((W,), HEAD_DIM ** -0.5 * LOG2E, _F32),
                                jnp.ones((2 * W,), _F32)]).reshape(1, 4 * W)

    w_in_t = jnp.swapaxes(w_in, 1, 2)
    forget_bias = lambda l: jnp.zeros((1, LANES), _F32).at[0, :nh].set(b_forget[l])
    h = None
    for l in range(L):
        mod = mod_all[l]
        if h is None:
            fcum, h = _forget(x, w_in_t, l, nh, forget_bias(l), norm=(norm_mix[l], mod, 0, 1))
        proj5 = _proj(h, w_in_t, l, colscale)
        pool_out = _pool(proj5, w_pool, l, pool_scale[l].reshape(1, W))
        attn = _attn(proj5, fcum, nh)
        merged = _merge(h, pool_out, attn, w_gate, b_gate, w_branch, l)
        x, eidx, pos, gate, counts = _out_proj_route(x, merged, w_out, l, mod, norm_moe[l], wr_t, br)
        counts = counts[:, 0].astype(jnp.int32)
        pcounts = (counts + bm - 1) // bm * bm
        pends = jnp.cumsum(pcounts)
        pstarts = pends - pcounts
        dest = pos
        for e in range(N_EXPERTS):
            dest = dest + jnp.where(eidx == e, pstarts[e], 0)
        n_used = (pends[-1] // bm).astype(jnp.int32)
        blk_ids = jnp.arange(rows // bm, dtype=jnp.int32)
        blk_ids = jnp.minimum(blk_ids, n_used - 1)
        block_expert = jnp.sum((pends[None, :] <= (blk_ids * bm)[:, None]).astype(jnp.int32), axis=1)
        block_expert = jnp.minimum(block_expert, N_EXPERTS - 1)
        ids = jnp.arange(N_EXPERTS, dtype=jnp.int32)
        later = (ids[None, :] > ids[:, None]) & (pcounts[None, :] > 0)
        next_of = jnp.min(jnp.where(later, ids[None, :], N_EXPERTS), axis=1)
        next_of = jnp.where(next_of == N_EXPERTS, -1, next_of)
        next_expert = jnp.sum(jnp.where(block_expert[:, None] == ids[None, :], next_of[None, :], 0), axis=1)
        n_used = n_used.reshape(1)
        td = min(DISPATCH_TILE, S)
        dest_d = dest.transpose(0, 2, 1).reshape(T // td, td, TOP_K).transpose(0, 2, 1)
        buf = _dispatch(x, norm_moe[l], mod, 3, 4, dest_d, pstarts + counts, pcounts - counts,
                        n_used, rows, bm)
        y = _experts(buf, block_expert, next_expert.astype(jnp.int32), n_used,
                     w_exp_gate, w_exp_up, w_exp_down, l, bm)
        plan = _combine_plan(eidx, dest, gate, pstarts, min(COMBINE_TILE, S))
        if l + 1 < L:
            x, h, fcum = _combine(x, y, *plan, mod, 5, norm_final, False, w_in_t, nh,
                                  (l + 1, norm_mix[l + 1], mod_all[l + 1], forget_bias(l + 1)))
        else:
            x = _combine(x, y, *plan, mod, 5, norm_final, True, w_in_t, nh, None)
    return x
```

```python
import functools

import jax
import jax.numpy as jnp
from jax import lax
from jax.experimental import pallas as pl
from jax.experimental.pallas import tpu as pltpu

N_MOD = 6
EPS = 1e-6
POOL_WINDOWS = (2, 4, 8, 16)
HEAD_DIM = 128
N_EXPERTS = 16
N_EXPERT_GROUPS = 4
EXPERTS_PER_GROUP = N_EXPERTS // N_EXPERT_GROUPS
TOP_K = 2
LOG2E = 1.4426950408889634
ROUTE_TILE = 512
DISPATCH_TILE = 512
COMBINE_TILE = 256
LANES = 128
VMEM_LIMIT = 56 * 1024 * 1024

_F32 = jnp.float32
_BF16 = jnp.bfloat16


def _params(sem, vmem=VMEM_LIMIT):
    return pltpu.CompilerParams(dimension_semantics=sem, vmem_limit_bytes=vmem)


def _sigmoid(v):
    return 1.0 / (1.0 + jnp.exp(-v))


def _rms_mod(x, g, scale, shift):
    ms = jnp.mean(x * x, axis=-1, keepdims=True)
    return x * lax.rsqrt(ms + EPS) * g * (1.0 + scale) + shift


def _ada_kernel(c_ref, w_ref, b_ref, o_ref):
    c = c_ref[...]
    ca = (c * _sigmoid(c)).astype(_BF16)
    o_ref[...] = jnp.dot(ca, w_ref[...].astype(_BF16), preferred_element_type=_F32) + b_ref[...]


def _ada(c, w_ada, b_ada):
    L, D, M = w_ada.shape
    B = c.shape[0]
    tn = min(1024, M)
    return pl.pallas_call(
        _ada_kernel,
        grid=(L, M // tn),
        in_specs=[
            pl.BlockSpec((B, D), lambda l, n: (0, 0)),
            pl.BlockSpec((None, D, tn), lambda l, n: (l, 0, n)),
            pl.BlockSpec((None, 1, tn), lambda l, n: (l, 0, n)),
        ],
        out_specs=pl.BlockSpec((None, B, tn), lambda l, n: (l, 0, n)),
        out_shape=jax.ShapeDtypeStruct((L, B, M), _F32),
        compiler_params=_params(("arbitrary", "arbitrary")),
        name="ada_mod",
    )(c, w_ada, b_ada.reshape(L, 1, M))


def _first_inner_step():
    return (pl.program_id(1) == 0) & (pl.program_id(2) == 0)


def _proj_kernel(h_ref, w_ref, cs_ref, o_ref, wb_ref):
    @pl.when(_first_inner_step())
    def _():
        wb_ref[...] = (w_ref[...].T * cs_ref[...]).astype(_BF16)

    acc = jnp.dot(h_ref[...], wb_ref[...], preferred_element_type=_F32)
    for j in range(o_ref.shape[0]):
        o_ref[j] = acc[:, j * LANES:(j + 1) * LANES].astype(o_ref.dtype)


def _proj(h, w_in_t, layer, colscale):
    B, S, D = h.shape
    N = colscale.shape[1]
    tn = min(2048, N)
    tm = min(1024, S)
    return pl.pallas_call(
        _proj_kernel,
        grid=(N // tn, B, S // tm),
        in_specs=[
            pl.BlockSpec((None, tm, D), lambda n, b, m: (b, m, 0)),
            pl.BlockSpec((None, tn, D), lambda n, b, m: (layer, n, 0), pipeline_mode=pl.Buffered(1)),
            pl.BlockSpec((1, tn), lambda n, b, m: (0, n)),
        ],
        out_specs=pl.BlockSpec((None, tn // LANES, tm, LANES), lambda n, b, m: (b, n, m, 0)),
        out_shape=jax.ShapeDtypeStruct((B, N // LANES, S, LANES), _BF16),
        scratch_shapes=[pltpu.VMEM((D, tn), _BF16)],
        compiler_params=_params(("arbitrary", "arbitrary", "arbitrary")),
        name="in_proj",
    )(h, w_in_t, colscale)


def _forget_kernel(*refs, from_x):
    if from_x:
        x_ref, g_ref, sc_ref, sh_ref, w_ref, b_ref, o_ref, h_ref, carry = refs
        h = _rms_mod(x_ref[...], g_ref[...], sc_ref[...], sh_ref[...]).astype(_BF16)
        h_ref[...] = h
    else:
        hin_ref, w_ref, b_ref, o_ref, carry = refs
        h = hin_ref[...]
    ts, D = h.shape
    nh = w_ref.shape[0]

    @pl.when(pl.program_id(1) == 0)
    def _():
        carry[...] = jnp.zeros_like(carry)

    w = jnp.concatenate([w_ref[...], jnp.zeros((LANES - nh, D), _F32)], axis=0).astype(_BF16)
    fl = lax.dot_general(h, w, (((1,), (1,)), ((), ())), preferred_element_type=_F32) + b_ref[...]
    acc = jnp.minimum(fl, 0.0) - jnp.log(1.0 + jnp.exp(-jnp.abs(fl)))
    row = lax.broadcasted_iota(jnp.int32, acc.shape, 0)
    sh = 1
    while sh < ts:
        acc = acc + jnp.where(row >= sh, pltpu.roll(acc, sh, axis=0), 0.0)
        sh *= 2
    acc = acc + carry[...]
    o_ref[...] = acc
    carry[...] = acc[ts - 1:ts, :]


def _forget(h_or_x, w_in_t, layer, nh, bf, norm=None):
    B, S, D = h_or_x.shape
    ts = min(1024, S)
    first = (w_in_t.shape[1] - nh) // nh
    rows = pl.BlockSpec((None, ts, D), lambda b, s: (b, s, 0))
    tail_specs = [pl.BlockSpec((None, nh, D), lambda b, s: (layer, first, 0)),
                  pl.BlockSpec((1, LANES), lambda b, s: (0, 0))]
    f_spec = pl.BlockSpec((None, ts, LANES), lambda b, s: (b, s, 0))
    f_shape = jax.ShapeDtypeStruct((B, S, LANES), _F32)
    if norm is None:
        in_specs, args = [rows] + tail_specs, (h_or_x, w_in_t, bf)
        out_specs, out_shape = f_spec, f_shape
    else:
        g, mod, i_shift, i_scale = norm
        in_specs = [rows, pl.BlockSpec((1, D), lambda b, s: (0, 0)),
                    pl.BlockSpec((None, None, 1, D), lambda b, s: (i_scale, b, 0, 0)),
                    pl.BlockSpec((None, None, 1, D), lambda b, s: (i_shift, b, 0, 0))] + tail_specs
        args = (h_or_x, g.reshape(1, D), mod, mod, w_in_t, bf)
        out_specs, out_shape = [f_spec, rows], [f_shape, jax.ShapeDtypeStruct((B, S, D), _BF16)]
    return pl.pallas_call(
        functools.partial(_forget_kernel, from_x=norm is not None),
        grid=(B, S // ts),
        in_specs=in_specs,
        out_specs=out_specs,
        out_shape=out_shape,
        scratch_shapes=[pltpu.VMEM((1, LANES), _F32)],
        compiler_params=_params(("arbitrary", "arbitrary")),
        name="forget_cumsum",
    )(*args)


def _pool_kernel(u_ref, wp_ref, ps_ref, o_ref, *, cpg):
    S = u_ref.shape[1]
    gw = cpg * LANES
    row = lax.broadcasted_iota(jnp.int32, (S, gw), 0)
    for g, w in enumerate(POOL_WINDOWS):
        parts = [u_ref[g * cpg + j] for j in range(cpg)]
        u = (parts[0] if cpg == 1 else jnp.concatenate(parts, axis=-1)).astype(_F32)
        s = u
        sh = 1
        while sh < w:
            s = s + jnp.where(row >= sh, pltpu.roll(s, sh, axis=0), 0.0)
            sh *= 2
        cnt = jnp.minimum(row + 1, w).astype(_F32)
        pooled = s / cnt - u
        mixed = jnp.dot(pooled.astype(_BF16), wp_ref[g].astype(_BF16), preferred_element_type=_F32)
        o_ref[:, g * gw:(g + 1) * gw] = (mixed * ps_ref[:, g * gw:(g + 1) * gw]).astype(o_ref.dtype)


def _pool(proj5, w_pool, layer, ps):
    B, _, S, _ = proj5.shape
    _, G, gw, _ = w_pool.shape
    cpg = gw // LANES
    W = G * gw
    return pl.pallas_call(
        functools.partial(_pool_kernel, cpg=cpg),
        grid=(B,),
        in_specs=[
            pl.BlockSpec((None, G * cpg, S, LANES), lambda b: (b, 0, 0, 0)),
            pl.BlockSpec((None, G, gw, gw), lambda b: (layer, 0, 0, 0)),
            pl.BlockSpec((1, W), lambda b: (0, 0)),
        ],
        out_specs=pl.BlockSpec((None, S, W), lambda b: (b, 0, 0)),
        out_shape=jax.ShapeDtypeStruct((B, S, W), _BF16),
        compiler_params=_params(("arbitrary",)),
        name="pool_mixer",
    )(proj5, w_pool, ps)


def _attn_kernel(q_ref, k_ref, v_ref, f_ref, o_ref, qa_ref, ka_ref, va_ref, *, blk):
    S = q_ref.shape[0]
    h = pl.program_id(1)
    lane = lax.broadcasted_iota(jnp.int32, (S, LANES), 1)
    f = jnp.sum(jnp.where(lane == h, f_ref[...], 0.0), axis=-1, keepdims=True) * LOG2E
    f1 = f.astype(_BF16).astype(_F32)
    r1 = f - f1
    f2 = r1.astype(_BF16).astype(_F32)
    f3 = (r1 - f2).astype(_BF16).astype(_F32)
    qa_ref[:, :HEAD_DIM] = q_ref[...]
    qa_ref[:, HEAD_DIM:] = jnp.where(lane == 0, f1, jnp.where(lane == 1, f2, jnp.where(
        lane == 2, f3, jnp.where(lane < 6, 1.0, 0.0)))).astype(_BF16)
    ka_ref[:, :HEAD_DIM] = k_ref[...]
    ka_ref[:, HEAD_DIM:] = jnp.where(lane < 3, 1.0, jnp.where(lane == 3, -f1, jnp.where(
        lane == 4, -f2, jnp.where(lane == 5, -f3, 0.0)))).astype(_BF16)
    va_ref[:, :HEAD_DIM] = v_ref[...]
    va_ref[:, HEAD_DIM:] = jnp.where(lane == 0, 1.0, 0.0).astype(_BF16)

    nt = (((1,), (1,)), ((), ()))
    r_io = lax.broadcasted_iota(jnp.int32, (blk, blk), 0)
    c_io = lax.broadcasted_iota(jnp.int32, (blk, blk), 1)
    causal = c_io <= r_io
    nb = S // blk
    m = [jnp.full((blk, 1), -jnp.inf, _F32)] * nb
    acc = [jnp.zeros((blk, 2 * HEAD_DIM), _F32)] * nb
    for j in range(nb):
        ka = ka_ref[j * blk:(j + 1) * blk, :]
        va = va_ref[j * blk:(j + 1) * blk, :]
        for i in range(j, nb):
            s = lax.dot_general(qa_ref[i * blk:(i + 1) * blk, :], ka, nt, preferred_element_type=_F32)
            if i == j:
                s = jnp.where(causal, s, -jnp.inf)
            m_new = jnp.maximum(m[i], jnp.max(s, axis=-1, keepdims=True))
            p = jnp.exp2(s - m_new).astype(_BF16)
            acc[i] = jnp.exp2(m[i] - m_new) * acc[i] + jnp.dot(p, va, preferred_element_type=_F32)
            m[i] = m_new
    for i in range(nb):
        o_ref[i * blk:(i + 1) * blk, :] = (
            acc[i][:, :HEAD_DIM] / acc[i][:, HEAD_DIM:HEAD_DIM + 1]).astype(o_ref.dtype)


def _attn(proj5, fcum, nh):
    B, _, S, _ = proj5.shape
    blk = min(512, S)
    chunk = lambda o: pl.BlockSpec((None, None, S, HEAD_DIM), lambda b, h: (b, o + h, 0, 0))
    return pl.pallas_call(
        functools.partial(_attn_kernel, blk=blk),
        grid=(B, nh),
        in_specs=[
            chunk(nh), chunk(2 * nh), chunk(3 * nh),
            pl.BlockSpec((None, S, LANES), lambda b, h: (b, 0, 0)),
        ],
        out_specs=pl.BlockSpec((None, None, S, HEAD_DIM), lambda b, h: (b, h, 0, 0)),
        out_shape=jax.ShapeDtypeStruct((B, nh, S, HEAD_DIM), _BF16),
        scratch_shapes=[pltpu.VMEM((S, 2 * HEAD_DIM), _BF16)] * 3,
        compiler_params=_params(("arbitrary", "arbitrary")),
        name="forget_attn",
    )(proj5, proj5, proj5, fcum)


def _merge_kernel(h_ref, p_ref, a_ref, wg0_ref, wg1_ref, bg0_ref, bg1_ref, wb0_ref, wb1_ref, o_ref,
                  cg0_ref, cg1_ref, cb0_ref, cb1_ref):
    @pl.when(_first_inner_step())
    def _():
        cg0_ref[...] = wg0_ref[...].astype(_BF16)
        cg1_ref[...] = wg1_ref[...].astype(_BF16)
        cb0_ref[...] = wb0_ref[...].astype(_BF16)
        cb1_ref[...] = wb1_ref[...].astype(_BF16)

    h = h_ref[...]
    nh = a_ref.shape[0]
    a = jnp.concatenate([a_ref[j] for j in range(nh)], axis=-1)
    g0 = _sigmoid(jnp.dot(h, cg0_ref[...], preferred_element_type=_F32) + bg0_ref[...])
    y0 = jnp.dot(p_ref[...], cb0_ref[...], preferred_element_type=_F32)
    acc = g0 * y0
    g1 = _sigmoid(jnp.dot(h, cg1_ref[...], preferred_element_type=_F32) + bg1_ref[...])
    y1 = jnp.dot(a, cb1_ref[...], preferred_element_type=_F32)
    o_ref[...] = (acc + g1 * y1).astype(o_ref.dtype)


def _merge(h, pool_out, attn, w_gate, b_gate, w_branch, layer):
    B, S, D = h.shape
    W = pool_out.shape[-1]
    nh = attn.shape[1]
    L = w_gate.shape[0]
    tn = min(512, D)
    tm = min(1024, S)
    nn = D // tn
    bg = b_gate.reshape(L, 1, 2 * D)
    once = pl.Buffered(1)
    return pl.pallas_call(
        _merge_kernel,
        grid=(nn, B, S // tm),
        in_specs=[
            pl.BlockSpec((None, tm, D), lambda n, b, m: (b, m, 0)),
            pl.BlockSpec((None, tm, W), lambda n, b, m: (b, m, 0)),
            pl.BlockSpec((None, nh, tm, HEAD_DIM), lambda n, b, m: (b, 0, m, 0)),
            pl.BlockSpec((None, D, tn), lambda n, b, m: (layer, 0, n), pipeline_mode=once),
            pl.BlockSpec((None, D, tn), lambda n, b, m: (layer, 0, nn + n), pipeline_mode=once),
            pl.BlockSpec((None, 1, tn), lambda n, b, m: (layer, 0, n)),
            pl.BlockSpec((None, 1, tn), lambda n, b, m: (layer, 0, nn + n)),
            pl.BlockSpec((None, None, W, tn), lambda n, b, m: (layer, 0, 0, n), pipeline_mode=once),
            pl.BlockSpec((None, None, W, tn), lambda n, b, m: (layer, 1, 0, n), pipeline_mode=once),
        ],
        out_specs=pl.BlockSpec((None, tm, tn), lambda n, b, m: (b, m, n)),
        out_shape=jax.ShapeDtypeStruct((B, S, D), _BF16),
        scratch_shapes=[pltpu.VMEM((D, tn), _BF16), pltpu.VMEM((D, tn), _BF16),
                        pltpu.VMEM((W, tn), _BF16), pltpu.VMEM((W, tn), _BF16)],
        compiler_params=_params(("arbitrary", "arbitrary", "arbitrary")),
        name="branch_merge",
    )(h, pool_out, attn, w_gate, w_gate, bg, bg, w_branch, w_branch)


def _route_tile(x, g_ref, sc_ref, sh_ref, wr_ref, br_ref, e_ref, pos_ref, gate_ref, cnt_ref):
    first_step = (pl.program_id(0) == 0) & (pl.program_id(1) == 0)

    @pl.when(first_step)
    def _():
        cnt_ref[...] = jnp.zeros_like(cnt_ref)

    h = _rms_mod(x, g_ref[...], sc_ref[...], sh_ref[...])
    tr, D = h.shape
    ns = tr // LANES

    nt = (((1,), (1,)), ((), ()))
    h_hi = h.astype(_BF16)
    h_lo = (h - h_hi.astype(_F32)).astype(_BF16)
    w = wr_ref[...]
    w_hi = w.astype(_BF16)
    w_lo = (w - w_hi.astype(_F32)).astype(_BF16)
    both = lax.dot_general(jnp.concatenate([w_hi, w_lo], axis=0), h_hi, nt, preferred_element_type=_F32)
    logits = (both[:N_EXPERTS] + both[N_EXPERTS:]
              + lax.dot_general(w_hi, h_lo, nt, preferred_element_type=_F32))
    rows = [jnp.concatenate([logits[e:e + 1, s * LANES:(s + 1) * LANES] for s in range(ns)], axis=0)
            for e in range(N_EXPERTS)]
    mx = functools.reduce(jnp.maximum, rows)
    ex = [jnp.exp(r - mx) for r in rows]
    den = functools.reduce(lambda a, b: a + b, ex)
    probs = [v / den for v in ex]
    sel = [probs[e] + br_ref[e:e + 1, :] for e in range(N_EXPERTS)]

    in_top = [None] * N_EXPERTS
    gscore = []
    for g in range(N_EXPERT_GROUPS):
        ids = range(g * EXPERTS_PER_GROUP, (g + 1) * EXPERTS_PER_GROUP)
        score = None
        for i in ids:
            rank = None
            for j in ids:
                if j == i:
                    continue
                beats = (sel[j] > sel[i]) | ((sel[j] == sel[i]) if j < i else False)
                beats = beats.astype(_F32)
                rank = beats if rank is None else rank + beats
            in_top[i] = rank < float(TOP_K)
            term = jnp.where(in_top[i], sel[i], 0.0)
            score = term if score is None else score + term
        gscore.append(score)
    chosen = []
    for g in range(N_EXPERT_GROUPS):
        lose = None
        for g2 in range(N_EXPERT_GROUPS):
            if g2 == g:
                continue
            b = (gscore[g2] > gscore[g]) | ((gscore[g2] == gscore[g]) if g2 < g else False)
            lose = b if lose is None else (lose | b)
        chosen.append(jnp.logical_not(lose))
    picked = [in_top[e] & chosen[e // EXPERTS_PER_GROUP] for e in range(N_EXPERTS)]
    pf = [p.astype(_F32) for p in picked]

    n = N_EXPERTS * ns
    onehot = jnp.concatenate(pf, axis=0).astype(_BF16)
    upper = (lax.broadcasted_iota(jnp.int32, (LANES, LANES), 0)
             <= lax.broadcasted_iota(jnp.int32, (LANES, LANES), 1)).astype(_BF16)
    incl = jnp.dot(onehot, upper, preferred_element_type=_F32)
    r_io = lax.broadcasted_iota(jnp.int32, (n, n), 0)
    c_io = lax.broadcasted_iota(jnp.int32, (n, n), 1)
    shift = ns.bit_length() - 1
    same_expert = lax.shift_right_logical(r_io, shift) == lax.shift_right_logical(c_io, shift)
    earlier = jnp.where(same_expert & (c_io < r_io), 1.0, 0.0).astype(_BF16)
    carry = jnp.dot(earlier, incl.astype(_BF16), preferred_element_type=_F32)[:, LANES - 1:LANES]
    base = cnt_ref[:, 0:1]
    posm = [incl[e * ns:(e + 1) * ns, :] - 1.0 + carry[e * ns:(e + 1) * ns, :] + base[e:e + 1, :]
            for e in range(N_EXPERTS)]
    last = [(incl[(e + 1) * ns - 1:(e + 1) * ns, LANES - 1:LANES] + carry[(e + 1) * ns - 1:(e + 1) * ns, :])
            for e in range(N_EXPERTS)]
    cnt_ref[...] = jnp.broadcast_to(base + jnp.concatenate(last, axis=0), cnt_ref.shape)

    gnum = [pf[e] * probs[e] for e in range(N_EXPERTS)]
    gden = functools.reduce(lambda a, b: a + b, gnum)
    zero = jnp.zeros_like(pf[0])
    seen = zero
    e_out = [zero, zero]
    p_out = [zero, zero]
    g_out = [zero, zero]
    for e in range(N_EXPERTS):
        for k in range(TOP_K):
            hit = pf[e] * (seen == float(k)).astype(_F32)
            e_out[k] = e_out[k] + hit * float(e)
            p_out[k] = p_out[k] + hit * posm[e]
            g_out[k] = g_out[k] + hit * gnum[e]
        seen = seen + pf[e]
    for k in range(TOP_K):
        e_ref[k] = e_out[k].astype(jnp.int32)
        pos_ref[k] = p_out[k].astype(jnp.int32)
        gate_ref[k] = g_out[k] / gden


def _out_route_kernel(x_ref, m_ref, w_ref, ga_ref, g_ref, sc_ref, sh_ref, wr_ref, br_ref,
                      o_ref, e_ref, pos_ref, gate_ref, cnt_ref, wb_ref):
    @pl.when((pl.program_id(0) == 0) & (pl.program_id(1) == 0))
    def _():
        wb_ref[...] = w_ref[...].astype(_BF16)

    xn = x_ref[...] + ga_ref[...] * jnp.dot(m_ref[...], wb_ref[...], preferred_element_type=_F32)
    o_ref[...] = xn
    _route_tile(xn, g_ref, sc_ref, sh_ref, wr_ref, br_ref, e_ref, pos_ref, gate_ref, cnt_ref)


def _out_proj_route(x, merged, w_out, layer, mod, g_moe, wr_t, br):
    B, S, D = x.shape
    tr = min(ROUTE_TILE, S)
    ns = tr // LANES
    assert ns & (ns - 1) == 0
    nb = S // tr
    E = N_EXPERTS
    tok = lambda dt: jax.ShapeDtypeStruct((B * nb, TOP_K, ns, LANES), dt)
    tok_spec = pl.BlockSpec((None, TOP_K, ns, LANES), lambda b, s: (b * nb + s, 0, 0, 0))
    rows = pl.BlockSpec((None, tr, D), lambda b, s: (b, s, 0))
    mod_row = lambda i: pl.BlockSpec((None, None, 1, D), lambda b, s: (i, b, 0, 0))
    xn, eidx, pos, gate, counts = pl.pallas_call(
        _out_route_kernel,
        grid=(B, nb),
        in_specs=[
            rows, rows,
            pl.BlockSpec((None, D, D), lambda b, s: (layer, 0, 0), pipeline_mode=pl.Buffered(1)),
            mod_row(2),
            pl.BlockSpec((1, D), lambda b, s: (0, 0)),
            mod_row(4), mod_row(3),
            pl.BlockSpec((E, D), lambda b, s: (0, 0)),
            pl.BlockSpec((E, 1), lambda b, s: (0, 0)),
        ],
        out_specs=[
            rows, tok_spec, tok_spec, tok_spec,
            pl.BlockSpec((E, LANES), lambda b, s: (0, 0)),
        ],
        out_shape=[
            jax.ShapeDtypeStruct((B, S, D), _F32),
            tok(jnp.int32), tok(jnp.int32), tok(_F32),
            jax.ShapeDtypeStruct((E, LANES), _F32),
        ],
        scratch_shapes=[pltpu.VMEM((D, D), _BF16)],
        compiler_params=_params(("arbitrary", "arbitrary")),
        name="out_proj_route",
    )(x, merged, w_out, mod, g_moe.reshape(1, D), mod, mod, wr_t, br)
    flat = lambda a: a.reshape(B * nb, TOP_K, tr)
    return xn, flat(eidx), flat(pos), flat(gate), counts


def _dispatch_kernel(pad_start_ref, pad_len_ref, nu_ref, dest_ref, x0_ref, xn_ref, g_ref,
                     sc0_ref, sh0_ref, scn_ref, shn_ref, buf_ref, h_ref, zblk, sem):
    i = pl.program_id(0)
    tr = xn_ref.shape[0]

    @pl.when(i == 0)
    def _():
        h_ref[0] = _rms_mod(x0_ref[...], g_ref[...], sc0_ref[...], sh0_ref[...])

    def row_wait():
        pltpu.make_async_copy(h_ref.at[0, pl.ds(0, 1), :], buf_ref.at[pl.ds(0, 1), :], sem).wait()

    def step(slot):
        for r in range(tr):
            for k in range(TOP_K):
                pltpu.make_async_copy(h_ref.at[slot, pl.ds(r, 1), :],
                                      buf_ref.at[pl.ds(dest_ref[k, r], 1), :], sem).start(priority=k)

        @pl.when(i + 1 < pl.num_programs(0))
        def _():
            h_ref[1 - slot] = _rms_mod(xn_ref[...], g_ref[...], scn_ref[...], shn_ref[...])

        for _ in range(tr * TOP_K):
            row_wait()

    for slot in range(2):
        pl.when(lax.rem(i, 2) == slot)(functools.partial(step, slot))

    @pl.when(pl.program_id(0) == 0)
    def _():
        bm = zblk.shape[0]
        nblk = buf_ref.shape[0] // bm
        zblk[...] = jnp.zeros_like(zblk)
        for e in range(N_EXPERTS):
            def zissue(r, carry, e=e):
                pltpu.make_async_copy(zblk.at[pl.ds(0, 1), :],
                                      buf_ref.at[pl.ds(pad_start_ref[e] + r, 1), :], sem).start()
                return carry
            lax.fori_loop(0, pad_len_ref[e], zissue, 0)
        for e in range(N_EXPERTS):
            def zdrain(r, carry):
                row_wait()
                return carry
            lax.fori_loop(0, pad_len_ref[e], zdrain, 0)

        def blk_copy(i):
            return pltpu.make_async_copy(zblk, buf_ref.at[pl.ds(pl.multiple_of(i * bm, bm), bm), :], sem)

        def bissue(i, carry):
            blk_copy(i).start()
            return carry

        def bdrain(i, carry):
            blk_copy(i).wait()
            return carry

        lax.fori_loop(nu_ref[0], nblk, bissue, 0)
        lax.fori_loop(nu_ref[0], nblk, bdrain, 0)


def _dispatch(x, g, mod, i_shift, i_scale, dest, pad_start, pad_len, n_used, rows, bm):
    B, S, D = x.shape
    nbt, _, tr = dest.shape
    nb = S // tr
    nxt = lambda i: jnp.minimum(i + 1, nbt - 1)
    grid_spec = pltpu.PrefetchScalarGridSpec(
        num_scalar_prefetch=3,
        grid=(nbt,),
        in_specs=[
            pl.BlockSpec((None, TOP_K, tr), lambda i, *_: (i, 0, 0), memory_space=pltpu.SMEM),
            pl.BlockSpec((None, tr, D), lambda i, *_: (0, 0, 0)),
            pl.BlockSpec((None, tr, D), lambda i, *_: (nxt(i) // nb, nxt(i) % nb, 0)),
            pl.BlockSpec((1, D), lambda i, *_: (0, 0)),
            pl.BlockSpec((None, None, 1, D), lambda i, *_: (i_scale, 0, 0, 0)),
            pl.BlockSpec((None, None, 1, D), lambda i, *_: (i_shift, 0, 0, 0)),
            pl.BlockSpec((None, None, 1, D), lambda i, *_: (i_scale, nxt(i) // nb, 0, 0)),
            pl.BlockSpec((None, None, 1, D), lambda i, *_: (i_shift, nxt(i) // nb, 0, 0)),
        ],
        out_specs=pl.BlockSpec(memory_space=pl.ANY),
        scratch_shapes=[pltpu.VMEM((2, tr, D), _F32), pltpu.VMEM((bm, D), _F32),
                        pltpu.SemaphoreType.DMA(())],
    )
    return pl.pallas_call(
        _dispatch_kernel,
        grid_spec=grid_spec,
        out_shape=jax.ShapeDtypeStruct((rows, D), _F32),
        compiler_params=_params(("arbitrary",)),
        name="moe_dispatch",
    )(pad_start, pad_len, n_used, dest, x, x, g.reshape(1, D), mod, mod, mod, mod)


def _expert_kernel(be_ref, nx_ref, vl_ref, nu_ref, x_ref, wg_hbm, wu_hbm, wd_hbm, o_ref,
                   wgb, wub, wdb, sg, su, sd, sem, *, layer):
    i = pl.program_id(0)
    e = be_ref[i]
    live = i < nu_ref[0]
    new_expert = (i == 0) | (e != be_ref[jnp.maximum(i - 1, 0)])

    def weight_copies(ex):
        return [pltpu.make_async_copy(w.at[layer, ex], s, sem.at[n])
                for n, (w, s) in enumerate(((wg_hbm, sg), (wu_hbm, su), (wd_hbm, sd)))]

    @pl.when(live & new_expert)
    def _():
        @pl.when(i == 0)
        def _():
            for cp in weight_copies(e):
                cp.start()

        for cp in weight_copies(e):
            cp.wait()
        wgb[...] = sg[...].astype(_BF16)
        wub[...] = su[...].astype(_BF16)
        wdb[...] = sd[...].astype(_BF16)

        @pl.when(nx_ref[i] >= 0)
        def _():
            for cp in weight_copies(nx_ref[i]):
                cp.start(priority=1)

    def ffn(rows):
        xb = x_ref[rows, :].astype(_BF16)
        a = jnp.dot(xb, wgb[...], preferred_element_type=_F32)
        u = jnp.dot(xb, wub[...], preferred_element_type=_F32)
        act = (a * _sigmoid(a) * u).astype(_BF16)
        o_ref[rows, :] = jnp.dot(act, wdb[...], preferred_element_type=_F32).astype(o_ref.dtype)

    half = x_ref.shape[0] // 2
    needs_all = vl_ref[i] > half

    @pl.when(live & needs_all)
    def _():
        ffn(slice(None))

    @pl.when(live & jnp.logical_not(needs_all))
    def _():
        ffn(slice(0, half))
        o_ref[half:, :] = jnp.zeros((half, o_ref.shape[1]), o_ref.dtype)

    @pl.when(jnp.logical_not(live))
    def _():
        o_ref[...] = jnp.zeros_like(o_ref)


def _experts(buf, block_expert, next_expert, valid_rows, n_used, wg, wu, wd, layer, bm):
    rows, D = buf.shape
    De = wg.shape[-1]
    nblk = rows // bm
    grid_spec = pltpu.PrefetchScalarGridSpec(
        num_scalar_prefetch=4,
        grid=(nblk,),
        in_specs=[
            pl.BlockSpec((bm, D), lambda i, be, nx, vl, nu: (jnp.minimum(i, jnp.maximum(nu[0] - 1, 0)), 0)),
            pl.BlockSpec(memory_space=pl.ANY),
            pl.BlockSpec(memory_space=pl.ANY),
            pl.BlockSpec(memory_space=pl.ANY),
        ],
        out_specs=pl.BlockSpec((bm, D), lambda i, be, nx, vl, nu: (i, 0)),
        scratch_shapes=[
            pltpu.VMEM((D, De), _BF16), pltpu.VMEM((D, De), _BF16), pltpu.VMEM((De, D), _BF16),
            pltpu.VMEM((D, De), _F32), pltpu.VMEM((D, De), _F32), pltpu.VMEM((De, D), _F32),
            pltpu.SemaphoreType.DMA((3,)),
        ],
    )
    return pl.pallas_call(
        functools.partial(_expert_kernel, layer=layer),
        grid_spec=grid_spec,
        out_shape=jax.ShapeDtypeStruct((rows, D), _BF16),
        compiler_params=_params(("arbitrary",), vmem=60 * 1024 * 1024),
        name="moe_experts",
    )(block_expert, next_expert, valid_rows, n_used, buf, wg, wu, wd)


ROW_CHUNK = 16


def _combine_kernel(src_ref, nch_ref, dst_ref, tot_ref,
                    x_ref, y_hbm, scol_ref, gcol_ref, gm_ref, nf_ref, gn_ref, scn_ref, shn_ref,
                    *rest, final, nb, has_next):
    if has_next:
        o_ref, h_ref, ystage, sem = rest
    else:
        o_ref, ystage, sem = rest
        h_ref = None
    i = pl.program_id(0) * nb + pl.program_id(1)
    n_tiles = pl.num_programs(0) * nb
    cur = lax.rem(i, 2)
    tc = x_ref.shape[0]
    R = ystage.shape[1]

    def chunk_copy(src, dst, buf):
        return pltpu.make_async_copy(y_hbm.at[pl.ds(src, ROW_CHUNK), :],
                                     ystage.at[buf, pl.ds(dst, ROW_CHUNK), :], sem.at[buf])

    def start_tile(tile, buf):
        for e in range(N_EXPERTS):
            idx = tile * N_EXPERTS + e

            def body(c, carry, idx=idx):
                chunk_copy(pl.multiple_of(src_ref[idx] + c * ROW_CHUNK, ROW_CHUNK),
                           pl.multiple_of(dst_ref[idx] + c * ROW_CHUNK, ROW_CHUNK), buf).start()
                return carry

            lax.fori_loop(0, nch_ref[idx], body, 0)

    @pl.when(i == 0)
    def _():
        ystage[...] = jnp.zeros_like(ystage)
        start_tile(0, 0)

    @pl.when(i + 1 < n_tiles)
    def _():
        start_tile(i + 1, 1 - cur)

    def wait_one(c, carry):
        chunk_copy(0, 0, cur).wait()
        return carry

    lax.fori_loop(0, tot_ref[i], wait_one, 0)

    scol = scol_ref[...]
    gcol = gcol_ref[...]
    c_io = lax.broadcasted_iota(jnp.int32, (tc, R), 1)
    g = jnp.where(c_io == scol[:, 0:1], gcol[:, 0:1],
                  jnp.where(c_io == scol[:, 1:2], gcol[:, 1:2], 0.0)).astype(_BF16)
    moe = jnp.dot(g, ystage[cur], preferred_element_type=_F32)
    xn = x_ref[...] + gm_ref[...] * moe
    if final:
        ms = jnp.mean(xn * xn, axis=-1, keepdims=True)
        xn = xn * lax.rsqrt(ms + EPS) * nf_ref[...]
    o_ref[...] = xn
    if has_next:
        h_ref[...] = _rms_mod(xn, gn_ref[...], scn_ref[...], shn_ref[...]).astype(h_ref.dtype)


def _combine(x, y, src, nch, dst, tot, slot_col, gate_col, mod, i_gate, nf, final, g_next, mod_next):
    B, S, D = x.shape
    n_tiles, tc, _ = slot_col.shape
    nb = S // tc
    R = _stage_rows(tc)
    has_next = g_next is not None
    if not has_next:
        g_next, mod_next = nf, mod
    tile = lambda b, s, *_: (b * nb + s, 0, 0)
    row = lambda b, s, *_: (b, s, 0)
    grid_spec = pltpu.PrefetchScalarGridSpec(
        num_scalar_prefetch=4,
        grid=(B, nb),
        in_specs=[
            pl.BlockSpec((None, tc, D), row),
            pl.BlockSpec(memory_space=pl.ANY),
            pl.BlockSpec((None, tc, TOP_K), tile),
            pl.BlockSpec((None, tc, TOP_K), tile),
            pl.BlockSpec((None, None, 1, D), lambda b, s, *_: (i_gate, b, 0, 0)),
            pl.BlockSpec((1, D), lambda b, s, *_: (0, 0)),
            pl.BlockSpec((1, D), lambda b, s, *_: (0, 0)),
            pl.BlockSpec((None, None, 1, D), lambda b, s, *_: (1, b, 0, 0)),
            pl.BlockSpec((None, None, 1, D), lambda b, s, *_: (0, b, 0, 0)),
        ],
        out_specs=([pl.BlockSpec((None, tc, D), row)] * 2) if has_next else pl.BlockSpec((None, tc, D), row),
        scratch_shapes=[pltpu.VMEM((2, R, D), _BF16), pltpu.SemaphoreType.DMA((2,))],
    )
    out_shape = jax.ShapeDtypeStruct((B, S, D), _F32)
    if has_next:
        out_shape = [out_shape, jax.ShapeDtypeStruct((B, S, D), _BF16)]
    return pl.pallas_call(
        functools.partial(_combine_kernel, final=final, nb=nb, has_next=has_next),
        grid_spec=grid_spec,
        out_shape=out_shape,
        compiler_params=_params(("arbitrary", "arbitrary")),
        name="moe_combine",
    )(src, nch, dst, tot, x, y, slot_col, gate_col, mod, nf.reshape(1, D), g_next.reshape(1, D),
      mod_next, mod_next)


def _stage_rows(tc):
    worst = TOP_K * tc + 2 * (ROW_CHUNK - 1) * N_EXPERTS
    return -(-worst // LANES) * LANES


def _combine_plan(eidx, dest, gate, pstarts, tc):
    T = eidx.shape[0] * eidx.shape[2]
    n_tiles = T // tc
    flat = lambda a: a.transpose(0, 2, 1).reshape(n_tiles, tc, TOP_K)
    e_f, d_f, g_f = flat(eidx), flat(dest), flat(gate)
    onehot = (e_f[..., None] == jnp.arange(N_EXPERTS, dtype=jnp.int32)).astype(jnp.int32)
    cnt = jnp.sum(onehot, axis=(1, 2))
    first = pstarts[None, :] + jnp.cumsum(cnt, axis=0) - cnt
    src = first // ROW_CHUNK * ROW_CHUNK
    nch = jnp.where(cnt > 0, (first + cnt - src + ROW_CHUNK - 1) // ROW_CHUNK, 0)
    dst = ROW_CHUNK * (jnp.cumsum(nch, axis=1) - nch)
    slot = d_f + jnp.sum(onehot * (dst - src)[:, None, None, :], axis=-1)
    return (src.reshape(-1), nch.reshape(-1), dst.reshape(-1), jnp.sum(nch, axis=1),
            slot, g_f)


def kernel(x, c, w_ada, b_ada, norm_mix, norm_moe, w_in, b_forget, w_pool, pool_scale, w_branch,
           w_gate, b_gate, w_out, w_router, b_router, w_exp_gate, w_exp_up, w_exp_down, norm_final):
    B, S, D = x.shape
    L = w_ada.shape[0]
    W = D // 2
    nh = W // HEAD_DIM
    T = B * S
    A = T * TOP_K
    bm = min(256, A // N_EXPERTS)
    rows = (A // bm + N_EXPERTS) * bm

    mod_all = _ada(c, w_ada, b_ada)
    mod_all = mod_all.reshape(L, B, N_MOD, D).transpose(0, 2, 1, 3)[:, :, :, None, :]
    wr_t = w_router.T
    br = b_router.reshape(N_EXPERTS, 1)
    colscale = jnp.concatenate([jnp.ones((W,), _F32), jnp.full((W,), HEAD_DIM ** -0.5 * LOG2E, _F32),
                                jnp.ones((2 * W,), _F32)]).reshape(1, 4 * W)

    w_in_t = jnp.swapaxes(w_in, 1, 2)
    h = None
    for l in range(L):
        mod = mod_all[l]
        bf = jnp.zeros((1, LANES), _F32).at[0, :nh].set(b_forget[l])
        if h is None:
            fcum, h = _forget(x, w_in_t, l, nh, bf, norm=(norm_mix[l], mod, 0, 1))
        else:
            fcum = _forget(h, w_in_t, l, nh, bf)
        proj5 = _proj(h, w_in_t, l, colscale)
        pool_out = _pool(proj5, w_pool, l, pool_scale[l].reshape(1, W))
        attn = _attn(proj5, fcum, nh)
        merged = _merge(h, pool_out, attn, w_gate, b_gate, w_branch, l)
        x, eidx, pos, gate, counts = _out_proj_route(x, merged, w_out, l, mod, norm_moe[l], wr_t, br)
        counts = counts[:, 0].astype(jnp.int32)
        pcounts = (counts + bm - 1) // bm * bm
        pends = jnp.cumsum(pcounts)
        pstarts = pends - pcounts
        dest = pos
        for e in range(N_EXPERTS):
            dest = dest + jnp.where(eidx == e, pstarts[e], 0)
        n_used = (pends[-1] // bm).astype(jnp.int32)
        blk_ids = jnp.arange(rows // bm, dtype=jnp.int32)
        blk_ids = jnp.minimum(blk_ids, n_used - 1)
        block_expert = jnp.sum((pends[None, :] <= (blk_ids * bm)[:, None]).astype(jnp.int32), axis=1)
        block_expert = jnp.minimum(block_expert, N_EXPERTS - 1)
        ids = jnp.arange(N_EXPERTS, dtype=jnp.int32)
        later = (ids[None, :] > ids[:, None]) & (pcounts[None, :] > 0)
        next_of = jnp.min(jnp.where(later, ids[None, :], N_EXPERTS), axis=1)
        next_of = jnp.where(next_of == N_EXPERTS, -1, next_of)
        next_expert = jnp.sum(jnp.where(block_expert[:, None] == ids[None, :], next_of[None, :], 0), axis=1)
        n_used = n_used.reshape(1)
        td = min(DISPATCH_TILE, S)
        dest_d = dest.transpose(0, 2, 1).reshape(T // td, td, TOP_K).transpose(0, 2, 1)
        buf = _dispatch(x, norm_moe[l], mod, 3, 4, dest_d, pstarts + counts, pcounts - counts,
                        n_used, rows, bm)
        ends = jnp.sum(jnp.where(block_expert[:, None] == ids[None, :], (pstarts + counts)[None, :], 0), axis=1)
        valid_rows = jnp.clip(ends - jnp.arange(rows // bm, dtype=jnp.int32) * bm, 0, bm)
        y = _experts(buf, block_expert, next_expert.astype(jnp.int32), valid_rows.astype(jnp.int32), n_used,
                     w_exp_gate, w_exp_up, w_exp_down, l, bm)
        plan = _combine_plan(eidx, dest, gate, pstarts, min(COMBINE_TILE, S))
        if l + 1 < L:
            x, h = _combine(x, y, *plan, mod, 5, norm_final, False, norm_mix[l + 1], mod_all[l + 1])
        else:
            x = _combine(x, y, *plan, mod, 5, norm_final, True, None, None)
    return x
```

```python
import functools

import jax
import jax.numpy as jnp
from jax import lax
from jax.experimental import pallas as pl
from jax.experimental.pallas import tpu as pltpu

N_MOD = 6
EPS = 1e-6
POOL_WINDOWS = (2, 4, 8, 16)
HEAD_DIM = 128
N_EXPERTS = 16
N_EXPERT_GROUPS = 4
EXPERTS_PER_GROUP = N_EXPERTS // N_EXPERT_GROUPS
TOP_K = 2
LOG2E = 1.4426950408889634
ROUTE_TILE = 512
DISPATCH_TILE = 512
COMBINE_TILE = 256
LANES = 128
VMEM_LIMIT = 56 * 1024 * 1024

_F32 = jnp.float32
_BF16 = jnp.bfloat16


def _params(sem, vmem=VMEM_LIMIT):
    return pltpu.CompilerParams(dimension_semantics=sem, vmem_limit_bytes=vmem)


def _sigmoid(v):
    return 1.0 / (1.0 + jnp.exp(-v))


def _rms_mod(x, g, scale, shift):
    ms = jnp.mean(x * x, axis=-1, keepdims=True)
    return x * lax.rsqrt(ms + EPS) * g * (1.0 + scale) + shift


def _ada_kernel(c_ref, w_ref, b_ref, o_ref):
    c = c_ref[...]
    ca = (c * _sigmoid(c)).astype(_BF16)
    o_ref[...] = jnp.dot(ca, w_ref[...].astype(_BF16), preferred_element_type=_F32) + b_ref[...]


def _ada(c, w_ada, b_ada):
    L, D, M = w_ada.shape
    B = c.shape[0]
    tn = min(1024, M)
    return pl.pallas_call(
        _ada_kernel,
        grid=(L, M // tn),
        in_specs=[
            pl.BlockSpec((B, D), lambda l, n: (0, 0)),
            pl.BlockSpec((None, D, tn), lambda l, n: (l, 0, n)),
            pl.BlockSpec((None, 1, tn), lambda l, n: (l, 0, n)),
        ],
        out_specs=pl.BlockSpec((None, B, tn), lambda l, n: (l, 0, n)),
        out_shape=jax.ShapeDtypeStruct((L, B, M), _F32),
        compiler_params=_params(("arbitrary", "arbitrary")),
        name="ada_mod",
    )(c, w_ada, b_ada.reshape(L, 1, M))


def _first_inner_step():
    return (pl.program_id(1) == 0) & (pl.program_id(2) == 0)


def _proj_kernel(h_ref, w_ref, cs_ref, o_ref, wb_ref):
    @pl.when(_first_inner_step())
    def _():
        wb_ref[...] = (w_ref[...].T * cs_ref[...]).astype(_BF16)

    acc = jnp.dot(h_ref[...], wb_ref[...], preferred_element_type=_F32)
    for j in range(o_ref.shape[0]):
        o_ref[j] = acc[:, j * LANES:(j + 1) * LANES].astype(o_ref.dtype)


def _proj(h, w_in_t, layer, colscale):
    B, S, D = h.shape
    N = colscale.shape[1]
    tn = min(2048, N)
    tm = min(1024, S)
    return pl.pallas_call(
        _proj_kernel,
        grid=(N // tn, B, S // tm),
        in_specs=[
            pl.BlockSpec((None, tm, D), lambda n, b, m: (b, m, 0)),
            pl.BlockSpec((None, tn, D), lambda n, b, m: (layer, n, 0), pipeline_mode=pl.Buffered(1)),
            pl.BlockSpec((1, tn), lambda n, b, m: (0, n)),
        ],
        out_specs=pl.BlockSpec((None, tn // LANES, tm, LANES), lambda n, b, m: (b, n, m, 0)),
        out_shape=jax.ShapeDtypeStruct((B, N // LANES, S, LANES), _BF16),
        scratch_shapes=[pltpu.VMEM((D, tn), _BF16)],
        compiler_params=_params(("arbitrary", "arbitrary", "arbitrary")),
        name="in_proj",
    )(h, w_in_t, colscale)


def _forget_kernel(*refs, from_x):
    if from_x:
        x_ref, g_ref, sc_ref, sh_ref, w_ref, b_ref, o_ref, h_ref, carry = refs
        h = _rms_mod(x_ref[...], g_ref[...], sc_ref[...], sh_ref[...]).astype(_BF16)
        h_ref[...] = h
    else:
        hin_ref, w_ref, b_ref, o_ref, carry = refs
        h = hin_ref[...]
    ts, D = h.shape
    nh = w_ref.shape[0]

    @pl.when(pl.program_id(1) == 0)
    def _():
        carry[...] = jnp.zeros_like(carry)

    w = jnp.concatenate([w_ref[...], jnp.zeros((LANES - nh, D), _F32)], axis=0).astype(_BF16)
    fl = lax.dot_general(h, w, (((1,), (1,)), ((), ())), preferred_element_type=_F32) + b_ref[...]
    acc = jnp.minimum(fl, 0.0) - jnp.log(1.0 + jnp.exp(-jnp.abs(fl)))
    row = lax.broadcasted_iota(jnp.int32, acc.shape, 0)
    sh = 1
    while sh < ts:
        acc = acc + jnp.where(row >= sh, pltpu.roll(acc, sh, axis=0), 0.0)
        sh *= 2
    acc = acc + carry[...]
    o_ref[...] = acc
    carry[...] = acc[ts - 1:ts, :]


def _forget(h_or_x, w_in_t, layer, nh, bf, norm=None):
    B, S, D = h_or_x.shape
    ts = min(1024, S)
    first = (w_in_t.shape[1] - nh) // nh
    rows = pl.BlockSpec((None, ts, D), lambda b, s: (b, s, 0))
    tail_specs = [pl.BlockSpec((None, nh, D), lambda b, s: (layer, first, 0)),
                  pl.BlockSpec((1, LANES), lambda b, s: (0, 0))]
    f_spec = pl.BlockSpec((None, ts, LANES), lambda b, s: (b, s, 0))
    f_shape = jax.ShapeDtypeStruct((B, S, LANES), _F32)
    if norm is None:
        in_specs, args = [rows] + tail_specs, (h_or_x, w_in_t, bf)
        out_specs, out_shape = f_spec, f_shape
    else:
        g, mod, i_shift, i_scale = norm
        in_specs = [rows, pl.BlockSpec((1, D), lambda b, s: (0, 0)),
                    pl.BlockSpec((None, None, 1, D), lambda b, s: (i_scale, b, 0, 0)),
                    pl.BlockSpec((None, None, 1, D), lambda b, s: (i_shift, b, 0, 0))] + tail_specs
        args = (h_or_x, g.reshape(1, D), mod, mod, w_in_t, bf)
        out_specs, out_shape = [f_spec, rows], [f_shape, jax.ShapeDtypeStruct((B, S, D), _BF16)]
    return pl.pallas_call(
        functools.partial(_forget_kernel, from_x=norm is not None),
        grid=(B, S // ts),
        in_specs=in_specs,
        out_specs=out_specs,
        out_shape=out_shape,
        scratch_shapes=[pltpu.VMEM((1, LANES), _F32)],
        compiler_params=_params(("arbitrary", "arbitrary")),
        name="forget_cumsum",
    )(*args)


def _pool_kernel(u_ref, wp_ref, ps_ref, o_ref, *, cpg):
    S = u_ref.shape[1]
    gw = cpg * LANES
    row = lax.broadcasted_iota(jnp.int32, (S, gw), 0)
    for g, w in enumerate(POOL_WINDOWS):
        parts = [u_ref[g * cpg + j] for j in range(cpg)]
        u = (parts[0] if cpg == 1 else jnp.concatenate(parts, axis=-1)).astype(_F32)
        s = u
        sh = 1
        while sh < w:
            s = s + jnp.where(row >= sh, pltpu.roll(s, sh, axis=0), 0.0)
            sh *= 2
        cnt = jnp.minimum(row + 1, w).astype(_F32)
        pooled = s / cnt - u
        mixed = jnp.dot(pooled.astype(_BF16), wp_ref[g].astype(_BF16), preferred_element_type=_F32)
        o_ref[:, g * gw:(g + 1) * gw] = (mixed * ps_ref[:, g * gw:(g + 1) * gw]).astype(o_ref.dtype)


def _pool(proj5, w_pool, layer, ps):
    B, _, S, _ = proj5.shape
    _, G, gw, _ = w_pool.shape
    cpg = gw // LANES
    W = G * gw
    return pl.pallas_call(
        functools.partial(_pool_kernel, cpg=cpg),
        grid=(B,),
        in_specs=[
            pl.BlockSpec((None, G * cpg, S, LANES), lambda b: (b, 0, 0, 0)),
            pl.BlockSpec((None, G, gw, gw), lambda b: (layer, 0, 0, 0)),
            pl.BlockSpec((1, W), lambda b: (0, 0)),
        ],
        out_specs=pl.BlockSpec((None, S, W), lambda b: (b, 0, 0)),
        out_shape=jax.ShapeDtypeStruct((B, S, W), _BF16),
        compiler_params=_params(("arbitrary",)),
        name="pool_mixer",
    )(proj5, w_pool, ps)


def _attn_kernel(q_ref, k_ref, v_ref, f_ref, o_ref, qa_ref, ka_ref, va_ref, *, blk):
    S = q_ref.shape[0]
    h = pl.program_id(1)
    lane = lax.broadcasted_iota(jnp.int32, (S, LANES), 1)
    f = jnp.sum(jnp.where(lane == h, f_ref[...], 0.0), axis=-1, keepdims=True) * LOG2E
    f1 = f.astype(_BF16).astype(_F32)
    r1 = f - f1
    f2 = r1.astype(_BF16).astype(_F32)
    f3 = (r1 - f2).astype(_BF16).astype(_F32)
    qa_ref[:, :HEAD_DIM] = q_ref[...]
    qa_ref[:, HEAD_DIM:] = jnp.where(lane == 0, f1, jnp.where(lane == 1, f2, jnp.where(
        lane == 2, f3, jnp.where(lane < 6, 1.0, 0.0)))).astype(_BF16)
    ka_ref[:, :HEAD_DIM] = k_ref[...]
    ka_ref[:, HEAD_DIM:] = jnp.where(lane < 3, 1.0, jnp.where(lane == 3, -f1, jnp.where(
        lane == 4, -f2, jnp.where(lane == 5, -f3, 0.0)))).astype(_BF16)
    va_ref[:, :HEAD_DIM] = v_ref[...]
    va_ref[:, HEAD_DIM:] = jnp.where(lane == 0, 1.0, 0.0).astype(_BF16)

    nt = (((1,), (1,)), ((), ()))
    r_io = lax.broadcasted_iota(jnp.int32, (blk, blk), 0)
    c_io = lax.broadcasted_iota(jnp.int32, (blk, blk), 1)
    causal = c_io <= r_io
    nb = S // blk
    m = [jnp.full((blk, 1), -jnp.inf, _F32)] * nb
    acc = [jnp.zeros((blk, 2 * HEAD_DIM), _F32)] * nb
    for j in range(nb):
        ka = ka_ref[j * blk:(j + 1) * blk, :]
        va = va_ref[j * blk:(j + 1) * blk, :]
        for i in range(j, nb):
            s = lax.dot_general(qa_ref[i * blk:(i + 1) * blk, :], ka, nt, preferred_element_type=_F32)
            if i == j:
                s = jnp.where(causal, s, -jnp.inf)
            m_new = jnp.maximum(m[i], jnp.max(s, axis=-1, keepdims=True))
            p = jnp.exp2(s - m_new).astype(_BF16)
            acc[i] = jnp.exp2(m[i] - m_new) * acc[i] + jnp.dot(p, va, preferred_element_type=_F32)
            m[i] = m_new
    for i in range(nb):
        o_ref[i * blk:(i + 1) * blk, :] = (
            acc[i][:, :HEAD_DIM] / acc[i][:, HEAD_DIM:HEAD_DIM + 1]).astype(o_ref.dtype)


def _attn(proj5, fcum, nh):
    B, _, S, _ = proj5.shape
    blk = min(512, S)
    chunk = lambda o: pl.BlockSpec((None, None, S, HEAD_DIM), lambda b, h: (b, o + h, 0, 0))
    return pl.pallas_call(
        functools.partial(_attn_kernel, blk=blk),
        grid=(B, nh),
        in_specs=[
            chunk(nh), chunk(2 * nh), chunk(3 * nh),
            pl.BlockSpec((None, S, LANES), lambda b, h: (b, 0, 0)),
        ],
        out_specs=pl.BlockSpec((None, None, S, HEAD_DIM), lambda b, h: (b, h, 0, 0)),
        out_shape=jax.ShapeDtypeStruct((B, nh, S, HEAD_DIM), _BF16),
        scratch_shapes=[pltpu.VMEM((S, 2 * HEAD_DIM), _BF16)] * 3,
        compiler_params=_params(("arbitrary", "arbitrary")),
        name="forget_attn",
    )(proj5, proj5, proj5, fcum)


def _merge_kernel(h_ref, p_ref, a_ref, wg0_ref, wg1_ref, bg0_ref, bg1_ref, wb0_ref, wb1_ref, o_ref,
                  cg0_ref, cg1_ref, cb0_ref, cb1_ref):
    @pl.when(_first_inner_step())
    def _():
        cg0_ref[...] = wg0_ref[...].astype(_BF16)
        cg1_ref[...] = wg1_ref[...].astype(_BF16)
        cb0_ref[...] = wb0_ref[...].astype(_BF16)
        cb1_ref[...] = wb1_ref[...].astype(_BF16)

    h = h_ref[...]
    nh = a_ref.shape[0]
    a = jnp.concatenate([a_ref[j] for j in range(nh)], axis=-1)
    g0 = _sigmoid(jnp.dot(h, cg0_ref[...], preferred_element_type=_F32) + bg0_ref[...])
    y0 = jnp.dot(p_ref[...], cb0_ref[...], preferred_element_type=_F32)
    acc = g0 * y0
    g1 = _sigmoid(jnp.dot(h, cg1_ref[...], preferred_element_type=_F32) + bg1_ref[...])
    y1 = jnp.dot(a, cb1_ref[...], preferred_element_type=_F32)
    o_ref[...] = (acc + g1 * y1).astype(o_ref.dtype)


def _merge(h, pool_out, attn, w_gate, b_gate, w_branch, layer):
    B, S, D = h.shape
    W = pool_out.shape[-1]
    nh = attn.shape[1]
    L = w_gate.shape[0]
    tn = min(512, D)
    tm = min(1024, S)
    nn = D // tn
    bg = b_gate.reshape(L, 1, 2 * D)
    once = pl.Buffered(1)
    return pl.pallas_call(
        _merge_kernel,
        grid=(nn, B, S // tm),
        in_specs=[
            pl.BlockSpec((None, tm, D), lambda n, b, m: (b, m, 0)),
            pl.BlockSpec((None, tm, W), lambda n, b, m: (b, m, 0)),
            pl.BlockSpec((None, nh, tm, HEAD_DIM), lambda n, b, m: (b, 0, m, 0)),
            pl.BlockSpec((None, D, tn), lambda n, b, m: (layer, 0, n), pipeline_mode=once),
            pl.BlockSpec((None, D, tn), lambda n, b, m: (layer, 0, nn + n), pipeline_mode=once),
            pl.BlockSpec((None, 1, tn), lambda n, b, m: (layer, 0, n)),
            pl.BlockSpec((None, 1, tn), lambda n, b, m: (layer, 0, nn + n)),
            pl.BlockSpec((None, None, W, tn), lambda n, b, m: (layer, 0, 0, n), pipeline_mode=once),
            pl.BlockSpec((None, None, W, tn), lambda n, b, m: (layer, 1, 0, n), pipeline_mode=once),
        ],
        out_specs=pl.BlockSpec((None, tm, tn), lambda n, b, m: (b, m, n)),
        out_shape=jax.ShapeDtypeStruct((B, S, D), _BF16),
        scratch_shapes=[pltpu.VMEM((D, tn), _BF16), pltpu.VMEM((D, tn), _BF16),
                        pltpu.VMEM((W, tn), _BF16), pltpu.VMEM((W, tn), _BF16)],
        compiler_params=_params(("arbitrary", "arbitrary", "arbitrary")),
        name="branch_merge",
    )(h, pool_out, attn, w_gate, w_gate, bg, bg, w_branch, w_branch)


def _route_tile(x, g_ref, sc_ref, sh_ref, wr_ref, br_ref, e_ref, pos_ref, gate_ref, cnt_ref):
    first_step = (pl.program_id(0) == 0) & (pl.program_id(1) == 0)

    @pl.when(first_step)
    def _():
        cnt_ref[...] = jnp.zeros_like(cnt_ref)

    h = _rms_mod(x, g_ref[...], sc_ref[...], sh_ref[...])
    tr, D = h.shape
    ns = tr // LANES

    nt = (((1,), (1,)), ((), ()))
    h_hi = h.astype(_BF16)
    h_lo = (h - h_hi.astype(_F32)).astype(_BF16)
    w = wr_ref[...]
    w_hi = w.astype(_BF16)
    w_lo = (w - w_hi.astype(_F32)).astype(_BF16)
    both = lax.dot_general(jnp.concatenate([w_hi, w_lo], axis=0), h_hi, nt, preferred_element_type=_F32)
    logits = (both[:N_EXPERTS] + both[N_EXPERTS:]
              + lax.dot_general(w_hi, h_lo, nt, preferred_element_type=_F32))
    rows = [jnp.concatenate([logits[e:e + 1, s * LANES:(s + 1) * LANES] for s in range(ns)], axis=0)
            for e in range(N_EXPERTS)]
    mx = functools.reduce(jnp.maximum, rows)
    ex = [jnp.exp(r - mx) for r in rows]
    den = functools.reduce(lambda a, b: a + b, ex)
    probs = [v / den for v in ex]
    sel = [probs[e] + br_ref[e:e + 1, :] for e in range(N_EXPERTS)]

    in_top = [None] * N_EXPERTS
    gscore = []
    for g in range(N_EXPERT_GROUPS):
        ids = range(g * EXPERTS_PER_GROUP, (g + 1) * EXPERTS_PER_GROUP)
        score = None
        for i in ids:
            rank = None
            for j in ids:
                if j == i:
                    continue
                beats = (sel[j] > sel[i]) | ((sel[j] == sel[i]) if j < i else False)
                beats = beats.astype(_F32)
                rank = beats if rank is None else rank + beats
            in_top[i] = rank < float(TOP_K)
            term = jnp.where(in_top[i], sel[i], 0.0)
            score = term if score is None else score + term
        gscore.append(score)
    chosen = []
    for g in range(N_EXPERT_GROUPS):
        lose = None
        for g2 in range(N_EXPERT_GROUPS):
            if g2 == g:
                continue
            b = (gscore[g2] > gscore[g]) | ((gscore[g2] == gscore[g]) if g2 < g else False)
            lose = b if lose is None else (lose | b)
        chosen.append(jnp.logical_not(lose))
    picked = [in_top[e] & chosen[e // EXPERTS_PER_GROUP] for e in range(N_EXPERTS)]
    pf = [p.astype(_F32) for p in picked]

    n = N_EXPERTS * ns
    onehot = jnp.concatenate(pf, axis=0).astype(_BF16)
    upper = (lax.broadcasted_iota(jnp.int32, (LANES, LANES), 0)
             <= lax.broadcasted_iota(jnp.int32, (LANES, LANES), 1)).astype(_BF16)
    incl = jnp.dot(onehot, upper, preferred_element_type=_F32)
    r_io = lax.broadcasted_iota(jnp.int32, (n, n), 0)
    c_io = lax.broadcasted_iota(jnp.int32, (n, n), 1)
    shift = ns.bit_length() - 1
    same_expert = lax.shift_right_logical(r_io, shift) == lax.shift_right_logical(c_io, shift)
    earlier = jnp.where(same_expert & (c_io < r_io), 1.0, 0.0).astype(_BF16)
    carry = jnp.dot(earlier, incl.astype(_BF16), preferred_element_type=_F32)[:, LANES - 1:LANES]
    base = cnt_ref[:, 0:1]
    posm = [incl[e * ns:(e + 1) * ns, :] - 1.0 + carry[e * ns:(e + 1) * ns, :] + base[e:e + 1, :]
            for e in range(N_EXPERTS)]
    last = [(incl[(e + 1) * ns - 1:(e + 1) * ns, LANES - 1:LANES] + carry[(e + 1) * ns - 1:(e + 1) * ns, :])
            for e in range(N_EXPERTS)]
    cnt_ref[...] = jnp.broadcast_to(base + jnp.concatenate(last, axis=0), cnt_ref.shape)

    gnum = [pf[e] * probs[e] for e in range(N_EXPERTS)]
    gden = functools.reduce(lambda a, b: a + b, gnum)
    zero = jnp.zeros_like(pf[0])
    seen = zero
    e_out = [zero, zero]
    p_out = [zero, zero]
    g_out = [zero, zero]
    for e in range(N_EXPERTS):
        for k in range(TOP_K):
            hit = pf[e] * (seen == float(k)).astype(_F32)
            e_out[k] = e_out[k] + hit * float(e)
            p_out[k] = p_out[k] + hit * posm[e]
            g_out[k] = g_out[k] + hit * gnum[e]
        seen = seen + pf[e]
    for k in range(TOP_K):
        e_ref[k] = e_out[k].astype(jnp.int32)
        pos_ref[k] = p_out[k].astype(jnp.int32)
        gate_ref[k] = g_out[k] / gden


def _out_route_kernel(x_ref, m_ref, w_ref, ga_ref, g_ref, sc_ref, sh_ref, wr_ref, br_ref,
                      o_ref, e_ref, pos_ref, gate_ref, cnt_ref, wb_ref):
    @pl.when((pl.program_id(0) == 0) & (pl.program_id(1) == 0))
    def _():
        wb_ref[...] = w_ref[...].astype(_BF16)

    xn = x_ref[...] + ga_ref[...] * jnp.dot(m_ref[...], wb_ref[...], preferred_element_type=_F32)
    o_ref[...] = xn
    _route_tile(xn, g_ref, sc_ref, sh_ref, wr_ref, br_ref, e_ref, pos_ref, gate_ref, cnt_ref)


def _out_proj_route(x, merged, w_out, layer, mod, g_moe, wr_t, br):
    B, S, D = x.shape
    tr = min(ROUTE_TILE, S)
    ns = tr // LANES
    assert ns & (ns - 1) == 0
    nb = S // tr
    E = N_EXPERTS
    tok = lambda dt: jax.ShapeDtypeStruct((B * nb, TOP_K, ns, LANES), dt)
    tok_spec = pl.BlockSpec((None, TOP_K, ns, LANES), lambda b, s: (b * nb + s, 0, 0, 0))
    rows = pl.BlockSpec((None, tr, D), lambda b, s: (b, s, 0))
    mod_row = lambda i: pl.BlockSpec((None, None, 1, D), lambda b, s: (i, b, 0, 0))
    xn, eidx, pos, gate, counts = pl.pallas_call(
        _out_route_kernel,
        grid=(B, nb),
        in_specs=[
            rows, rows,
            pl.BlockSpec((None, D, D), lambda b, s: (layer, 0, 0), pipeline_mode=pl.Buffered(1)),
            mod_row(2),
            pl.BlockSpec((1, D), lambda b, s: (0, 0)),
            mod_row(4), mod_row(3),
            pl.BlockSpec((E, D), lambda b, s: (0, 0)),
            pl.BlockSpec((E, 1), lambda b, s: (0, 0)),
        ],
        out_specs=[
            rows, tok_spec, tok_spec, tok_spec,
            pl.BlockSpec((E, LANES), lambda b, s: (0, 0)),
        ],
        out_shape=[
            jax.ShapeDtypeStruct((B, S, D), _F32),
            tok(jnp.int32), tok(jnp.int32), tok(_F32),
            jax.ShapeDtypeStruct((E, LANES), _F32),
        ],
        scratch_shapes=[pltpu.VMEM((D, D), _BF16)],
        compiler_params=_params(("arbitrary", "arbitrary")),
        name="out_proj_route",
    )(x, merged, w_out, mod, g_moe.reshape(1, D), mod, mod, wr_t, br)
    flat = lambda a: a.reshape(B * nb, TOP_K, tr)
    return xn, flat(eidx), flat(pos), flat(gate), counts


def _dispatch_kernel(pad_start_ref, pad_len_ref, nu_ref, dest_ref, x0_ref, xn_ref, g_ref,
                     sc0_ref, sh0_ref, scn_ref, shn_ref, buf_ref, h_ref, zblk, sem):
    i = pl.program_id(0)
    tr = xn_ref.shape[0]

    @pl.when(i == 0)
    def _():
        h_ref[0] = _rms_mod(x0_ref[...], g_ref[...], sc0_ref[...], sh0_ref[...])

    def row_wait():
        pltpu.make_async_copy(h_ref.at[0, pl.ds(0, 1), :], buf_ref.at[pl.ds(0, 1), :], sem).wait()

    def step(slot):
        for r in range(tr):
            for k in range(TOP_K):
                pltpu.make_async_copy(h_ref.at[slot, pl.ds(r, 1), :],
                                      buf_ref.at[pl.ds(dest_ref[k, r], 1), :], sem).start(priority=k)

        @pl.when(i + 1 < pl.num_programs(0))
        def _():
            h_ref[1 - slot] = _rms_mod(xn_ref[...], g_ref[...], scn_ref[...], shn_ref[...])

        for _ in range(tr * TOP_K):
            row_wait()

    for slot in range(2):
        pl.when(lax.rem(i, 2) == slot)(functools.partial(step, slot))

    @pl.when(pl.program_id(0) == 0)
    def _():
        bm = zblk.shape[0]
        nblk = buf_ref.shape[0] // bm
        zblk[...] = jnp.zeros_like(zblk)
        for e in range(N_EXPERTS):
            def zissue(r, carry, e=e):
                pltpu.make_async_copy(zblk.at[pl.ds(0, 1), :],
                                      buf_ref.at[pl.ds(pad_start_ref[e] + r, 1), :], sem).start()
                return carry
            lax.fori_loop(0, pad_len_ref[e], zissue, 0)
        for e in range(N_EXPERTS):
            def zdrain(r, carry):
                row_wait()
                return carry
            lax.fori_loop(0, pad_len_ref[e], zdrain, 0)

        def blk_copy(i):
            return pltpu.make_async_copy(zblk, buf_ref.at[pl.ds(pl.multiple_of(i * bm, bm), bm), :], sem)

        def bissue(i, carry):
            blk_copy(i).start()
            return carry

        def bdrain(i, carry):
            blk_copy(i).wait()
            return carry

        lax.fori_loop(nu_ref[0], nblk, bissue, 0)
        lax.fori_loop(nu_ref[0], nblk, bdrain, 0)


def _dispatch(x, g, mod, i_shift, i_scale, dest, pad_start, pad_len, n_used, rows, bm):
    B, S, D = x.shape
    nbt, _, tr = dest.shape
    nb = S // tr
    nxt = lambda i: jnp.minimum(i + 1, nbt - 1)
    grid_spec = pltpu.PrefetchScalarGridSpec(
        num_scalar_prefetch=3,
        grid=(nbt,),
        in_specs=[
            pl.BlockSpec((None, TOP_K, tr), lambda i, *_: (i, 0, 0), memory_space=pltpu.SMEM),
            pl.BlockSpec((None, tr, D), lambda i, *_: (0, 0, 0)),
            pl.BlockSpec((None, tr, D), lambda i, *_: (nxt(i) // nb, nxt(i) % nb, 0)),
            pl.BlockSpec((1, D), lambda i, *_: (0, 0)),
            pl.BlockSpec((None, None, 1, D), lambda i, *_: (i_scale, 0, 0, 0)),
            pl.BlockSpec((None, None, 1, D), lambda i, *_: (i_shift, 0, 0, 0)),
            pl.BlockSpec((None, None, 1, D), lambda i, *_: (i_scale, nxt(i) // nb, 0, 0)),
            pl.BlockSpec((None, None, 1, D), lambda i, *_: (i_shift, nxt(i) // nb, 0, 0)),
        ],
        out_specs=pl.BlockSpec(memory_space=pl.ANY),
        scratch_shapes=[pltpu.VMEM((2, tr, D), _F32), pltpu.VMEM((bm, D), _F32),
                        pltpu.SemaphoreType.DMA(())],
    )
    return pl.pallas_call(
        _dispatch_kernel,
        grid_spec=grid_spec,
        out_shape=jax.ShapeDtypeStruct((rows, D), _F32),
        compiler_params=_params(("arbitrary",)),
        name="moe_dispatch",
    )(pad_start, pad_len, n_used, dest, x, x, g.reshape(1, D), mod, mod, mod, mod)


def _expert_kernel(be_ref, nx_ref, vl_ref, nu_ref, x_ref, wg_hbm, wu_hbm, wd_hbm, o_ref,
                   wgb, wub, wdb, sg, su, sd, sem, *, layer):
    i = pl.program_id(0)
    e = be_ref[i]
    live = i < nu_ref[0]
    new_expert = (i == 0) | (e != be_ref[jnp.maximum(i - 1, 0)])

    def weight_copies(ex):
        return [pltpu.make_async_copy(w.at[layer, ex], s, sem.at[n])
                for n, (w, s) in enumerate(((wg_hbm, sg), (wu_hbm, su), (wd_hbm, sd)))]

    @pl.when(live & new_expert)
    def _():
        @pl.when(i == 0)
        def _():
            for cp in weight_copies(e):
                cp.start()

        for cp in weight_copies(e):
            cp.wait()
        wgb[...] = sg[...].astype(_BF16)
        wub[...] = su[...].astype(_BF16)
        wdb[...] = sd[...].astype(_BF16)

        @pl.when(nx_ref[i] >= 0)
        def _():
            for cp in weight_copies(nx_ref[i]):
                cp.start(priority=1)

    def ffn(rows):
        xb = x_ref[rows, :].astype(_BF16)
        a = jnp.dot(xb, wgb[...], preferred_element_type=_F32)
        u = jnp.dot(xb, wub[...], preferred_element_type=_F32)
        act = (a * _sigmoid(a) * u).astype(_BF16)
        o_ref[rows, :] = jnp.dot(act, wdb[...], preferred_element_type=_F32).astype(o_ref.dtype)

    half = x_ref.shape[0] // 2
    needs_all = vl_ref[i] > half

    @pl.when(live & needs_all)
    def _():
        ffn(slice(None))

    @pl.when(live & jnp.logical_not(needs_all))
    def _():
        ffn(slice(0, half))
        o_ref[half:, :] = jnp.zeros((half, o_ref.shape[1]), o_ref.dtype)

    @pl.when(jnp.logical_not(live))
    def _():
        o_ref[...] = jnp.zeros_like(o_ref)


def _experts(buf, block_expert, next_expert, valid_rows, n_used, wg, wu, wd, layer, bm):
    rows, D = buf.shape
    De = wg.shape[-1]
    nblk = rows // bm
    grid_spec = pltpu.PrefetchScalarGridSpec(
        num_scalar_prefetch=4,
        grid=(nblk,),
        in_specs=[
            pl.BlockSpec((bm, D), lambda i, be, nx, vl, nu: (jnp.minimum(i, jnp.maximum(nu[0] - 1, 0)), 0)),
            pl.BlockSpec(memory_space=pl.ANY),
            pl.BlockSpec(memory_space=pl.ANY),
            pl.BlockSpec(memory_space=pl.ANY),
        ],
        out_specs=pl.BlockSpec((bm, D), lambda i, be, nx, vl, nu: (i, 0)),
        scratch_shapes=[
            pltpu.VMEM((D, De), _BF16), pltpu.VMEM((D, De), _BF16), pltpu.VMEM((De, D), _BF16),
            pltpu.VMEM((D, De), _F32), pltpu.VMEM((D, De), _F32), pltpu.VMEM((De, D), _F32),
            pltpu.SemaphoreType.DMA((3,)),
        ],
    )
    return pl.pallas_call(
        functools.partial(_expert_kernel, layer=layer),
        grid_spec=grid_spec,
        out_shape=jax.ShapeDtypeStruct((rows, D), _BF16),
        compiler_params=_params(("arbitrary",), vmem=60 * 1024 * 1024),
        name="moe_experts",
    )(block_expert, next_expert, valid_rows, n_used, buf, wg, wu, wd)


ROW_CHUNK = 16


def _combine_kernel(src_ref, nch_ref, dst_ref, tot_ref,
                    x_ref, y_hbm, scol_ref, gcol_ref, gm_ref, nf_ref, gn_ref, scn_ref, shn_ref,
                    *rest, final, nb, has_next):
    if has_next:
        o_ref, h_ref, ystage, sem = rest
    else:
        o_ref, ystage, sem = rest
        h_ref = None
    i = pl.program_id(0) * nb + pl.program_id(1)
    n_tiles = pl.num_programs(0) * nb
    cur = lax.rem(i, 2)
    tc = x_ref.shape[0]
    R = ystage.shape[1]

    def chunk_copy(src, dst, buf):
        return pltpu.make_async_copy(y_hbm.at[pl.ds(src, ROW_CHUNK), :],
                                     ystage.at[buf, pl.ds(dst, ROW_CHUNK), :], sem.at[buf])

    def start_tile(tile, buf):
        for e in range(N_EXPERTS):
            idx = tile * N_EXPERTS + e

            def body(c, carry, idx=idx):
                chunk_copy(pl.multiple_of(src_ref[idx] + c * ROW_CHUNK, ROW_CHUNK),
                           pl.multiple_of(dst_ref[idx] + c * ROW_CHUNK, ROW_CHUNK), buf).start()
                return carry

            lax.fori_loop(0, nch_ref[idx], body, 0)

    @pl.when(i == 0)
    def _():
        ystage[...] = jnp.zeros_like(ystage)
        start_tile(0, 0)

    @pl.when(i + 1 < n_tiles)
    def _():
        start_tile(i + 1, 1 - cur)

    def wait_one(c, carry):
        chunk_copy(0, 0, cur).wait()
        return carry

    lax.fori_loop(0, tot_ref[i], wait_one, 0)

    def finish(used):
        scol = scol_ref[...]
        gcol = gcol_ref[...]
        c_io = lax.broadcasted_iota(jnp.int32, (tc, used), 1)
        g = jnp.where(c_io == scol[:, 0:1], gcol[:, 0:1],
                      jnp.where(c_io == scol[:, 1:2], gcol[:, 1:2], 0.0)).astype(_BF16)
        moe = jnp.dot(g, ystage[cur, :used, :], preferred_element_type=_F32)
        xn = x_ref[...] + gm_ref[...] * moe
        if final:
            ms = jnp.mean(xn * xn, axis=-1, keepdims=True)
            xn = xn * lax.rsqrt(ms + EPS) * nf_ref[...]
        o_ref[...] = xn
        if has_next:
            h_ref[...] = _rms_mod(xn, gn_ref[...], scn_ref[...], shn_ref[...]).astype(h_ref.dtype)

    short = (R * 3 // 4) // LANES * LANES
    fits = tot_ref[i] * ROW_CHUNK <= short
    pl.when(fits)(functools.partial(finish, short))
    pl.when(jnp.logical_not(fits))(functools.partial(finish, R))


def _combine(x, y, src, nch, dst, tot, slot_col, gate_col, mod, i_gate, nf, final, g_next, mod_next):
    B, S, D = x.shape
    n_tiles, tc, _ = slot_col.shape
    nb = S // tc
    R = _stage_rows(tc)
    has_next = g_next is not None
    if not has_next:
        g_next, mod_next = nf, mod
    tile = lambda b, s, *_: (b * nb + s, 0, 0)
    row = lambda b, s, *_: (b, s, 0)
    grid_spec = pltpu.PrefetchScalarGridSpec(
        num_scalar_prefetch=4,
        grid=(B, nb),
        in_specs=[
            pl.BlockSpec((None, tc, D), row),
            pl.BlockSpec(memory_space=pl.ANY),
            pl.BlockSpec((None, tc, TOP_K), tile),
            pl.BlockSpec((None, tc, TOP_K), tile),
            pl.BlockSpec((None, None, 1, D), lambda b, s, *_: (i_gate, b, 0, 0)),
            pl.BlockSpec((1, D), lambda b, s, *_: (0, 0)),
            pl.BlockSpec((1, D), lambda b, s, *_: (0, 0)),
            pl.BlockSpec((None, None, 1, D), lambda b, s, *_: (1, b, 0, 0)),
            pl.BlockSpec((None, None, 1, D), lambda b, s, *_: (0, b, 0, 0)),
        ],
        out_specs=([pl.BlockSpec((None, tc, D), row)] * 2) if has_next else pl.BlockSpec((None, tc, D), row),
        scratch_shapes=[pltpu.VMEM((2, R, D), _BF16), pltpu.SemaphoreType.DMA((2,))],
    )
    out_shape = jax.ShapeDtypeStruct((B, S, D), _F32)
    if has_next:
        out_shape = [out_shape, jax.ShapeDtypeStruct((B, S, D), _BF16)]
    return pl.pallas_call(
        functools.partial(_combine_kernel, final=final, nb=nb, has_next=has_next),
        grid_spec=grid_spec,
        out_shape=out_shape,
        compiler_params=_params(("arbitrary", "arbitrary")),
        name="moe_combine",
    )(src, nch, dst, tot, x, y, slot_col, gate_col, mod, nf.reshape(1, D), g_next.reshape(1, D),
      mod_next, mod_next)


def _stage_rows(tc):
    worst = TOP_K * tc + 2 * (ROW_CHUNK - 1) * N_EXPERTS
    return -(-worst // LANES) * LANES


def _combine_plan(eidx, dest, gate, pstarts, tc):
    T = eidx.shape[0] * eidx.shape[2]
    n_tiles = T // tc
    flat = lambda a: a.transpose(0, 2, 1).reshape(n_tiles, tc, TOP_K)
    e_f, d_f, g_f = flat(eidx), flat(dest), flat(gate)
    onehot = (e_f[..., None] == jnp.arange(N_EXPERTS, dtype=jnp.int32)).astype(jnp.int32)
    cnt = jnp.sum(onehot, axis=(1, 2))
    first = pstarts[None, :] + jnp.cumsum(cnt, axis=0) - cnt
    src = first // ROW_CHUNK * ROW_CHUNK
    nch = jnp.where(cnt > 0, (first + cnt - src + ROW_CHUNK - 1) // ROW_CHUNK, 0)
    dst = ROW_CHUNK * (jnp.cumsum(nch, axis=1) - nch)
    slot = d_f + jnp.sum(onehot * (dst - src)[:, None, None, :], axis=-1)
    return (src.reshape(-1), nch.reshape(-1), dst.reshape(-1), jnp.sum(nch, axis=1),
            slot, g_f)


def kernel(x, c, w_ada, b_ada, norm_mix, norm_moe, w_in, b_forget, w_pool, pool_scale, w_branch,
           w_gate, b_gate, w_out, w_router, b_router, w_exp_gate, w_exp_up, w_exp_down, norm_final):
    B, S, D = x.shape
    L = w_ada.shape[0]
    W = D // 2
    nh = W // HEAD_DIM
    T = B * S
    A = T * TOP_K
    bm = min(256, A // N_EXPERTS)
    rows = (A // bm + N_EXPERTS) * bm

    mod_all = _ada(c, w_ada, b_ada)
    mod_all = mod_all.reshape(L, B, N_MOD, D).transpose(0, 2, 1, 3)[:, :, :, None, :]
    wr_t = w_router.T
    br = b_router.reshape(N_EXPERTS, 1)
    colscale = jnp.concatenate([jnp.ones((W,), _F32), jnp.full((W,), HEAD_DIM ** -0.5 * LOG2E, _F32),
                                jnp.ones((2 * W,), _F32)]).reshape(1, 4 * W)

    w_in_t = jnp.swapaxes(w_in, 1, 2)
    h = None
    for l in range(L):
        mod = mod_all[l]
        bf = jnp.zeros((1, LANES), _F32).at[0, :nh].set(b_forget[l])
        if h is None:
            fcum, h = _forget(x, w_in_t, l, nh, bf, norm=(norm_mix[l], mod, 0, 1))
        else:
            fcum = _forget(h, w_in_t, l, nh, bf)
        proj5 = _proj(h, w_in_t, l, colscale)
        pool_out = _pool(proj5, w_pool, l, pool_scale[l].reshape(1, W))
        attn = _attn(proj5, fcum, nh)
        merged = _merge(h, pool_out, attn, w_gate, b_gate, w_branch, l)
        x, eidx, pos, gate, counts = _out_proj_route(x, merged, w_out, l, mod, norm_moe[l], wr_t, br)
        counts = counts[:, 0].astype(jnp.int32)
        pcounts = (counts + bm - 1) // bm * bm
        pends = jnp.cumsum(pcounts)
        pstarts = pends - pcounts
        dest = pos
        for e in range(N_EXPERTS):
            dest = dest + jnp.where(eidx == e, pstarts[e], 0)
        n_used = (pends[-1] // bm).astype(jnp.int32)
        blk_ids = jnp.arange(rows // bm, dtype=jnp.int32)
        blk_ids = jnp.minimum(blk_ids, n_used - 1)
        block_expert = jnp.sum((pends[None, :] <= (blk_ids * bm)[:, None]).astype(jnp.int32), axis=1)
        block_expert = jnp.minimum(block_expert, N_EXPERTS - 1)
        ids = jnp.arange(N_EXPERTS, dtype=jnp.int32)
        later = (ids[None, :] > ids[:, None]) & (pcounts[None, :] > 0)
        next_of = jnp.min(jnp.where(later, ids[None, :], N_EXPERTS), axis=1)
        next_of = jnp.where(next_of == N_EXPERTS, -1, next_of)
        next_expert = jnp.sum(jnp.where(block_expert[:, None] == ids[None, :], next_of[None, :], 0), axis=1)
        n_used = n_used.reshape(1)
        td = min(DISPATCH_TILE, S)
        dest_d = dest.transpose(0, 2, 1).reshape(T // td, td, TOP_K).transpose(0, 2, 1)
        buf = _dispatch(x, norm_moe[l], mod, 3, 4, dest_d, pstarts + counts, pcounts - counts,
                        n_used, rows, bm)
        ends = jnp.sum(jnp.where(block_expert[:, None] == ids[None, :], (pstarts + counts)[None, :], 0), axis=1)
        valid_rows = jnp.clip(ends - jnp.arange(rows // bm, dtype=jnp.int32) * bm, 0, bm)
        y = _experts(buf, block_expert, next_expert.astype(jnp.int32), valid_rows.astype(jnp.int32), n_used,
                     w_exp_gate, w_exp_up, w_exp_down, l, bm)
        plan = _combine_plan(eidx, dest, gate, pstarts, min(COMBINE_TILE, S))
        if l + 1 < L:
            x, h = _combine(x, y, *plan, mod, 5, norm_final, False, norm_mix[l + 1], mod_all[l + 1])
        else:
            x = _combine(x, y, *plan, mod, 5, norm_final, True, None, None)
    return x
```

```python
import functools

import jax
import jax.numpy as jnp
from jax import lax
from jax.experimental import pallas as pl
from jax.experimental.pallas import tpu as pltpu

N_MOD = 6
EPS = 1e-6
POOL_WINDOWS = (2, 4, 8, 16)
HEAD_DIM = 128
N_EXPERTS = 16
N_EXPERT_GROUPS = 4
EXPERTS_PER_GROUP = N_EXPERTS // N_EXPERT_GROUPS
TOP_K = 2
LOG2E = 1.4426950408889634
ROUTE_TILE = 512
DISPATCH_TILE = 512
COMBINE_TILE = 256
LANES = 128
VMEM_LIMIT = 56 * 1024 * 1024

_F32 = jnp.float32
_BF16 = jnp.bfloat16


def _params(sem, vmem=VMEM_LIMIT):
    return pltpu.CompilerParams(dimension_semantics=sem, vmem_limit_bytes=vmem)


def _sigmoid(v):
    return 1.0 / (1.0 + jnp.exp(-v))


def _rms_mod(x, g, scale, shift):
    ms = jnp.mean(x * x, axis=-1, keepdims=True)
    return x * lax.rsqrt(ms + EPS) * g * (1.0 + scale) + shift


def _ada_kernel(c_ref, w_ref, b_ref, o_ref):
    c = c_ref[...]
    ca = (c * _sigmoid(c)).astype(_BF16)
    o_ref[...] = jnp.dot(ca, w_ref[...].astype(_BF16), preferred_element_type=_F32) + b_ref[...]


def _ada(c, w_ada, b_ada):
    L, D, M = w_ada.shape
    B = c.shape[0]
    tn = min(1024, M)
    return pl.pallas_call(
        _ada_kernel,
        grid=(L, M // tn),
        in_specs=[
            pl.BlockSpec((B, D), lambda l, n: (0, 0)),
            pl.BlockSpec((None, D, tn), lambda l, n: (l, 0, n)),
            pl.BlockSpec((None, 1, tn), lambda l, n: (l, 0, n)),
        ],
        out_specs=pl.BlockSpec((None, B, tn), lambda l, n: (l, 0, n)),
        out_shape=jax.ShapeDtypeStruct((L, B, M), _F32),
        compiler_params=_params(("arbitrary", "arbitrary")),
        name="ada_mod",
    )(c, w_ada, b_ada.reshape(L, 1, M))


def _first_inner_step():
    return (pl.program_id(1) == 0) & (pl.program_id(2) == 0)


def _proj_kernel(h_ref, w_ref, cs_ref, o_ref, wb_ref):
    @pl.when(_first_inner_step())
    def _():
        wb_ref[...] = (w_ref[...].T * cs_ref[...]).astype(_BF16)

    acc = jnp.dot(h_ref[...], wb_ref[...], preferred_element_type=_F32)
    for j in range(o_ref.shape[0]):
        o_ref[j] = acc[:, j * LANES:(j + 1) * LANES].astype(o_ref.dtype)


def _proj(h, w_in_t, layer, colscale):
    B, S, D = h.shape
    N = colscale.shape[1]
    tn = min(2048, N)
    tm = min(1024, S)
    return pl.pallas_call(
        _proj_kernel,
        grid=(N // tn, B, S // tm),
        in_specs=[
            pl.BlockSpec((None, tm, D), lambda n, b, m: (b, m, 0)),
            pl.BlockSpec((None, tn, D), lambda n, b, m: (layer, n, 0), pipeline_mode=pl.Buffered(1)),
            pl.BlockSpec((1, tn), lambda n, b, m: (0, n)),
        ],
        out_specs=pl.BlockSpec((None, tn // LANES, tm, LANES), lambda n, b, m: (b, n, m, 0)),
        out_shape=jax.ShapeDtypeStruct((B, N // LANES, S, LANES), _BF16),
        scratch_shapes=[pltpu.VMEM((D, tn), _BF16)],
        compiler_params=_params(("arbitrary", "arbitrary", "arbitrary")),
        name="in_proj",
    )(h, w_in_t, colscale)


def _forget_kernel(*refs, from_x):
    if from_x:
        x_ref, g_ref, sc_ref, sh_ref, w_ref, b_ref, o_ref, h_ref, carry = refs
        h = _rms_mod(x_ref[...], g_ref[...], sc_ref[...], sh_ref[...]).astype(_BF16)
        h_ref[...] = h
    else:
        hin_ref, w_ref, b_ref, o_ref, carry = refs
        h = hin_ref[...]
    ts, D = h.shape
    nh = w_ref.shape[0]

    @pl.when(pl.program_id(1) == 0)
    def _():
        carry[...] = jnp.zeros_like(carry)

    w = jnp.concatenate([w_ref[...], jnp.zeros((LANES - nh, D), _F32)], axis=0).astype(_BF16)
    fl = lax.dot_general(h, w, (((1,), (1,)), ((), ())), preferred_element_type=_F32) + b_ref[...]
    acc = jnp.minimum(fl, 0.0) - jnp.log(1.0 + jnp.exp(-jnp.abs(fl)))
    row = lax.broadcasted_iota(jnp.int32, acc.shape, 0)
    sh = 1
    while sh < ts:
        acc = acc + jnp.where(row >= sh, pltpu.roll(acc, sh, axis=0), 0.0)
        sh *= 2
    acc = acc + carry[...]
    o_ref[...] = acc
    carry[...] = acc[ts - 1:ts, :]


def _forget(h_or_x, w_in_t, layer, nh, bf, norm=None):
    B, S, D = h_or_x.shape
    ts = min(1024, S)
    first = (w_in_t.shape[1] - nh) // nh
    rows = pl.BlockSpec((None, ts, D), lambda b, s: (b, s, 0))
    tail_specs = [pl.BlockSpec((None, nh, D), lambda b, s: (layer, first, 0)),
                  pl.BlockSpec((1, LANES), lambda b, s: (0, 0))]
    f_spec = pl.BlockSpec((None, ts, LANES), lambda b, s: (b, s, 0))
    f_shape = jax.ShapeDtypeStruct((B, S, LANES), _F32)
    if norm is None:
        in_specs, args = [rows] + tail_specs, (h_or_x, w_in_t, bf)
        out_specs, out_shape = f_spec, f_shape
    else:
        g, mod, i_shift, i_scale = norm
        in_specs = [rows, pl.BlockSpec((1, D), lambda b, s: (0, 0)),
                    pl.BlockSpec((None, None, 1, D), lambda b, s: (i_scale, b, 0, 0)),
                    pl.BlockSpec((None, None, 1, D), lambda b, s: (i_shift, b, 0, 0))] + tail_specs
        args = (h_or_x, g.reshape(1, D), mod, mod, w_in_t, bf)
        out_specs, out_shape = [f_spec, rows], [f_shape, jax.ShapeDtypeStruct((B, S, D), _BF16)]
    return pl.pallas_call(
        functools.partial(_forget_kernel, from_x=norm is not None),
        grid=(B, S // ts),
        in_specs=in_specs,
        out_specs=out_specs,
        out_shape=out_shape,
        scratch_shapes=[pltpu.VMEM((1, LANES), _F32)],
        compiler_params=_params(("arbitrary", "arbitrary")),
        name="forget_cumsum",
    )(*args)


def _pool_kernel(u_ref, wp_ref, ps_ref, o_ref, *, cpg):
    S = u_ref.shape[1]
    gw = cpg * LANES
    row = lax.broadcasted_iota(jnp.int32, (S, gw), 0)
    for g, w in enumerate(POOL_WINDOWS):
        parts = [u_ref[g * cpg + j] for j in range(cpg)]
        u = (parts[0] if cpg == 1 else jnp.concatenate(parts, axis=-1)).astype(_F32)
        s = u
        sh = 1
        while sh < w:
            s = s + jnp.where(row >= sh, pltpu.roll(s, sh, axis=0), 0.0)
            sh *= 2
        cnt = jnp.minimum(row + 1, w).astype(_F32)
        pooled = s / cnt - u
        mixed = jnp.dot(pooled.astype(_BF16), wp_ref[g].astype(_BF16), preferred_element_type=_F32)
        o_ref[:, g * gw:(g + 1) * gw] = (mixed * ps_ref[:, g * gw:(g + 1) * gw]).astype(o_ref.dtype)


def _pool(proj5, w_pool, layer, ps):
    B, _, S, _ = proj5.shape
    _, G, gw, _ = w_pool.shape
    cpg = gw // LANES
    W = G * gw
    return pl.pallas_call(
        functools.partial(_pool_kernel, cpg=cpg),
        grid=(B,),
        in_specs=[
            pl.BlockSpec((None, G * cpg, S, LANES), lambda b: (b, 0, 0, 0)),
            pl.BlockSpec((None, G, gw, gw), lambda b: (layer, 0, 0, 0)),
            pl.BlockSpec((1, W), lambda b: (0, 0)),
        ],
        out_specs=pl.BlockSpec((None, S, W), lambda b: (b, 0, 0)),
        out_shape=jax.ShapeDtypeStruct((B, S, W), _BF16),
        compiler_params=_params(("arbitrary",)),
        name="pool_mixer",
    )(proj5, w_pool, ps)


def _attn_kernel(q_ref, k_ref, v_ref, f_ref, o_ref, qa_ref, ka_ref, va_ref, *, blk):
    S = q_ref.shape[0]
    h = pl.program_id(1)
    lane = lax.broadcasted_iota(jnp.int32, (S, LANES), 1)
    f = jnp.sum(jnp.where(lane == h, f_ref[...], 0.0), axis=-1, keepdims=True) * LOG2E
    f1 = f.astype(_BF16).astype(_F32)
    r1 = f - f1
    f2 = r1.astype(_BF16).astype(_F32)
    f3 = (r1 - f2).astype(_BF16).astype(_F32)
    qa_ref[:, :HEAD_DIM] = q_ref[...]
    qa_ref[:, HEAD_DIM:] = jnp.where(lane == 0, f1, jnp.where(lane == 1, f2, jnp.where(
        lane == 2, f3, jnp.where(lane < 6, 1.0, 0.0)))).astype(_BF16)
    ka_ref[:, :HEAD_DIM] = k_ref[...]
    ka_ref[:, HEAD_DIM:] = jnp.where(lane < 3, 1.0, jnp.where(lane == 3, -f1, jnp.where(
        lane == 4, -f2, jnp.where(lane == 5, -f3, 0.0)))).astype(_BF16)
    va_ref[:, :HEAD_DIM] = v_ref[...]
    va_ref[:, HEAD_DIM:] = jnp.where(lane == 0, 1.0, 0.0).astype(_BF16)

    nt = (((1,), (1,)), ((), ()))
    r_io = lax.broadcasted_iota(jnp.int32, (blk, blk), 0)
    c_io = lax.broadcasted_iota(jnp.int32, (blk, blk), 1)
    causal = c_io <= r_io
    nb = S // blk
    m = [jnp.full((blk, 1), -jnp.inf, _F32)] * nb
    acc = [jnp.zeros((blk, 2 * HEAD_DIM), _F32)] * nb
    for j in range(nb):
        ka = ka_ref[j * blk:(j + 1) * blk, :]
        va = va_ref[j * blk:(j + 1) * blk, :]
        for i in range(j, nb):
            s = lax.dot_general(qa_ref[i * blk:(i + 1) * blk, :], ka, nt, preferred_element_type=_F32)
            if i == j:
                s = jnp.where(causal, s, -jnp.inf)
            m_new = jnp.maximum(m[i], jnp.max(s, axis=-1, keepdims=True))
            p = jnp.exp2(s - m_new).astype(_BF16)
            acc[i] = jnp.exp2(m[i] - m_new) * acc[i] + jnp.dot(p, va, preferred_element_type=_F32)
            m[i] = m_new
    for i in range(nb):
        o_ref[i * blk:(i + 1) * blk, :] = (
            acc[i][:, :HEAD_DIM] / acc[i][:, HEAD_DIM:HEAD_DIM + 1]).astype(o_ref.dtype)


def _attn(proj5, fcum, nh):
    B, _, S, _ = proj5.shape
    blk = min(512, S)
    chunk = lambda o: pl.BlockSpec((None, None, S, HEAD_DIM), lambda b, h: (b, o + h, 0, 0))
    return pl.pallas_call(
        functools.partial(_attn_kernel, blk=blk),
        grid=(B, nh),
        in_specs=[
            chunk(nh), chunk(2 * nh), chunk(3 * nh),
            pl.BlockSpec((None, S, LANES), lambda b, h: (b, 0, 0)),
        ],
        out_specs=pl.BlockSpec((None, None, S, HEAD_DIM), lambda b, h: (b, h, 0, 0)),
        out_shape=jax.ShapeDtypeStruct((B, nh, S, HEAD_DIM), _BF16),
        scratch_shapes=[pltpu.VMEM((S, 2 * HEAD_DIM), _BF16)] * 3,
        compiler_params=_params(("arbitrary", "arbitrary")),
        name="forget_attn",
    )(proj5, proj5, proj5, fcum)


def _merge_kernel(h_ref, p_ref, a_ref, wg0_ref, wg1_ref, bg0_ref, bg1_ref, wb0_ref, wb1_ref, o_ref,
                  cg0_ref, cg1_ref, cb0_ref, cb1_ref):
    @pl.when(_first_inner_step())
    def _():
        cg0_ref[...] = wg0_ref[...].astype(_BF16)
        cg1_ref[...] = wg1_ref[...].astype(_BF16)
        cb0_ref[...] = wb0_ref[...].astype(_BF16)
        cb1_ref[...] = wb1_ref[...].astype(_BF16)

    h = h_ref[...]
    nh = a_ref.shape[0]
    a = jnp.concatenate([a_ref[j] for j in range(nh)], axis=-1)
    g0 = _sigmoid(jnp.dot(h, cg0_ref[...], preferred_element_type=_F32) + bg0_ref[...])
    y0 = jnp.dot(p_ref[...], cb0_ref[...], preferred_element_type=_F32)
    acc = g0 * y0
    g1 = _sigmoid(jnp.dot(h, cg1_ref[...], preferred_element_type=_F32) + bg1_ref[...])
    y1 = jnp.dot(a, cb1_ref[...], preferred_element_type=_F32)
    o_ref[...] = (acc + g1 * y1).astype(o_ref.dtype)


def _merge(h, pool_out, attn, w_gate, b_gate, w_branch, layer):
    B, S, D = h.shape
    W = pool_out.shape[-1]
    nh = attn.shape[1]
    L = w_gate.shape[0]
    tn = min(512, D)
    tm = min(1024, S)
    nn = D // tn
    bg = b_gate.reshape(L, 1, 2 * D)
    once = pl.Buffered(1)
    return pl.pallas_call(
        _merge_kernel,
        grid=(nn, B, S // tm),
        in_specs=[
            pl.BlockSpec((None, tm, D), lambda n, b, m: (b, m, 0)),
            pl.BlockSpec((None, tm, W), lambda n, b, m: (b, m, 0)),
            pl.BlockSpec((None, nh, tm, HEAD_DIM), lambda n, b, m: (b, 0, m, 0)),
            pl.BlockSpec((None, D, tn), lambda n, b, m: (layer, 0, n), pipeline_mode=once),
            pl.BlockSpec((None, D, tn), lambda n, b, m: (layer, 0, nn + n), pipeline_mode=once),
            pl.BlockSpec((None, 1, tn), lambda n, b, m: (layer, 0, n)),
            pl.BlockSpec((None, 1, tn), lambda n, b, m: (layer, 0, nn + n)),
            pl.BlockSpec((None, None, W, tn), lambda n, b, m: (layer, 0, 0, n), pipeline_mode=once),
            pl.BlockSpec((None, None, W, tn), lambda n, b, m: (layer, 1, 0, n), pipeline_mode=once),
        ],
        out_specs=pl.BlockSpec((None, tm, tn), lambda n, b, m: (b, m, n)),
        out_shape=jax.ShapeDtypeStruct((B, S, D), _BF16),
        scratch_shapes=[pltpu.VMEM((D, tn), _BF16), pltpu.VMEM((D, tn), _BF16),
                        pltpu.VMEM((W, tn), _BF16), pltpu.VMEM((W, tn), _BF16)],
        compiler_params=_params(("arbitrary", "arbitrary", "arbitrary")),
        name="branch_merge",
    )(h, pool_out, attn, w_gate, w_gate, bg, bg, w_branch, w_branch)


def _route_tile(x, g_ref, sc_ref, sh_ref, wr_ref, br_ref, e_ref, pos_ref, gate_ref, cnt_ref):
    first_step = (pl.program_id(0) == 0) & (pl.program_id(1) == 0)

    @pl.when(first_step)
    def _():
        cnt_ref[...] = jnp.zeros_like(cnt_ref)

    h = _rms_mod(x, g_ref[...], sc_ref[...], sh_ref[...])
    tr, D = h.shape
    ns = tr // LANES

    nt = (((1,), (1,)), ((), ()))
    h_hi = h.astype(_BF16)
    h_lo = (h - h_hi.astype(_F32)).astype(_BF16)
    w = wr_ref[...]
    w_hi = w.astype(_BF16)
    w_lo = (w - w_hi.astype(_F32)).astype(_BF16)
    both = lax.dot_general(jnp.concatenate([w_hi, w_lo], axis=0), h_hi, nt, preferred_element_type=_F32)
    logits = (both[:N_EXPERTS] + both[N_EXPERTS:]
              + lax.dot_general(w_hi, h_lo, nt, preferred_element_type=_F32))
    rows = [jnp.concatenate([logits[e:e + 1, s * LANES:(s + 1) * LANES] for s in range(ns)], axis=0)
            for e in range(N_EXPERTS)]
    mx = functools.reduce(jnp.maximum, rows)
    ex = [jnp.exp(r - mx) for r in rows]
    den = functools.reduce(lambda a, b: a + b, ex)
    probs = [v / den for v in ex]
    sel = [probs[e] + br_ref[e:e + 1, :] for e in range(N_EXPERTS)]

    in_top = [None] * N_EXPERTS
    gscore = []
    for g in range(N_EXPERT_GROUPS):
        ids = range(g * EXPERTS_PER_GROUP, (g + 1) * EXPERTS_PER_GROUP)
        score = None
        for i in ids:
            rank = None
            for j in ids:
                if j == i:
                    continue
                beats = (sel[j] > sel[i]) | ((sel[j] == sel[i]) if j < i else False)
                beats = beats.astype(_F32)
                rank = beats if rank is None else rank + beats
            in_top[i] = rank < float(TOP_K)
            term = jnp.where(in_top[i], sel[i], 0.0)
            score = term if score is None else score + term
        gscore.append(score)
    chosen = []
    for g in range(N_EXPERT_GROUPS):
        lose = None
        for g2 in range(N_EXPERT_GROUPS):
            if g2 == g:
                continue
            b = (gscore[g2] > gscore[g]) | ((gscore[g2] == gscore[g]) if g2 < g else False)
            lose = b if lose is None else (lose | b)
        chosen.append(jnp.logical_not(lose))
    picked = [in_top[e] & chosen[e // EXPERTS_PER_GROUP] for e in range(N_EXPERTS)]
    pf = [p.astype(_F32) for p in picked]

    n = N_EXPERTS * ns
    onehot = jnp.concatenate(pf, axis=0).astype(_BF16)
    upper = (lax.broadcasted_iota(jnp.int32, (LANES, LANES), 0)
             <= lax.broadcasted_iota(jnp.int32, (LANES, LANES), 1)).astype(_BF16)
    incl = jnp.dot(onehot, upper, preferred_element_type=_F32)
    r_io = lax.broadcasted_iota(jnp.int32, (n, n), 0)
    c_io = lax.broadcasted_iota(jnp.int32, (n, n), 1)
    shift = ns.bit_length() - 1
    same_expert = lax.shift_right_logical(r_io, shift) == lax.shift_right_logical(c_io, shift)
    earlier = jnp.where(same_expert & (c_io < r_io), 1.0, 0.0).astype(_BF16)
    carry = jnp.dot(earlier, incl.astype(_BF16), preferred_element_type=_F32)[:, LANES - 1:LANES]
    base = cnt_ref[:, 0:1]
    posm = [incl[e * ns:(e + 1) * ns, :] - 1.0 + carry[e * ns:(e + 1) * ns, :] + base[e:e + 1, :]
            for e in range(N_EXPERTS)]
    last = [(incl[(e + 1) * ns - 1:(e + 1) * ns, LANES - 1:LANES] + carry[(e + 1) * ns - 1:(e + 1) * ns, :])
            for e in range(N_EXPERTS)]
    cnt_ref[...] = jnp.broadcast_to(base + jnp.concatenate(last, axis=0), cnt_ref.shape)

    gnum = [pf[e] * probs[e] for e in range(N_EXPERTS)]
    gden = functools.reduce(lambda a, b: a + b, gnum)
    zero = jnp.zeros_like(pf[0])
    seen = zero
    e_out = [zero, zero]
    p_out = [zero, zero]
    g_out = [zero, zero]
    for e in range(N_EXPERTS):
        for k in range(TOP_K):
            hit = pf[e] * (seen == float(k)).astype(_F32)
            e_out[k] = e_out[k] + hit * float(e)
            p_out[k] = p_out[k] + hit * posm[e]
            g_out[k] = g_out[k] + hit * gnum[e]
        seen = seen + pf[e]
    for k in range(TOP_K):
        e_ref[k] = e_out[k].astype(jnp.int32)
        pos_ref[k] = p_out[k].astype(jnp.int32)
        gate_ref[k] = g_out[k] / gden


def _out_route_kernel(x_ref, m_ref, w_ref, ga_ref, g_ref, sc_ref, sh_ref, wr_ref, br_ref,
                      o_ref, e_ref, pos_ref, gate_ref, cnt_ref, wb_ref):
    @pl.when((pl.program_id(0) == 0) & (pl.program_id(1) == 0))
    def _():
        wb_ref[...] = w_ref[...].astype(_BF16)

    xn = x_ref[...] + ga_ref[...] * jnp.dot(m_ref[...], wb_ref[...], preferred_element_type=_F32)
    o_ref[...] = xn
    _route_tile(xn, g_ref, sc_ref, sh_ref, wr_ref, br_ref, e_ref, pos_ref, gate_ref, cnt_ref)


def _out_proj_route(x, merged, w_out, layer, mod, g_moe, wr_t, br):
    B, S, D = x.shape
    tr = min(ROUTE_TILE, S)
    ns = tr // LANES
    assert ns & (ns - 1) == 0
    nb = S // tr
    E = N_EXPERTS
    tok = lambda dt: jax.ShapeDtypeStruct((B * nb, TOP_K, ns, LANES), dt)
    tok_spec = pl.BlockSpec((None, TOP_K, ns, LANES), lambda b, s: (b * nb + s, 0, 0, 0))
    rows = pl.BlockSpec((None, tr, D), lambda b, s: (b, s, 0))
    mod_row = lambda i: pl.BlockSpec((None, None, 1, D), lambda b, s: (i, b, 0, 0))
    xn, eidx, pos, gate, counts = pl.pallas_call(
        _out_route_kernel,
        grid=(B, nb),
        in_specs=[
            rows, rows,
            pl.BlockSpec((None, D, D), lambda b, s: (layer, 0, 0), pipeline_mode=pl.Buffered(1)),
            mod_row(2),
            pl.BlockSpec((1, D), lambda b, s: (0, 0)),
            mod_row(4), mod_row(3),
            pl.BlockSpec((E, D), lambda b, s: (0, 0)),
            pl.BlockSpec((E, 1), lambda b, s: (0, 0)),
        ],
        out_specs=[
            rows, tok_spec, tok_spec, tok_spec,
            pl.BlockSpec((E, LANES), lambda b, s: (0, 0)),
        ],
        out_shape=[
            jax.ShapeDtypeStruct((B, S, D), _F32),
            tok(jnp.int32), tok(jnp.int32), tok(_F32),
            jax.ShapeDtypeStruct((E, LANES), _F32),
        ],
        scratch_shapes=[pltpu.VMEM((D, D), _BF16)],
        compiler_params=_params(("arbitrary", "arbitrary")),
        name="out_proj_route",
    )(x, merged, w_out, mod, g_moe.reshape(1, D), mod, mod, wr_t, br)
    flat = lambda a: a.reshape(B * nb, TOP_K, tr)
    return xn, flat(eidx), flat(pos), flat(gate), counts


def _dispatch_kernel(pad_start_ref, pad_len_ref, nu_ref, dest_ref, x0_ref, xn_ref, g_ref,
                     sc0_ref, sh0_ref, scn_ref, shn_ref, buf_ref, h_ref, zblk, sem):
    i = pl.program_id(0)
    tr = xn_ref.shape[0]

    @pl.when(i == 0)
    def _():
        h_ref[0] = _rms_mod(x0_ref[...], g_ref[...], sc0_ref[...], sh0_ref[...])

    def row_wait():
        pltpu.make_async_copy(h_ref.at[0, pl.ds(0, 1), :], buf_ref.at[pl.ds(0, 1), :], sem).wait()

    def step(slot):
        for r in range(tr):
            for k in range(TOP_K):
                pltpu.make_async_copy(h_ref.at[slot, pl.ds(r, 1), :],
                                      buf_ref.at[pl.ds(dest_ref[k, r], 1), :], sem).start(priority=k)

        @pl.when(i + 1 < pl.num_programs(0))
        def _():
            h_ref[1 - slot] = _rms_mod(xn_ref[...], g_ref[...], scn_ref[...], shn_ref[...])

        for _ in range(tr * TOP_K):
            row_wait()

    for slot in range(2):
        pl.when(lax.rem(i, 2) == slot)(functools.partial(step, slot))

    @pl.when(pl.program_id(0) == 0)
    def _():
        bm = zblk.shape[0]
        nblk = buf_ref.shape[0] // bm
        zblk[...] = jnp.zeros_like(zblk)
        def chunk_copy(row):
            return pltpu.make_async_copy(zblk.at[pl.ds(0, 8), :],
                                         buf_ref.at[pl.ds(pl.multiple_of(row, 8), 8), :], sem)

        def pad_plan(e):
            p0, n = pad_start_ref[e], pad_len_ref[e]
            head = jnp.minimum(n, (8 - (p0 & 7)) & 7)
            return p0, head, p0 + head, lax.shift_right_logical(n - head, 3)

        for e in range(N_EXPERTS):
            p0, head, base, chunks = pad_plan(e)

            def zissue(r, carry, p0=p0):
                pltpu.make_async_copy(zblk.at[pl.ds(0, 1), :], buf_ref.at[pl.ds(p0 + r, 1), :], sem).start()
                return carry

            def cissue(c, carry, base=base):
                chunk_copy(base + c * 8).start()
                return carry

            lax.fori_loop(0, head, zissue, 0)
            lax.fori_loop(0, chunks, cissue, 0)
        for e in range(N_EXPERTS):
            _, head, _, chunks = pad_plan(e)

            def zdrain(r, carry):
                row_wait()
                return carry

            def cdrain(c, carry):
                chunk_copy(0).wait()
                return carry

            lax.fori_loop(0, head, zdrain, 0)
            lax.fori_loop(0, chunks, cdrain, 0)

        def blk_copy(i):
            return pltpu.make_async_copy(zblk, buf_ref.at[pl.ds(pl.multiple_of(i * bm, bm), bm), :], sem)

        def bissue(i, carry):
            blk_copy(i).start()
            return carry

        def bdrain(i, carry):
            blk_copy(i).wait()
            return carry

        lax.fori_loop(nu_ref[0], nblk, bissue, 0)
        lax.fori_loop(nu_ref[0], nblk, bdrain, 0)


def _dispatch(x, g, mod, i_shift, i_scale, dest, pad_start, pad_len, n_used, rows, bm):
    B, S, D = x.shape
    nbt, _, tr = dest.shape
    nb = S // tr
    nxt = lambda i: jnp.minimum(i + 1, nbt - 1)
    grid_spec = pltpu.PrefetchScalarGridSpec(
        num_scalar_prefetch=3,
        grid=(nbt,),
        in_specs=[
            pl.BlockSpec((None, TOP_K, tr), lambda i, *_: (i, 0, 0), memory_space=pltpu.SMEM),
            pl.BlockSpec((None, tr, D), lambda i, *_: (0, 0, 0)),
            pl.BlockSpec((None, tr, D), lambda i, *_: (nxt(i) // nb, nxt(i) % nb, 0)),
            pl.BlockSpec((1, D), lambda i, *_: (0, 0)),
            pl.BlockSpec((None, None, 1, D), lambda i, *_: (i_scale, 0, 0, 0)),
            pl.BlockSpec((None, None, 1, D), lambda i, *_: (i_shift, 0, 0, 0)),
            pl.BlockSpec((None, None, 1, D), lambda i, *_: (i_scale, nxt(i) // nb, 0, 0)),
            pl.BlockSpec((None, None, 1, D), lambda i, *_: (i_shift, nxt(i) // nb, 0, 0)),
        ],
        out_specs=pl.BlockSpec(memory_space=pl.ANY),
        scratch_shapes=[pltpu.VMEM((2, tr, D), _F32), pltpu.VMEM((bm, D), _F32),
                        pltpu.SemaphoreType.DMA(())],
    )
    return pl.pallas_call(
        _dispatch_kernel,
        grid_spec=grid_spec,
        out_shape=jax.ShapeDtypeStruct((rows, D), _F32),
        compiler_params=_params(("arbitrary",)),
        name="moe_dispatch",
    )(pad_start, pad_len, n_used, dest, x, x, g.reshape(1, D), mod, mod, mod, mod)


def _expert_kernel(be_ref, nx_ref, vl_ref, nu_ref, x_ref, wg_hbm, wu_hbm, wd_hbm, o_ref,
                   wgb, wub, wdb, sg, su, sd, sem, *, layer):
    i = pl.program_id(0)
    e = be_ref[i]
    live = i < nu_ref[0]
    new_expert = (i == 0) | (e != be_ref[jnp.maximum(i - 1, 0)])

    def weight_copies(ex):
        return [pltpu.make_async_copy(w.at[layer, ex], s, sem.at[n])
                for n, (w, s) in enumerate(((wg_hbm, sg), (wu_hbm, su), (wd_hbm, sd)))]

    @pl.when(live & new_expert)
    def _():
        @pl.when(i == 0)
        def _():
            for cp in weight_copies(e):
                cp.start()

        for cp in weight_copies(e):
            cp.wait()
        wgb[...] = sg[...].astype(_BF16)
        wub[...] = su[...].astype(_BF16)
        wdb[...] = sd[...].astype(_BF16)

        @pl.when(nx_ref[i] >= 0)
        def _():
            for cp in weight_copies(nx_ref[i]):
                cp.start(priority=1)

    def ffn(rows):
        xb = x_ref[rows, :].astype(_BF16)
        a = jnp.dot(xb, wgb[...], preferred_element_type=_F32)
        u = jnp.dot(xb, wub[...], preferred_element_type=_F32)
        act = (a * _sigmoid(a) * u).astype(_BF16)
        o_ref[rows, :] = jnp.dot(act, wdb[...], preferred_element_type=_F32).astype(o_ref.dtype)

    half = x_ref.shape[0] // 2
    needs_all = vl_ref[i] > half

    @pl.when(live & needs_all)
    def _():
        ffn(slice(None))

    @pl.when(live & jnp.logical_not(needs_all))
    def _():
        ffn(slice(0, half))
        o_ref[half:, :] = jnp.zeros((half, o_ref.shape[1]), o_ref.dtype)

    @pl.when(jnp.logical_not(live))
    def _():
        o_ref[...] = jnp.zeros_like(o_ref)


def _experts(buf, block_expert, next_expert, valid_rows, n_used, wg, wu, wd, layer, bm):
    rows, D = buf.shape
    De = wg.shape[-1]
    nblk = rows // bm
    grid_spec = pltpu.PrefetchScalarGridSpec(
        num_scalar_prefetch=4,
        grid=(nblk,),
        in_specs=[
            pl.BlockSpec((bm, D), lambda i, be, nx, vl, nu: (jnp.minimum(i, jnp.maximum(nu[0] - 1, 0)), 0)),
            pl.BlockSpec(memory_space=pl.ANY),
            pl.BlockSpec(memory_space=pl.ANY),
            pl.BlockSpec(memory_space=pl.ANY),
        ],
        out_specs=pl.BlockSpec((bm, D), lambda i, be, nx, vl, nu: (i, 0)),
        scratch_shapes=[
            pltpu.VMEM((D, De), _BF16), pltpu.VMEM((D, De), _BF16), pltpu.VMEM((De, D), _BF16),
            pltpu.VMEM((D, De), _F32), pltpu.VMEM((D, De), _F32), pltpu.VMEM((De, D), _F32),
            pltpu.SemaphoreType.DMA((3,)),
        ],
    )
    return pl.pallas_call(
        functools.partial(_expert_kernel, layer=layer),
        grid_spec=grid_spec,
        out_shape=jax.ShapeDtypeStruct((rows, D), _BF16),
        compiler_params=_params(("arbitrary",), vmem=60 * 1024 * 1024),
        name="moe_experts",
    )(block_expert, next_expert, valid_rows, n_used, buf, wg, wu, wd)


ROW_CHUNK = 16


def _combine_kernel(src_ref, nch_ref, dst_ref, tot_ref,
                    x_ref, y_hbm, scol_ref, gcol_ref, gm_ref, nf_ref, gn_ref, scn_ref, shn_ref,
                    *rest, final, nb, has_next):
    if has_next:
        o_ref, h_ref, ystage, sem = rest
    else:
        o_ref, ystage, sem = rest
        h_ref = None
    i = pl.program_id(0) * nb + pl.program_id(1)
    n_tiles = pl.num_programs(0) * nb
    cur = lax.rem(i, 2)
    tc = x_ref.shape[0]
    R = ystage.shape[1]

    def chunk_copy(src, dst, buf):
        return pltpu.make_async_copy(y_hbm.at[pl.ds(src, ROW_CHUNK), :],
                                     ystage.at[buf, pl.ds(dst, ROW_CHUNK), :], sem.at[buf])

    def start_tile(tile, buf):
        for e in range(N_EXPERTS):
            idx = tile * N_EXPERTS + e

            def body(c, carry, idx=idx):
                chunk_copy(pl.multiple_of(src_ref[idx] + c * ROW_CHUNK, ROW_CHUNK),
                           pl.multiple_of(dst_ref[idx] + c * ROW_CHUNK, ROW_CHUNK), buf).start()
                return carry

            lax.fori_loop(0, nch_ref[idx], body, 0)

    @pl.when(i == 0)
    def _():
        ystage[...] = jnp.zeros_like(ystage)
        start_tile(0, 0)

    @pl.when(i + 1 < n_tiles)
    def _():
        start_tile(i + 1, 1 - cur)

    def wait_one(c, carry):
        chunk_copy(0, 0, cur).wait()
        return carry

    lax.fori_loop(0, tot_ref[i], wait_one, 0)

    def finish(used):
        scol = scol_ref[...]
        gcol = gcol_ref[...]
        c_io = lax.broadcasted_iota(jnp.int32, (tc, used), 1)
        g = jnp.where(c_io == scol[:, 0:1], gcol[:, 0:1],
                      jnp.where(c_io == scol[:, 1:2], gcol[:, 1:2], 0.0)).astype(_BF16)
        moe = jnp.dot(g, ystage[cur, :used, :], preferred_element_type=_F32)
        xn = x_ref[...] + gm_ref[...] * moe
        if final:
            ms = jnp.mean(xn * xn, axis=-1, keepdims=True)
            xn = xn * lax.rsqrt(ms + EPS) * nf_ref[...]
        o_ref[...] = xn
        if has_next:
            h_ref[...] = _rms_mod(xn, gn_ref[...], scn_ref[...], shn_ref[...]).astype(h_ref.dtype)

    short = (R * 3 // 4) // LANES * LANES
    fits = tot_ref[i] * ROW_CHUNK <= short
    pl.when(fits)(functools.partial(finish, short))
    pl.when(jnp.logical_not(fits))(functools.partial(finish, R))


def _combine(x, y, src, nch, dst, tot, slot_col, gate_col, mod, i_gate, nf, final, g_next, mod_next):
    B, S, D = x.shape
    n_tiles, tc, _ = slot_col.shape
    nb = S // tc
    R = _stage_rows(tc)
    has_next = g_next is not None
    if not has_next:
        g_next, mod_next = nf, mod
    tile = lambda b, s, *_: (b * nb + s, 0, 0)
    row = lambda b, s, *_: (b, s, 0)
    grid_spec = pltpu.PrefetchScalarGridSpec(
        num_scalar_prefetch=4,
        grid=(B, nb),
        in_specs=[
            pl.BlockSpec((None, tc, D), row),
            pl.BlockSpec(memory_space=pl.ANY),
            pl.BlockSpec((None, tc, TOP_K), tile),
            pl.BlockSpec((None, tc, TOP_K), tile),
            pl.BlockSpec((None, None, 1, D), lambda b, s, *_: (i_gate, b, 0, 0)),
            pl.BlockSpec((1, D), lambda b, s, *_: (0, 0)),
            pl.BlockSpec((1, D), lambda b, s, *_: (0, 0)),
            pl.BlockSpec((None, None, 1, D), lambda b, s, *_: (1, b, 0, 0)),
            pl.BlockSpec((None, None, 1, D), lambda b, s, *_: (0, b, 0, 0)),
        ],
        out_specs=([pl.BlockSpec((None, tc, D), row)] * 2) if has_next else pl.BlockSpec((None, tc, D), row),
        scratch_shapes=[pltpu.VMEM((2, R, D), _BF16), pltpu.SemaphoreType.DMA((2,))],
    )
    out_shape = jax.ShapeDtypeStruct((B, S, D), _F32)
    if has_next:
        out_shape = [out_shape, jax.ShapeDtypeStruct((B, S, D), _BF16)]
    return pl.pallas_call(
        functools.partial(_combine_kernel, final=final, nb=nb, has_next=has_next),
        grid_spec=grid_spec,
        out_shape=out_shape,
        compiler_params=_params(("arbitrary", "arbitrary")),
        name="moe_combine",
    )(src, nch, dst, tot, x, y, slot_col, gate_col, mod, nf.reshape(1, D), g_next.reshape(1, D),
      mod_next, mod_next)


def _stage_rows(tc):
    worst = TOP_K * tc + 2 * (ROW_CHUNK - 1) * N_EXPERTS
    return -(-worst // LANES) * LANES


def _combine_plan(eidx, dest, gate, pstarts, tc):
    T = eidx.shape[0] * eidx.shape[2]
    n_tiles = T // tc
    flat = lambda a: a.transpose(0, 2, 1).reshape(n_tiles, tc, TOP_K)
    e_f, d_f, g_f = flat(eidx), flat(dest), flat(gate)
    onehot = (e_f[..., None] == jnp.arange(N_EXPERTS, dtype=jnp.int32)).astype(jnp.int32)
    cnt = jnp.sum(onehot, axis=(1, 2))
    first = pstarts[None, :] + jnp.cumsum(cnt, axis=0) - cnt
    src = first // ROW_CHUNK * ROW_CHUNK
    nch = jnp.where(cnt > 0, (first + cnt - src + ROW_CHUNK - 1) // ROW_CHUNK, 0)
    dst = ROW_CHUNK * (jnp.cumsum(nch, axis=1) - nch)
    slot = d_f + jnp.sum(onehot * (dst - src)[:, None, None, :], axis=-1)
    return (src.reshape(-1), nch.reshape(-1), dst.reshape(-1), jnp.sum(nch, axis=1),
            slot, g_f)


def kernel(x, c, w_ada, b_ada, norm_mix, norm_moe, w_in, b_forget, w_pool, pool_scale, w_branch,
           w_gate, b_gate, w_out, w_router, b_router, w_exp_gate, w_exp_up, w_exp_down, norm_final):
    B, S, D = x.shape
    L = w_ada.shape[0]
    W = D // 2
    nh = W // HEAD_DIM
    T = B * S
    A = T * TOP_K
    bm = min(256, A // N_EXPERTS)
    rows = (A // bm + N_EXPERTS) * bm

    mod_all = _ada(c, w_ada, b_ada)
    mod_all = mod_all.reshape(L, B, N_MOD, D).transpose(0, 2, 1, 3)[:, :, :, None, :]
    wr_t = w_router.T
    br = b_router.reshape(N_EXPERTS, 1)
    colscale = jnp.concatenate([jnp.ones((W,), _F32), jnp.full((W,), HEAD_DIM ** -0.5 * LOG2E, _F32),
                                jnp.ones((2 * W,), _F32)]).reshape(1, 4 * W)

    w_in_t = jnp.swapaxes(w_in, 1, 2)
    h = None
    for l in range(L):
        mod = mod_all[l]
        bf = jnp.zeros((1, LANES), _F32).at[0, :nh].set(b_forget[l])
        if h is None:
            fcum, h = _forget(x, w_in_t, l, nh, bf, norm=(norm_mix[l], mod, 0, 1))
        else:
            fcum = _forget(h, w_in_t, l, nh, bf)
        proj5 = _proj(h, w_in_t, l, colscale)
        pool_out = _pool(proj5, w_pool, l, pool_scale[l].reshape(1, W))
        attn = _attn(proj5, fcum, nh)
        merged = _merge(h, pool_out, attn, w_gate, b_gate, w_branch, l)
        x, eidx, pos, gate, counts = _out_proj_route(x, merged, w_out, l, mod, norm_moe[l], wr_t, br)
        counts = counts[:, 0].astype(jnp.int32)
        pcounts = (counts + bm - 1) // bm * bm
        pends = jnp.cumsum(pcounts)
        pstarts = pends - pcounts
        dest = pos
        for e in range(N_EXPERTS):
            dest = dest + jnp.where(eidx == e, pstarts[e], 0)
        n_used = (pends[-1] // bm).astype(jnp.int32)
        blk_ids = jnp.arange(rows // bm, dtype=jnp.int32)
        blk_ids = jnp.minimum(blk_ids, n_used - 1)
        block_expert = jnp.sum((pends[None, :] <= (blk_ids * bm)[:, None]).astype(jnp.int32), axis=1)
        block_expert = jnp.minimum(block_expert, N_EXPERTS - 1)
        ids = jnp.arange(N_EXPERTS, dtype=jnp.int32)
        later = (ids[None, :] > ids[:, None]) & (pcounts[None, :] > 0)
        next_of = jnp.min(jnp.where(later, ids[None, :], N_EXPERTS), axis=1)
        next_of = jnp.where(next_of == N_EXPERTS, -1, next_of)
        next_expert = jnp.sum(jnp.where(block_expert[:, None] == ids[None, :], next_of[None, :], 0), axis=1)
        n_used = n_used.reshape(1)
        td = min(DISPATCH_TILE, S)
        dest_d = dest.transpose(0, 2, 1).reshape(T // td, td, TOP_K).transpose(0, 2, 1)
        buf = _dispatch(x, norm_moe[l], mod, 3, 4, dest_d, pstarts + counts, pcounts - counts,
                        n_used, rows, bm)
        ends = jnp.sum(jnp.where(block_expert[:, None] == ids[None, :], (pstarts + counts)[None, :], 0), axis=1)
        valid_rows = jnp.clip(ends - jnp.arange(rows // bm, dtype=jnp.int32) * bm, 0, bm)
        y = _experts(buf, block_expert, next_expert.astype(jnp.int32), valid_rows.astype(jnp.int32), n_used,
                     w_exp_gate, w_exp_up, w_exp_down, l, bm)
        plan = _combine_plan(eidx, dest, gate, pstarts, min(COMBINE_TILE, S))
        if l + 1 < L:
            x, h = _combine(x, y, *plan, mod, 5, norm_final, False, norm_mix[l + 1], mod_all[l + 1])
        else:
            x = _combine(x, y, *plan, mod, 5, norm_final, True, None, None)
    return x
```
